```python
import jax, jax.numpy as jnp
from jax import lax
import numpy as np

D_MODEL = 2048
BATCH = 4
SEQ = 2048
DEPTH = 1
DEC_BATCH = 128
DEC_SEQ = 8
PAST_LEN = 16384
PAGE_SIZE = 128

N_META = 16
MIX_W = D_MODEL
POOL_W = MIX_W // 2
MLSTM_W = MIX_W - POOL_W
POOL_WINDOWS = (2, 4, 8, 16)
N_POOL_GROUPS = len(POOL_WINDOWS)
POOL_GW = POOL_W // N_POOL_GROUPS
POOL_HIST = max(POOL_WINDOWS) - 1
N_HEADS = 4
HEAD_DIM = MLSTM_W // N_HEADS
CHUNK = 64
D_FF = -(-8 * D_MODEL // (3 * 256)) * 256
IN_W = POOL_W + 4 * MLSTM_W + 2 * N_HEADS
EPS = 1e-6

kernel_name = 'hymba_pool_mlstm_decoder_step'


def rmsnorm(x, w):
    xf = x.astype(jnp.float32)
    y = xf * lax.rsqrt(jnp.mean(xf * xf, axis=-1, keepdims=True) + EPS)
    return (y * w.astype(jnp.float32)).astype(x.dtype)


def pool_mix(u_prev, u_new, pos0, w_pool, scale):
    B, T, _ = u_new.shape
    ext = jnp.concatenate([u_prev.astype(u_new.dtype), u_new], axis=1)
    cs = jnp.pad(jnp.cumsum(ext.astype(jnp.float32), axis=1), ((0, 0), (1, 0), (0, 0)))
    end = cs[:, POOL_HIST + 1:]
    pos = pos0 + jnp.arange(T, dtype=jnp.int32)
    u32 = u_new.astype(jnp.float32)
    outs = []
    for g, w in enumerate(POOL_WINDOWS):
        sl = slice(g * POOL_GW, (g + 1) * POOL_GW)
        start = cs[:, POOL_HIST + 1 - w:POOL_HIST + 1 - w + T, sl]
        cnt = jnp.minimum(w, pos + 1).astype(jnp.float32)[None, :, None]
        outs.append((end[..., sl] - start) / cnt - u32[..., sl])
    d = jnp.stack(outs, axis=2)
    y = jnp.einsum('btgc,gcd->btgd', d, w_pool.astype(jnp.float32)).reshape(B, T, POOL_W)
    y = y * scale.astype(jnp.float32)
    return y.astype(u_new.dtype), ext[:, -POOL_HIST:]


def _mlstm_chunk(carry, inp):
    C, n, m = carry
    q, k, v, ig, lf = inp
    L = q.shape[-2]
    b = jnp.cumsum(lf, axis=-1)
    a = ig - b
    m_t = jnp.maximum(m[..., None] + b, lax.cummax(a, axis=a.ndim - 1) + b)
    inter = jnp.exp(m[..., None] + b - m_t)
    causal = jnp.tril(jnp.ones((L, L), dtype=bool))
    log_d = a[..., None, :] + (b - m_t)[..., :, None]
    dmat = jnp.exp(jnp.where(causal, log_d, -jnp.inf))
    s = jnp.einsum('bhtd,bhsd->bhts', q, k) * dmat
    num = jnp.einsum('bhts,bhsv->bhtv', s, v) + inter[..., None] * jnp.einsum('bhtd,bhdv->bhtv', q, C)
    qn = s.sum(-1) + inter * jnp.einsum('bhtd,bhd->bht', q, n)
    h = num / jnp.maximum(jnp.abs(qn), jnp.exp(-m_t))[..., None]
    m_new = m_t[..., -1]
    decay = jnp.exp(m + b[..., -1] - m_new)
    w_s = jnp.exp(a + (b[..., -1] - m_new)[..., None])
    C_new = decay[..., None, None] * C + jnp.einsum('bhs,bhsd,bhsv->bhdv', w_s, k, v)
    n_new = decay[..., None] * n + jnp.einsum('bhs,bhsd->bhd', w_s, k)
    return (C_new, n_new, m_new), h


def mlstm_scan(q, k, v, ig, lf, C, n, m, chunk):
    B, H, T, Dh = q.shape
    nc = T // chunk

    def to_chunks(a):
        return jnp.moveaxis(a.reshape(B, H, nc, chunk, *a.shape[3:]), 2, 0)

    (C, n, m), h = lax.scan(_mlstm_chunk, (C, n, m), (to_chunks(q), to_chunks(k), to_chunks(v), to_chunks(ig), to_chunks(lf)))
    h = jnp.moveaxis(h, 0, 2).reshape(B, H, T, Dh)
    return h, C, n, m


def mlstm_mixer(q, k, v, o, gi, gf, C0, n0, m0, segments, b_i, b_f, g_norm):
    B, T, _ = q.shape

    def heads(a):
        return a.reshape(B, T, N_HEADS, HEAD_DIM).transpose(0, 2, 1, 3).astype(jnp.float32)

    qh, kh, vh = heads(q), heads(k) * (HEAD_DIM ** -0.5), heads(v)
    ig = (gi.astype(jnp.float32) + b_i.astype(jnp.float32)).transpose(0, 2, 1)
    lf = jax.nn.log_sigmoid(gf.astype(jnp.float32) + b_f.astype(jnp.float32)).transpose(0, 2, 1)
    C, n, m = C0.astype(jnp.float32), n0.astype(jnp.float32), m0.astype(jnp.float32)
    hs = []
    start = 0
    for length, chunk in segments:
        sl = slice(start, start + length)
        h, C, n, m = mlstm_scan(qh[:, :, sl], kh[:, :, sl], vh[:, :, sl], ig[:, :, sl], lf[:, :, sl], C, n, m, chunk)
        hs.append(h)
        start += length
    h = jnp.concatenate(hs, axis=2)
    h = h * lax.rsqrt(jnp.mean(h * h, axis=-1, keepdims=True) + EPS) * g_norm.astype(jnp.float32).reshape(N_HEADS, 1, HEAD_DIM)
    h = h.transpose(0, 2, 1, 3).reshape(B, T, MLSTM_W)
    y = jax.nn.sigmoid(o.astype(jnp.float32)) * h
    return y.astype(q.dtype), C, n, m


def block(x, pool_prev, C0, n0, m0, pos0, segments, norm_mix_w, w_in, b_i, b_f, w_pool, pool_scale, g_norm, w_out, norm_ffn_w, w_gate, w_up, w_down):
    xn = rmsnorm(x, norm_mix_w)
    p = xn @ w_in.astype(x.dtype)
    u = p[..., :POOL_W]
    off = POOL_W
    q = p[..., off:off + MLSTM_W]
    k = p[..., off + MLSTM_W:off + 2 * MLSTM_W]
    v = p[..., off + 2 * MLSTM_W:off + 3 * MLSTM_W]
    o = p[..., off + 3 * MLSTM_W:off + 4 * MLSTM_W]
    gi = p[..., off + 4 * MLSTM_W:off + 4 * MLSTM_W + N_HEADS]
    gf = p[..., off + 4 * MLSTM_W + N_HEADS:]
    y_pool, pool_new = pool_mix(pool_prev, u, pos0, w_pool, pool_scale)
    y_ml, C, n, m = mlstm_mixer(q, k, v, o, gi, gf, C0, n0, m0, segments, b_i, b_f, g_norm)
    x = x + jnp.concatenate([y_pool, y_ml], axis=-1) @ w_out.astype(x.dtype)
    xn2 = rmsnorm(x, norm_ffn_w)
    x = x + (jax.nn.silu(xn2 @ w_gate.astype(x.dtype)) * (xn2 @ w_up.astype(x.dtype))) @ w_down.astype(x.dtype)
    return x, pool_new, C, n, m


def setup_inputs(seed: int = 0) -> dict:
    key = jax.random.key(seed)
    ks = jax.random.split(key, 20)
    f32 = jnp.float32
    nrm = lambda k, s: jax.random.normal(k, s, dtype=f32)
    return {
        'x_prompt': nrm(ks[0], (BATCH, SEQ, D_MODEL)),
        'x_sample': nrm(ks[1], (DEC_BATCH, DEC_SEQ, D_MODEL)),
        'state_pool': nrm(ks[2], (DEPTH, DEC_BATCH, POOL_HIST, POOL_W)),
        'state_mlstm_C': nrm(ks[3], (DEPTH, DEC_BATCH, N_HEADS, HEAD_DIM, HEAD_DIM)) * HEAD_DIM ** -0.5,
        'state_mlstm_n': nrm(ks[4], (DEPTH, DEC_BATCH, N_HEADS, HEAD_DIM)) * HEAD_DIM ** -0.5,
        'state_mlstm_m': nrm(ks[5], (DEPTH, DEC_BATCH, N_HEADS)) * 0.5,
        'meta_tokens': nrm(ks[6], (N_META, D_MODEL)),
        'norm_mix_w': 1.0 + 0.02 * nrm(ks[7], (DEPTH, D_MODEL)),
        'w_in': nrm(ks[8], (DEPTH, D_MODEL, IN_W)) * D_MODEL ** -0.5,
        'b_igate': 0.1 * nrm(ks[9], (DEPTH, N_HEADS)),
        'b_fgate': jnp.linspace(3.0, 6.0, N_HEADS, dtype=f32)[None, :] + 0.01 * nrm(ks[10], (DEPTH, N_HEADS)),
        'w_pool': nrm(ks[11], (DEPTH, N_POOL_GROUPS, POOL_GW, POOL_GW)) * POOL_GW ** -0.5,
        'pool_scale': 1.0 + 0.1 * nrm(ks[12], (DEPTH, POOL_W)),
        'mlstm_norm_w': 1.0 + 0.02 * nrm(ks[13], (DEPTH, MLSTM_W)),
        'w_out': nrm(ks[14], (DEPTH, MIX_W, D_MODEL)) * MIX_W ** -0.5,
        'norm_ffn_w': 1.0 + 0.02 * nrm(ks[15], (DEPTH, D_MODEL)),
        'w_gate': nrm(ks[16], (DEPTH, D_MODEL, D_FF)) * D_MODEL ** -0.5,
        'w_up': nrm(ks[17], (DEPTH, D_MODEL, D_FF)) * D_MODEL ** -0.5,
        'w_down': nrm(ks[18], (DEPTH, D_FF, D_MODEL)) * D_FF ** -0.5,
        'norm_final_w': 1.0 + 0.02 * nrm(ks[19], (D_MODEL,)),
    }


def reference(x_prompt, x_sample, state_pool, state_mlstm_C, state_mlstm_n, state_mlstm_m, meta_tokens, norm_mix_w, w_in, b_igate, b_fgate, w_pool, pool_scale, mlstm_norm_w, w_out, norm_ffn_w, w_gate, w_up, w_down, norm_final_w):
    B, T_p, _ = x_prompt.shape
    T_s = x_sample.shape[1]
    dt = x_prompt.dtype
    hp = jnp.concatenate([jnp.broadcast_to(meta_tokens.astype(dt)[None], (B, N_META, D_MODEL)), x_prompt], axis=1)
    hs = x_sample
    seg_p = ((N_META, N_META), (T_p, CHUNK))
    seg_s = ((T_s, T_s),)
    pool_p, C_p, n_p, m_p = [], [], [], []
    pool_s, C_s, n_s, m_s = [], [], [], []
    for l in range(DEPTH):
        wl = (norm_mix_w[l], w_in[l], b_igate[l], b_fgate[l], w_pool[l], pool_scale[l], mlstm_norm_w[l], w_out[l], norm_ffn_w[l], w_gate[l], w_up[l], w_down[l])
        zp = jnp.zeros((B, POOL_HIST, POOL_W), dt)
        zC = jnp.zeros((B, N_HEADS, HEAD_DIM, HEAD_DIM), jnp.float32)
        zn = jnp.zeros((B, N_HEADS, HEAD_DIM), jnp.float32)
        zm = jnp.zeros((B, N_HEADS), jnp.float32)
        hp, sp, Cp, np_, mp = block(hp, zp, zC, zn, zm, 0, seg_p, *wl)
        hs, ss, Cs, ns, ms = block(hs, state_pool[l], state_mlstm_C[l], state_mlstm_n[l], state_mlstm_m[l], PAST_LEN, seg_s, *wl)
        pool_p.append(sp); C_p.append(Cp); n_p.append(np_); m_p.append(mp)
        pool_s.append(ss); C_s.append(Cs); n_s.append(ns); m_s.append(ms)
    y_prompt = rmsnorm(hp, norm_final_w)[:, N_META:]
    y_sample = rmsnorm(hs, norm_final_w)
    return (y_prompt, y_sample, jnp.stack(pool_p), jnp.stack(C_p), jnp.stack(n_p), jnp.stack(m_p), jnp.stack(pool_s), jnp.stack(C_s), jnp.stack(n_s), jnp.stack(m_s))
```

```python
import functools

import jax
import jax.numpy as jnp
from jax import lax
from jax.experimental import pallas as pl
from jax.experimental.pallas import tpu as pltpu

D_MODEL = 2048
N_META = 16
POOL_W = 1024
MLSTM_W = 1024
POOL_WINDOWS = (2, 4, 8, 16)
N_POOL_GROUPS = 4
POOL_GW = 256
POOL_HIST = 15
N_HEADS = 4
HEAD_DIM = 256
D_FF = 5632
QKVO_W = 4 * MLSTM_W
MAIN_W = POOL_W + QKVO_W
PAST_LEN = 16384
EPS = 1e-6

LANES = 128
HIST_PAD = 16
VMEM_LIMIT = 56 * 1024 * 1024

BF16 = jnp.bfloat16
F32 = jnp.float32


def _params(sem):
    return pltpu.CompilerParams(dimension_semantics=sem, vmem_limit_bytes=VMEM_LIMIT)


def _rms(x, w):
    return x * lax.rsqrt(jnp.mean(x * x, axis=-1, keepdims=True) + EPS) * w


def _inproj_kernel(x_ref, nw_ref, w_ref, wg_ref, u_ref, qkvo_ref, g_ref, xn_ref, *, n_u_tiles):
    j = pl.program_id(1)

    @pl.when(j == 0)
    def _():
        xn = _rms(x_ref[...], nw_ref[...]).astype(BF16)
        xn_ref[...] = xn
        g_ref[...] = jnp.dot(xn, wg_ref[...], preferred_element_type=F32)

    p = jnp.dot(xn_ref[...], w_ref[...], preferred_element_type=F32)

    @pl.when(j < n_u_tiles)
    def _():
        u_ref[...] = p

    @pl.when(j >= n_u_tiles)
    def _():
        qkvo_ref[...] = p.astype(BF16)


def _inproj(x, norm_w, w_main, w_gate, *, tm, tn):
    m = x.shape[0]
    n_u = POOL_W // tn
    grid = (m // tm, MAIN_W // tn)
    return pl.pallas_call(
        functools.partial(_inproj_kernel, n_u_tiles=n_u),
        grid=grid,
        in_specs=[
            pl.BlockSpec((tm, D_MODEL), lambda i, j: (i, 0)),
            pl.BlockSpec((1, D_MODEL), lambda i, j: (0, 0)),
            pl.BlockSpec((D_MODEL, tn), lambda i, j: (0, j)),
            pl.BlockSpec((D_MODEL, LANES), lambda i, j: (0, 0)),
        ],
        out_specs=[
            pl.BlockSpec((tm, tn), lambda i, j: (i, jnp.minimum(j, n_u - 1))),
            pl.BlockSpec((tm, tn), lambda i, j: (i, jnp.maximum(j - n_u, 0))),
            pl.BlockSpec((tm, LANES), lambda i, j: (i, 0)),
        ],
        out_shape=[
            jax.ShapeDtypeStruct((m, POOL_W), F32),
            jax.ShapeDtypeStruct((m, QKVO_W), BF16),
            jax.ShapeDtypeStruct((m, LANES), F32),
        ],
        scratch_shapes=[pltpu.VMEM((tm, D_MODEL), BF16)],
        compiler_params=_params(("arbitrary", "arbitrary")),
        name="inproj",
    )(x, norm_w, w_main, w_gate)


def _pool_kernel(u_ref, prev_ref, wp_ref, sc_ref, y_ref, st_ref, e_ref, d_ref, *, bb, tt, pos0):
    ti = pl.program_id(1)
    nt = pl.num_programs(1)

    @pl.when(ti == 0)
    def _():
        e_ref[:, 0:1, :] = jnp.zeros((bb, 1, POOL_W), F32)
        e_ref[:, 1:HIST_PAD, :] = prev_ref[...]

    e_ref[:, HIST_PAD:, :] = u_ref[...]

    pos = pos0 + ti * tt + lax.broadcasted_iota(jnp.int32, (tt, 1), 0)
    for b in range(bb):
        for g, w in enumerate(POOL_WINDOWS):
            cols = slice(g * POOL_GW, (g + 1) * POOL_GW)
            e = e_ref[b, :, cols]
            s = e + pltpu.roll(e, 1, axis=0)
            for k in range(1, g + 1):
                s = s + pltpu.roll(s, 2**k, axis=0)
            cnt = jnp.minimum(w, pos + 1).astype(F32)
            d_ref[b * tt:(b + 1) * tt, cols] = s[HIST_PAD:, :] / cnt - e[HIST_PAD:, :]

    for g in range(N_POOL_GROUPS):
        cols = slice(g * POOL_GW, (g + 1) * POOL_GW)
        y = jnp.dot(d_ref[:, cols].astype(BF16), wp_ref[g], preferred_element_type=F32) * sc_ref[:, cols]
        y_ref[:, :, cols] = y.reshape(bb, tt, POOL_GW).astype(BF16)

    @pl.when(ti == nt - 1)
    def _():
        st_ref[...] = e_ref[:, tt + 1:tt + HIST_PAD, :]

    e_ref[:, 0:HIST_PAD, :] = e_ref[:, tt:tt + HIST_PAD, :]


def _pool(u, prev, w_pool, scale, *, bb, tt, pos0, shared_prev):
    b, t, _ = u.shape
    prev_map = (lambda bi, ti: (0, 0, 0)) if shared_prev else (lambda bi, ti: (bi, 0, 0))
    return pl.pallas_call(
        functools.partial(_pool_kernel, bb=bb, tt=tt, pos0=pos0),
        grid=(b // bb, t // tt),
        in_specs=[
            pl.BlockSpec((bb, tt, POOL_W), lambda bi, ti: (bi, ti, 0)),
            pl.BlockSpec((bb, POOL_HIST, POOL_W), prev_map),
            pl.BlockSpec((N_POOL_GROUPS, POOL_GW, POOL_GW), lambda bi, ti: (0, 0, 0)),
            pl.BlockSpec((1, POOL_W), lambda bi, ti: (0, 0)),
        ],
        out_specs=[
            pl.BlockSpec((bb, tt, POOL_W), lambda bi, ti: (bi, ti, 0)),
            pl.BlockSpec((bb, POOL_HIST, POOL_W), lambda bi, ti: (bi, 0, 0)),
        ],
        out_shape=[
            jax.ShapeDtypeStruct((b, t, POOL_W), BF16),
            jax.ShapeDtypeStruct((b, POOL_HIST, POOL_W), F32),
        ],
        scratch_shapes=[
            pltpu.VMEM((bb, HIST_PAD + tt, POOL_W), F32),
            pltpu.VMEM((bb * tt, POOL_W), F32),
        ],
        compiler_params=_params(("arbitrary", "arbitrary")),
        name="pool",
    )(u, prev, w_pool, scale)


def _mlstm_chunk(q, k, v, o, gi, gf, c, n, m, gnorm, *, chunk):
    ti = lax.broadcasted_iota(jnp.int32, (chunk, chunk), 0)
    si = lax.broadcasted_iota(jnp.int32, (chunk, chunk), 1)
    tril = si <= ti
    eye = si == ti

    def to_row(col):
        return jnp.sum(jnp.where(eye, col, 0.0), axis=0, keepdims=True)

    lf = jax.nn.log_sigmoid(gf)
    b = jnp.sum(jnp.where(tril, to_row(lf), 0.0), axis=1, keepdims=True)
    a = gi - b
    a_row = to_row(a)
    cummax_a = jnp.max(jnp.where(tril, a_row, -jnp.inf), axis=1, keepdims=True)
    m_t = jnp.maximum(m + b, cummax_a + b)
    inter = jnp.exp(m + b - m_t)
    dmat = jnp.exp(jnp.where(tril, a_row + (b - m_t), -jnp.inf))

    k = k * (HEAD_DIM ** -0.5)
    s = lax.dot_general(q, k, (((1,), (1,)), ((), ())), preferred_element_type=F32) * dmat
    num = jnp.dot(s.astype(BF16), v, preferred_element_type=F32)
    num = num + inter * jnp.dot(q, c.astype(BF16), preferred_element_type=F32)
    qn = jnp.sum(s, axis=1, keepdims=True) + inter * jnp.sum(q.astype(F32) * n, axis=1, keepdims=True)
    h = num / jnp.maximum(jnp.abs(qn), jnp.exp(-m_t))

    b_last = b[chunk - 1:chunk, :]
    m_new = m_t[chunk - 1:chunk, :]
    decay = jnp.exp(m + b_last - m_new)
    w = jnp.exp(a + (b_last - m_new))
    wv = (w * v.astype(F32)).astype(BF16)
    c_new = decay * c + lax.dot_general(k, wv, (((0,), (0,)), ((), ())), preferred_element_type=F32)
    n_new = decay * n + jnp.sum(w * k.astype(F32), axis=0, keepdims=True)

    y = jax.nn.sigmoid(o.astype(F32)) * _rms(h, gnorm)
    return y, c_new, n_new, m_new


def _gate_cols(g, bias_ref, head):
    lane = lax.broadcasted_iota(jnp.int32, g.shape, 1)
    gi = jnp.sum(jnp.where(lane == head, g, 0.0), axis=1, keepdims=True) + bias_ref[head]
    gf = jnp.sum(jnp.where(lane == head + N_HEADS, g, 0.0), axis=1, keepdims=True) + bias_ref[head + N_HEADS]
    return gi, gf


def _mlstm_seq_kernel(bias_ref, q_ref, k_ref, v_ref, o_ref, g_ref, gn_ref, c0_ref, n0_ref, m0_ref,
                      y_ref, c_ref, n_ref, m_ref, *, chunk):
    head = pl.program_id(1)
    ci = pl.program_id(2)

    @pl.when(ci == 0)
    def _():
        c_ref[...] = c0_ref[...]
        n_ref[...] = n0_ref[...]
        m_ref[...] = m0_ref[...]

    gi, gf = _gate_cols(g_ref[...], bias_ref, head)
    y, c_new, n_new, m_new = _mlstm_chunk(
        q_ref[...], k_ref[...], v_ref[...], o_ref[...], gi, gf,
        c_ref[0, 0], n_ref[0, 0], m_ref[0, 0], gn_ref[...], chunk=chunk)
    y_ref[...] = y.astype(BF16)
    c_ref[0, 0] = c_new
    n_ref[0, 0] = n_new
    m_ref[0, 0] = m_new


def _mlstm_seq(qkvo, gates, bias, gnorm, c0, n0, m0, *, batch, seq, chunk, shared_init):
    nc = seq // chunk
    rows = lambda bi, h, ci: bi * nc + ci
    st_b = (lambda bi: 0) if shared_init else (lambda bi: bi)

    def col_spec(off):
        return pl.BlockSpec((chunk, HEAD_DIM), lambda bi, h, ci: (rows(bi, h, ci), off + h))

    def st_spec(shape, in_b):
        nd = len(shape)
        return pl.BlockSpec((1, 1) + shape, lambda bi, h, ci: (in_b(bi), h) + (0,) * nd)

    ident = lambda bi: bi
    return pl.pallas_call(
        functools.partial(_mlstm_seq_kernel, chunk=chunk),
        grid=(batch, N_HEADS, nc),
        in_specs=[
            pl.BlockSpec(memory_space=pltpu.SMEM),
            col_spec(0), col_spec(N_HEADS), col_spec(2 * N_HEADS), col_spec(3 * N_HEADS),
            pl.BlockSpec((chunk, LANES), lambda bi, h, ci: (rows(bi, h, ci), 0)),
            pl.BlockSpec((1, HEAD_DIM), lambda bi, h, ci: (0, h)),
            st_spec((HEAD_DIM, HEAD_DIM), st_b), st_spec((1, HEAD_DIM), st_b), st_spec((1, 1), st_b),
        ],
        out_specs=[
            pl.BlockSpec((chunk, HEAD_DIM), lambda bi, h, ci: (rows(bi, h, ci), h)),
            st_spec((HEAD_DIM, HEAD_DIM), ident), st_spec((1, HEAD_DIM), ident), st_spec((1, 1), ident),
        ],
        out_shape=[
            jax.ShapeDtypeStruct((batch * seq, MLSTM_W), BF16),
            jax.ShapeDtypeStruct((batch, N_HEADS, HEAD_DIM, HEAD_DIM), F32),
            jax.ShapeDtypeStruct((batch, N_HEADS, 1, HEAD_DIM), F32),
            jax.ShapeDtypeStruct((batch, N_HEADS, 1, 1), F32),
        ],
        compiler_params=_params(("arbitrary", "arbitrary", "arbitrary")),
        name="mlstm_seq",
    )(bias, qkvo, qkvo, qkvo, qkvo, gates, gnorm, c0, n0, m0)


def _mlstm_step_kernel(bias_ref, q_ref, k_ref, v_ref, o_ref, g_ref, gn_ref, c0_ref, n0_ref, m0_ref,
                       y_ref, c_ref, n_ref, m_ref, qf_ref, kf_ref, vf_ref, of_ref, yf_ref, *, bb, seq):
    head = pl.program_id(1)
    qf_ref[...] = q_ref[...].astype(F32)
    kf_ref[...] = k_ref[...].astype(F32)
    vf_ref[...] = v_ref[...].astype(F32)
    of_ref[...] = o_ref[...].astype(F32)
    gnorm = gn_ref[...]

    def body(b, carry):
        r = pl.ds(pl.multiple_of(b * seq, seq), seq)
        gi, gf = _gate_cols(g_ref[r, :], bias_ref, head)
        y, c_new, n_new, m_new = _mlstm_chunk(
            qf_ref[r, :].astype(BF16), kf_ref[r, :].astype(BF16), vf_ref[r, :].astype(BF16),
            of_ref[r, :].astype(BF16), gi, gf, c0_ref[b, 0], n0_ref[b, 0], m0_ref[b, 0], gnorm, chunk=seq)
        yf_ref[r, :] = y
        c_ref[b, 0] = c_new
        n_ref[b, 0] = n_new
        m_ref[b, 0] = m_new
        return carry

    lax.fori_loop(0, bb, body, 0)
    y_ref[...] = yf_ref[...].astype(BF16)


def _mlstm_step(qkvo, gates, bias, gnorm, c0, n0, m0, *, batch, seq, bb):
    def col_spec(off):
        return pl.BlockSpec((bb * seq, HEAD_DIM), lambda bi, h: (bi, off + h))

    def st_spec(shape):
        nd = len(shape)
        return pl.BlockSpec((bb, 1) + shape, lambda bi, h: (bi, h) + (0,) * nd)

    states = [st_spec((HEAD_DIM, HEAD_DIM)), st_spec((1, HEAD_DIM)), st_spec((1, 1))]
    return pl.pallas_call(
        functools.partial(_mlstm_step_kernel, bb=bb, seq=seq),
        grid=(batch // bb, N_HEADS),
        in_specs=[
            pl.BlockSpec(memory_space=pltpu.SMEM),
            col_spec(0), col_spec(N_HEADS), col_spec(2 * N_HEADS), col_spec(3 * N_HEADS),
            pl.BlockSpec((bb * seq, LANES), lambda bi, h: (bi, 0)),
            pl.BlockSpec((1, HEAD_DIM), lambda bi, h: (0, h)),
        ] + states,
        out_specs=[pl.BlockSpec((bb * seq, HEAD_DIM), lambda bi, h: (bi, h))] + states,
        out_shape=[
            jax.ShapeDtypeStruct((batch * seq, MLSTM_W), BF16),
            jax.ShapeDtypeStruct((batch, N_HEADS, HEAD_DIM, HEAD_DIM), F32),
            jax.ShapeDtypeStruct((batch, N_HEADS, 1, HEAD_DIM), F32),
            jax.ShapeDtypeStruct((batch, N_HEADS, 1, 1), F32),
        ],
        scratch_shapes=[pltpu.VMEM((bb * seq, HEAD_DIM), F32) for _ in range(5)],
        compiler_params=_params(("arbitrary", "arbitrary")),
        name="mlstm_step",
    )(bias, qkvo, qkvo, qkvo, qkvo, gates, gnorm, c0, n0, m0)


def _outproj_kernel(x_ref, yp_ref, ym_ref, wo_ref, nw_ref, x2_ref, xn2_ref):
    x2 = x_ref[...] + jnp.dot(yp_ref[...], wo_ref[0:POOL_W, :], preferred_element_type=F32)
    x2 = x2 + jnp.dot(ym_ref[...], wo_ref[POOL_W:, :], preferred_element_type=F32)
    x2_ref[...] = x2
    xn2_ref[...] = _rms(x2, nw_ref[...]).astype(BF16)


def _outproj(x, y_pool, y_ml, w_out, norm_w, *, tm):
    m = x.shape[0]
    row = lambda w: pl.BlockSpec((tm, w), lambda i: (i, 0))
    return pl.pallas_call(
        _outproj_kernel,
        grid=(m // tm,),
        in_specs=[
            row(D_MODEL), row(POOL_W), row(MLSTM_W),
            pl.BlockSpec((D_MODEL, D_MODEL), lambda i: (0, 0)),
            pl.BlockSpec((1, D_MODEL), lambda i: (0, 0)),
        ],
        out_specs=[row(D_MODEL), row(D_MODEL)],
        out_shape=[
            jax.ShapeDtypeStruct((m, D_MODEL), F32),
            jax.ShapeDtypeStruct((m, D_MODEL), BF16),
        ],
        compiler_params=_params(("arbitrary",)),
        name="outproj",
    )(x, y_pool, y_ml, w_out, norm_w)


def _ffn_kernel(xn_ref, x2_ref, wg_ref, wu_ref, wd_ref, nw_ref, y_ref, acc_ref):
    f = pl.program_id(1)
    nf = pl.num_programs(1)

    @pl.when(f == 0)
    def _():
        acc_ref[...] = x2_ref[...]

    xn = xn_ref[...]
    g = jnp.dot(xn, wg_ref[...], preferred_element_type=F32)
    u = jnp.dot(xn, wu_ref[...], preferred_element_type=F32)
    h = (jax.nn.silu(g) * u).astype(BF16)
    acc_ref[...] += jnp.dot(h, wd_ref[...], preferred_element_type=F32)

    @pl.when(f == nf - 1)
    def _():
        y_ref[...] = _rms(acc_ref[...], nw_ref[...])


def _ffn(xn2, x2, w_gate, w_up, w_down, norm_w, *, tm, tf):
    m = xn2.shape[0]
    return pl.pallas_call(
        _ffn_kernel,
        grid=(m // tm, D_FF // tf),
        in_specs=[
            pl.BlockSpec((tm, D_MODEL), lambda i, f: (i, 0)),
            pl.BlockSpec((tm, D_MODEL), lambda i, f: (i, 0)),
            pl.BlockSpec((D_MODEL, tf), lambda i, f: (0, f)),
            pl.BlockSpec((D_MODEL, tf), lambda i, f: (0, f)),
            pl.BlockSpec((tf, D_MODEL), lambda i, f: (f, 0)),
            pl.BlockSpec((1, D_MODEL), lambda i, f: (0, 0)),
        ],
        out_specs=pl.BlockSpec((tm, D_MODEL), lambda i, f: (i, 0)),
        out_shape=jax.ShapeDtypeStruct((m, D_MODEL), F32),
        scratch_shapes=[pltpu.VMEM((tm, D_MODEL), F32)],
        compiler_params=_params(("arbitrary", "arbitrary")),
        name="ffn",
    )(xn2, x2, w_gate, w_up, w_down, norm_w)


def kernel(x_prompt, x_sample, state_pool, state_mlstm_C, state_mlstm_n, state_mlstm_m, meta_tokens, norm_mix_w, w_in, b_igate, b_fgate, w_pool, pool_scale, mlstm_norm_w, w_out, norm_ffn_w, w_gate, w_up, w_down, norm_final_w):
    bp, tp, _ = x_prompt.shape
    bs, ts, _ = x_sample.shape

    w_main = w_in[0, :, :MAIN_W].astype(BF16)
    w_g8 = jnp.pad(w_in[0, :, MAIN_W:], ((0, 0), (0, LANES - 2 * N_HEADS))).astype(BF16)
    wp = w_pool[0].astype(BF16)
    wo = w_out[0].astype(BF16)
    wg = w_gate[0].astype(BF16)
    wu = w_up[0].astype(BF16)
    wd = w_down[0].astype(BF16)
    nmix = norm_mix_w[0].reshape(1, D_MODEL)
    nffn = norm_ffn_w[0].reshape(1, D_MODEL)
    nfin = norm_final_w.reshape(1, D_MODEL)
    scale = pool_scale[0].reshape(1, POOL_W)
    gnorm = mlstm_norm_w[0].reshape(1, MLSTM_W)
    bias = jnp.concatenate([b_igate[0], b_fgate[0]]).astype(F32)

    xp = x_prompt.reshape(bp * tp, D_MODEL)
    xs = x_sample.reshape(bs * ts, D_MODEL)

    u_m, qkvo_m, g_m = _inproj(meta_tokens, nmix, w_main, w_g8, tm=N_META, tn=1024)
    u_p, qkvo_p, g_p = _inproj(xp, nmix, w_main, w_g8, tm=1024, tn=1024)
    u_s, qkvo_s, g_s = _inproj(xs, nmix, w_main, w_g8, tm=1024, tn=1024)

    prev_p = u_m[1:N_META].reshape(1, POOL_HIST, POOL_W)
    yp_p, pool_p = _pool(u_p.reshape(bp, tp, POOL_W), prev_p, wp, scale, bb=1, tt=512, pos0=N_META, shared_prev=True)
    yp_s, pool_s = _pool(u_s.reshape(bs, ts, POOL_W), state_pool[0], wp, scale, bb=16, tt=ts, pos0=PAST_LEN, shared_prev=False)

    zc = jnp.zeros((1, N_HEADS, HEAD_DIM, HEAD_DIM), F32)
    zn = jnp.zeros((1, N_HEADS, 1, HEAD_DIM), F32)
    zm = jnp.zeros((1, N_HEADS, 1, 1), F32)
    _, c_m, n_m, m_m = _mlstm_seq(qkvo_m, g_m, bias, gnorm, zc, zn, zm, batch=1, seq=N_META, chunk=N_META, shared_init=False)
    ym_p, c_p, n_p, m_p = _mlstm_seq(qkvo_p, g_p, bias, gnorm, c_m, n_m, m_m, batch=bp, seq=tp, chunk=256, shared_init=True)
    ym_s, c_s, n_s, m_s = _mlstm_step(
        qkvo_s, g_s, bias, gnorm, state_mlstm_C[0],
        state_mlstm_n[0].reshape(bs, N_HEADS, 1, HEAD_DIM), state_mlstm_m[0].reshape(bs, N_HEADS, 1, 1),
        batch=bs, seq=ts, bb=16)

    x2_p, xn2_p = _outproj(xp, yp_p.reshape(bp * tp, POOL_W), ym_p, wo, nffn, tm=512)
    x2_s, xn2_s = _outproj(xs, yp_s.reshape(bs * ts, POOL_W), ym_s, wo, nffn, tm=512)
    y_p = _ffn(xn2_p, x2_p, wg, wu, wd, nfin, tm=512, tf=512)
    y_s = _ffn(xn2_s, x2_s, wg, wu, wd, nfin, tm=512, tf=512)

    return (
        y_p.reshape(bp, tp, D_MODEL),
        y_s.reshape(bs, ts, D_MODEL),
        pool_p[None],
        c_p[None],
        n_p.reshape(1, bp, N_HEADS, HEAD_DIM),
        m_p.reshape(1, bp, N_HEADS),
        pool_s[None],
        c_s[None],
        n_s.reshape(1, bs, N_HEADS, HEAD_DIM),
        m_s.reshape(1, bs, N_HEADS),
    )
```

```python
import functools

import jax
import jax.numpy as jnp
from jax import lax
from jax.experimental import pallas as pl
from jax.experimental.pallas import tpu as pltpu

D_MODEL = 2048
N_META = 16
POOL_W = 1024
MLSTM_W = 1024
POOL_WINDOWS = (2, 4, 8, 16)
N_POOL_GROUPS = 4
POOL_GW = 256
POOL_HIST = 15
N_HEADS = 4
HEAD_DIM = 256
D_FF = 5632
QKVO_W = 4 * MLSTM_W
MAIN_W = POOL_W + QKVO_W
PAST_LEN = 16384
EPS = 1e-6

LANES = 128
HIST_PAD = 16
VMEM_LIMIT = 56 * 1024 * 1024

BF16 = jnp.bfloat16
F32 = jnp.float32


def _params(sem):
    return pltpu.CompilerParams(dimension_semantics=sem, vmem_limit_bytes=VMEM_LIMIT)


def _rms(x, w):
    return x * lax.rsqrt(jnp.mean(x * x, axis=-1, keepdims=True) + EPS) * w


def _inproj_kernel(x_ref, nw_ref, w_ref, wg_ref, u_ref, qkvo_ref, g_ref, xn_ref, *, n_u_tiles):
    j = pl.program_id(1)

    @pl.when(j == 0)
    def _():
        xn = _rms(x_ref[...], nw_ref[...]).astype(BF16)
        xn_ref[...] = xn
        g_ref[...] = jnp.dot(xn, wg_ref[...], preferred_element_type=F32)

    p = jnp.dot(xn_ref[...], w_ref[...], preferred_element_type=F32)

    @pl.when(j < n_u_tiles)
    def _():
        u_ref[...] = p

    @pl.when(j >= n_u_tiles)
    def _():
        qkvo_ref[...] = p.astype(BF16)


def _inproj(x, norm_w, w_main, w_gate, *, tm, tn):
    m = x.shape[0]
    n_u = POOL_W // tn
    grid = (m // tm, MAIN_W // tn)
    return pl.pallas_call(
        functools.partial(_inproj_kernel, n_u_tiles=n_u),
        grid=grid,
        in_specs=[
            pl.BlockSpec((tm, D_MODEL), lambda i, j: (i, 0)),
            pl.BlockSpec((1, D_MODEL), lambda i, j: (0, 0)),
            pl.BlockSpec((D_MODEL, tn), lambda i, j: (0, j)),
            pl.BlockSpec((D_MODEL, LANES), lambda i, j: (0, 0)),
        ],
        out_specs=[
            pl.BlockSpec((tm, tn), lambda i, j: (i, jnp.minimum(j, n_u - 1))),
            pl.BlockSpec((tm, tn), lambda i, j: (i, jnp.maximum(j - n_u, 0))),
            pl.BlockSpec((tm, LANES), lambda i, j: (i, 0)),
        ],
        out_shape=[
            jax.ShapeDtypeStruct((m, POOL_W), F32),
            jax.ShapeDtypeStruct((m, QKVO_W), BF16),
            jax.ShapeDtypeStruct((m, LANES), F32),
        ],
        scratch_shapes=[pltpu.VMEM((tm, D_MODEL), BF16)],
        compiler_params=_params(("arbitrary", "arbitrary")),
        name="inproj",
    )(x, norm_w, w_main, w_gate)


def _pool_kernel(u_ref, prev_ref, wp_ref, sc_ref, y_ref, st_ref, e_ref, d_ref, *, bb, tt, pos0):
    ti = pl.program_id(1)
    nt = pl.num_programs(1)

    @pl.when(ti == 0)
    def _():
        e_ref[:, 0:1, :] = jnp.zeros((bb, 1, POOL_W), F32)
        e_ref[:, 1:HIST_PAD, :] = prev_ref[...]

    e_ref[:, HIST_PAD:, :] = u_ref[...]

    pos = pos0 + ti * tt + lax.broadcasted_iota(jnp.int32, (tt, 1), 0)
    for b in range(bb):
        for g, w in enumerate(POOL_WINDOWS):
            cols = slice(g * POOL_GW, (g + 1) * POOL_GW)
            e = e_ref[b, :, cols]
            s = e + pltpu.roll(e, 1, axis=0)
            for k in range(1, g + 1):
                s = s + pltpu.roll(s, 2**k, axis=0)
            cnt = jnp.minimum(w, pos + 1).astype(F32)
            d_ref[b * tt:(b + 1) * tt, cols] = s[HIST_PAD:, :] / cnt - e[HIST_PAD:, :]

    for g in range(N_POOL_GROUPS):
        cols = slice(g * POOL_GW, (g + 1) * POOL_GW)
        y = jnp.dot(d_ref[:, cols].astype(BF16), wp_ref[g], preferred_element_type=F32) * sc_ref[:, cols]
        y_ref[:, :, cols] = y.reshape(bb, tt, POOL_GW).astype(BF16)

    @pl.when(ti == nt - 1)
    def _():
        st_ref[...] = e_ref[:, tt + 1:tt + HIST_PAD, :]

    e_ref[:, 0:HIST_PAD, :] = e_ref[:, tt:tt + HIST_PAD, :]


def _pool(u, prev, w_pool, scale, *, bb, tt, pos0, shared_prev):
    b, t, _ = u.shape
    prev_map = (lambda bi, ti: (0, 0, 0)) if shared_prev else (lambda bi, ti: (bi, 0, 0))
    return pl.pallas_call(
        functools.partial(_pool_kernel, bb=bb, tt=tt, pos0=pos0),
        grid=(b // bb, t // tt),
        in_specs=[
            pl.BlockSpec((bb, tt, POOL_W), lambda bi, ti: (bi, ti, 0)),
            pl.BlockSpec((bb, POOL_HIST, POOL_W), prev_map),
            pl.BlockSpec((N_POOL_GROUPS, POOL_GW, POOL_GW), lambda bi, ti: (0, 0, 0)),
            pl.BlockSpec((1, POOL_W), lambda bi, ti: (0, 0)),
        ],
        out_specs=[
            pl.BlockSpec((bb, tt, POOL_W), lambda bi, ti: (bi, ti, 0)),
            pl.BlockSpec((bb, POOL_HIST, POOL_W), lambda bi, ti: (bi, 0, 0)),
        ],
        out_shape=[
            jax.ShapeDtypeStruct((b, t, POOL_W), BF16),
            jax.ShapeDtypeStruct((b, POOL_HIST, POOL_W), F32),
        ],
        scratch_shapes=[
            pltpu.VMEM((bb, HIST_PAD + tt, POOL_W), F32),
            pltpu.VMEM((bb * tt, POOL_W), F32),
        ],
        compiler_params=_params(("arbitrary", "arbitrary")),
        name="pool",
    )(u, prev, w_pool, scale)


def _mlstm_chunk(q, k, v, o, gi, gf, c, n, m, gnorm, *, chunk):
    ti = lax.broadcasted_iota(jnp.int32, (chunk, chunk), 0)
    si = lax.broadcasted_iota(jnp.int32, (chunk, chunk), 1)
    tril = si <= ti
    eye = si == ti

    def to_row(col):
        return jnp.sum(jnp.where(eye, col, 0.0), axis=0, keepdims=True)

    lf = jax.nn.log_sigmoid(gf)
    b = jnp.sum(jnp.where(tril, to_row(lf), 0.0), axis=1, keepdims=True)
    a = gi - b
    a_row = to_row(a)
    cummax_a = jnp.max(jnp.where(tril, a_row, -jnp.inf), axis=1, keepdims=True)
    m_t = jnp.maximum(m + b, cummax_a + b)
    inter = jnp.exp(m + b - m_t)
    dmat = jnp.exp(jnp.where(tril, a_row + (b - m_t), -jnp.inf))

    k = k * (HEAD_DIM ** -0.5)
    s = lax.dot_general(q, k, (((1,), (1,)), ((), ())), preferred_element_type=F32) * dmat
    num = jnp.dot(s.astype(BF16), v, preferred_element_type=F32)
    num = num + inter * jnp.dot(q, c.astype(BF16), preferred_element_type=F32)
    qn = jnp.sum(s, axis=1, keepdims=True) + inter * jnp.sum(q.astype(F32) * n, axis=1, keepdims=True)
    h = num / jnp.maximum(jnp.abs(qn), jnp.exp(-m_t))

    b_last = b[chunk - 1:chunk, :]
    m_new = m_t[chunk - 1:chunk, :]
    decay = jnp.exp(m + b_last - m_new)
    w = jnp.exp(a + (b_last - m_new))
    wv = (w * v.astype(F32)).astype(BF16)
    c_new = decay * c + lax.dot_general(k, wv, (((0,), (0,)), ((), ())), preferred_element_type=F32)
    n_new = decay * n + jnp.sum(w * k.astype(F32), axis=0, keepdims=True)

    y = jax.nn.sigmoid(o.astype(F32)) * _rms(h, gnorm)
    return y, c_new, n_new, m_new


def _gate_cols(g, bias_ref, head):
    lane = lax.broadcasted_iota(jnp.int32, g.shape, 1)
    gi = jnp.sum(jnp.where(lane == head, g, 0.0), axis=1, keepdims=True) + bias_ref[head]
    gf = jnp.sum(jnp.where(lane == head + N_HEADS, g, 0.0), axis=1, keepdims=True) + bias_ref[head + N_HEADS]
    return gi, gf


def _mlstm_seq_kernel(bias_ref, q_ref, k_ref, v_ref, o_ref, g_ref, gn_ref, c0_ref, n0_ref, m0_ref,
                      y_ref, c_ref, n_ref, m_ref, *, chunk):
    ci = pl.program_id(1)

    @pl.when(ci == 0)
    def _():
        c_ref[...] = c0_ref[...]
        n_ref[...] = n0_ref[...]
        m_ref[...] = m0_ref[...]

    g = g_ref[...]
    for head in range(N_HEADS):
        cols = slice(head * HEAD_DIM, (head + 1) * HEAD_DIM)
        gi, gf = _gate_cols(g, bias_ref, head)
        y, c_new, n_new, m_new = _mlstm_chunk(
            q_ref[:, cols], k_ref[:, cols], v_ref[:, cols], o_ref[:, cols], gi, gf,
            c_ref[0, head], n_ref[0, head], m_ref[0, head], gn_ref[:, cols], chunk=chunk)
        y_ref[:, cols] = y.astype(BF16)
        c_ref[0, head] = c_new
        n_ref[0, head] = n_new
        m_ref[0, head] = m_new


def _mlstm_seq(qkvo, gates, bias, gnorm, c0, n0, m0, *, batch, seq, chunk, shared_init):
    nc = seq // chunk
    rows = lambda bi, ci: bi * nc + ci
    st_b = (lambda bi: 0) if shared_init else (lambda bi: bi)

    def col_spec(group):
        return pl.BlockSpec((chunk, MLSTM_W), lambda bi, ci: (rows(bi, ci), group))

    def st_spec(shape, in_b):
        nd = len(shape)
        return pl.BlockSpec((1, N_HEADS) + shape, lambda bi, ci: (in_b(bi), 0) + (0,) * nd)

    ident = lambda bi: bi
    return pl.pallas_call(
        functools.partial(_mlstm_seq_kernel, chunk=chunk),
        grid=(batch, nc),
        in_specs=[
            pl.BlockSpec(memory_space=pltpu.SMEM),
            col_spec(0), col_spec(1), col_spec(2), col_spec(3),
            pl.BlockSpec((chunk, LANES), lambda bi, ci: (rows(bi, ci), 0)),
            pl.BlockSpec((1, MLSTM_W), lambda bi, ci: (0, 0)),
            st_spec((HEAD_DIM, HEAD_DIM), st_b), st_spec((1, HEAD_DIM), st_b), st_spec((1, 1), st_b),
        ],
        out_specs=[
            pl.BlockSpec((chunk, MLSTM_W), lambda bi, ci: (rows(bi, ci), 0)),
            st_spec((HEAD_DIM, HEAD_DIM), ident), st_spec((1, HEAD_DIM), ident), st_spec((1, 1), ident),
        ],
        out_shape=[
            jax.ShapeDtypeStruct((batch * seq, MLSTM_W), BF16),
            jax.ShapeDtypeStruct((batch, N_HEADS, HEAD_DIM, HEAD_DIM), F32),
            jax.ShapeDtypeStruct((batch, N_HEADS, 1, HEAD_DIM), F32),
            jax.ShapeDtypeStruct((batch, N_HEADS, 1, 1), F32),
        ],
        compiler_params=_params(("arbitrary", "arbitrary")),
        name="mlstm_seq",
    )(bias, qkvo, qkvo, qkvo, qkvo, gates, gnorm, c0, n0, m0)


def _mlstm_step_kernel(bias_ref, q_ref, k_ref, v_ref, o_ref, g_ref, gn_ref, c0_ref, n0_ref, m0_ref,
                       y_ref, c_ref, n_ref, m_ref, *, bb, seq):
    head = pl.program_id(1)
    rows = bb * seq
    ti = lax.broadcasted_iota(jnp.int32, (rows, rows), 0)
    si = lax.broadcasted_iota(jnp.int32, (rows, rows), 1)
    same = (ti // seq) == (si // seq)
    tril = same & (si <= ti)
    eye = si == ti

    def to_row(col):
        return jnp.sum(jnp.where(eye, col, 0.0), axis=0, keepdims=True)

    gi, gf = _gate_cols(g_ref[...], bias_ref, head)
    lane = lax.broadcasted_iota(jnp.int32, m0_ref.shape, 1)
    m0 = jnp.sum(jnp.where(lane == head, m0_ref[...], 0.0), axis=1, keepdims=True)

    lf_row = to_row(jax.nn.log_sigmoid(gf))
    b = jnp.sum(jnp.where(tril, lf_row, 0.0), axis=1, keepdims=True)
    b_last = jnp.sum(jnp.where(same, lf_row, 0.0), axis=1, keepdims=True)
    a = gi - b
    a_row = to_row(a)
    cummax_a = jnp.max(jnp.where(tril, a_row, -jnp.inf), axis=1, keepdims=True)
    seqmax_a = jnp.max(jnp.where(same, a_row, -jnp.inf), axis=1, keepdims=True)
    m_t = jnp.maximum(m0 + b, cummax_a + b)
    m_new = jnp.maximum(m0 + b_last, seqmax_a + b_last)
    inter = jnp.exp(m0 + b - m_t)
    dmat = jnp.exp(jnp.where(tril, a_row + (b - m_t), -jnp.inf))
    decay = jnp.exp(m0 + b_last - m_new)
    w = jnp.exp(a + (b_last - m_new))

    q = q_ref[...]
    k = k_ref[...] * (HEAD_DIM ** -0.5)
    v = v_ref[...]
    s = lax.dot_general(q, k, (((1,), (1,)), ((), ())), preferred_element_type=F32) * dmat
    sv = jnp.dot(s.astype(BF16), v, preferred_element_type=F32)

    qf = q.astype(F32)
    kf = k.astype(F32)
    wk = w * kf
    wv = w * v.astype(F32)
    qc, qdn = [], []
    for i in range(bb):
        r = slice(i * seq, (i + 1) * seq)
        c_i = c0_ref[i, 0]
        n_i = n0_ref[i:i + 1, :]
        d_i = decay[i * seq:i * seq + 1, :]
        qc.append(jnp.dot(qf[r].astype(BF16), c_i.astype(BF16), preferred_element_type=F32))
        qdn.append(jnp.sum(qf[r] * n_i, axis=1, keepdims=True))
        upd = lax.dot_general(kf[r].astype(BF16), wv[r].astype(BF16), (((0,), (0,)), ((), ())),
                              preferred_element_type=F32)
        c_ref[i, 0] = d_i * c_i + upd
        n_ref[i:i + 1, :] = d_i * n_i + jnp.sum(wk[r], axis=0, keepdims=True)

    num = sv + inter * jnp.concatenate(qc, axis=0)
    qn = jnp.sum(s, axis=1, keepdims=True) + inter * jnp.concatenate(qdn, axis=0)
    h = num / jnp.maximum(jnp.abs(qn), jnp.exp(-m_t))
    y_ref[...] = (jax.nn.sigmoid(o_ref[...].astype(F32)) * _rms(h, gn_ref[...])).astype(BF16)
    m_ref[0, 0] = to_row(m_new)


def _mlstm_step(qkvo, gates, bias, gnorm, c0, n0, m0, *, batch, seq, bb):
    rows = bb * seq

    def col_spec(off):
        return pl.BlockSpec((rows, HEAD_DIM), lambda bi, h: (bi, off + h))

    c_spec = pl.BlockSpec((bb, 1, HEAD_DIM, HEAD_DIM), lambda bi, h: (bi, h, 0, 0))
    n_spec = pl.BlockSpec((bb, HEAD_DIM), lambda bi, h: (bi, h))
    return pl.pallas_call(
        functools.partial(_mlstm_step_kernel, bb=bb, seq=seq),
        grid=(batch // bb, N_HEADS),
        in_specs=[
            pl.BlockSpec(memory_space=pltpu.SMEM),
            col_spec(0), col_spec(N_HEADS), col_spec(2 * N_HEADS), col_spec(3 * N_HEADS),
            pl.BlockSpec((rows, LANES), lambda bi, h: (bi, 0)),
            pl.BlockSpec((1, HEAD_DIM), lambda bi, h: (0, h)),
            c_spec, n_spec,
            pl.BlockSpec((rows, N_HEADS), lambda bi, h: (bi, 0)),
        ],
        out_specs=[
            pl.BlockSpec((rows, HEAD_DIM), lambda bi, h: (bi, h)),
            c_spec, n_spec,
            pl.BlockSpec((1, 1, 1, rows), lambda bi, h: (bi, h, 0, 0)),
        ],
        out_shape=[
            jax.ShapeDtypeStruct((batch * seq, MLSTM_W), BF16),
            jax.ShapeDtypeStruct((batch, N_HEADS, HEAD_DIM, HEAD_DIM), F32),
            jax.ShapeDtypeStruct((batch, MLSTM_W), F32),
            jax.ShapeDtypeStruct((batch // bb, N_HEADS, 1, rows), F32),
        ],
        compiler_params=_params(("arbitrary", "arbitrary")),
        name="mlstm_step",
    )(bias, qkvo, qkvo, qkvo, qkvo, gates, gnorm, c0, n0, m0)


def _outproj_kernel(x_ref, yp_ref, ym_ref, wo_ref, nw_ref, x2_ref, xn2_ref):
    x2 = x_ref[...] + jnp.dot(yp_ref[...], wo_ref[0:POOL_W, :], preferred_element_type=F32)
    x2 = x2 + jnp.dot(ym_ref[...], wo_ref[POOL_W:, :], preferred_element_type=F32)
    x2_ref[...] = x2
    xn2_ref[...] = _rms(x2, nw_ref[...]).astype(BF16)


def _outproj(x, y_pool, y_ml, w_out, norm_w, *, tm):
    m = x.shape[0]
    row = lambda w: pl.BlockSpec((tm, w), lambda i: (i, 0))
    return pl.pallas_call(
        _outproj_kernel,
        grid=(m // tm,),
        in_specs=[
            row(D_MODEL), row(POOL_W), row(MLSTM_W),
            pl.BlockSpec((D_MODEL, D_MODEL), lambda i: (0, 0)),
            pl.BlockSpec((1, D_MODEL), lambda i: (0, 0)),
        ],
        out_specs=[row(D_MODEL), row(D_MODEL)],
        out_shape=[
            jax.ShapeDtypeStruct((m, D_MODEL), F32),
            jax.ShapeDtypeStruct((m, D_MODEL), BF16),
        ],
        compiler_params=_params(("arbitrary",)),
        name="outproj",
    )(x, y_pool, y_ml, w_out, norm_w)


def _ffn_kernel(xn_ref, x2_ref, wg_ref, wu_ref, wd_ref, nw_ref, y_ref, acc_ref):
    f = pl.program_id(1)
    nf = pl.num_programs(1)

    @pl.when(f == 0)
    def _():
        acc_ref[...] = x2_ref[...]

    xn = xn_ref[...]
    g = jnp.dot(xn, wg_ref[...], preferred_element_type=F32)
    u = jnp.dot(xn, wu_ref[...], preferred_element_type=F32)
    h = (jax.nn.silu(g) * u).astype(BF16)
    acc_ref[...] += jnp.dot(h, wd_ref[...], preferred_element_type=F32)

    @pl.when(f == nf - 1)
    def _():
        y_ref[...] = _rms(acc_ref[...], nw_ref[...])


def _ffn(xn2, x2, w_gate, w_up, w_down, norm_w, *, tm, tf):
    m = xn2.shape[0]
    return pl.pallas_call(
        _ffn_kernel,
        grid=(m // tm, D_FF // tf),
        in_specs=[
            pl.BlockSpec((tm, D_MODEL), lambda i, f: (i, 0)),
            pl.BlockSpec((tm, D_MODEL), lambda i, f: (i, 0)),
            pl.BlockSpec((D_MODEL, tf), lambda i, f: (0, f)),
            pl.BlockSpec((D_MODEL, tf), lambda i, f: (0, f)),
            pl.BlockSpec((tf, D_MODEL), lambda i, f: (f, 0)),
            pl.BlockSpec((1, D_MODEL), lambda i, f: (0, 0)),
        ],
        out_specs=pl.BlockSpec((tm, D_MODEL), lambda i, f: (i, 0)),
        out_shape=jax.ShapeDtypeStruct((m, D_MODEL), F32),
        scratch_shapes=[pltpu.VMEM((tm, D_MODEL), F32)],
        compiler_params=_params(("arbitrary", "arbitrary")),
        name="ffn",
    )(xn2, x2, w_gate, w_up, w_down, norm_w)


def kernel(x_prompt, x_sample, state_pool, state_mlstm_C, state_mlstm_n, state_mlstm_m, meta_tokens, norm_mix_w, w_in, b_igate, b_fgate, w_pool, pool_scale, mlstm_norm_w, w_out, norm_ffn_w, w_gate, w_up, w_down, norm_final_w):
    bp, tp, _ = x_prompt.shape
    bs, ts, _ = x_sample.shape

    w_main = w_in[0, :, :MAIN_W].astype(BF16)
    w_g8 = jnp.pad(w_in[0, :, MAIN_W:], ((0, 0), (0, LANES - 2 * N_HEADS))).astype(BF16)
    wp = w_pool[0].astype(BF16)
    wo = w_out[0].astype(BF16)
    wg = w_gate[0].astype(BF16)
    wu = w_up[0].astype(BF16)
    wd = w_down[0].astype(BF16)
    nmix = norm_mix_w[0].reshape(1, D_MODEL)
    nffn = norm_ffn_w[0].reshape(1, D_MODEL)
    nfin = norm_final_w.reshape(1, D_MODEL)
    scale = pool_scale[0].reshape(1, POOL_W)
    gnorm = mlstm_norm_w[0].reshape(1, MLSTM_W)
    bias = jnp.concatenate([b_igate[0], b_fgate[0]]).astype(F32)

    xp = x_prompt.reshape(bp * tp, D_MODEL)
    xs = x_sample.reshape(bs * ts, D_MODEL)

    u_m, qkvo_m, g_m = _inproj(meta_tokens, nmix, w_main, w_g8, tm=N_META, tn=1024)
    u_p, qkvo_p, g_p = _inproj(xp, nmix, w_main, w_g8, tm=1024, tn=1024)
    u_s, qkvo_s, g_s = _inproj(xs, nmix, w_main, w_g8, tm=1024, tn=1024)

    prev_p = u_m[1:N_META].reshape(1, POOL_HIST, POOL_W)
    yp_p, pool_p = _pool(u_p.reshape(bp, tp, POOL_W), prev_p, wp, scale, bb=1, tt=512, pos0=N_META, shared_prev=True)
    yp_s, pool_s = _pool(u_s.reshape(bs, ts, POOL_W), state_pool[0], wp, scale, bb=16, tt=ts, pos0=PAST_LEN, shared_prev=False)

    zc = jnp.zeros((1, N_HEADS, HEAD_DIM, HEAD_DIM), F32)
    zn = jnp.zeros((1, N_HEADS, 1, HEAD_DIM), F32)
    zm = jnp.zeros((1, N_HEADS, 1, 1), F32)
    _, c_m, n_m, m_m = _mlstm_seq(qkvo_m, g_m, bias, gnorm, zc, zn, zm, batch=1, seq=N_META, chunk=N_META, shared_init=False)
    ym_p, c_p, n_p, m_p = _mlstm_seq(qkvo_p, g_p, bias, gnorm, c_m, n_m, m_m, batch=bp, seq=tp, chunk=256, shared_init=True)
    step_bb = 16
    ym_s, c_s, n_s, m_s = _mlstm_step(
        qkvo_s, g_s, bias, gnorm, state_mlstm_C[0],
        state_mlstm_n[0].reshape(bs, MLSTM_W), jnp.repeat(state_mlstm_m[0], ts, axis=0),
        batch=bs, seq=ts, bb=step_bb)
    m_s = m_s.reshape(bs // step_bb, N_HEADS, step_bb, ts)[..., 0].transpose(0, 2, 1)

    x2_p, xn2_p = _outproj(xp, yp_p.reshape(bp * tp, POOL_W), ym_p, wo, nffn, tm=512)
    x2_s, xn2_s = _outproj(xs, yp_s.reshape(bs * ts, POOL_W), ym_s, wo, nffn, tm=512)
    y_p = _ffn(xn2_p, x2_p, wg, wu, wd, nfin, tm=512, tf=512)
    y_s = _ffn(xn2_s, x2_s, wg, wu, wd, nfin, tm=512, tf=512)

    return (
        y_p.reshape(bp, tp, D_MODEL),
        y_s.reshape(bs, ts, D_MODEL),
        pool_p[None],
        c_p[None],
        n_p.reshape(1, bp, N_HEADS, HEAD_DIM),
        m_p.reshape(1, bp, N_HEADS),
        pool_s[None],
        c_s[None],
        n_s.reshape(1, bs, N_HEADS, HEAD_DIM),
        m_s.reshape(1, bs, N_HEADS),
    )
```

```python
import functools

import jax
import jax.numpy as jnp
from jax import lax
from jax.experimental import pallas as pl
from jax.experimental.pallas import tpu as pltpu

D_MODEL = 2048
N_META = 16
POOL_W = 1024
MLSTM_W = 1024
POOL_WINDOWS = (2, 4, 8, 16)
N_POOL_GROUPS = 4
POOL_GW = 256
POOL_HIST = 15
N_HEADS = 4
HEAD_DIM = 256
D_FF = 5632
QKVO_W = 4 * MLSTM_W
MAIN_W = POOL_W + QKVO_W
PAST_LEN = 16384
EPS = 1e-6

LANES = 128
HIST_PAD = 16
VMEM_LIMIT = 56 * 1024 * 1024

BF16 = jnp.bfloat16
F32 = jnp.float32


def _params(sem):
    return pltpu.CompilerParams(dimension_semantics=sem, vmem_limit_bytes=VMEM_LIMIT)


def _rms(x, w):
    return x * lax.rsqrt(jnp.mean(x * x, axis=-1, keepdims=True) + EPS) * w


def _inproj_kernel(x_ref, nw_ref, w_ref, wg_ref, u_ref, qkvo_ref, g_ref, *rest, n_u_tiles, emit_copy):
    xn_ref = rest[-1]
    j = pl.program_id(1)

    @pl.when(j == 0)
    def _():
        xn = _rms(x_ref[...], nw_ref[...]).astype(BF16)
        xn_ref[...] = xn
        g_ref[...] = jnp.dot(xn, wg_ref[...], preferred_element_type=F32)

    w = w_ref[...]
    if emit_copy:
        w = w.astype(BF16)
        rest[0][...] = w
    p = jnp.dot(xn_ref[...], w, preferred_element_type=F32)

    @pl.when(j < n_u_tiles)
    def _():
        u_ref[...] = p

    @pl.when(j >= n_u_tiles)
    def _():
        qkvo_ref[...] = p.astype(BF16)


def _inproj(x, norm_w, w, w_gate, *, tm, tn):
    m = x.shape[0]
    n_u = POOL_W // tn
    emit_copy = w.ndim == 3
    if emit_copy:
        assert m == tm, "the bf16 copy is written once per column tile"
        w_spec = pl.BlockSpec((None, D_MODEL, tn), lambda i, j: (0, 0, j))
    else:
        w_spec = pl.BlockSpec((D_MODEL, tn), lambda i, j: (0, j))
    copy_spec = [pl.BlockSpec((D_MODEL, tn), lambda i, j: (0, j))] if emit_copy else []
    copy_shape = [jax.ShapeDtypeStruct((D_MODEL, MAIN_W), BF16)] if emit_copy else []
    return pl.pallas_call(
        functools.partial(_inproj_kernel, n_u_tiles=n_u, emit_copy=emit_copy),
        grid=(m // tm, MAIN_W // tn),
        in_specs=[
            pl.BlockSpec((tm, D_MODEL), lambda i, j: (i, 0)),
            pl.BlockSpec((1, D_MODEL), lambda i, j: (0, 0)),
            w_spec,
            pl.BlockSpec((D_MODEL, LANES), lambda i, j: (0, 0)),
        ],
        out_specs=[
            pl.BlockSpec((tm, tn), lambda i, j: (i, jnp.minimum(j, n_u - 1))),
            pl.BlockSpec((tm, tn), lambda i, j: (i, jnp.maximum(j - n_u, 0))),
            pl.BlockSpec((tm, LANES), lambda i, j: (i, 0)),
        ] + copy_spec,
        out_shape=[
            jax.ShapeDtypeStruct((m, POOL_W), F32),
            jax.ShapeDtypeStruct((m, QKVO_W), BF16),
            jax.ShapeDtypeStruct((m, LANES), F32),
        ] + copy_shape,
        scratch_shapes=[pltpu.VMEM((tm, D_MODEL), BF16)],
        compiler_params=_params(("arbitrary", "arbitrary")),
        name="inproj",
    )(x, norm_w, w, w_gate)


def _convert_specs(weights, n_steps, step_of):
    specs, shapes = [], []
    for w, n_blocks in weights:
        assert n_blocks <= n_steps
        rows = w.shape[0] // n_blocks
        specs.append(pl.BlockSpec(
            (rows, w.shape[1]), lambda *idx, n_blocks=n_blocks: (jnp.minimum(step_of(*idx), n_blocks - 1), 0)))
        shapes.append(jax.ShapeDtypeStruct(w.shape, BF16))
    return specs, shapes


def _convert_blocks(step, srcs, dsts, n_blocks):
    for src, dst, n in zip(srcs, dsts, n_blocks):
        @pl.when(step < n)
        def _(src=src, dst=dst):
            dst[...] = src[...].astype(BF16)


def _pool_kernel(u_ref, prev_ref, wp_ref, sc_ref, y_ref, st_ref, e_ref, d_ref, *, bb, tt, pos0):
    ti = pl.program_id(1)
    nt = pl.num_programs(1)

    @pl.when(ti == 0)
    def _():
        e_ref[:, 0:1, :] = jnp.zeros((bb, 1, POOL_W), F32)
        e_ref[:, 1:HIST_PAD, :] = prev_ref[...]

    e_ref[:, HIST_PAD:, :] = u_ref[...]

    pos = pos0 + ti * tt + lax.broadcasted_iota(jnp.int32, (tt, 1), 0)
    for b in range(bb):
        for g, w in enumerate(POOL_WINDOWS):
            cols = slice(g * POOL_GW, (g + 1) * POOL_GW)
            e = e_ref[b, :, cols]
            s = e + pltpu.roll(e, 1, axis=0)
            for k in range(1, g + 1):
                s = s + pltpu.roll(s, 2**k, axis=0)
            cnt = jnp.minimum(w, pos + 1).astype(F32)
            d_ref[b * tt:(b + 1) * tt, cols] = s[HIST_PAD:, :] / cnt - e[HIST_PAD:, :]

    for g in range(N_POOL_GROUPS):
        cols = slice(g * POOL_GW, (g + 1) * POOL_GW)
        y = jnp.dot(d_ref[:, cols].astype(BF16), wp_ref[g], preferred_element_type=F32) * sc_ref[:, cols]
        y_ref[:, :, cols] = y.reshape(bb, tt, POOL_GW).astype(BF16)

    @pl.when(ti == nt - 1)
    def _():
        st_ref[...] = e_ref[:, tt + 1:tt + HIST_PAD, :]

    e_ref[:, 0:HIST_PAD, :] = e_ref[:, tt:tt + HIST_PAD, :]


def _pool(u, prev, w_pool, scale, *, bb, tt, pos0, shared_prev):
    b, t, _ = u.shape
    prev_map = (lambda bi, ti: (0, 0, 0)) if shared_prev else (lambda bi, ti: (bi, 0, 0))
    return pl.pallas_call(
        functools.partial(_pool_kernel, bb=bb, tt=tt, pos0=pos0),
        grid=(b // bb, t // tt),
        in_specs=[
            pl.BlockSpec((bb, tt, POOL_W), lambda bi, ti: (bi, ti, 0)),
            pl.BlockSpec((bb, POOL_HIST, POOL_W), prev_map),
            pl.BlockSpec((N_POOL_GROUPS, POOL_GW, POOL_GW), lambda bi, ti: (0, 0, 0)),
            pl.BlockSpec((1, POOL_W), lambda bi, ti: (0, 0)),
        ],
        out_specs=[
            pl.BlockSpec((bb, tt, POOL_W), lambda bi, ti: (bi, ti, 0)),
            pl.BlockSpec((bb, POOL_HIST, POOL_W), lambda bi, ti: (bi, 0, 0)),
        ],
        out_shape=[
            jax.ShapeDtypeStruct((b, t, POOL_W), BF16),
            jax.ShapeDtypeStruct((b, POOL_HIST, POOL_W), F32),
        ],
        scratch_shapes=[
            pltpu.VMEM((bb, HIST_PAD + tt, POOL_W), F32),
            pltpu.VMEM((bb * tt, POOL_W), F32),
        ],
        compiler_params=_params(("arbitrary", "arbitrary")),
        name="pool",
    )(u, prev, w_pool, scale)


def _mlstm_chunk(q, k, v, o, gi, gf, c, n, m, gnorm, *, chunk):
    ti = lax.broadcasted_iota(jnp.int32, (chunk, chunk), 0)
    si = lax.broadcasted_iota(jnp.int32, (chunk, chunk), 1)
    tril = si <= ti
    eye = si == ti

    def to_row(col):
        return jnp.sum(jnp.where(eye, col, 0.0), axis=0, keepdims=True)

    lf = jax.nn.log_sigmoid(gf)
    b = jnp.sum(jnp.where(tril, to_row(lf), 0.0), axis=1, keepdims=True)
    a = gi - b
    a_row = to_row(a)
    cummax_a = jnp.max(jnp.where(tril, a_row, -jnp.inf), axis=1, keepdims=True)
    m_t = jnp.maximum(m + b, cummax_a + b)
    inter = jnp.exp(m + b - m_t)
    dmat = jnp.exp(jnp.where(tril, a_row + (b - m_t), -jnp.inf))

    k = k * (HEAD_DIM ** -0.5)
    s = lax.dot_general(q, k, (((1,), (1,)), ((), ())), preferred_element_type=F32) * dmat
    num = jnp.dot(s.astype(BF16), v, preferred_element_type=F32)
    num = num + inter * jnp.dot(q, c.astype(BF16), preferred_element_type=F32)
    qn = jnp.sum(s, axis=1, keepdims=True) + inter * jnp.sum(q.astype(F32) * n, axis=1, keepdims=True)
    h = num / jnp.maximum(jnp.abs(qn), jnp.exp(-m_t))

    b_last = b[chunk - 1:chunk, :]
    m_new = m_t[chunk - 1:chunk, :]
    decay = jnp.exp(m + b_last - m_new)
    w = jnp.exp(a + (b_last - m_new))
    wv = (w * v.astype(F32)).astype(BF16)
    c_new = decay * c + lax.dot_general(k, wv, (((0,), (0,)), ((), ())), preferred_element_type=F32)
    n_new = decay * n + jnp.sum(w * k.astype(F32), axis=0, keepdims=True)

    y = jax.nn.sigmoid(o.astype(F32)) * _rms(h, gnorm)
    return y, c_new, n_new, m_new


def _gate_cols(g, bias_ref, head):
    lane = lax.broadcasted_iota(jnp.int32, g.shape, 1)
    gi = jnp.sum(jnp.where(lane == head, g, 0.0), axis=1, keepdims=True) + bias_ref[head]
    gf = jnp.sum(jnp.where(lane == head + N_HEADS, g, 0.0), axis=1, keepdims=True) + bias_ref[head + N_HEADS]
    return gi, gf


def _mlstm_seq_kernel(bias_ref, q_ref, k_ref, v_ref, o_ref, g_ref, gn_ref, c0_ref, n0_ref, m0_ref, *rest,
                      chunk, conv_blocks):
    nc = len(conv_blocks)
    conv_in, (y_ref, c_ref, n_ref, m_ref), conv_out = rest[:nc], rest[nc:nc + 4], rest[nc + 4:]
    ci = pl.program_id(1)
    _convert_blocks(pl.program_id(0) * pl.num_programs(1) + ci, conv_in, conv_out, conv_blocks)

    @pl.when(ci == 0)
    def _():
        c_ref[...] = c0_ref[...]
        n_ref[...] = n0_ref[...]
        m_ref[...] = m0_ref[...]

    g = g_ref[...]
    for head in range(N_HEADS):
        cols = slice(head * HEAD_DIM, (head + 1) * HEAD_DIM)
        gi, gf = _gate_cols(g, bias_ref, head)
        y, c_new, n_new, m_new = _mlstm_chunk(
            q_ref[:, cols], k_ref[:, cols], v_ref[:, cols], o_ref[:, cols], gi, gf,
            c_ref[0, head], n_ref[0, head], m_ref[0, head], gn_ref[:, cols], chunk=chunk)
        y_ref[:, cols] = y.astype(BF16)
        c_ref[0, head] = c_new
        n_ref[0, head] = n_new
        m_ref[0, head] = m_new


def _mlstm_seq(qkvo, gates, bias, gnorm, c0, n0, m0, *, batch, seq, chunk, shared_init, convert=()):
    nc = seq // chunk
    conv_specs, conv_shapes = _convert_specs(convert, batch * nc, lambda bi, ci: bi * nc + ci)
    rows = lambda bi, ci: bi * nc + ci
    st_b = (lambda bi: 0) if shared_init else (lambda bi: bi)

    def col_spec(group):
        return pl.BlockSpec((chunk, MLSTM_W), lambda bi, ci: (rows(bi, ci), group))

    def st_spec(shape, in_b):
        nd = len(shape)
        return pl.BlockSpec((1, N_HEADS) + shape, lambda bi, ci: (in_b(bi), 0) + (0,) * nd)

    ident = lambda bi: bi
    return pl.pallas_call(
        functools.partial(_mlstm_seq_kernel, chunk=chunk, conv_blocks=tuple(n for _, n in convert)),
        grid=(batch, nc),
        in_specs=[
            pl.BlockSpec(memory_space=pltpu.SMEM),
            col_spec(0), col_spec(1), col_spec(2), col_spec(3),
            pl.BlockSpec((chunk, LANES), lambda bi, ci: (rows(bi, ci), 0)),
            pl.BlockSpec((1, MLSTM_W), lambda bi, ci: (0, 0)),
            st_spec((HEAD_DIM, HEAD_DIM), st_b), st_spec((1, HEAD_DIM), st_b), st_spec((1, 1), st_b),
        ] + conv_specs,
        out_specs=[
            pl.BlockSpec((chunk, MLSTM_W), lambda bi, ci: (rows(bi, ci), 0)),
            st_spec((HEAD_DIM, HEAD_DIM), ident), st_spec((1, HEAD_DIM), ident), st_spec((1, 1), ident),
        ] + conv_specs,
        out_shape=[
            jax.ShapeDtypeStruct((batch * seq, MLSTM_W), BF16),
            jax.ShapeDtypeStruct((batch, N_HEADS, HEAD_DIM, HEAD_DIM), F32),
            jax.ShapeDtypeStruct((batch, N_HEADS, 1, HEAD_DIM), F32),
            jax.ShapeDtypeStruct((batch, N_HEADS, 1, 1), F32),
        ] + conv_shapes,
        compiler_params=_params(("arbitrary", "arbitrary")),
        name="mlstm_seq",
    )(bias, qkvo, qkvo, qkvo, qkvo, gates, gnorm, c0, n0, m0, *[w for w, _ in convert])


def _mlstm_step_kernel(bias_ref, q_ref, k_ref, v_ref, o_ref, g_ref, gn_ref, c0_ref, n0_ref, m0_ref,
                       y_ref, c_ref, n_ref, m_ref, *, bb, seq):
    head = pl.program_id(1)
    rows = bb * seq
    ti = lax.broadcasted_iota(jnp.int32, (rows, rows), 0)
    si = lax.broadcasted_iota(jnp.int32, (rows, rows), 1)
    same = (ti // seq) == (si // seq)
    tril = same & (si <= ti)
    eye = si == ti

    def to_row(col):
        return jnp.sum(jnp.where(eye, col, 0.0), axis=0, keepdims=True)

    gi, gf = _gate_cols(g_ref[...], bias_ref, head)
    lane = lax.broadcasted_iota(jnp.int32, m0_ref.shape, 1)
    m0 = jnp.sum(jnp.where(lane == head, m0_ref[...], 0.0), axis=1, keepdims=True)

    lf_row = to_row(jax.nn.log_sigmoid(gf))
    b = jnp.sum(jnp.where(tril, lf_row, 0.0), axis=1, keepdims=True)
    b_last = jnp.sum(jnp.where(same, lf_row, 0.0), axis=1, keepdims=True)
    a = gi - b
    a_row = to_row(a)
    cummax_a = jnp.max(jnp.where(tril, a_row, -jnp.inf), axis=1, keepdims=True)
    seqmax_a = jnp.max(jnp.where(same, a_row, -jnp.inf), axis=1, keepdims=True)
    m_t = jnp.maximum(m0 + b, cummax_a + b)
    m_new = jnp.maximum(m0 + b_last, seqmax_a + b_last)
    inter = jnp.exp(m0 + b - m_t)
    dmat = jnp.exp(jnp.where(tril, a_row + (b - m_t), -jnp.inf))
    decay = jnp.exp(m0 + b_last - m_new)
    w = jnp.exp(a + (b_last - m_new))

    q = q_ref[...]
    k = k_ref[...] * (HEAD_DIM ** -0.5)
    v = v_ref[...]
    s = lax.dot_general(q, k, (((1,), (1,)), ((), ())), preferred_element_type=F32) * dmat
    sv = jnp.dot(s.astype(BF16), v, preferred_element_type=F32)

    qf = q.astype(F32)
    kf = k.astype(F32)
    wk = w * kf
    wv = w * v.astype(F32)
    qc, qdn = [], []
    for i in range(bb):
        r = slice(i * seq, (i + 1) * seq)
        c_i = c0_ref[i, 0]
        n_i = n0_ref[i:i + 1, :]
        d_i = decay[i * seq:i * seq + 1, :]
        qc.append(jnp.dot(qf[r].astype(BF16), c_i.astype(BF16), preferred_element_type=F32))
        qdn.append(jnp.sum(qf[r] * n_i, axis=1, keepdims=True))
        upd = lax.dot_general(kf[r].astype(BF16), wv[r].astype(BF16), (((0,), (0,)), ((), ())),
                              preferred_element_type=F32)
        c_ref[i, 0] = d_i * c_i + upd
        n_ref[i:i + 1, :] = d_i * n_i + jnp.sum(wk[r], axis=0, keepdims=True)

    num = sv + inter * jnp.concatenate(qc, axis=0)
    qn = jnp.sum(s, axis=1, keepdims=True) + inter * jnp.concatenate(qdn, axis=0)
    h = num / jnp.maximum(jnp.abs(qn), jnp.exp(-m_t))
    y_ref[...] = (jax.nn.sigmoid(o_ref[...].astype(F32)) * _rms(h, gn_ref[...])).astype(BF16)
    m_ref[0, 0] = to_row(m_new)


def _mlstm_step(qkvo, gates, bias, gnorm, c0, n0, m0, *, batch, seq, bb):
    rows = bb * seq

    def col_spec(off):
        return pl.BlockSpec((rows, HEAD_DIM), lambda bi, h: (bi, off + h))

    c_spec = pl.BlockSpec((bb, 1, HEAD_DIM, HEAD_DIM), lambda bi, h: (bi, h, 0, 0))
    n_spec = pl.BlockSpec((bb, HEAD_DIM), lambda bi, h: (bi, h))
    return pl.pallas_call(
        functools.partial(_mlstm_step_kernel, bb=bb, seq=seq),
        grid=(batch // bb, N_HEADS),
        in_specs=[
            pl.BlockSpec(memory_space=pltpu.SMEM),
            col_spec(0), col_spec(N_HEADS), col_spec(2 * N_HEADS), col_spec(3 * N_HEADS),
            pl.BlockSpec((rows, LANES), lambda bi, h: (bi, 0)),
            pl.BlockSpec((1, HEAD_DIM), lambda bi, h: (0, h)),
            c_spec, n_spec,
            pl.BlockSpec((rows, N_HEADS), lambda bi, h: (bi, 0)),
        ],
        out_specs=[
            pl.BlockSpec((rows, HEAD_DIM), lambda bi, h: (bi, h)),
            c_spec, n_spec,
            pl.BlockSpec((1, 1, 1, rows), lambda bi, h: (bi, h, 0, 0)),
        ],
        out_shape=[
            jax.ShapeDtypeStruct((batch * seq, MLSTM_W), BF16),
            jax.ShapeDtypeStruct((batch, N_HEADS, HEAD_DIM, HEAD_DIM), F32),
            jax.ShapeDtypeStruct((batch, MLSTM_W), F32),
            jax.ShapeDtypeStruct((batch // bb, N_HEADS, 1, rows), F32),
        ],
        compiler_params=_params(("arbitrary", "arbitrary")),
        name="mlstm_step",
    )(bias, qkvo, qkvo, qkvo, qkvo, gates, gnorm, c0, n0, m0)


def _outproj_kernel(x_ref, yp_ref, ym_ref, wo_ref, nw_ref, x2_ref, xn2_ref):
    x2 = x_ref[...] + jnp.dot(yp_ref[...], wo_ref[0:POOL_W, :], preferred_element_type=F32)
    x2 = x2 + jnp.dot(ym_ref[...], wo_ref[POOL_W:, :], preferred_element_type=F32)
    x2_ref[...] = x2
    xn2_ref[...] = _rms(x2, nw_ref[...]).astype(BF16)


def _outproj(x, y_pool, y_ml, w_out, norm_w, *, tm):
    m = x.shape[0]
    row = lambda w: pl.BlockSpec((tm, w), lambda i: (i, 0))
    return pl.pallas_call(
        _outproj_kernel,
        grid=(m // tm,),
        in_specs=[
            row(D_MODEL), row(POOL_W), row(MLSTM_W),
            pl.BlockSpec((D_MODEL, D_MODEL), lambda i: (0, 0)),
            pl.BlockSpec((1, D_MODEL), lambda i: (0, 0)),
        ],
        out_specs=[row(D_MODEL), row(D_MODEL)],
        out_shape=[
            jax.ShapeDtypeStruct((m, D_MODEL), F32),
            jax.ShapeDtypeStruct((m, D_MODEL), BF16),
        ],
        compiler_params=_params(("arbitrary",)),
        name="outproj",
    )(x, y_pool, y_ml, w_out, norm_w)


def _ffn_kernel(xn_ref, x2_ref, wg_ref, wu_ref, wd_ref, nw_ref, y_ref, acc_ref):
    f = pl.program_id(1)
    nf = pl.num_programs(1)

    @pl.when(f == 0)
    def _():
        acc_ref[...] = x2_ref[...]

    xn = xn_ref[...]
    g = jnp.dot(xn, wg_ref[...], preferred_element_type=F32)
    u = jnp.dot(xn, wu_ref[...], preferred_element_type=F32)
    h = (jax.nn.silu(g) * u).astype(BF16)
    acc_ref[...] += jnp.dot(h, wd_ref[...], preferred_element_type=F32)

    @pl.when(f == nf - 1)
    def _():
        y_ref[...] = _rms(acc_ref[...], nw_ref[...])


def _ffn(xn2, x2, w_gate, w_up, w_down, norm_w, *, tm, tf):
    m = xn2.shape[0]
    return pl.pallas_call(
        _ffn_kernel,
        grid=(m // tm, D_FF // tf),
        in_specs=[
            pl.BlockSpec((tm, D_MODEL), lambda i, f: (i, 0)),
            pl.BlockSpec((tm, D_MODEL), lambda i, f: (i, 0)),
            pl.BlockSpec((D_MODEL, tf), lambda i, f: (0, f)),
            pl.BlockSpec((D_MODEL, tf), lambda i, f: (0, f)),
            pl.BlockSpec((tf, D_MODEL), lambda i, f: (f, 0)),
            pl.BlockSpec((1, D_MODEL), lambda i, f: (0, 0)),
        ],
        out_specs=pl.BlockSpec((tm, D_MODEL), lambda i, f: (i, 0)),
        out_shape=jax.ShapeDtypeStruct((m, D_MODEL), F32),
        scratch_shapes=[pltpu.VMEM((tm, D_MODEL), F32)],
        compiler_params=_params(("arbitrary", "arbitrary")),
        name="ffn",
    )(xn2, x2, w_gate, w_up, w_down, norm_w)


def kernel(x_prompt, x_sample, state_pool, state_mlstm_C, state_mlstm_n, state_mlstm_m, meta_tokens, norm_mix_w, w_in, b_igate, b_fgate, w_pool, pool_scale, mlstm_norm_w, w_out, norm_ffn_w, w_gate, w_up, w_down, norm_final_w):
    bp, tp, _ = x_prompt.shape
    bs, ts, _ = x_sample.shape

    w_g8 = jnp.pad(w_in[0, :, MAIN_W:], ((0, 0), (0, LANES - 2 * N_HEADS))).astype(BF16)
    wp = w_pool[0].astype(BF16)
    nmix = norm_mix_w[0].reshape(1, D_MODEL)
    nffn = norm_ffn_w[0].reshape(1, D_MODEL)
    nfin = norm_final_w.reshape(1, D_MODEL)
    scale = pool_scale[0].reshape(1, POOL_W)
    gnorm = mlstm_norm_w[0].reshape(1, MLSTM_W)
    bias = jnp.concatenate([b_igate[0], b_fgate[0]]).astype(F32)

    xp = x_prompt.reshape(bp * tp, D_MODEL)
    xs = x_sample.reshape(bs * ts, D_MODEL)

    u_m, qkvo_m, g_m, w_main = _inproj(meta_tokens, nmix, w_in, w_g8, tm=N_META, tn=512)
    u_p, qkvo_p, g_p = _inproj(xp, nmix, w_main, w_g8, tm=1024, tn=1024)
    u_s, qkvo_s, g_s = _inproj(xs, nmix, w_main, w_g8, tm=1024, tn=1024)

    prev_p = u_m[1:N_META].reshape(1, POOL_HIST, POOL_W)
    yp_p, pool_p = _pool(u_p.reshape(bp, tp, POOL_W), prev_p, wp, scale, bb=1, tt=512, pos0=N_META, shared_prev=True)
    yp_s, pool_s = _pool(u_s.reshape(bs, ts, POOL_W), state_pool[0], wp, scale, bb=16, tt=ts, pos0=PAST_LEN, shared_prev=False)

    zc = jnp.zeros((1, N_HEADS, HEAD_DIM, HEAD_DIM), F32)
    zn = jnp.zeros((1, N_HEADS, 1, HEAD_DIM), F32)
    zm = jnp.zeros((1, N_HEADS, 1, 1), F32)
    _, c_m, n_m, m_m = _mlstm_seq(qkvo_m, g_m, bias, gnorm, zc, zn, zm, batch=1, seq=N_META, chunk=N_META, shared_init=False)
    convert = ((w_out[0], 32), (w_gate[0], 32), (w_up[0], 32), (w_down[0], 32))
    ym_p, c_p, n_p, m_p, wo, wg, wu, wd = _mlstm_seq(
        qkvo_p, g_p, bias, gnorm, c_m, n_m, m_m, batch=bp, seq=tp, chunk=256, shared_init=True, convert=convert)
    step_bb = 16
    ym_s, c_s, n_s, m_s = _mlstm_step(
        qkvo_s, g_s, bias, gnorm, state_mlstm_C[0],
        state_mlstm_n[0].reshape(bs, MLSTM_W), jnp.repeat(state_mlstm_m[0], ts, axis=0),
        batch=bs, seq=ts, bb=step_bb)
    m_s = m_s.reshape(bs // step_bb, N_HEADS, step_bb, ts)[..., 0].transpose(0, 2, 1)

    x2_p, xn2_p = _outproj(xp, yp_p.reshape(bp * tp, POOL_W), ym_p, wo, nffn, tm=512)
    x2_s, xn2_s = _outproj(xs, yp_s.reshape(bs * ts, POOL_W), ym_s, wo, nffn, tm=512)
    y_p = _ffn(xn2_p, x2_p, wg, wu, wd, nfin, tm=512, tf=512)
    y_s = _ffn(xn2_s, x2_s, wg, wu, wd, nfin, tm=512, tf=512)

    return (
        y_p.reshape(bp, tp, D_MODEL),
        y_s.reshape(bs, ts, D_MODEL),
        pool_p[None],
        c_p[None],
        n_p.reshape(1, bp, N_HEADS, HEAD_DIM),
        m_p.reshape(1, bp, N_HEADS),
        pool_s[None],
        c_s[None],
        n_s.reshape(1, bs, N_HEADS, HEAD_DIM),
        m_s.reshape(1, bs, N_HEADS),
    )
```

```python
import functools

import jax
import jax.numpy as jnp
from jax import lax
from jax.experimental import pallas as pl
from jax.experimental.pallas import tpu as pltpu

D_MODEL = 2048
N_META = 16
POOL_W = 1024
MLSTM_W = 1024
POOL_WINDOWS = (2, 4, 8, 16)
N_POOL_GROUPS = 4
POOL_GW = 256
POOL_HIST = 15
N_HEADS = 4
HEAD_DIM = 256
D_FF = 5632
QKVO_W = 4 * MLSTM_W
MAIN_W = POOL_W + QKVO_W
PAST_LEN = 16384
EPS = 1e-6

LANES = 128
HIST_PAD = 16
VMEM_LIMIT = 56 * 1024 * 1024

BF16 = jnp.bfloat16
F32 = jnp.float32


def _params(sem):
    return pltpu.CompilerParams(dimension_semantics=sem, vmem_limit_bytes=VMEM_LIMIT)


def _rms(x, w):
    return x * lax.rsqrt(jnp.mean(x * x, axis=-1, keepdims=True) + EPS) * w


def _inproj_kernel(x_ref, nw_ref, w_ref, wg_ref, u_ref, qkvo_ref, g_ref, *rest, n_u_tiles, emit_copy):
    xn_ref = rest[-1]
    j = pl.program_id(1)

    nt = (((1,), (1,)), ((), ()))

    @pl.when(j == 0)
    def _():
        xn = _rms(x_ref[...], nw_ref[...]).astype(BF16)
        xn_ref[...] = xn
        g_ref[...] = lax.dot_general(xn, wg_ref[...], nt, preferred_element_type=F32)

    w = w_ref[...]
    if emit_copy:
        w = w.astype(BF16)
        rest[0][...] = w
    p = lax.dot_general(xn_ref[...], w, nt, preferred_element_type=F32)

    @pl.when(j < n_u_tiles)
    def _():
        u_ref[...] = p

    @pl.when(j >= n_u_tiles)
    def _():
        qkvo_ref[...] = p.astype(BF16)


def _inproj(x, norm_w, w_t, w_gate_t, *, tm, tn):
    m = x.shape[0]
    n_u = POOL_W // tn
    emit_copy = w_t.dtype == F32
    if emit_copy:
        assert m == tm, "the bf16 copy is written once per column tile"
    w_spec = pl.BlockSpec((tn, D_MODEL), lambda i, j: (j, 0))
    copy_spec = [w_spec] if emit_copy else []
    copy_shape = [jax.ShapeDtypeStruct((MAIN_W, D_MODEL), BF16)] if emit_copy else []
    return pl.pallas_call(
        functools.partial(_inproj_kernel, n_u_tiles=n_u, emit_copy=emit_copy),
        grid=(m // tm, MAIN_W // tn),
        in_specs=[
            pl.BlockSpec((tm, D_MODEL), lambda i, j: (i, 0)),
            pl.BlockSpec((1, D_MODEL), lambda i, j: (0, 0)),
            w_spec,
            pl.BlockSpec((LANES, D_MODEL), lambda i, j: (0, 0)),
        ],
        out_specs=[
            pl.BlockSpec((tm, tn), lambda i, j: (i, jnp.minimum(j, n_u - 1))),
            pl.BlockSpec((tm, tn), lambda i, j: (i, jnp.maximum(j - n_u, 0))),
            pl.BlockSpec((tm, LANES), lambda i, j: (i, 0)),
        ] + copy_spec,
        out_shape=[
            jax.ShapeDtypeStruct((m, POOL_W), F32),
            jax.ShapeDtypeStruct((m, QKVO_W), BF16),
            jax.ShapeDtypeStruct((m, LANES), F32),
        ] + copy_shape,
        scratch_shapes=[pltpu.VMEM((tm, D_MODEL), BF16)],
        compiler_params=_params(("arbitrary", "arbitrary")),
        name="inproj",
    )(x, norm_w, w_t, w_gate_t)


def _convert_specs(weights, n_steps, step_of):
    specs, shapes = [], []
    for w, n_blocks in weights:
        assert n_blocks <= n_steps
        rows = w.shape[0] // n_blocks
        specs.append(pl.BlockSpec(
            (rows, w.shape[1]), lambda *idx, n_blocks=n_blocks: (jnp.minimum(step_of(*idx), n_blocks - 1), 0)))
        shapes.append(jax.ShapeDtypeStruct(w.shape, BF16))
    return specs, shapes


def _convert_blocks(step, srcs, dsts, n_blocks):
    for src, dst, n in zip(srcs, dsts, n_blocks):
        @pl.when(step < n)
        def _(src=src, dst=dst):
            dst[...] = src[...].astype(BF16)


def _pool_kernel(u_ref, prev_ref, wp_ref, sc_ref, y_ref, st_ref, e_ref, d_ref, *, bb, tt, pos0):
    ti = pl.program_id(1)
    nt = pl.num_programs(1)

    @pl.when(ti == 0)
    def _():
        e_ref[:, 0:1, :] = jnp.zeros((bb, 1, POOL_W), F32)
        e_ref[:, 1:HIST_PAD, :] = prev_ref[...]

    e_ref[:, HIST_PAD:, :] = u_ref[...]

    pos = pos0 + ti * tt + lax.broadcasted_iota(jnp.int32, (tt, 1), 0)
    for b in range(bb):
        for g, w in enumerate(POOL_WINDOWS):
            cols = slice(g * POOL_GW, (g + 1) * POOL_GW)
            e = e_ref[b, :, cols]
            s = e + pltpu.roll(e, 1, axis=0)
            for k in range(1, g + 1):
                s = s + pltpu.roll(s, 2**k, axis=0)
            cnt = jnp.minimum(w, pos + 1).astype(F32)
            d_ref[b * tt:(b + 1) * tt, cols] = s[HIST_PAD:, :] / cnt - e[HIST_PAD:, :]

    for g in range(N_POOL_GROUPS):
        cols = slice(g * POOL_GW, (g + 1) * POOL_GW)
        y = jnp.dot(d_ref[:, cols].astype(BF16), wp_ref[g], preferred_element_type=F32) * sc_ref[:, cols]
        y_ref[:, :, cols] = y.reshape(bb, tt, POOL_GW).astype(BF16)

    @pl.when(ti == nt - 1)
    def _():
        st_ref[...] = e_ref[:, tt + 1:tt + HIST_PAD, :]

    e_ref[:, 0:HIST_PAD, :] = e_ref[:, tt:tt + HIST_PAD, :]


def _pool(u, prev, w_pool, scale, *, bb, tt, pos0, shared_prev):
    b, t, _ = u.shape
    prev_map = (lambda bi, ti: (0, 0, 0)) if shared_prev else (lambda bi, ti: (bi, 0, 0))
    return pl.pallas_call(
        functools.partial(_pool_kernel, bb=bb, tt=tt, pos0=pos0),
        grid=(b // bb, t // tt),
        in_specs=[
            pl.BlockSpec((bb, tt, POOL_W), lambda bi, ti: (bi, ti, 0)),
            pl.BlockSpec((bb, POOL_HIST, POOL_W), prev_map),
            pl.BlockSpec((N_POOL_GROUPS, POOL_GW, POOL_GW), lambda bi, ti: (0, 0, 0)),
            pl.BlockSpec((1, POOL_W), lambda bi, ti: (0, 0)),
        ],
        out_specs=[
            pl.BlockSpec((bb, tt, POOL_W), lambda bi, ti: (bi, ti, 0)),
            pl.BlockSpec((bb, POOL_HIST, POOL_W), lambda bi, ti: (bi, 0, 0)),
        ],
        out_shape=[
            jax.ShapeDtypeStruct((b, t, POOL_W), BF16),
            jax.ShapeDtypeStruct((b, POOL_HIST, POOL_W), F32),
        ],
        scratch_shapes=[
            pltpu.VMEM((bb, HIST_PAD + tt, POOL_W), F32),
            pltpu.VMEM((bb * tt, POOL_W), F32),
        ],
        compiler_params=_params(("arbitrary", "arbitrary")),
        name="pool",
    )(u, prev, w_pool, scale)


def _mlstm_chunk(q, k, v, o, gi, gf, c, n, m, gnorm, *, chunk):
    ti = lax.broadcasted_iota(jnp.int32, (chunk, chunk), 0)
    si = lax.broadcasted_iota(jnp.int32, (chunk, chunk), 1)
    tril = si <= ti
    eye = si == ti

    def to_row(col):
        return jnp.sum(jnp.where(eye, col, 0.0), axis=0, keepdims=True)

    lf = jax.nn.log_sigmoid(gf)
    b = jnp.sum(jnp.where(tril, to_row(lf), 0.0), axis=1, keepdims=True)
    a = gi - b
    a_row = to_row(a)
    cummax_a = jnp.max(jnp.where(tril, a_row, -jnp.inf), axis=1, keepdims=True)
    m_t = jnp.maximum(m + b, cummax_a + b)
    inter = jnp.exp(m + b - m_t)
    dmat = jnp.exp(jnp.where(tril, a_row + (b - m_t), -jnp.inf))

    k = k * (HEAD_DIM ** -0.5)
    s = lax.dot_general(q, k, (((1,), (1,)), ((), ())), preferred_element_type=F32) * dmat
    num = jnp.dot(s.astype(BF16), v, preferred_element_type=F32)
    num = num + inter * jnp.dot(q, c.astype(BF16), preferred_element_type=F32)
    qn = jnp.sum(s, axis=1, keepdims=True) + inter * jnp.sum(q.astype(F32) * n, axis=1, keepdims=True)
    h = num / jnp.maximum(jnp.abs(qn), jnp.exp(-m_t))

    b_last = b[chunk - 1:chunk, :]
    m_new = m_t[chunk - 1:chunk, :]
    decay = jnp.exp(m + b_last - m_new)
    w = jnp.exp(a + (b_last - m_new))
    wv = (w * v.astype(F32)).astype(BF16)
    c_new = decay * c + lax.dot_general(k, wv, (((0,), (0,)), ((), ())), preferred_element_type=F32)
    n_new = decay * n + jnp.sum(w * k.astype(F32), axis=0, keepdims=True)

    y = jax.nn.sigmoid(o.astype(F32)) * _rms(h, gnorm)
    return y, c_new, n_new, m_new


def _gate_cols(g, bias_ref, head):
    lane = lax.broadcasted_iota(jnp.int32, g.shape, 1)
    gi = jnp.sum(jnp.where(lane == head, g, 0.0), axis=1, keepdims=True) + bias_ref[head]
    gf = jnp.sum(jnp.where(lane == head + N_HEADS, g, 0.0), axis=1, keepdims=True) + bias_ref[head + N_HEADS]
    return gi, gf


def _mlstm_seq_kernel(bias_ref, q_ref, k_ref, v_ref, o_ref, g_ref, gn_ref, c0_ref, n0_ref, m0_ref, *rest,
                      chunk, conv_blocks):
    nc = len(conv_blocks)
    conv_in, (y_ref, c_ref, n_ref, m_ref), conv_out = rest[:nc], rest[nc:nc + 4], rest[nc + 4:]
    ci = pl.program_id(1)
    _convert_blocks(pl.program_id(0) * pl.num_programs(1) + ci, conv_in, conv_out, conv_blocks)

    @pl.when(ci == 0)
    def _():
        c_ref[...] = c0_ref[...]
        n_ref[...] = n0_ref[...]
        m_ref[...] = m0_ref[...]

    g = g_ref[...]
    for head in range(N_HEADS):
        cols = slice(head * HEAD_DIM, (head + 1) * HEAD_DIM)
        gi, gf = _gate_cols(g, bias_ref, head)
        y, c_new, n_new, m_new = _mlstm_chunk(
            q_ref[:, cols], k_ref[:, cols], v_ref[:, cols], o_ref[:, cols], gi, gf,
            c_ref[0, head], n_ref[0, head], m_ref[0, head], gn_ref[:, cols], chunk=chunk)
        y_ref[:, cols] = y.astype(BF16)
        c_ref[0, head] = c_new
        n_ref[0, head] = n_new
        m_ref[0, head] = m_new


def _mlstm_seq(qkvo, gates, bias, gnorm, c0, n0, m0, *, batch, seq, chunk, shared_init, convert=()):
    nc = seq // chunk
    conv_specs, conv_shapes = _convert_specs(convert, batch * nc, lambda bi, ci: bi * nc + ci)
    rows = lambda bi, ci: bi * nc + ci
    st_b = (lambda bi: 0) if shared_init else (lambda bi: bi)

    def col_spec(group):
        return pl.BlockSpec((chunk, MLSTM_W), lambda bi, ci: (rows(bi, ci), group))

    def st_spec(shape, in_b):
        nd = len(shape)
        return pl.BlockSpec((1, N_HEADS) + shape, lambda bi, ci: (in_b(bi), 0) + (0,) * nd)

    ident = lambda bi: bi
    return pl.pallas_call(
        functools.partial(_mlstm_seq_kernel, chunk=chunk, conv_blocks=tuple(n for _, n in convert)),
        grid=(batch, nc),
        in_specs=[
            pl.BlockSpec(memory_space=pltpu.SMEM),
            col_spec(0), col_spec(1), col_spec(2), col_spec(3),
            pl.BlockSpec((chunk, LANES), lambda bi, ci: (rows(bi, ci), 0)),
            pl.BlockSpec((1, MLSTM_W), lambda bi, ci: (0, 0)),
            st_spec((HEAD_DIM, HEAD_DIM), st_b), st_spec((1, HEAD_DIM), st_b), st_spec((1, 1), st_b),
        ] + conv_specs,
        out_specs=[
            pl.BlockSpec((chunk, MLSTM_W), lambda bi, ci: (rows(bi, ci), 0)),
            st_spec((HEAD_DIM, HEAD_DIM), ident), st_spec((1, HEAD_DIM), ident), st_spec((1, 1), ident),
        ] + conv_specs,
        out_shape=[
            jax.ShapeDtypeStruct((batch * seq, MLSTM_W), BF16),
            jax.ShapeDtypeStruct((batch, N_HEADS, HEAD_DIM, HEAD_DIM), F32),
            jax.ShapeDtypeStruct((batch, N_HEADS, 1, HEAD_DIM), F32),
            jax.ShapeDtypeStruct((batch, N_HEADS, 1, 1), F32),
        ] + conv_shapes,
        compiler_params=_params(("arbitrary", "arbitrary")),
        name="mlstm_seq",
    )(bias, qkvo, qkvo, qkvo, qkvo, gates, gnorm, c0, n0, m0, *[w for w, _ in convert])


def _mlstm_step_kernel(bias_ref, q_ref, k_ref, v_ref, o_ref, g_ref, gn_ref, c0_ref, n0_ref, m0_ref,
                       y_ref, c_ref, n_ref, m_ref, *, bb, seq):
    head = pl.program_id(1)
    rows = bb * seq
    ti = lax.broadcasted_iota(jnp.int32, (rows, rows), 0)
    si = lax.broadcasted_iota(jnp.int32, (rows, rows), 1)
    same = (ti // seq) == (si // seq)
    tril = same & (si <= ti)
    eye = si == ti

    def to_row(col):
        return jnp.sum(jnp.where(eye, col, 0.0), axis=0, keepdims=True)

    gi, gf = _gate_cols(g_ref[...], bias_ref, head)
    lane = lax.broadcasted_iota(jnp.int32, m0_ref.shape, 1)
    m0 = jnp.sum(jnp.where(lane == head, m0_ref[...], 0.0), axis=1, keepdims=True)

    lf_row = to_row(jax.nn.log_sigmoid(gf))
    b = jnp.sum(jnp.where(tril, lf_row, 0.0), axis=1, keepdims=True)
    b_last = jnp.sum(jnp.where(same, lf_row, 0.0), axis=1, keepdims=True)
    a = gi - b
    a_row = to_row(a)
    cummax_a = jnp.max(jnp.where(tril, a_row, -jnp.inf), axis=1, keepdims=True)
    seqmax_a = jnp.max(jnp.where(same, a_row, -jnp.inf), axis=1, keepdims=True)
    m_t = jnp.maximum(m0 + b, cummax_a + b)
    m_new = jnp.maximum(m0 + b_last, seqmax_a + b_last)
    inter = jnp.exp(m0 + b - m_t)
    dmat = jnp.exp(jnp.where(tril, a_row + (b - m_t), -jnp.inf))
    decay = jnp.exp(m0 + b_last - m_new)
    w = jnp.exp(a + (b_last - m_new))

    q = q_ref[...]
    k = k_ref[...] * (HEAD_DIM ** -0.5)
    v = v_ref[...]
    s = lax.dot_general(q, k, (((1,), (1,)), ((), ())), preferred_element_type=F32) * dmat
    sv = jnp.dot(s.astype(BF16), v, preferred_element_type=F32)

    qf = q.astype(F32)
    kf = k.astype(F32)
    wk = w * kf
    wv = w * v.astype(F32)
    qc, qdn = [], []
    for i in range(bb):
        r = slice(i * seq, (i + 1) * seq)
        c_i = c0_ref[i, 0]
        n_i = n0_ref[i:i + 1, :]
        d_i = decay[i * seq:i * seq + 1, :]
        qc.append(jnp.dot(qf[r].astype(BF16), c_i.astype(BF16), preferred_element_type=F32))
        qdn.append(jnp.sum(qf[r] * n_i, axis=1, keepdims=True))
        upd = lax.dot_general(kf[r].astype(BF16), wv[r].astype(BF16), (((0,), (0,)), ((), ())),
                              preferred_element_type=F32)
        c_ref[i, 0] = d_i * c_i + upd
        n_ref[i:i + 1, :] = d_i * n_i + jnp.sum(wk[r], axis=0, keepdims=True)

    num = sv + inter * jnp.concatenate(qc, axis=0)
    qn = jnp.sum(s, axis=1, keepdims=True) + inter * jnp.concatenate(qdn, axis=0)
    h = num / jnp.maximum(jnp.abs(qn), jnp.exp(-m_t))
    y_ref[...] = (jax.nn.sigmoid(o_ref[...].astype(F32)) * _rms(h, gn_ref[...])).astype(BF16)
    m_ref[0, 0] = to_row(m_new)


def _mlstm_step(qkvo, gates, bias, gnorm, c0, n0, m0, *, batch, seq, bb):
    rows = bb * seq

    def col_spec(off):
        return pl.BlockSpec((rows, HEAD_DIM), lambda bi, h: (bi, off + h))

    c_spec = pl.BlockSpec((bb, 1, HEAD_DIM, HEAD_DIM), lambda bi, h: (bi, h, 0, 0))
    n_spec = pl.BlockSpec((bb, HEAD_DIM), lambda bi, h: (bi, h))
    return pl.pallas_call(
        functools.partial(_mlstm_step_kernel, bb=bb, seq=seq),
        grid=(batch // bb, N_HEADS),
        in_specs=[
            pl.BlockSpec(memory_space=pltpu.SMEM),
            col_spec(0), col_spec(N_HEADS), col_spec(2 * N_HEADS), col_spec(3 * N_HEADS),
            pl.BlockSpec((rows, LANES), lambda bi, h: (bi, 0)),
            pl.BlockSpec((1, HEAD_DIM), lambda bi, h: (0, h)),
            c_spec, n_spec,
            pl.BlockSpec((rows, N_HEADS), lambda bi, h: (bi, 0)),
        ],
        out_specs=[
            pl.BlockSpec((rows, HEAD_DIM), lambda bi, h: (bi, h)),
            c_spec, n_spec,
            pl.BlockSpec((1, 1, 1, rows), lambda bi, h: (bi, h, 0, 0)),
        ],
        out_shape=[
            jax.ShapeDtypeStruct((batch * seq, MLSTM_W), BF16),
            jax.ShapeDtypeStruct((batch, N_HEADS, HEAD_DIM, HEAD_DIM), F32),
            jax.ShapeDtypeStruct((batch, MLSTM_W), F32),
            jax.ShapeDtypeStruct((batch // bb, N_HEADS, 1, rows), F32),
        ],
        compiler_params=_params(("arbitrary", "arbitrary")),
        name="mlstm_step",
    )(bias, qkvo, qkvo, qkvo, qkvo, gates, gnorm, c0, n0, m0)


def _outproj_kernel(x_ref, yp_ref, ym_ref, wo_ref, nw_ref, x2_ref, xn2_ref):
    x2 = x_ref[...] + jnp.dot(yp_ref[...], wo_ref[0:POOL_W, :], preferred_element_type=F32)
    x2 = x2 + jnp.dot(ym_ref[...], wo_ref[POOL_W:, :], preferred_element_type=F32)
    x2_ref[...] = x2
    xn2_ref[...] = _rms(x2, nw_ref[...]).astype(BF16)


def _outproj(x, y_pool, y_ml, w_out, norm_w, *, tm):
    m = x.shape[0]
    row = lambda w: pl.BlockSpec((tm, w), lambda i: (i, 0))
    return pl.pallas_call(
        _outproj_kernel,
        grid=(m // tm,),
        in_specs=[
            row(D_MODEL), row(POOL_W), row(MLSTM_W),
            pl.BlockSpec((D_MODEL, D_MODEL), lambda i: (0, 0)),
            pl.BlockSpec((1, D_MODEL), lambda i: (0, 0)),
        ],
        out_specs=[row(D_MODEL), row(D_MODEL)],
        out_shape=[
            jax.ShapeDtypeStruct((m, D_MODEL), F32),
            jax.ShapeDtypeStruct((m, D_MODEL), BF16),
        ],
        compiler_params=_params(("arbitrary",)),
        name="outproj",
    )(x, y_pool, y_ml, w_out, norm_w)


def _ffn_kernel(xn_ref, x2_ref, wg_ref, wu_ref, wd_ref, nw_ref, y_ref, acc_ref):
    f = pl.program_id(1)
    nf = pl.num_programs(1)

    @pl.when(f == 0)
    def _():
        acc_ref[...] = x2_ref[...]

    xn = xn_ref[...]
    g = jnp.dot(xn, wg_ref[...], preferred_element_type=F32)
    u = jnp.dot(xn, wu_ref[...], preferred_element_type=F32)
    h = (jax.nn.silu(g) * u).astype(BF16)
    acc_ref[...] += jnp.dot(h, wd_ref[...], preferred_element_type=F32)

    @pl.when(f == nf - 1)
    def _():
        y_ref[...] = _rms(acc_ref[...], nw_ref[...])


def _ffn(xn2, x2, w_gate, w_up, w_down, norm_w, *, tm, tf):
    m = xn2.shape[0]
    return pl.pallas_call(
        _ffn_kernel,
        grid=(m // tm, D_FF // tf),
        in_specs=[
            pl.BlockSpec((tm, D_MODEL), lambda i, f: (i, 0)),
            pl.BlockSpec((tm, D_MODEL), lambda i, f: (i, 0)),
            pl.BlockSpec((D_MODEL, tf), lambda i, f: (0, f)),
            pl.BlockSpec((D_MODEL, tf), lambda i, f: (0, f)),
            pl.BlockSpec((tf, D_MODEL), lambda i, f: (f, 0)),
            pl.BlockSpec((1, D_MODEL), lambda i, f: (0, 0)),
        ],
        out_specs=pl.BlockSpec((tm, D_MODEL), lambda i, f: (i, 0)),
        out_shape=jax.ShapeDtypeStruct((m, D_MODEL), F32),
        scratch_shapes=[pltpu.VMEM((tm, D_MODEL), F32)],
        compiler_params=_params(("arbitrary", "arbitrary")),
        name="ffn",
    )(xn2, x2, w_gate, w_up, w_down, norm_w)


def kernel(x_prompt, x_sample, state_pool, state_mlstm_C, state_mlstm_n, state_mlstm_m, meta_tokens, norm_mix_w, w_in, b_igate, b_fgate, w_pool, pool_scale, mlstm_norm_w, w_out, norm_ffn_w, w_gate, w_up, w_down, norm_final_w):
    bp, tp, _ = x_prompt.shape
    bs, ts, _ = x_sample.shape

    w_in_t = w_in[0].T
    w_g8 = jnp.pad(w_in_t[MAIN_W:], ((0, LANES - 2 * N_HEADS), (0, 0))).astype(BF16)
    wp = w_pool[0].astype(BF16)
    nmix = norm_mix_w[0].reshape(1, D_MODEL)
    nffn = norm_ffn_w[0].reshape(1, D_MODEL)
    nfin = norm_final_w.reshape(1, D_MODEL)
    scale = pool_scale[0].reshape(1, POOL_W)
    gnorm = mlstm_norm_w[0].reshape(1, MLSTM_W)
    bias = jnp.concatenate([b_igate[0], b_fgate[0]]).astype(F32)

    xp = x_prompt.reshape(bp * tp, D_MODEL)
    xs = x_sample.reshape(bs * ts, D_MODEL)

    u_m, qkvo_m, g_m, w_main = _inproj(meta_tokens, nmix, w_in_t, w_g8, tm=N_META, tn=512)
    u_p, qkvo_p, g_p = _inproj(xp, nmix, w_main, w_g8, tm=1024, tn=1024)
    u_s, qkvo_s, g_s = _inproj(xs, nmix, w_main, w_g8, tm=1024, tn=1024)

    prev_p = u_m[1:N_META].reshape(1, POOL_HIST, POOL_W)
    yp_p, pool_p = _pool(u_p.reshape(bp, tp, POOL_W), prev_p, wp, scale, bb=1, tt=512, pos0=N_META, shared_prev=True)
    yp_s, pool_s = _pool(u_s.reshape(bs, ts, POOL_W), state_pool[0], wp, scale, bb=16, tt=ts, pos0=PAST_LEN, shared_prev=False)

    zc = jnp.zeros((1, N_HEADS, HEAD_DIM, HEAD_DIM), F32)
    zn = jnp.zeros((1, N_HEADS, 1, HEAD_DIM), F32)
    zm = jnp.zeros((1, N_HEADS, 1, 1), F32)
    _, c_m, n_m, m_m = _mlstm_seq(qkvo_m, g_m, bias, gnorm, zc, zn, zm, batch=1, seq=N_META, chunk=N_META, shared_init=False)
    convert = ((w_out[0], 32), (w_gate[0], 32), (w_up[0], 32), (w_down[0], 32))
    ym_p, c_p, n_p, m_p, wo, wg, wu, wd = _mlstm_seq(
        qkvo_p, g_p, bias, gnorm, c_m, n_m, m_m, batch=bp, seq=tp, chunk=256, shared_init=True, convert=convert)
    step_bb = 16
    ym_s, c_s, n_s, m_s = _mlstm_step(
        qkvo_s, g_s, bias, gnorm, state_mlstm_C[0],
        state_mlstm_n[0].reshape(bs, MLSTM_W), jnp.repeat(state_mlstm_m[0], ts, axis=0),
        batch=bs, seq=ts, bb=step_bb)
    m_s = m_s.reshape(bs // step_bb, N_HEADS, step_bb, ts)[..., 0].transpose(0, 2, 1)

    x2_p, xn2_p = _outproj(xp, yp_p.reshape(bp * tp, POOL_W), ym_p, wo, nffn, tm=512)
    x2_s, xn2_s = _outproj(xs, yp_s.reshape(bs * ts, POOL_W), ym_s, wo, nffn, tm=512)
    y_p = _ffn(xn2_p, x2_p, wg, wu, wd, nfin, tm=512, tf=512)
    y_s = _ffn(xn2_s, x2_s, wg, wu, wd, nfin, tm=512, tf=512)

    return (
        y_p.reshape(bp, tp, D_MODEL),
        y_s.reshape(bs, ts, D_MODEL),
        pool_p[None],
        c_p[None],
        n_p.reshape(1, bp, N_HEADS, HEAD_DIM),
        m_p.reshape(1, bp, N_HEADS),
        pool_s[None],
        c_s[None],
        n_s.reshape(1, bs, N_HEADS, HEAD_DIM),
        m_s.reshape(1, bs, N_HEADS),
    )
```

```python
import functools

import jax
import jax.numpy as jnp
from jax import lax
from jax.experimental import pallas as pl
from jax.experimental.pallas import tpu as pltpu

D_MODEL = 2048
N_META = 16
POOL_W = 1024
MLSTM_W = 1024
POOL_WINDOWS = (2, 4, 8, 16)
N_POOL_GROUPS = 4
POOL_GW = 256
POOL_HIST = 15
N_HEADS = 4
HEAD_DIM = 256
D_FF = 5632
QKVO_W = 4 * MLSTM_W
MAIN_W = POOL_W + QKVO_W
PAST_LEN = 16384
EPS = 1e-6

LANES = 128
HIST_PAD = 16
VMEM_LIMIT = 56 * 1024 * 1024

BF16 = jnp.bfloat16
F32 = jnp.float32


def _params(sem):
    return pltpu.CompilerParams(dimension_semantics=sem, vmem_limit_bytes=VMEM_LIMIT)


def _rms(x, w):
    return x * lax.rsqrt(jnp.mean(x * x, axis=-1, keepdims=True) + EPS) * w


def _inproj_kernel(x_ref, nw_ref, w_ref, wg_ref, u_ref, qkvo_ref, g_ref, *rest, n_u_tiles, emit_copy):
    xn_ref = rest[-1]
    j = pl.program_id(1)

    nt = (((1,), (1,)), ((), ()))

    @pl.when(j == 0)
    def _():
        xn = _rms(x_ref[...], nw_ref[...]).astype(BF16)
        xn_ref[...] = xn
        g_ref[...] = lax.dot_general(xn, wg_ref[...], nt, preferred_element_type=F32)

    w = w_ref[...]
    if emit_copy:
        w = w.astype(BF16)
        rest[0][...] = w
    p = lax.dot_general(xn_ref[...], w, nt, preferred_element_type=F32)

    @pl.when(j < n_u_tiles)
    def _():
        u_ref[...] = p

    @pl.when(j >= n_u_tiles)
    def _():
        qkvo_ref[...] = p.astype(BF16)


def _inproj(x, norm_w, w_t, w_gate_t, *, tm, tn):
    m = x.shape[0]
    n_u = POOL_W // tn
    emit_copy = w_t.dtype == F32
    if emit_copy:
        assert m == tm, "the bf16 copy is written once per column tile"
    w_spec = pl.BlockSpec((tn, D_MODEL), lambda i, j: (j, 0))
    copy_spec = [w_spec] if emit_copy else []
    copy_shape = [jax.ShapeDtypeStruct((MAIN_W, D_MODEL), BF16)] if emit_copy else []
    return pl.pallas_call(
        functools.partial(_inproj_kernel, n_u_tiles=n_u, emit_copy=emit_copy),
        grid=(m // tm, MAIN_W // tn),
        in_specs=[
            pl.BlockSpec((tm, D_MODEL), lambda i, j: (i, 0)),
            pl.BlockSpec((1, D_MODEL), lambda i, j: (0, 0)),
            w_spec,
            pl.BlockSpec((LANES, D_MODEL), lambda i, j: (0, 0)),
        ],
        out_specs=[
            pl.BlockSpec((tm, tn), lambda i, j: (i, jnp.minimum(j, n_u - 1))),
            pl.BlockSpec((tm, tn), lambda i, j: (i, jnp.maximum(j - n_u, 0))),
            pl.BlockSpec((tm, LANES), lambda i, j: (i, 0)),
        ] + copy_spec,
        out_shape=[
            jax.ShapeDtypeStruct((m, POOL_W), F32),
            jax.ShapeDtypeStruct((m, QKVO_W), BF16),
            jax.ShapeDtypeStruct((m, LANES), F32),
        ] + copy_shape,
        scratch_shapes=[pltpu.VMEM((tm, D_MODEL), BF16)],
        compiler_params=_params(("arbitrary", "arbitrary")),
        name="inproj",
    )(x, norm_w, w_t, w_gate_t)


def _convert_specs(weights, n_steps, step_of):
    specs, shapes = [], []
    for w, n_blocks in weights:
        assert n_blocks <= n_steps
        rows = w.shape[0] // n_blocks
        specs.append(pl.BlockSpec(
            (rows, w.shape[1]), lambda *idx, n_blocks=n_blocks: (jnp.minimum(step_of(*idx), n_blocks - 1), 0)))
        shapes.append(jax.ShapeDtypeStruct(w.shape, BF16))
    return specs, shapes


def _convert_blocks(step, srcs, dsts, n_blocks):
    for src, dst, n in zip(srcs, dsts, n_blocks):
        @pl.when(step < n)
        def _(src=src, dst=dst):
            dst[...] = src[...].astype(BF16)


def _pool_kernel(u_ref, prev_ref, wp_ref, sc_ref, y_ref, st_ref, e_ref, d_ref, *, bb, tt, pos0):
    ti = pl.program_id(1)
    nt = pl.num_programs(1)

    @pl.when(ti == 0)
    def _():
        e_ref[:, 0:1, :] = jnp.zeros((bb, 1, POOL_W), F32)
        e_ref[:, 1:HIST_PAD, :] = prev_ref[...]

    e_ref[:, HIST_PAD:, :] = u_ref[...]

    pos = pos0 + ti * tt + lax.broadcasted_iota(jnp.int32, (tt, 1), 0)
    for b in range(bb):
        for g, w in enumerate(POOL_WINDOWS):
            cols = slice(g * POOL_GW, (g + 1) * POOL_GW)
            e = e_ref[b, :, cols]
            s = e + pltpu.roll(e, 1, axis=0)
            for k in range(1, g + 1):
                s = s + pltpu.roll(s, 2**k, axis=0)
            cnt = jnp.minimum(w, pos + 1).astype(F32)
            d_ref[b * tt:(b + 1) * tt, cols] = s[HIST_PAD:, :] / cnt - e[HIST_PAD:, :]

    for g in range(N_POOL_GROUPS):
        cols = slice(g * POOL_GW, (g + 1) * POOL_GW)
        y = jnp.dot(d_ref[:, cols].astype(BF16), wp_ref[g], preferred_element_type=F32) * sc_ref[:, cols]
        y_ref[:, :, cols] = y.reshape(bb, tt, POOL_GW).astype(BF16)

    @pl.when(ti == nt - 1)
    def _():
        st_ref[...] = e_ref[:, tt + 1:tt + HIST_PAD, :]

    e_ref[:, 0:HIST_PAD, :] = e_ref[:, tt:tt + HIST_PAD, :]


def _pool(u, prev, w_pool, scale, *, bb, tt, pos0, shared_prev):
    b, t, _ = u.shape
    prev_map = (lambda bi, ti: (0, 0, 0)) if shared_prev else (lambda bi, ti: (bi, 0, 0))
    return pl.pallas_call(
        functools.partial(_pool_kernel, bb=bb, tt=tt, pos0=pos0),
        grid=(b // bb, t // tt),
        in_specs=[
            pl.BlockSpec((bb, tt, POOL_W), lambda bi, ti: (bi, ti, 0)),
            pl.BlockSpec((bb, POOL_HIST, POOL_W), prev_map),
            pl.BlockSpec((N_POOL_GROUPS, POOL_GW, POOL_GW), lambda bi, ti: (0, 0, 0)),
            pl.BlockSpec((1, POOL_W), lambda bi, ti: (0, 0)),
        ],
        out_specs=[
            pl.BlockSpec((bb, tt, POOL_W), lambda bi, ti: (bi, ti, 0)),
            pl.BlockSpec((bb, POOL_HIST, POOL_W), lambda bi, ti: (bi, 0, 0)),
        ],
        out_shape=[
            jax.ShapeDtypeStruct((b, t, POOL_W), BF16),
            jax.ShapeDtypeStruct((b, POOL_HIST, POOL_W), F32),
        ],
        scratch_shapes=[
            pltpu.VMEM((bb, HIST_PAD + tt, POOL_W), F32),
            pltpu.VMEM((bb * tt, POOL_W), F32),
        ],
        compiler_params=_params(("arbitrary", "arbitrary")),
        name="pool",
    )(u, prev, w_pool, scale)


def _mlstm_chunk(q, k, v, o, gi, gf, c, n, m, gnorm, *, chunk):
    ti = lax.broadcasted_iota(jnp.int32, (chunk, chunk), 0)
    si = lax.broadcasted_iota(jnp.int32, (chunk, chunk), 1)
    tril = si <= ti
    eye = si == ti

    def to_row(col):
        return jnp.sum(jnp.where(eye, col, 0.0), axis=0, keepdims=True)

    lf = jax.nn.log_sigmoid(gf)
    b = jnp.sum(jnp.where(tril, to_row(lf), 0.0), axis=1, keepdims=True)
    a = gi - b
    a_row = to_row(a)
    cummax_a = jnp.max(jnp.where(tril, a_row, -jnp.inf), axis=1, keepdims=True)
    m_t = jnp.maximum(m + b, cummax_a + b)
    inter = jnp.exp(m + b - m_t)
    dmat = jnp.exp(jnp.where(tril, a_row + (b - m_t), -jnp.inf))

    k = k * (HEAD_DIM ** -0.5)
    s = lax.dot_general(q, k, (((1,), (1,)), ((), ())), preferred_element_type=F32) * dmat
    num = jnp.dot(s.astype(BF16), v, preferred_element_type=F32)
    num = num + inter * jnp.dot(q, c.astype(BF16), preferred_element_type=F32)
    qn = jnp.sum(s, axis=1, keepdims=True) + inter * jnp.sum(q.astype(F32) * n, axis=1, keepdims=True)
    h = num / jnp.maximum(jnp.abs(qn), jnp.exp(-m_t))

    b_last = b[chunk - 1:chunk, :]
    m_new = m_t[chunk - 1:chunk, :]
    decay = jnp.exp(m + b_last - m_new)
    w = jnp.exp(a + (b_last - m_new))
    wv = (w * v.astype(F32)).astype(BF16)
    c_new = decay * c + lax.dot_general(k, wv, (((0,), (0,)), ((), ())), preferred_element_type=F32)
    n_new = decay * n + jnp.sum(w * k.astype(F32), axis=0, keepdims=True)

    y = jax.nn.sigmoid(o.astype(F32)) * _rms(h, gnorm)
    return y, c_new, n_new, m_new


def _gate_cols(g, bias_ref, head):
    lane = lax.broadcasted_iota(jnp.int32, g.shape, 1)
    gi = jnp.sum(jnp.where(lane == head, g, 0.0), axis=1, keepdims=True) + bias_ref[head]
    gf = jnp.sum(jnp.where(lane == head + N_HEADS, g, 0.0), axis=1, keepdims=True) + bias_ref[head + N_HEADS]
    return gi, gf


N_STEP_IN = 9
N_STEP_OUT = 4


def _mlstm_seq_kernel(bias_ref, q_ref, k_ref, v_ref, o_ref, g_ref, gn_ref, c0_ref, n0_ref, m0_ref, *rest,
                      chunk, conv_blocks, step):
    nc = len(conv_blocks)
    ns_in = N_STEP_IN if step else 0
    conv_in, step_in = rest[:nc], rest[nc:nc + ns_in]
    outs = rest[nc + ns_in:]
    (y_ref, c_ref, n_ref, m_ref), conv_out, step_out = outs[:4], outs[4:4 + nc], outs[4 + nc:]
    ci = pl.program_id(1)
    grid_step = pl.program_id(0) * pl.num_programs(1) + ci
    _convert_blocks(grid_step, conv_in, conv_out, conv_blocks)

    @pl.when(ci == 0)
    def _():
        c_ref[...] = c0_ref[...]
        n_ref[...] = n0_ref[...]
        m_ref[...] = m0_ref[...]

    stages = iter(())
    if step:
        stages = _mlstm_step_stages(grid_step % N_HEADS, bias_ref, *step_in, *step_out, bb=step[0], seq=step[1])
    next(stages, None)

    g = g_ref[...]
    for head in range(N_HEADS):
        cols = slice(head * HEAD_DIM, (head + 1) * HEAD_DIM)
        gi, gf = _gate_cols(g, bias_ref, head)
        y, c_new, n_new, m_new = _mlstm_chunk(
            q_ref[:, cols], k_ref[:, cols], v_ref[:, cols], o_ref[:, cols], gi, gf,
            c_ref[0, head], n_ref[0, head], m_ref[0, head], gn_ref[:, cols], chunk=chunk)
        y_ref[:, cols] = y.astype(BF16)
        c_ref[0, head] = c_new
        n_ref[0, head] = n_new
        m_ref[0, head] = m_new
        next(stages, None)
    for _ in stages:
        pass


def _mlstm_seq(qkvo, gates, bias, gnorm, c0, n0, m0, *, batch, seq, chunk, shared_init, convert=(), step=None):
    nc = seq // chunk
    rows = lambda bi, ci: bi * nc + ci
    conv_specs, conv_shapes = _convert_specs(convert, batch * nc, rows)
    step_args, step_in, step_out, step_shapes, step_cfg = [], [], [], [], None
    if step:
        (s_qkvo, s_gates, s_c0, s_n0, s_m0), s_batch, s_seq, s_bb = step
        step_in, step_out, step_shapes = _mlstm_step_specs(s_batch, s_seq, s_bb, batch * nc, rows)
        step_args = [s_qkvo, s_qkvo, s_qkvo, s_qkvo, s_gates, gnorm, s_c0, s_n0, s_m0]
        step_cfg = (s_bb, s_seq)
    st_b = (lambda bi: 0) if shared_init else (lambda bi: bi)

    def col_spec(group):
        return pl.BlockSpec((chunk, MLSTM_W), lambda bi, ci: (rows(bi, ci), group))

    def st_spec(shape, in_b):
        nd = len(shape)
        return pl.BlockSpec((1, N_HEADS) + shape, lambda bi, ci: (in_b(bi), 0) + (0,) * nd)

    ident = lambda bi: bi
    return pl.pallas_call(
        functools.partial(_mlstm_seq_kernel, chunk=chunk, conv_blocks=tuple(n for _, n in convert), step=step_cfg),
        grid=(batch, nc),
        in_specs=[
            pl.BlockSpec(memory_space=pltpu.SMEM),
            col_spec(0), col_spec(1), col_spec(2), col_spec(3),
            pl.BlockSpec((chunk, LANES), lambda bi, ci: (rows(bi, ci), 0)),
            pl.BlockSpec((1, MLSTM_W), lambda bi, ci: (0, 0)),
            st_spec((HEAD_DIM, HEAD_DIM), st_b), st_spec((1, HEAD_DIM), st_b), st_spec((1, 1), st_b),
        ] + conv_specs + step_in,
        out_specs=[
            pl.BlockSpec((chunk, MLSTM_W), lambda bi, ci: (rows(bi, ci), 0)),
            st_spec((HEAD_DIM, HEAD_DIM), ident), st_spec((1, HEAD_DIM), ident), st_spec((1, 1), ident),
        ] + conv_specs + step_out,
        out_shape=[
            jax.ShapeDtypeStruct((batch * seq, MLSTM_W), BF16),
            jax.ShapeDtypeStruct((batch, N_HEADS, HEAD_DIM, HEAD_DIM), F32),
            jax.ShapeDtypeStruct((batch, N_HEADS, 1, HEAD_DIM), F32),
            jax.ShapeDtypeStruct((batch, N_HEADS, 1, 1), F32),
        ] + conv_shapes + step_shapes,
        compiler_params=_params(("arbitrary", "arbitrary")),
        name="mlstm_seq",
    )(bias, qkvo, qkvo, qkvo, qkvo, gates, gnorm, c0, n0, m0, *[w for w, _ in convert], *step_args)


def _mlstm_step_stages(head, bias_ref, q_ref, k_ref, v_ref, o_ref, g_ref, gn_ref, c0_ref, n0_ref, m0_ref,
                       y_ref, c_ref, n_ref, m_ref, *, bb, seq):
    rows = bb * seq
    ti = lax.broadcasted_iota(jnp.int32, (rows, rows), 0)
    si = lax.broadcasted_iota(jnp.int32, (rows, rows), 1)
    same = (ti // seq) == (si // seq)
    tril = same & (si <= ti)
    eye = si == ti

    def to_row(col):
        return jnp.sum(jnp.where(eye, col, 0.0), axis=0, keepdims=True)

    gi, gf = _gate_cols(g_ref[...], bias_ref, head)
    lane = lax.broadcasted_iota(jnp.int32, m0_ref.shape, 1)
    m0 = jnp.sum(jnp.where(lane == head, m0_ref[...], 0.0), axis=1, keepdims=True)

    lf_row = to_row(jax.nn.log_sigmoid(gf))
    b = jnp.sum(jnp.where(tril, lf_row, 0.0), axis=1, keepdims=True)
    b_last = jnp.sum(jnp.where(same, lf_row, 0.0), axis=1, keepdims=True)
    a = gi - b
    a_row = to_row(a)
    cummax_a = jnp.max(jnp.where(tril, a_row, -jnp.inf), axis=1, keepdims=True)
    seqmax_a = jnp.max(jnp.where(same, a_row, -jnp.inf), axis=1, keepdims=True)
    m_t = jnp.maximum(m0 + b, cummax_a + b)
    m_new = jnp.maximum(m0 + b_last, seqmax_a + b_last)
    inter = jnp.exp(m0 + b - m_t)
    dmat = jnp.exp(jnp.where(tril, a_row + (b - m_t), -jnp.inf))
    decay = jnp.exp(m0 + b_last - m_new)
    w = jnp.exp(a + (b_last - m_new))

    q = q_ref[...]
    k = k_ref[...] * (HEAD_DIM ** -0.5)
    v = v_ref[...]
    s = lax.dot_general(q, k, (((1,), (1,)), ((), ())), preferred_element_type=F32) * dmat
    sv = jnp.dot(s.astype(BF16), v, preferred_element_type=F32)

    qf = q.astype(F32)
    kf = k.astype(F32)
    wk = w * kf
    wv = w * v.astype(F32)
    qc, qdn = [], []
    yield
    for i in range(bb):
        r = slice(i * seq, (i + 1) * seq)
        c_i = c0_ref[i, 0]
        n_i = n0_ref[i:i + 1, :]
        d_i = decay[i * seq:i * seq + 1, :]
        qc.append(jnp.dot(qf[r].astype(BF16), c_i.astype(BF16), preferred_element_type=F32))
        qdn.append(jnp.sum(qf[r] * n_i, axis=1, keepdims=True))
        upd = lax.dot_general(kf[r].astype(BF16), wv[r].astype(BF16), (((0,), (0,)), ((), ())),
                              preferred_element_type=F32)
        c_ref[i, 0] = d_i * c_i + upd
        n_ref[i:i + 1, :] = d_i * n_i + jnp.sum(wk[r], axis=0, keepdims=True)
        if (i + 1) % (bb // N_HEADS) == 0:
            yield

    num = sv + inter * jnp.concatenate(qc, axis=0)
    qn = jnp.sum(s, axis=1, keepdims=True) + inter * jnp.concatenate(qdn, axis=0)
    h = num / jnp.maximum(jnp.abs(qn), jnp.exp(-m_t))
    y_ref[...] = (jax.nn.sigmoid(o_ref[...].astype(F32)) * _rms(h, gn_ref[...])).astype(BF16)
    m_ref[0, 0] = to_row(m_new)


def _mlstm_step_specs(batch, seq, bb, n_steps, step_of):
    assert (batch // bb) * N_HEADS == n_steps
    rows = bb * seq
    blk = lambda *idx: step_of(*idx) // N_HEADS
    head = lambda *idx: step_of(*idx) % N_HEADS

    def col_spec(off):
        return pl.BlockSpec((rows, HEAD_DIM), lambda *idx: (blk(*idx), off + head(*idx)))

    c_spec = pl.BlockSpec((bb, 1, HEAD_DIM, HEAD_DIM), lambda *idx: (blk(*idx), head(*idx), 0, 0))
    n_spec = pl.BlockSpec((bb, HEAD_DIM), lambda *idx: (blk(*idx), head(*idx)))
    in_specs = [
        col_spec(0), col_spec(N_HEADS), col_spec(2 * N_HEADS), col_spec(3 * N_HEADS),
        pl.BlockSpec((rows, LANES), lambda *idx: (blk(*idx), 0)),
        pl.BlockSpec((1, HEAD_DIM), lambda *idx: (0, head(*idx))),
        c_spec, n_spec,
        pl.BlockSpec((rows, N_HEADS), lambda *idx: (blk(*idx), 0)),
    ]
    out_specs = [
        pl.BlockSpec((rows, HEAD_DIM), lambda *idx: (blk(*idx), head(*idx))),
        c_spec, n_spec,
        pl.BlockSpec((1, 1, 1, rows), lambda *idx: (blk(*idx), head(*idx), 0, 0)),
    ]
    out_shapes = [
        jax.ShapeDtypeStruct((batch * seq, MLSTM_W), BF16),
        jax.ShapeDtypeStruct((batch, N_HEADS, HEAD_DIM, HEAD_DIM), F32),
        jax.ShapeDtypeStruct((batch, MLSTM_W), F32),
        jax.ShapeDtypeStruct((batch // bb, N_HEADS, 1, rows), F32),
    ]
    return in_specs, out_specs, out_shapes


def _outproj_kernel(x_ref, yp_ref, ym_ref, wo_ref, nw_ref, x2_ref, xn2_ref):
    x2 = x_ref[...] + jnp.dot(yp_ref[...], wo_ref[0:POOL_W, :], preferred_element_type=F32)
    x2 = x2 + jnp.dot(ym_ref[...], wo_ref[POOL_W:, :], preferred_element_type=F32)
    x2_ref[...] = x2
    xn2_ref[...] = _rms(x2, nw_ref[...]).astype(BF16)


def _outproj(x, y_pool, y_ml, w_out, norm_w, *, tm):
    m = x.shape[0]
    row = lambda w: pl.BlockSpec((tm, w), lambda i: (i, 0))
    return pl.pallas_call(
        _outproj_kernel,
        grid=(m // tm,),
        in_specs=[
            row(D_MODEL), row(POOL_W), row(MLSTM_W),
            pl.BlockSpec((D_MODEL, D_MODEL), lambda i: (0, 0)),
            pl.BlockSpec((1, D_MODEL), lambda i: (0, 0)),
        ],
        out_specs=[row(D_MODEL), row(D_MODEL)],
        out_shape=[
            jax.ShapeDtypeStruct((m, D_MODEL), F32),
            jax.ShapeDtypeStruct((m, D_MODEL), BF16),
        ],
        compiler_params=_params(("arbitrary",)),
        name="outproj",
    )(x, y_pool, y_ml, w_out, norm_w)


def _ffn_kernel(xn_ref, x2_ref, wg_ref, wu_ref, wd_ref, nw_ref, y_ref, acc_ref):
    f = pl.program_id(1)
    nf = pl.num_programs(1)

    @pl.when(f == 0)
    def _():
        acc_ref[...] = x2_ref[...]

    xn = xn_ref[...]
    g = jnp.dot(xn, wg_ref[...], preferred_element_type=F32)
    u = jnp.dot(xn, wu_ref[...], preferred_element_type=F32)
    h = (jax.nn.silu(g) * u).astype(BF16)
    acc_ref[...] += jnp.dot(h, wd_ref[...], preferred_element_type=F32)

    @pl.when(f == nf - 1)
    def _():
        y_ref[...] = _rms(acc_ref[...], nw_ref[...])


def _ffn(xn2, x2, w_gate, w_up, w_down, norm_w, *, tm, tf):
    m = xn2.shape[0]
    return pl.pallas_call(
        _ffn_kernel,
        grid=(m // tm, D_FF // tf),
        in_specs=[
            pl.BlockSpec((tm, D_MODEL), lambda i, f: (i, 0)),
            pl.BlockSpec((tm, D_MODEL), lambda i, f: (i, 0)),
            pl.BlockSpec((D_MODEL, tf), lambda i, f: (0, f)),
            pl.BlockSpec((D_MODEL, tf), lambda i, f: (0, f)),
            pl.BlockSpec((tf, D_MODEL), lambda i, f: (f, 0)),
            pl.BlockSpec((1, D_MODEL), lambda i, f: (0, 0)),
        ],
        out_specs=pl.BlockSpec((tm, D_MODEL), lambda i, f: (i, 0)),
        out_shape=jax.ShapeDtypeStruct((m, D_MODEL), F32),
        scratch_shapes=[pltpu.VMEM((tm, D_MODEL), F32)],
        compiler_params=_params(("arbitrary", "arbitrary")),
        name="ffn",
    )(xn2, x2, w_gate, w_up, w_down, norm_w)


def kernel(x_prompt, x_sample, state_pool, state_mlstm_C, state_mlstm_n, state_mlstm_m, meta_tokens, norm_mix_w, w_in, b_igate, b_fgate, w_pool, pool_scale, mlstm_norm_w, w_out, norm_ffn_w, w_gate, w_up, w_down, norm_final_w):
    bp, tp, _ = x_prompt.shape
    bs, ts, _ = x_sample.shape

    w_in_t = w_in[0].T
    w_g8 = jnp.pad(w_in_t[MAIN_W:], ((0, LANES - 2 * N_HEADS), (0, 0))).astype(BF16)
    wp = w_pool[0].astype(BF16)
    nmix = norm_mix_w[0].reshape(1, D_MODEL)
    nffn = norm_ffn_w[0].reshape(1, D_MODEL)
    nfin = norm_final_w.reshape(1, D_MODEL)
    scale = pool_scale[0].reshape(1, POOL_W)
    gnorm = mlstm_norm_w[0].reshape(1, MLSTM_W)
    bias = jnp.concatenate([b_igate[0], b_fgate[0]]).astype(F32)

    xp = x_prompt.reshape(bp * tp, D_MODEL)
    xs = x_sample.reshape(bs * ts, D_MODEL)

    u_m, qkvo_m, g_m, w_main = _inproj(meta_tokens, nmix, w_in_t, w_g8, tm=N_META, tn=512)
    u_p, qkvo_p, g_p = _inproj(xp, nmix, w_main, w_g8, tm=1024, tn=1024)
    u_s, qkvo_s, g_s = _inproj(xs, nmix, w_main, w_g8, tm=1024, tn=1024)

    prev_p = u_m[1:N_META].reshape(1, POOL_HIST, POOL_W)
    yp_p, pool_p = _pool(u_p.reshape(bp, tp, POOL_W), prev_p, wp, scale, bb=1, tt=512, pos0=N_META, shared_prev=True)
    yp_s, pool_s = _pool(u_s.reshape(bs, ts, POOL_W), state_pool[0], wp, scale, bb=16, tt=ts, pos0=PAST_LEN, shared_prev=False)

    zc = jnp.zeros((1, N_HEADS, HEAD_DIM, HEAD_DIM), F32)
    zn = jnp.zeros((1, N_HEADS, 1, HEAD_DIM), F32)
    zm = jnp.zeros((1, N_HEADS, 1, 1), F32)
    _, c_m, n_m, m_m = _mlstm_seq(qkvo_m, g_m, bias, gnorm, zc, zn, zm, batch=1, seq=N_META, chunk=N_META, shared_init=False)
    convert = ((w_out[0], 32), (w_gate[0], 32), (w_up[0], 32), (w_down[0], 32))
    step_bb = 16
    step_ops = (qkvo_s, g_s, state_mlstm_C[0], state_mlstm_n[0].reshape(bs, MLSTM_W),
                jnp.repeat(state_mlstm_m[0], ts, axis=0))
    ym_p, c_p, n_p, m_p, wo, wg, wu, wd, ym_s, c_s, n_s, m_s = _mlstm_seq(
        qkvo_p, g_p, bias, gnorm, c_m, n_m, m_m, batch=bp, seq=tp, chunk=256, shared_init=True, convert=convert,
        step=(step_ops, bs, ts, step_bb))
    m_s = m_s.reshape(bs // step_bb, N_HEADS, step_bb, ts)[..., 0].transpose(0, 2, 1)

    x2_p, xn2_p = _outproj(xp, yp_p.reshape(bp * tp, POOL_W), ym_p, wo, nffn, tm=512)
    x2_s, xn2_s = _outproj(xs, yp_s.reshape(bs * ts, POOL_W), ym_s, wo, nffn, tm=512)
    y_p = _ffn(xn2_p, x2_p, wg, wu, wd, nfin, tm=512, tf=512)
    y_s = _ffn(xn2_s, x2_s, wg, wu, wd, nfin, tm=512, tf=512)

    return (
        y_p.reshape(bp, tp, D_MODEL),
        y_s.reshape(bs, ts, D_MODEL),
        pool_p[None],
        c_p[None],
        n_p.reshape(1, bp, N_HEADS, HEAD_DIM),
        m_p.reshape(1, bp, N_HEADS),
        pool_s[None],
        c_s[None],
        n_s.reshape(1, bs, N_HEADS, HEAD_DIM),
        m_s.reshape(1, bs, N_HEADS),
    )
```

```python
import functools

import jax
import jax.numpy as jnp
from jax import lax
from jax.experimental import pallas as pl
from jax.experimental.pallas import tpu as pltpu

D_MODEL = 2048
N_META = 16
POOL_W = 1024
MLSTM_W = 1024
POOL_WINDOWS = (2, 4, 8, 16)
N_POOL_GROUPS = 4
POOL_GW = 256
POOL_HIST = 15
N_HEADS = 4
HEAD_DIM = 256
D_FF = 5632
QKVO_W = 4 * MLSTM_W
MAIN_W = POOL_W + QKVO_W
PAST_LEN = 16384
EPS = 1e-6

LANES = 128
HIST_PAD = 16
VMEM_LIMIT = 56 * 1024 * 1024

BF16 = jnp.bfloat16
F32 = jnp.float32


def _params(sem):
    return pltpu.CompilerParams(dimension_semantics=sem, vmem_limit_bytes=VMEM_LIMIT)


def _rms(x, w):
    return x * lax.rsqrt(jnp.mean(x * x, axis=-1, keepdims=True) + EPS) * w


def _inproj_kernel(x_ref, nw_ref, w_ref, wg_ref, *rest, n_u_tiles, emit_copy, conv_blocks):
    nc = len(conv_blocks)
    conv_in, (u_ref, qkvo_ref, g_ref), rest = rest[:nc], rest[nc:nc + 3], rest[nc + 3:]
    conv_out, xn_ref = rest[len(rest) - 1 - nc:-1], rest[-1]
    j = pl.program_id(1)
    _convert_blocks(pl.program_id(0) * pl.num_programs(1) + j, conv_in, conv_out, conv_blocks)

    nt = (((1,), (1,)), ((), ()))

    @pl.when(j == 0)
    def _():
        xn = _rms(x_ref[...], nw_ref[...]).astype(BF16)
        xn_ref[...] = xn
        g_ref[...] = lax.dot_general(xn, wg_ref[...], nt, preferred_element_type=F32)

    w = w_ref[...]
    if emit_copy:
        w = w.astype(BF16)
        rest[0][...] = w
    p = lax.dot_general(xn_ref[...], w, nt, preferred_element_type=F32)

    @pl.when(j < n_u_tiles)
    def _():
        u_ref[...] = p

    @pl.when(j >= n_u_tiles)
    def _():
        qkvo_ref[...] = p.astype(BF16)


def _inproj(x, norm_w, w_t, w_gate_t, *, tm, tn, convert=()):
    m = x.shape[0]
    n_u = POOL_W // tn
    n_j = MAIN_W // tn
    emit_copy = w_t.dtype == F32
    if emit_copy:
        assert m == tm, "the bf16 copy is written once per column tile"
    w_spec = pl.BlockSpec((tn, D_MODEL), lambda i, j: (j, 0))
    copy_spec = [w_spec] if emit_copy else []
    copy_shape = [jax.ShapeDtypeStruct((MAIN_W, D_MODEL), BF16)] if emit_copy else []
    conv_specs, conv_shapes = _convert_specs(convert, (m // tm) * n_j, lambda i, j: i * n_j + j)
    return pl.pallas_call(
        functools.partial(_inproj_kernel, n_u_tiles=n_u, emit_copy=emit_copy,
                          conv_blocks=tuple(n for _, n in convert)),
        grid=(m // tm, n_j),
        in_specs=[
            pl.BlockSpec((tm, D_MODEL), lambda i, j: (i, 0)),
            pl.BlockSpec((1, D_MODEL), lambda i, j: (0, 0)),
            w_spec,
            pl.BlockSpec((LANES, D_MODEL), lambda i, j: (0, 0)),
        ] + conv_specs,
        out_specs=[
            pl.BlockSpec((tm, tn), lambda i, j: (i, jnp.minimum(j, n_u - 1))),
            pl.BlockSpec((tm, tn), lambda i, j: (i, jnp.maximum(j - n_u, 0))),
            pl.BlockSpec((tm, LANES), lambda i, j: (i, 0)),
        ] + copy_spec + conv_specs,
        out_shape=[
            jax.ShapeDtypeStruct((m, POOL_W), F32),
            jax.ShapeDtypeStruct((m, QKVO_W), BF16),
            jax.ShapeDtypeStruct((m, LANES), F32),
        ] + copy_shape + conv_shapes,
        scratch_shapes=[pltpu.VMEM((tm, D_MODEL), BF16)],
        compiler_params=_params(("arbitrary", "arbitrary")),
        name="inproj",
    )(x, norm_w, w_t, w_gate_t, *[w for w, _ in convert])


def _convert_specs(weights, n_steps, step_of):
    specs, shapes = [], []
    for w, n_blocks in weights:
        assert n_blocks <= n_steps
        rows = w.shape[0] // n_blocks
        specs.append(pl.BlockSpec(
            (rows, w.shape[1]), lambda *idx, n_blocks=n_blocks: (jnp.minimum(step_of(*idx), n_blocks - 1), 0)))
        shapes.append(jax.ShapeDtypeStruct(w.shape, BF16))
    return specs, shapes


def _convert_blocks(step, srcs, dsts, n_blocks):
    for src, dst, n in zip(srcs, dsts, n_blocks):
        @pl.when(step < n)
        def _(src=src, dst=dst):
            dst[...] = src[...].astype(BF16)


def _pool_kernel(u_ref, prev_ref, wp_ref, sc_ref, y_ref, st_ref, e_ref, d_ref, *, bb, tt, pos0):
    ti = pl.program_id(1)
    nt = pl.num_programs(1)

    @pl.when(ti == 0)
    def _():
        e_ref[:, 0:1, :] = jnp.zeros((bb, 1, POOL_W), F32)
        e_ref[:, 1:HIST_PAD, :] = prev_ref[...]

    e_ref[:, HIST_PAD:, :] = u_ref[...]

    pos = pos0 + ti * tt + lax.broadcasted_iota(jnp.int32, (tt, 1), 0)
    for b in range(bb):
        for g, w in enumerate(POOL_WINDOWS):
            cols = slice(g * POOL_GW, (g + 1) * POOL_GW)
            e = e_ref[b, :, cols]
            s = e + pltpu.roll(e, 1, axis=0)
            for k in range(1, g + 1):
                s = s + pltpu.roll(s, 2**k, axis=0)
            cnt = jnp.minimum(w, pos + 1).astype(F32)
            d_ref[b * tt:(b + 1) * tt, cols] = s[HIST_PAD:, :] / cnt - e[HIST_PAD:, :]

    for g in range(N_POOL_GROUPS):
        cols = slice(g * POOL_GW, (g + 1) * POOL_GW)
        y = jnp.dot(d_ref[:, cols].astype(BF16), wp_ref[g], preferred_element_type=F32) * sc_ref[:, cols]
        y_ref[:, :, cols] = y.reshape(bb, tt, POOL_GW).astype(BF16)

    @pl.when(ti == nt - 1)
    def _():
        st_ref[...] = e_ref[:, tt + 1:tt + HIST_PAD, :]

    e_ref[:, 0:HIST_PAD, :] = e_ref[:, tt:tt + HIST_PAD, :]


def _pool(u, prev, w_pool, scale, *, bb, tt, pos0, shared_prev):
    b, t, _ = u.shape
    prev_map = (lambda bi, ti: (0, 0, 0)) if shared_prev else (lambda bi, ti: (bi, 0, 0))
    return pl.pallas_call(
        functools.partial(_pool_kernel, bb=bb, tt=tt, pos0=pos0),
        grid=(b // bb, t // tt),
        in_specs=[
            pl.BlockSpec((bb, tt, POOL_W), lambda bi, ti: (bi, ti, 0)),
            pl.BlockSpec((bb, POOL_HIST, POOL_W), prev_map),
            pl.BlockSpec((N_POOL_GROUPS, POOL_GW, POOL_GW), lambda bi, ti: (0, 0, 0)),
            pl.BlockSpec((1, POOL_W), lambda bi, ti: (0, 0)),
        ],
        out_specs=[
            pl.BlockSpec((bb, tt, POOL_W), lambda bi, ti: (bi, ti, 0)),
            pl.BlockSpec((bb, POOL_HIST, POOL_W), lambda bi, ti: (bi, 0, 0)),
        ],
        out_shape=[
            jax.ShapeDtypeStruct((b, t, POOL_W), BF16),
            jax.ShapeDtypeStruct((b, POOL_HIST, POOL_W), F32),
        ],
        scratch_shapes=[
            pltpu.VMEM((bb, HIST_PAD + tt, POOL_W), F32),
            pltpu.VMEM((bb * tt, POOL_W), F32),
        ],
        compiler_params=_params(("arbitrary", "arbitrary")),
        name="pool",
    )(u, prev, w_pool, scale)


def _mlstm_chunk(q, k, v, o, gi, gf, c, n, m, gnorm, *, chunk):
    ti = lax.broadcasted_iota(jnp.int32, (chunk, chunk), 0)
    si = lax.broadcasted_iota(jnp.int32, (chunk, chunk), 1)
    tril = si <= ti
    eye = si == ti

    def to_row(col):
        return jnp.sum(jnp.where(eye, col, 0.0), axis=0, keepdims=True)

    lf = jax.nn.log_sigmoid(gf)
    b = jnp.sum(jnp.where(tril, to_row(lf), 0.0), axis=1, keepdims=True)
    a = gi - b
    a_row = to_row(a)
    cummax_a = jnp.max(jnp.where(tril, a_row, -jnp.inf), axis=1, keepdims=True)
    m_t = jnp.maximum(m + b, cummax_a + b)
    inter = jnp.exp(m + b - m_t)
    dmat = jnp.exp(jnp.where(tril, a_row + (b - m_t), -jnp.inf))

    k = k * (HEAD_DIM ** -0.5)
    s = lax.dot_general(q, k, (((1,), (1,)), ((), ())), preferred_element_type=F32) * dmat
    num = jnp.dot(s.astype(BF16), v, preferred_element_type=F32)
    num = num + inter * jnp.dot(q, c.astype(BF16), preferred_element_type=F32)
    qn = jnp.sum(s, axis=1, keepdims=True) + inter * jnp.sum(q.astype(F32) * n, axis=1, keepdims=True)
    h = num / jnp.maximum(jnp.abs(qn), jnp.exp(-m_t))

    b_last = b[chunk - 1:chunk, :]
    m_new = m_t[chunk - 1:chunk, :]
    decay = jnp.exp(m + b_last - m_new)
    w = jnp.exp(a + (b_last - m_new))
    wv = (w * v.astype(F32)).astype(BF16)
    c_new = decay * c + lax.dot_general(k, wv, (((0,), (0,)), ((), ())), preferred_element_type=F32)
    n_new = decay * n + jnp.sum(w * k.astype(F32), axis=0, keepdims=True)

    y = jax.nn.sigmoid(o.astype(F32)) * _rms(h, gnorm)
    return y, c_new, n_new, m_new


def _gate_cols(g, bias_ref, head):
    lane = lax.broadcasted_iota(jnp.int32, g.shape, 1)
    gi = jnp.sum(jnp.where(lane == head, g, 0.0), axis=1, keepdims=True) + bias_ref[head]
    gf = jnp.sum(jnp.where(lane == head + N_HEADS, g, 0.0), axis=1, keepdims=True) + bias_ref[head + N_HEADS]
    return gi, gf


N_STEP_IN = 9
N_STEP_OUT = 4


def _mlstm_seq_kernel(bias_ref, q_ref, k_ref, v_ref, o_ref, g_ref, gn_ref, c0_ref, n0_ref, m0_ref, *rest,
                      chunk, conv_blocks, step):
    nc = len(conv_blocks)
    ns_in = N_STEP_IN if step else 0
    conv_in, step_in = rest[:nc], rest[nc:nc + ns_in]
    outs = rest[nc + ns_in:]
    (y_ref, c_ref, n_ref, m_ref), conv_out, step_out = outs[:4], outs[4:4 + nc], outs[4 + nc:]
    ci = pl.program_id(1)
    grid_step = pl.program_id(0) * pl.num_programs(1) + ci
    _convert_blocks(grid_step, conv_in, conv_out, conv_blocks)

    @pl.when(ci == 0)
    def _():
        c_ref[...] = c0_ref[...]
        n_ref[...] = n0_ref[...]
        m_ref[...] = m0_ref[...]

    stages = iter(())
    if step:
        stages = _mlstm_step_stages(grid_step % N_HEADS, bias_ref, *step_in, *step_out, bb=step[0], seq=step[1])
    next(stages, None)

    g = g_ref[...]
    for head in range(N_HEADS):
        cols = slice(head * HEAD_DIM, (head + 1) * HEAD_DIM)
        gi, gf = _gate_cols(g, bias_ref, head)
        y, c_new, n_new, m_new = _mlstm_chunk(
            q_ref[:, cols], k_ref[:, cols], v_ref[:, cols], o_ref[:, cols], gi, gf,
            c_ref[0, head], n_ref[0, head], m_ref[0, head], gn_ref[:, cols], chunk=chunk)
        y_ref[:, cols] = y.astype(BF16)
        c_ref[0, head] = c_new
        n_ref[0, head] = n_new
        m_ref[0, head] = m_new
        next(stages, None)
    for _ in stages:
        pass


def _mlstm_seq(qkvo, gates, bias, gnorm, c0, n0, m0, *, batch, seq, chunk, shared_init, convert=(), step=None):
    nc = seq // chunk
    rows = lambda bi, ci: bi * nc + ci
    conv_specs, conv_shapes = _convert_specs(convert, batch * nc, rows)
    step_args, step_in, step_out, step_shapes, step_cfg = [], [], [], [], None
    if step:
        (s_qkvo, s_gates, s_c0, s_n0, s_m0), s_batch, s_seq, s_bb = step
        step_in, step_out, step_shapes = _mlstm_step_specs(s_batch, s_seq, s_bb, batch * nc, rows)
        step_args = [s_qkvo, s_qkvo, s_qkvo, s_qkvo, s_gates, gnorm, s_c0, s_n0, s_m0]
        step_cfg = (s_bb, s_seq)
    st_b = (lambda bi: 0) if shared_init else (lambda bi: bi)

    def col_spec(group):
        return pl.BlockSpec((chunk, MLSTM_W), lambda bi, ci: (rows(bi, ci), group))

    def st_spec(shape, in_b):
        nd = len(shape)
        return pl.BlockSpec((1, N_HEADS) + shape, lambda bi, ci: (in_b(bi), 0) + (0,) * nd)

    ident = lambda bi: bi
    return pl.pallas_call(
        functools.partial(_mlstm_seq_kernel, chunk=chunk, conv_blocks=tuple(n for _, n in convert), step=step_cfg),
        grid=(batch, nc),
        in_specs=[
            pl.BlockSpec(memory_space=pltpu.SMEM),
            col_spec(0), col_spec(1), col_spec(2), col_spec(3),
            pl.BlockSpec((chunk, LANES), lambda bi, ci: (rows(bi, ci), 0)),
            pl.BlockSpec((1, MLSTM_W), lambda bi, ci: (0, 0)),
            st_spec((HEAD_DIM, HEAD_DIM), st_b), st_spec((1, HEAD_DIM), st_b), st_spec((1, 1), st_b),
        ] + conv_specs + step_in,
        out_specs=[
            pl.BlockSpec((chunk, MLSTM_W), lambda bi, ci: (rows(bi, ci), 0)),
            st_spec((HEAD_DIM, HEAD_DIM), ident), st_spec((1, HEAD_DIM), ident), st_spec((1, 1), ident),
        ] + conv_specs + step_out,
        out_shape=[
            jax.ShapeDtypeStruct((batch * seq, MLSTM_W), BF16),
            jax.ShapeDtypeStruct((batch, N_HEADS, HEAD_DIM, HEAD_DIM), F32),
            jax.ShapeDtypeStruct((batch, N_HEADS, 1, HEAD_DIM), F32),
            jax.ShapeDtypeStruct((batch, N_HEADS, 1, 1), F32),
        ] + conv_shapes + step_shapes,
        compiler_params=_params(("arbitrary", "arbitrary")),
        name="mlstm_seq",
    )(bias, qkvo, qkvo, qkvo, qkvo, gates, gnorm, c0, n0, m0, *[w for w, _ in convert], *step_args)


def _mlstm_step_stages(head, bias_ref, q_ref, k_ref, v_ref, o_ref, g_ref, gn_ref, c0_ref, n0_ref, m0_ref,
                       y_ref, c_ref, n_ref, m_ref, *, bb, seq):
    rows = bb * seq
    ti = lax.broadcasted_iota(jnp.int32, (rows, rows), 0)
    si = lax.broadcasted_iota(jnp.int32, (rows, rows), 1)
    same = (ti // seq) == (si // seq)
    tril = same & (si <= ti)
    eye = si == ti

    def to_row(col):
        return jnp.sum(jnp.where(eye, col, 0.0), axis=0, keepdims=True)

    gi, gf = _gate_cols(g_ref[...], bias_ref, head)
    lane = lax.broadcasted_iota(jnp.int32, m0_ref.shape, 1)
    m0 = jnp.sum(jnp.where(lane == head, m0_ref[...], 0.0), axis=1, keepdims=True)

    lf_row = to_row(jax.nn.log_sigmoid(gf))
    b = jnp.sum(jnp.where(tril, lf_row, 0.0), axis=1, keepdims=True)
    b_last = jnp.sum(jnp.where(same, lf_row, 0.0), axis=1, keepdims=True)
    a = gi - b
    a_row = to_row(a)
    cummax_a = jnp.max(jnp.where(tril, a_row, -jnp.inf), axis=1, keepdims=True)
    seqmax_a = jnp.max(jnp.where(same, a_row, -jnp.inf), axis=1, keepdims=True)
    m_t = jnp.maximum(m0 + b, cummax_a + b)
    m_new = jnp.maximum(m0 + b_last, seqmax_a + b_last)
    inter = jnp.exp(m0 + b - m_t)
    dmat = jnp.exp(jnp.where(tril, a_row + (b - m_t), -jnp.inf))
    decay = jnp.exp(m0 + b_last - m_new)
    w = jnp.exp(a + (b_last - m_new))

    q = q_ref[...]
    k = k_ref[...] * (HEAD_DIM ** -0.5)
    v = v_ref[...]
    s = lax.dot_general(q, k, (((1,), (1,)), ((), ())), preferred_element_type=F32) * dmat
    sv = jnp.dot(s.astype(BF16), v, preferred_element_type=F32)

    qf = q.astype(F32)
    kf = k.astype(F32)
    wk = w * kf
    wv = w * v.astype(F32)
    qc, qdn = [], []
    yield
    for i in range(bb):
        r = slice(i * seq, (i + 1) * seq)
        c_i = c0_ref[i, 0]
        n_i = n0_ref[i:i + 1, :]
        d_i = decay[i * seq:i * seq + 1, :]
        qc.append(jnp.dot(qf[r].astype(BF16), c_i.astype(BF16), preferred_element_type=F32))
        qdn.append(jnp.sum(qf[r] * n_i, axis=1, keepdims=True))
        upd = lax.dot_general(kf[r].astype(BF16), wv[r].astype(BF16), (((0,), (0,)), ((), ())),
                              preferred_element_type=F32)
        c_ref[i, 0] = d_i * c_i + upd
        n_ref[i:i + 1, :] = d_i * n_i + jnp.sum(wk[r], axis=0, keepdims=True)
        if (i + 1) % (bb // N_HEADS) == 0:
            yield

    num = sv + inter * jnp.concatenate(qc, axis=0)
    qn = jnp.sum(s, axis=1, keepdims=True) + inter * jnp.concatenate(qdn, axis=0)
    h = num / jnp.maximum(jnp.abs(qn), jnp.exp(-m_t))
    y_ref[...] = (jax.nn.sigmoid(o_ref[...].astype(F32)) * _rms(h, gn_ref[...])).astype(BF16)
    m_ref[0, 0] = to_row(m_new)


def _mlstm_step_specs(batch, seq, bb, n_steps, step_of):
    assert (batch // bb) * N_HEADS == n_steps
    rows = bb * seq
    blk = lambda *idx: step_of(*idx) // N_HEADS
    head = lambda *idx: step_of(*idx) % N_HEADS

    def col_spec(off):
        return pl.BlockSpec((rows, HEAD_DIM), lambda *idx: (blk(*idx), off + head(*idx)))

    c_spec = pl.BlockSpec((bb, 1, HEAD_DIM, HEAD_DIM), lambda *idx: (blk(*idx), head(*idx), 0, 0))
    n_spec = pl.BlockSpec((bb, HEAD_DIM), lambda *idx: (blk(*idx), head(*idx)))
    in_specs = [
        col_spec(0), col_spec(N_HEADS), col_spec(2 * N_HEADS), col_spec(3 * N_HEADS),
        pl.BlockSpec((rows, LANES), lambda *idx: (blk(*idx), 0)),
        pl.BlockSpec((1, HEAD_DIM), lambda *idx: (0, head(*idx))),
        c_spec, n_spec,
        pl.BlockSpec((rows, N_HEADS), lambda *idx: (blk(*idx), 0)),
    ]
    out_specs = [
        pl.BlockSpec((rows, HEAD_DIM), lambda *idx: (blk(*idx), head(*idx))),
        c_spec, n_spec,
        pl.BlockSpec((1, 1, 1, rows), lambda *idx: (blk(*idx), head(*idx), 0, 0)),
    ]
    out_shapes = [
        jax.ShapeDtypeStruct((batch * seq, MLSTM_W), BF16),
        jax.ShapeDtypeStruct((batch, N_HEADS, HEAD_DIM, HEAD_DIM), F32),
        jax.ShapeDtypeStruct((batch, MLSTM_W), F32),
        jax.ShapeDtypeStruct((batch // bb, N_HEADS, 1, rows), F32),
    ]
    return in_specs, out_specs, out_shapes


def _outproj_kernel(x_ref, yp_ref, ym_ref, wo_ref, nw_ref, x2_ref, xn2_ref):
    x2 = x_ref[...] + jnp.dot(yp_ref[...], wo_ref[0:POOL_W, :], preferred_element_type=F32)
    x2 = x2 + jnp.dot(ym_ref[...], wo_ref[POOL_W:, :], preferred_element_type=F32)
    x2_ref[...] = x2
    xn2_ref[...] = _rms(x2, nw_ref[...]).astype(BF16)


def _outproj(x, y_pool, y_ml, w_out, norm_w, *, tm):
    m = x.shape[0]
    row = lambda w: pl.BlockSpec((tm, w), lambda i: (i, 0))
    return pl.pallas_call(
        _outproj_kernel,
        grid=(m // tm,),
        in_specs=[
            row(D_MODEL), row(POOL_W), row(MLSTM_W),
            pl.BlockSpec((D_MODEL, D_MODEL), lambda i: (0, 0)),
            pl.BlockSpec((1, D_MODEL), lambda i: (0, 0)),
        ],
        out_specs=[row(D_MODEL), row(D_MODEL)],
        out_shape=[
            jax.ShapeDtypeStruct((m, D_MODEL), F32),
            jax.ShapeDtypeStruct((m, D_MODEL), BF16),
        ],
        compiler_params=_params(("arbitrary",)),
        name="outproj",
    )(x, y_pool, y_ml, w_out, norm_w)


def _ffn_kernel(xn_ref, x2_ref, wg_ref, wu_ref, wd_ref, nw_ref, y_ref, acc_ref):
    f = pl.program_id(1)
    nf = pl.num_programs(1)

    @pl.when(f == 0)
    def _():
        acc_ref[...] = x2_ref[...]

    xn = xn_ref[...]
    g = jnp.dot(xn, wg_ref[...], preferred_element_type=F32)
    u = jnp.dot(xn, wu_ref[...], preferred_element_type=F32)
    h = (jax.nn.silu(g) * u).astype(BF16)
    acc_ref[...] += jnp.dot(h, wd_ref[...], preferred_element_type=F32)

    @pl.when(f == nf - 1)
    def _():
        y_ref[...] = _rms(acc_ref[...], nw_ref[...])


def _ffn(xn2, x2, w_gate, w_up, w_down, norm_w, *, tm, tf):
    m = xn2.shape[0]
    return pl.pallas_call(
        _ffn_kernel,
        grid=(m // tm, D_FF // tf),
        in_specs=[
            pl.BlockSpec((tm, D_MODEL), lambda i, f: (i, 0)),
            pl.BlockSpec((tm, D_MODEL), lambda i, f: (i, 0)),
            pl.BlockSpec((D_MODEL, tf), lambda i, f: (0, f)),
            pl.BlockSpec((D_MODEL, tf), lambda i, f: (0, f)),
            pl.BlockSpec((tf, D_MODEL), lambda i, f: (f, 0)),
            pl.BlockSpec((1, D_MODEL), lambda i, f: (0, 0)),
        ],
        out_specs=pl.BlockSpec((tm, D_MODEL), lambda i, f: (i, 0)),
        out_shape=jax.ShapeDtypeStruct((m, D_MODEL), F32),
        scratch_shapes=[pltpu.VMEM((tm, D_MODEL), F32)],
        compiler_params=_params(("arbitrary", "arbitrary")),
        name="ffn",
    )(xn2, x2, w_gate, w_up, w_down, norm_w)


def kernel(x_prompt, x_sample, state_pool, state_mlstm_C, state_mlstm_n, state_mlstm_m, meta_tokens, norm_mix_w, w_in, b_igate, b_fgate, w_pool, pool_scale, mlstm_norm_w, w_out, norm_ffn_w, w_gate, w_up, w_down, norm_final_w):
    bp, tp, _ = x_prompt.shape
    bs, ts, _ = x_sample.shape

    w_in_t = w_in[0].T
    w_g8 = jnp.pad(w_in_t[MAIN_W:], ((0, LANES - 2 * N_HEADS), (0, 0))).astype(BF16)
    wp = w_pool[0].astype(BF16)
    nmix = norm_mix_w[0].reshape(1, D_MODEL)
    nffn = norm_ffn_w[0].reshape(1, D_MODEL)
    nfin = norm_final_w.reshape(1, D_MODEL)
    scale = pool_scale[0].reshape(1, POOL_W)
    gnorm = mlstm_norm_w[0].reshape(1, MLSTM_W)
    bias = jnp.concatenate([b_igate[0], b_fgate[0]]).astype(F32)

    xp = x_prompt.reshape(bp * tp, D_MODEL)
    xs = x_sample.reshape(bs * ts, D_MODEL)

    u_m, qkvo_m, g_m, w_main = _inproj(meta_tokens, nmix, w_in_t, w_g8, tm=N_META, tn=512)
    u_p, qkvo_p, g_p, wu = _inproj(xp, nmix, w_main, w_g8, tm=1024, tn=1024, convert=((w_up[0], 32),))
    u_s, qkvo_s, g_s = _inproj(xs, nmix, w_main, w_g8, tm=1024, tn=1024)

    prev_p = u_m[1:N_META].reshape(1, POOL_HIST, POOL_W)
    yp_p, pool_p = _pool(u_p.reshape(bp, tp, POOL_W), prev_p, wp, scale, bb=1, tt=512, pos0=N_META, shared_prev=True)
    yp_s, pool_s = _pool(u_s.reshape(bs, ts, POOL_W), state_pool[0], wp, scale, bb=16, tt=ts, pos0=PAST_LEN, shared_prev=False)

    zc = jnp.zeros((1, N_HEADS, HEAD_DIM, HEAD_DIM), F32)
    zn = jnp.zeros((1, N_HEADS, 1, HEAD_DIM), F32)
    zm = jnp.zeros((1, N_HEADS, 1, 1), F32)
    _, c_m, n_m, m_m = _mlstm_seq(qkvo_m, g_m, bias, gnorm, zc, zn, zm, batch=1, seq=N_META, chunk=N_META, shared_init=False)
    convert = ((w_out[0], 32), (w_gate[0], 32), (w_down[0], 32))
    step_bb = 16
    step_ops = (qkvo_s, g_s, state_mlstm_C[0], state_mlstm_n[0].reshape(bs, MLSTM_W),
                jnp.repeat(state_mlstm_m[0], ts, axis=0))
    ym_p, c_p, n_p, m_p, wo, wg, wd, ym_s, c_s, n_s, m_s = _mlstm_seq(
        qkvo_p, g_p, bias, gnorm, c_m, n_m, m_m, batch=bp, seq=tp, chunk=256, shared_init=True, convert=convert,
        step=(step_ops, bs, ts, step_bb))
    m_s = m_s.reshape(bs // step_bb, N_HEADS, step_bb, ts)[..., 0].transpose(0, 2, 1)

    x2_p, xn2_p = _outproj(xp, yp_p.reshape(bp * tp, POOL_W), ym_p, wo, nffn, tm=512)
    x2_s, xn2_s = _outproj(xs, yp_s.reshape(bs * ts, POOL_W), ym_s, wo, nffn, tm=512)
    y_p = _ffn(xn2_p, x2_p, wg, wu, wd, nfin, tm=512, tf=512)
    y_s = _ffn(xn2_s, x2_s, wg, wu, wd, nfin, tm=512, tf=512)

    return (
        y_p.reshape(bp, tp, D_MODEL),
        y_s.reshape(bs, ts, D_MODEL),
        pool_p[None],
        c_p[None],
        n_p.reshape(1, bp, N_HEADS, HEAD_DIM),
        m_p.reshape(1, bp, N_HEADS),
        pool_s[None],
        c_s[None],
        n_s.reshape(1, bs, N_HEADS, HEAD_DIM),
        m_s.reshape(1, bs, N_HEADS),
    )
```

```python
import functools

import jax
import jax.numpy as jnp
from jax import lax
from jax.experimental import pallas as pl
from jax.experimental.pallas import tpu as pltpu

D_MODEL = 2048
N_META = 16
POOL_W = 1024
MLSTM_W = 1024
POOL_WINDOWS = (2, 4, 8, 16)
N_POOL_GROUPS = 4
POOL_GW = 256
POOL_HIST = 15
N_HEADS = 4
HEAD_DIM = 256
D_FF = 5632
QKVO_W = 4 * MLSTM_W
MAIN_W = POOL_W + QKVO_W
PAST_LEN = 16384
EPS = 1e-6

LANES = 128
HIST_PAD = 16
VMEM_LIMIT = 56 * 1024 * 1024

BF16 = jnp.bfloat16
F32 = jnp.float32


def _params(sem):
    return pltpu.CompilerParams(dimension_semantics=sem, vmem_limit_bytes=VMEM_LIMIT)


def _rms(x, w):
    return x * lax.rsqrt(jnp.mean(x * x, axis=-1, keepdims=True) + EPS) * w


def _inproj_kernel(x_ref, nw_ref, w_ref, wg_ref, *rest, n_u_tiles, emit_copy, conv_blocks):
    nc = len(conv_blocks)
    conv_in, (u_ref, qkvo_ref, g_ref), rest = rest[:nc], rest[nc:nc + 3], rest[nc + 3:]
    conv_out, xn_ref = rest[len(rest) - 1 - nc:-1], rest[-1]
    j = pl.program_id(1)
    _convert_blocks(pl.program_id(0) * pl.num_programs(1) + j, conv_in, conv_out, conv_blocks)

    nt = (((1,), (1,)), ((), ()))

    @pl.when(j == 0)
    def _():
        xn = _rms(x_ref[...], nw_ref[...]).astype(BF16)
        xn_ref[...] = xn
        g_ref[...] = lax.dot_general(xn, wg_ref[...], nt, preferred_element_type=F32)

    w = w_ref[...]
    if emit_copy:
        w = w.astype(BF16)
        rest[0][...] = w
    p = lax.dot_general(xn_ref[...], w, nt, preferred_element_type=F32)

    @pl.when(j < n_u_tiles)
    def _():
        u_ref[...] = p

    @pl.when(j >= n_u_tiles)
    def _():
        qkvo_ref[...] = p.astype(BF16)


def _inproj(x, norm_w, w_t, w_gate_t, *, tm, tn, convert=()):
    m = x.shape[0]
    n_u = POOL_W // tn
    n_j = MAIN_W // tn
    emit_copy = w_t.dtype == F32
    if emit_copy:
        assert m == tm, "the bf16 copy is written once per column tile"
    w_spec = pl.BlockSpec((tn, D_MODEL), lambda i, j: (j, 0))
    copy_spec = [w_spec] if emit_copy else []
    copy_shape = [jax.ShapeDtypeStruct((MAIN_W, D_MODEL), BF16)] if emit_copy else []
    conv_specs, conv_shapes = _convert_specs(convert, (m // tm) * n_j, lambda i, j: i * n_j + j)
    return pl.pallas_call(
        functools.partial(_inproj_kernel, n_u_tiles=n_u, emit_copy=emit_copy,
                          conv_blocks=tuple(n for _, n in convert)),
        grid=(m // tm, n_j),
        in_specs=[
            pl.BlockSpec((tm, D_MODEL), lambda i, j: (i, 0)),
            pl.BlockSpec((1, D_MODEL), lambda i, j: (0, 0)),
            w_spec,
            pl.BlockSpec((LANES, D_MODEL), lambda i, j: (0, 0)),
        ] + conv_specs,
        out_specs=[
            pl.BlockSpec((tm, tn), lambda i, j: (i, jnp.minimum(j, n_u - 1))),
            pl.BlockSpec((tm, tn), lambda i, j: (i, jnp.maximum(j - n_u, 0))),
            pl.BlockSpec((tm, LANES), lambda i, j: (i, 0)),
        ] + copy_spec + conv_specs,
        out_shape=[
            jax.ShapeDtypeStruct((m, POOL_W), F32),
            jax.ShapeDtypeStruct((m, QKVO_W), BF16),
            jax.ShapeDtypeStruct((m, LANES), F32),
        ] + copy_shape + conv_shapes,
        scratch_shapes=[pltpu.VMEM((tm, D_MODEL), BF16)],
        compiler_params=_params(("arbitrary", "arbitrary")),
        name="inproj",
    )(x, norm_w, w_t, w_gate_t, *[w for w, _ in convert])


def _convert_specs(weights, n_steps, step_of):
    specs, shapes = [], []
    for w, n_blocks in weights:
        assert n_blocks <= n_steps
        rows = w.shape[0] // n_blocks
        specs.append(pl.BlockSpec(
            (rows, w.shape[1]), lambda *idx, n_blocks=n_blocks: (jnp.minimum(step_of(*idx), n_blocks - 1), 0)))
        shapes.append(jax.ShapeDtypeStruct(w.shape, BF16))
    return specs, shapes


def _convert_blocks(step, srcs, dsts, n_blocks):
    for src, dst, n in zip(srcs, dsts, n_blocks):
        @pl.when(step < n)
        def _(src=src, dst=dst):
            dst[...] = src[...].astype(BF16)


def _pool_kernel(u_ref, prev_ref, wp_ref, sc_ref, y_ref, st_ref, e_ref, d_ref, *, bb, tt, pos0):
    ti = pl.program_id(1)
    nt = pl.num_programs(1)

    @pl.when(ti == 0)
    def _():
        e_ref[:, 0:1, :] = jnp.zeros((bb, 1, POOL_W), F32)
        e_ref[:, 1:HIST_PAD, :] = prev_ref[...]

    e_ref[:, HIST_PAD:, :] = u_ref[...]

    pos = pos0 + ti * tt + lax.broadcasted_iota(jnp.int32, (tt, 1), 0)
    for b in range(bb):
        for g, w in enumerate(POOL_WINDOWS):
            cols = slice(g * POOL_GW, (g + 1) * POOL_GW)
            e = e_ref[b, :, cols]
            s = e + pltpu.roll(e, 1, axis=0)
            for k in range(1, g + 1):
                s = s + pltpu.roll(s, 2**k, axis=0)
            cnt = jnp.minimum(w, pos + 1).astype(F32)
            d_ref[b * tt:(b + 1) * tt, cols] = s[HIST_PAD:, :] / cnt - e[HIST_PAD:, :]

    for g in range(N_POOL_GROUPS):
        cols = slice(g * POOL_GW, (g + 1) * POOL_GW)
        y = jnp.dot(d_ref[:, cols].astype(BF16), wp_ref[g], preferred_element_type=F32) * sc_ref[:, cols]
        y_ref[:, :, cols] = y.reshape(bb, tt, POOL_GW).astype(BF16)

    @pl.when(ti == nt - 1)
    def _():
        st_ref[...] = e_ref[:, tt + 1:tt + HIST_PAD, :]

    e_ref[:, 0:HIST_PAD, :] = e_ref[:, tt:tt + HIST_PAD, :]


def _pool(u, prev, w_pool, scale, *, bb, tt, pos0, shared_prev):
    b, t, _ = u.shape
    prev_map = (lambda bi, ti: (0, 0, 0)) if shared_prev else (lambda bi, ti: (bi, 0, 0))
    return pl.pallas_call(
        functools.partial(_pool_kernel, bb=bb, tt=tt, pos0=pos0),
        grid=(b // bb, t // tt),
        in_specs=[
            pl.BlockSpec((bb, tt, POOL_W), lambda bi, ti: (bi, ti, 0)),
            pl.BlockSpec((bb, POOL_HIST, POOL_W), prev_map),
            pl.BlockSpec((N_POOL_GROUPS, POOL_GW, POOL_GW), lambda bi, ti: (0, 0, 0)),
            pl.BlockSpec((1, POOL_W), lambda bi, ti: (0, 0)),
        ],
        out_specs=[
            pl.BlockSpec((bb, tt, POOL_W), lambda bi, ti: (bi, ti, 0)),
            pl.BlockSpec((bb, POOL_HIST, POOL_W), lambda bi, ti: (bi, 0, 0)),
        ],
        out_shape=[
            jax.ShapeDtypeStruct((b, t, POOL_W), BF16),
            jax.ShapeDtypeStruct((b, POOL_HIST, POOL_W), F32),
        ],
        scratch_shapes=[
            pltpu.VMEM((bb, HIST_PAD + tt, POOL_W), F32),
            pltpu.VMEM((bb * tt, POOL_W), F32),
        ],
        compiler_params=_params(("arbitrary", "arbitrary")),
        name="pool",
    )(u, prev, w_pool, scale)


def _mlstm_chunk(q, k, v, o, gi, gf, c, n, m, gnorm, *, chunk):
    ti = lax.broadcasted_iota(jnp.int32, (chunk, chunk), 0)
    si = lax.broadcasted_iota(jnp.int32, (chunk, chunk), 1)
    tril = si <= ti
    eye = si == ti

    def to_row(col):
        return jnp.sum(jnp.where(eye, col, 0.0), axis=0, keepdims=True)

    lf = jax.nn.log_sigmoid(gf)
    b = jnp.sum(jnp.where(tril, to_row(lf), 0.0), axis=1, keepdims=True)
    a = gi - b
    a_row = to_row(a)
    cummax_a = jnp.max(jnp.where(tril, a_row, -jnp.inf), axis=1, keepdims=True)
    m_t = jnp.maximum(m + b, cummax_a + b)
    inter = jnp.exp(m + b - m_t)
    dmat = jnp.exp(jnp.where(tril, a_row + (b - m_t), -jnp.inf))

    k = k * (HEAD_DIM ** -0.5)
    s = lax.dot_general(q, k, (((1,), (1,)), ((), ())), preferred_element_type=F32) * dmat
    num = jnp.dot(s.astype(BF16), v, preferred_element_type=F32)
    num = num + inter * jnp.dot(q, c.astype(BF16), preferred_element_type=F32)
    qn = jnp.sum(s, axis=1, keepdims=True) + inter * jnp.sum(q.astype(F32) * n, axis=1, keepdims=True)
    h = num / jnp.maximum(jnp.abs(qn), jnp.exp(-m_t))

    b_last = b[chunk - 1:chunk, :]
    m_new = m_t[chunk - 1:chunk, :]
    decay = jnp.exp(m + b_last - m_new)
    w = jnp.exp(a + (b_last - m_new))
    wv = (w * v.astype(F32)).astype(BF16)
    c_new = decay * c + lax.dot_general(k, wv, (((0,), (0,)), ((), ())), preferred_element_type=F32)
    n_new = decay * n + jnp.sum(w * k.astype(F32), axis=0, keepdims=True)

    y = jax.nn.sigmoid(o.astype(F32)) * _rms(h, gnorm)
    return y, c_new, n_new, m_new


def _gate_cols(g, bias_ref, head):
    lane = lax.broadcasted_iota(jnp.int32, g.shape, 1)
    gi = jnp.sum(jnp.where(lane == head, g, 0.0), axis=1, keepdims=True) + bias_ref[head]
    gf = jnp.sum(jnp.where(lane == head + N_HEADS, g, 0.0), axis=1, keepdims=True) + bias_ref[head + N_HEADS]
    return gi, gf


N_STEP_IN = 9
N_STEP_OUT = 4


def _mlstm_seq_kernel(bias_ref, q_ref, k_ref, v_ref, o_ref, g_ref, gn_ref, c0_ref, n0_ref, m0_ref, *rest,
                      chunk, conv_blocks, step):
    nc = len(conv_blocks)
    ns_in = N_STEP_IN if step else 0
    conv_in, step_in = rest[:nc], rest[nc:nc + ns_in]
    outs = rest[nc + ns_in:]
    (y_ref, c_ref, n_ref, m_ref), conv_out, step_out = outs[:4], outs[4:4 + nc], outs[4 + nc:]
    ci = pl.program_id(1)
    grid_step = pl.program_id(0) * pl.num_programs(1) + ci
    _convert_blocks(grid_step, conv_in, conv_out, conv_blocks)

    @pl.when(ci == 0)
    def _():
        c_ref[...] = c0_ref[...]
        n_ref[...] = n0_ref[...]
        m_ref[...] = m0_ref[...]

    stages = iter(())
    if step:
        stages = _mlstm_step_stages(grid_step % N_HEADS, bias_ref, *step_in, *step_out, bb=step[0], seq=step[1])
    next(stages, None)

    g = g_ref[...]
    for head in range(N_HEADS):
        cols = slice(head * HEAD_DIM, (head + 1) * HEAD_DIM)
        gi, gf = _gate_cols(g, bias_ref, head)
        y, c_new, n_new, m_new = _mlstm_chunk(
            q_ref[:, cols], k_ref[:, cols], v_ref[:, cols], o_ref[:, cols], gi, gf,
            c_ref[0, head], n_ref[0, head], m_ref[0, head], gn_ref[:, cols], chunk=chunk)
        y_ref[:, cols] = y.astype(BF16)
        c_ref[0, head] = c_new
        n_ref[0, head] = n_new
        m_ref[0, head] = m_new
        next(stages, None)
    for _ in stages:
        pass


def _mlstm_seq(qkvo, gates, bias, gnorm, c0, n0, m0, *, batch, seq, chunk, shared_init, convert=(), step=None):
    nc = seq // chunk
    rows = lambda bi, ci: bi * nc + ci
    conv_specs, conv_shapes = _convert_specs(convert, batch * nc, rows)
    step_args, step_in, step_out, step_shapes, step_cfg = [], [], [], [], None
    if step:
        (s_qkvo, s_gates, s_c0, s_n0, s_m0), s_batch, s_seq, s_bb = step
        step_in, step_out, step_shapes = _mlstm_step_specs(s_batch, s_seq, s_bb, batch * nc, rows)
        step_args = [s_qkvo, s_qkvo, s_qkvo, s_qkvo, s_gates, gnorm, s_c0, s_n0, s_m0]
        step_cfg = (s_bb, s_seq)
    st_b = (lambda bi: 0) if shared_init else (lambda bi: bi)

    def col_spec(group):
        return pl.BlockSpec((chunk, MLSTM_W), lambda bi, ci: (rows(bi, ci), group))

    def st_spec(shape, in_b):
        nd = len(shape)
        return pl.BlockSpec((1, N_HEADS) + shape, lambda bi, ci: (in_b(bi), 0) + (0,) * nd)

    ident = lambda bi: bi
    return pl.pallas_call(
        functools.partial(_mlstm_seq_kernel, chunk=chunk, conv_blocks=tuple(n for _, n in convert), step=step_cfg),
        grid=(batch, nc),
        in_specs=[
            pl.BlockSpec(memory_space=pltpu.SMEM),
            col_spec(0), col_spec(1), col_spec(2), col_spec(3),
            pl.BlockSpec((chunk, LANES), lambda bi, ci: (rows(bi, ci), 0)),
            pl.BlockSpec((1, MLSTM_W), lambda bi, ci: (0, 0)),
            st_spec((HEAD_DIM, HEAD_DIM), st_b), st_spec((1, HEAD_DIM), st_b), st_spec((1, 1), st_b),
        ] + conv_specs + step_in,
        out_specs=[
            pl.BlockSpec((chunk, MLSTM_W), lambda bi, ci: (rows(bi, ci), 0)),
            st_spec((HEAD_DIM, HEAD_DIM), ident), st_spec((1, HEAD_DIM), ident), st_spec((1, 1), ident),
        ] + conv_specs + step_out,
        out_shape=[
            jax.ShapeDtypeStruct((batch * seq, MLSTM_W), BF16),
            jax.ShapeDtypeStruct((batch, N_HEADS, HEAD_DIM, HEAD_DIM), F32),
            jax.ShapeDtypeStruct((batch, N_HEADS, 1, HEAD_DIM), F32),
            jax.ShapeDtypeStruct((batch, N_HEADS, 1, 1), F32),
        ] + conv_shapes + step_shapes,
        compiler_params=_params(("arbitrary", "arbitrary")),
        name="mlstm_seq",
    )(bias, qkvo, qkvo, qkvo, qkvo, gates, gnorm, c0, n0, m0, *[w for w, _ in convert], *step_args)


def _mlstm_step_stages(head, bias_ref, q_ref, k_ref, v_ref, o_ref, g_ref, gn_ref, c0_ref, n0_ref, m0_ref,
                       y_ref, c_ref, n_ref, m_ref, *, bb, seq):
    rows = bb * seq
    ti = lax.broadcasted_iota(jnp.int32, (rows, rows), 0)
    si = lax.broadcasted_iota(jnp.int32, (rows, rows), 1)
    same = (ti // seq) == (si // seq)
    tril = same & (si <= ti)
    eye = si == ti

    def to_row(col):
        return jnp.sum(jnp.where(eye, col, 0.0), axis=0, keepdims=True)

    gi, gf = _gate_cols(g_ref[...], bias_ref, head)
    lane = lax.broadcasted_iota(jnp.int32, m0_ref.shape, 1)
    m0 = jnp.sum(jnp.where(lane == head, m0_ref[...], 0.0), axis=1, keepdims=True)

    lf_row = to_row(jax.nn.log_sigmoid(gf))
    b = jnp.sum(jnp.where(tril, lf_row, 0.0), axis=1, keepdims=True)
    b_last = jnp.sum(jnp.where(same, lf_row, 0.0), axis=1, keepdims=True)
    a = gi - b
    a_row = to_row(a)
    cummax_a = jnp.max(jnp.where(tril, a_row, -jnp.inf), axis=1, keepdims=True)
    seqmax_a = jnp.max(jnp.where(same, a_row, -jnp.inf), axis=1, keepdims=True)
    m_t = jnp.maximum(m0 + b, cummax_a + b)
    m_new = jnp.maximum(m0 + b_last, seqmax_a + b_last)
    inter = jnp.exp(m0 + b - m_t)
    dmat = jnp.exp(jnp.where(tril, a_row + (b - m_t), -jnp.inf))
    decay = jnp.exp(m0 + b_last - m_new)
    w = jnp.exp(a + (b_last - m_new))

    q = q_ref[...]
    k = k_ref[...] * (HEAD_DIM ** -0.5)
    v = v_ref[...]
    s = lax.dot_general(q, k, (((1,), (1,)), ((), ())), preferred_element_type=F32) * dmat
    sv = jnp.dot(s.astype(BF16), v, preferred_element_type=F32)

    qf = q.astype(F32)
    kf = k.astype(F32)
    wk = w * kf
    wv = w * v.astype(F32)
    qc, qdn = [], []
    yield
    for i in range(bb):
        r = slice(i * seq, (i + 1) * seq)
        c_i = c0_ref[i, 0]
        n_i = n0_ref[i:i + 1, :]
        d_i = decay[i * seq:i * seq + 1, :]
        qc.append(jnp.dot(qf[r].astype(BF16), c_i.astype(BF16), preferred_element_type=F32))
        qdn.append(jnp.sum(qf[r] * n_i, axis=1, keepdims=True))
        upd = lax.dot_general(kf[r].astype(BF16), wv[r].astype(BF16), (((0,), (0,)), ((), ())),
                              preferred_element_type=F32)
        c_ref[i, 0] = d_i * c_i + upd
        n_ref[i:i + 1, :] = d_i * n_i + jnp.sum(wk[r], axis=0, keepdims=True)
        if (i + 1) % (bb // N_HEADS) == 0:
            yield

    num = sv + inter * jnp.concatenate(qc, axis=0)
    qn = jnp.sum(s, axis=1, keepdims=True) + inter * jnp.concatenate(qdn, axis=0)
    h = num / jnp.maximum(jnp.abs(qn), jnp.exp(-m_t))
    y_ref[...] = (jax.nn.sigmoid(o_ref[...].astype(F32)) * _rms(h, gn_ref[...])).astype(BF16)
    m_ref[0, 0] = to_row(m_new)


def _mlstm_step_specs(batch, seq, bb, n_steps, step_of):
    assert (batch // bb) * N_HEADS == n_steps
    rows = bb * seq
    blk = lambda *idx: step_of(*idx) // N_HEADS
    head = lambda *idx: step_of(*idx) % N_HEADS

    def col_spec(off):
        return pl.BlockSpec((rows, HEAD_DIM), lambda *idx: (blk(*idx), off + head(*idx)))

    c_spec = pl.BlockSpec((bb, 1, HEAD_DIM, HEAD_DIM), lambda *idx: (blk(*idx), head(*idx), 0, 0))
    n_spec = pl.BlockSpec((bb, HEAD_DIM), lambda *idx: (blk(*idx), head(*idx)))
    in_specs = [
        col_spec(0), col_spec(N_HEADS), col_spec(2 * N_HEADS), col_spec(3 * N_HEADS),
        pl.BlockSpec((rows, LANES), lambda *idx: (blk(*idx), 0)),
        pl.BlockSpec((1, HEAD_DIM), lambda *idx: (0, head(*idx))),
        c_spec, n_spec,
        pl.BlockSpec((rows, N_HEADS), lambda *idx: (blk(*idx), 0)),
    ]
    out_specs = [
        pl.BlockSpec((rows, HEAD_DIM), lambda *idx: (blk(*idx), head(*idx))),
        c_spec, n_spec,
        pl.BlockSpec((1, 1, 1, rows), lambda *idx: (blk(*idx), head(*idx), 0, 0)),
    ]
    out_shapes = [
        jax.ShapeDtypeStruct((batch * seq, MLSTM_W), BF16),
        jax.ShapeDtypeStruct((batch, N_HEADS, HEAD_DIM, HEAD_DIM), F32),
        jax.ShapeDtypeStruct((batch, MLSTM_W), F32),
        jax.ShapeDtypeStruct((batch // bb, N_HEADS, 1, rows), F32),
    ]
    return in_specs, out_specs, out_shapes


def _outproj_kernel(x_ref, yp_ref, ym_ref, wo_ref, nw_ref, x2_ref, xn2_ref):
    x2 = x_ref[...] + jnp.dot(yp_ref[...], wo_ref[0:POOL_W, :], preferred_element_type=F32)
    x2 = x2 + jnp.dot(ym_ref[...], wo_ref[POOL_W:, :], preferred_element_type=F32)
    x2_ref[...] = x2
    xn2_ref[...] = _rms(x2, nw_ref[...]).astype(BF16)


def _outproj(x, y_pool, y_ml, w_out, norm_w, *, tm):
    m = x.shape[0]
    row = lambda w: pl.BlockSpec((tm, w), lambda i: (i, 0))
    return pl.pallas_call(
        _outproj_kernel,
        grid=(m // tm,),
        in_specs=[
            row(D_MODEL), row(POOL_W), row(MLSTM_W),
            pl.BlockSpec((D_MODEL, D_MODEL), lambda i: (0, 0)),
            pl.BlockSpec((1, D_MODEL), lambda i: (0, 0)),
        ],
        out_specs=[row(D_MODEL), row(D_MODEL)],
        out_shape=[
            jax.ShapeDtypeStruct((m, D_MODEL), F32),
            jax.ShapeDtypeStruct((m, D_MODEL), BF16),
        ],
        compiler_params=_params(("arbitrary",)),
        name="outproj",
    )(x, y_pool, y_ml, w_out, norm_w)


FFN_DOWN_CHUNK = 512


def _ffn_kernel(xn_ref, x2_hbm, wg_ref, wu_ref, wd_ref, nw_ref, y_ref, sem, *, tm):
    i = pl.program_id(0)
    f = pl.program_id(1)
    nf = pl.num_programs(1)
    residual_copy = pltpu.make_async_copy(x2_hbm.at[pl.ds(i * tm, tm), :], y_ref, sem)

    @pl.when(f == 0)
    def _():
        residual_copy.start()

    xn = xn_ref[...]
    g = jnp.dot(xn, wg_ref[...], preferred_element_type=F32)
    u = jnp.dot(xn, wu_ref[...], preferred_element_type=F32)
    h = (jax.nn.silu(g) * u).astype(BF16)

    @pl.when(f == 0)
    def _():
        residual_copy.wait()

    for c in range(0, D_MODEL, FFN_DOWN_CHUNK):
        cols = slice(c, c + FFN_DOWN_CHUNK)
        y_ref[:, cols] += jnp.dot(h, wd_ref[:, cols], preferred_element_type=F32)

    @pl.when(f == nf - 1)
    def _():
        y_ref[...] = _rms(y_ref[...], nw_ref[...])


def _ffn(xn2, x2, w_gate, w_up, w_down, norm_w, *, tm, tf):
    m = xn2.shape[0]
    return pl.pallas_call(
        functools.partial(_ffn_kernel, tm=tm),
        grid=(m // tm, D_FF // tf),
        in_specs=[
            pl.BlockSpec((tm, D_MODEL), lambda i, f: (i, 0)),
            pl.BlockSpec(memory_space=pl.ANY),
            pl.BlockSpec((D_MODEL, tf), lambda i, f: (0, f)),
            pl.BlockSpec((D_MODEL, tf), lambda i, f: (0, f)),
            pl.BlockSpec((tf, D_MODEL), lambda i, f: (f, 0)),
            pl.BlockSpec((1, D_MODEL), lambda i, f: (0, 0)),
        ],
        out_specs=pl.BlockSpec((tm, D_MODEL), lambda i, f: (i, 0)),
        out_shape=jax.ShapeDtypeStruct((m, D_MODEL), F32),
        scratch_shapes=[pltpu.SemaphoreType.DMA(())],
        compiler_params=_params(("arbitrary", "arbitrary")),
        name="ffn",
    )(xn2, x2, w_gate, w_up, w_down, norm_w)


def kernel(x_prompt, x_sample, state_pool, state_mlstm_C, state_mlstm_n, state_mlstm_m, meta_tokens, norm_mix_w, w_in, b_igate, b_fgate, w_pool, pool_scale, mlstm_norm_w, w_out, norm_ffn_w, w_gate, w_up, w_down, norm_final_w):
    bp, tp, _ = x_prompt.shape
    bs, ts, _ = x_sample.shape

    w_in_t = w_in[0].T
    w_g8 = jnp.pad(w_in_t[MAIN_W:], ((0, LANES - 2 * N_HEADS), (0, 0))).astype(BF16)
    wp = w_pool[0].astype(BF16)
    nmix = norm_mix_w[0].reshape(1, D_MODEL)
    nffn = norm_ffn_w[0].reshape(1, D_MODEL)
    nfin = norm_final_w.reshape(1, D_MODEL)
    scale = pool_scale[0].reshape(1, POOL_W)
    gnorm = mlstm_norm_w[0].reshape(1, MLSTM_W)
    bias = jnp.concatenate([b_igate[0], b_fgate[0]]).astype(F32)

    xp = x_prompt.reshape(bp * tp, D_MODEL)
    xs = x_sample.reshape(bs * ts, D_MODEL)

    u_m, qkvo_m, g_m, w_main = _inproj(meta_tokens, nmix, w_in_t, w_g8, tm=N_META, tn=512)
    u_p, qkvo_p, g_p, wu = _inproj(xp, nmix, w_main, w_g8, tm=1024, tn=1024, convert=((w_up[0], 32),))
    u_s, qkvo_s, g_s = _inproj(xs, nmix, w_main, w_g8, tm=1024, tn=1024)

    prev_p = u_m[1:N_META].reshape(1, POOL_HIST, POOL_W)
    yp_p, pool_p = _pool(u_p.reshape(bp, tp, POOL_W), prev_p, wp, scale, bb=1, tt=512, pos0=N_META, shared_prev=True)
    yp_s, pool_s = _pool(u_s.reshape(bs, ts, POOL_W), state_pool[0], wp, scale, bb=16, tt=ts, pos0=PAST_LEN, shared_prev=False)

    zc = jnp.zeros((1, N_HEADS, HEAD_DIM, HEAD_DIM), F32)
    zn = jnp.zeros((1, N_HEADS, 1, HEAD_DIM), F32)
    zm = jnp.zeros((1, N_HEADS, 1, 1), F32)
    _, c_m, n_m, m_m = _mlstm_seq(qkvo_m, g_m, bias, gnorm, zc, zn, zm, batch=1, seq=N_META, chunk=N_META, shared_init=False)
    convert = ((w_out[0], 32), (w_gate[0], 32), (w_down[0], 32))
    step_bb = 16
    step_ops = (qkvo_s, g_s, state_mlstm_C[0], state_mlstm_n[0].reshape(bs, MLSTM_W),
                jnp.repeat(state_mlstm_m[0], ts, axis=0))
    ym_p, c_p, n_p, m_p, wo, wg, wd, ym_s, c_s, n_s, m_s = _mlstm_seq(
        qkvo_p, g_p, bias, gnorm, c_m, n_m, m_m, batch=bp, seq=tp, chunk=256, shared_init=True, convert=convert,
        step=(step_ops, bs, ts, step_bb))
    m_s = m_s.reshape(bs // step_bb, N_HEADS, step_bb, ts)[..., 0].transpose(0, 2, 1)

    x2_p, xn2_p = _outproj(xp, yp_p.reshape(bp * tp, POOL_W), ym_p, wo, nffn, tm=512)
    x2_s, xn2_s = _outproj(xs, yp_s.reshape(bs * ts, POOL_W), ym_s, wo, nffn, tm=512)
    y_p = _ffn(xn2_p, x2_p, wg, wu, wd, nfin, tm=1024, tf=512)
    y_s = _ffn(xn2_s, x2_s, wg, wu, wd, nfin, tm=1024, tf=512)

    return (
        y_p.reshape(bp, tp, D_MODEL),
        y_s.reshape(bs, ts, D_MODEL),
        pool_p[None],
        c_p[None],
        n_p.reshape(1, bp, N_HEADS, HEAD_DIM),
        m_p.reshape(1, bp, N_HEADS),
        pool_s[None],
        c_s[None],
        n_s.reshape(1, bs, N_HEADS, HEAD_DIM),
        m_s.reshape(1, bs, N_HEADS),
    )
```

```python
import functools

import jax
import jax.numpy as jnp
from jax import lax
from jax.experimental import pallas as pl
from jax.experimental.pallas import tpu as pltpu

D_MODEL = 2048
N_META = 16
POOL_W = 1024
MLSTM_W = 1024
POOL_WINDOWS = (2, 4, 8, 16)
N_POOL_GROUPS = 4
POOL_GW = 256
POOL_HIST = 15
N_HEADS = 4
HEAD_DIM = 256
D_FF = 5632
QKVO_W = 4 * MLSTM_W
MAIN_W = POOL_W + QKVO_W
PAST_LEN = 16384
EPS = 1e-6

LANES = 128
HIST_PAD = 16
VMEM_LIMIT = 56 * 1024 * 1024

BF16 = jnp.bfloat16
F32 = jnp.float32


def _params(sem):
    return pltpu.CompilerParams(dimension_semantics=sem, vmem_limit_bytes=VMEM_LIMIT)


def _rms(x, w):
    return x * lax.rsqrt(jnp.mean(x * x, axis=-1, keepdims=True) + EPS) * w


def _inproj_kernel(x_ref, nw_ref, w_ref, wg_ref, *rest, n_u_tiles, emit_copy, conv_blocks):
    nc = len(conv_blocks)
    conv_in, (u_ref, qkvo_ref, g_ref), rest = rest[:nc], rest[nc:nc + 3], rest[nc + 3:]
    conv_out, xn_ref = rest[len(rest) - 1 - nc:-1], rest[-1]
    j = pl.program_id(1)
    _convert_blocks(pl.program_id(0) * pl.num_programs(1) + j, conv_in, conv_out, conv_blocks)

    nt = (((1,), (1,)), ((), ()))

    @pl.when(j == 0)
    def _():
        xn = _rms(x_ref[...], nw_ref[...]).astype(BF16)
        xn_ref[...] = xn
        g_ref[...] = lax.dot_general(xn, wg_ref[...], nt, preferred_element_type=F32)

    w = w_ref[...]
    if emit_copy:
        w = w.astype(BF16)
        rest[0][...] = w
    p = lax.dot_general(xn_ref[...], w, nt, preferred_element_type=F32)

    @pl.when(j < n_u_tiles)
    def _():
        u_ref[...] = p

    @pl.when(j >= n_u_tiles)
    def _():
        qkvo_ref[...] = p.astype(BF16)


def _inproj(x, norm_w, w_t, w_gate_t, *, tm, tn, convert=()):
    m = x.shape[0]
    n_u = POOL_W // tn
    n_j = MAIN_W // tn
    emit_copy = w_t.dtype == F32
    if emit_copy:
        assert m == tm, "the bf16 copy is written once per column tile"
    w_spec = pl.BlockSpec((tn, D_MODEL), lambda i, j: (j, 0))
    copy_spec = [w_spec] if emit_copy else []
    copy_shape = [jax.ShapeDtypeStruct((MAIN_W, D_MODEL), BF16)] if emit_copy else []
    conv_specs, conv_shapes = _convert_specs(convert, (m // tm) * n_j, lambda i, j: i * n_j + j)
    return pl.pallas_call(
        functools.partial(_inproj_kernel, n_u_tiles=n_u, emit_copy=emit_copy,
                          conv_blocks=tuple(n for _, n in convert)),
        grid=(m // tm, n_j),
        in_specs=[
            pl.BlockSpec((tm, D_MODEL), lambda i, j: (i, 0)),
            pl.BlockSpec((1, D_MODEL), lambda i, j: (0, 0)),
            w_spec,
            pl.BlockSpec((LANES, D_MODEL), lambda i, j: (0, 0)),
        ] + conv_specs,
        out_specs=[
            pl.BlockSpec((tm, tn), lambda i, j: (i, jnp.minimum(j, n_u - 1))),
            pl.BlockSpec((tm, tn), lambda i, j: (i, jnp.maximum(j - n_u, 0))),
            pl.BlockSpec((tm, LANES), lambda i, j: (i, 0)),
        ] + copy_spec + conv_specs,
        out_shape=[
            jax.ShapeDtypeStruct((m, POOL_W), F32),
            jax.ShapeDtypeStruct((m, QKVO_W), BF16),
            jax.ShapeDtypeStruct((m, LANES), F32),
        ] + copy_shape + conv_shapes,
        scratch_shapes=[pltpu.VMEM((tm, D_MODEL), BF16)],
        compiler_params=_params(("arbitrary", "arbitrary")),
        name="inproj",
    )(x, norm_w, w_t, w_gate_t, *[w for w, _ in convert])


INPROJ_COL_CHUNK = 1024


def _inproj_rows_kernel(x_ref, nw_ref, w_ref, wg_ref, u_ref, qkvo_ref, g_ref):
    nt = (((1,), (1,)), ((), ()))
    xn = _rms(x_ref[...], nw_ref[...]).astype(BF16)
    g_ref[...] = lax.dot_general(xn, wg_ref[...], nt, preferred_element_type=F32)
    for c in range(0, MAIN_W, INPROJ_COL_CHUNK):
        p = lax.dot_general(xn, w_ref[c:c + INPROJ_COL_CHUNK, :], nt, preferred_element_type=F32)
        if c < POOL_W:
            u_ref[:, c:c + INPROJ_COL_CHUNK] = p
        else:
            qkvo_ref[:, c - POOL_W:c - POOL_W + INPROJ_COL_CHUNK] = p.astype(BF16)


def _inproj_rows(x, norm_w, w_t, w_gate_t, *, tm):
    m = x.shape[0]
    row = lambda w: pl.BlockSpec((tm, w), lambda i: (i, 0))
    whole = lambda a: pl.BlockSpec(a.shape, lambda i: (0, 0), pipeline_mode=pl.Buffered(1))
    return pl.pallas_call(
        _inproj_rows_kernel,
        grid=(m // tm,),
        in_specs=[row(D_MODEL), whole(norm_w), whole(w_t), whole(w_gate_t)],
        out_specs=[row(POOL_W), row(QKVO_W), row(LANES)],
        out_shape=[
            jax.ShapeDtypeStruct((m, POOL_W), F32),
            jax.ShapeDtypeStruct((m, QKVO_W), BF16),
            jax.ShapeDtypeStruct((m, LANES), F32),
        ],
        compiler_params=_params(("arbitrary",)),
        name="inproj_rows",
    )(x, norm_w, w_t, w_gate_t)


def _convert_specs(weights, n_steps, step_of):
    specs, shapes = [], []
    for w, n_blocks in weights:
        assert n_blocks <= n_steps
        rows = w.shape[0] // n_blocks
        specs.append(pl.BlockSpec(
            (rows, w.shape[1]), lambda *idx, n_blocks=n_blocks: (jnp.minimum(step_of(*idx), n_blocks - 1), 0)))
        shapes.append(jax.ShapeDtypeStruct(w.shape, BF16))
    return specs, shapes


def _convert_blocks(step, srcs, dsts, n_blocks):
    for src, dst, n in zip(srcs, dsts, n_blocks):
        @pl.when(step < n)
        def _(src=src, dst=dst):
            dst[...] = src[...].astype(BF16)


def _pool_kernel(u_ref, prev_ref, wp_ref, sc_ref, y_ref, st_ref, e_ref, d_ref, *, bb, tt, pos0):
    ti = pl.program_id(1)
    nt = pl.num_programs(1)

    @pl.when(ti == 0)
    def _():
        e_ref[:, 0:1, :] = jnp.zeros((bb, 1, POOL_W), F32)
        e_ref[:, 1:HIST_PAD, :] = prev_ref[...]

    e_ref[:, HIST_PAD:, :] = u_ref[...]

    pos = pos0 + ti * tt + lax.broadcasted_iota(jnp.int32, (tt, 1), 0)
    for b in range(bb):
        for g, w in enumerate(POOL_WINDOWS):
            cols = slice(g * POOL_GW, (g + 1) * POOL_GW)
            e = e_ref[b, :, cols]
            s = e + pltpu.roll(e, 1, axis=0)
            for k in range(1, g + 1):
                s = s + pltpu.roll(s, 2**k, axis=0)
            cnt = jnp.minimum(w, pos + 1).astype(F32)
            d_ref[b * tt:(b + 1) * tt, cols] = s[HIST_PAD:, :] / cnt - e[HIST_PAD:, :]

    for g in range(N_POOL_GROUPS):
        cols = slice(g * POOL_GW, (g + 1) * POOL_GW)
        y = jnp.dot(d_ref[:, cols].astype(BF16), wp_ref[g], preferred_element_type=F32) * sc_ref[:, cols]
        y_ref[:, :, cols] = y.reshape(bb, tt, POOL_GW).astype(BF16)

    @pl.when(ti == nt - 1)
    def _():
        st_ref[...] = e_ref[:, tt + 1:tt + HIST_PAD, :]

    e_ref[:, 0:HIST_PAD, :] = e_ref[:, tt:tt + HIST_PAD, :]


def _pool(u, prev, w_pool, scale, *, bb, tt, pos0, shared_prev):
    b, t, _ = u.shape
    prev_map = (lambda bi, ti: (0, 0, 0)) if shared_prev else (lambda bi, ti: (bi, 0, 0))
    return pl.pallas_call(
        functools.partial(_pool_kernel, bb=bb, tt=tt, pos0=pos0),
        grid=(b // bb, t // tt),
        in_specs=[
            pl.BlockSpec((bb, tt, POOL_W), lambda bi, ti: (bi, ti, 0)),
            pl.BlockSpec((bb, POOL_HIST, POOL_W), prev_map),
            pl.BlockSpec((N_POOL_GROUPS, POOL_GW, POOL_GW), lambda bi, ti: (0, 0, 0)),
            pl.BlockSpec((1, POOL_W), lambda bi, ti: (0, 0)),
        ],
        out_specs=[
            pl.BlockSpec((bb, tt, POOL_W), lambda bi, ti: (bi, ti, 0)),
            pl.BlockSpec((bb, POOL_HIST, POOL_W), lambda bi, ti: (bi, 0, 0)),
        ],
        out_shape=[
            jax.ShapeDtypeStruct((b, t, POOL_W), BF16),
            jax.ShapeDtypeStruct((b, POOL_HIST, POOL_W), F32),
        ],
        scratch_shapes=[
            pltpu.VMEM((bb, HIST_PAD + tt, POOL_W), F32),
            pltpu.VMEM((bb * tt, POOL_W), F32),
        ],
        compiler_params=_params(("arbitrary", "arbitrary")),
        name="pool",
    )(u, prev, w_pool, scale)


def _mlstm_chunk(q, k, v, o, gi, gf, c, n, m, gnorm, *, chunk):
    ti = lax.broadcasted_iota(jnp.int32, (chunk, chunk), 0)
    si = lax.broadcasted_iota(jnp.int32, (chunk, chunk), 1)
    tril = si <= ti
    eye = si == ti

    def to_row(col):
        return jnp.sum(jnp.where(eye, col, 0.0), axis=0, keepdims=True)

    lf = jax.nn.log_sigmoid(gf)
    b = jnp.sum(jnp.where(tril, to_row(lf), 0.0), axis=1, keepdims=True)
    a = gi - b
    a_row = to_row(a)
    cummax_a = jnp.max(jnp.where(tril, a_row, -jnp.inf), axis=1, keepdims=True)
    m_t = jnp.maximum(m + b, cummax_a + b)
    inter = jnp.exp(m + b - m_t)
    dmat = jnp.exp(jnp.where(tril, a_row + (b - m_t), -jnp.inf))

    k = k * (HEAD_DIM ** -0.5)
    s = lax.dot_general(q, k, (((1,), (1,)), ((), ())), preferred_element_type=F32) * dmat
    num = jnp.dot(s.astype(BF16), v, preferred_element_type=F32)
    num = num + inter * jnp.dot(q, c.astype(BF16), preferred_element_type=F32)
    qn = jnp.sum(s, axis=1, keepdims=True) + inter * jnp.sum(q.astype(F32) * n, axis=1, keepdims=True)
    h = num / jnp.maximum(jnp.abs(qn), jnp.exp(-m_t))

    b_last = b[chunk - 1:chunk, :]
    m_new = m_t[chunk - 1:chunk, :]
    decay = jnp.exp(m + b_last - m_new)
    w = jnp.exp(a + (b_last - m_new))
    wv = (w * v.astype(F32)).astype(BF16)
    c_new = decay * c + lax.dot_general(k, wv, (((0,), (0,)), ((), ())), preferred_element_type=F32)
    n_new = decay * n + jnp.sum(w * k.astype(F32), axis=0, keepdims=True)

    y = jax.nn.sigmoid(o.astype(F32)) * _rms(h, gnorm)
    return y, c_new, n_new, m_new


def _gate_cols(g, bias_ref, head):
    lane = lax.broadcasted_iota(jnp.int32, g.shape, 1)
    gi = jnp.sum(jnp.where(lane == head, g, 0.0), axis=1, keepdims=True) + bias_ref[head]
    gf = jnp.sum(jnp.where(lane == head + N_HEADS, g, 0.0), axis=1, keepdims=True) + bias_ref[head + N_HEADS]
    return gi, gf


N_STEP_IN = 9
N_STEP_OUT = 4


def _mlstm_seq_kernel(bias_ref, q_ref, k_ref, v_ref, o_ref, g_ref, gn_ref, c0_ref, n0_ref, m0_ref, *rest,
                      chunk, conv_blocks, step):
    nc = len(conv_blocks)
    ns_in = N_STEP_IN if step else 0
    conv_in, step_in = rest[:nc], rest[nc:nc + ns_in]
    outs = rest[nc + ns_in:]
    (y_ref, c_ref, n_ref, m_ref), conv_out, step_out = outs[:4], outs[4:4 + nc], outs[4 + nc:]
    ci = pl.program_id(1)
    grid_step = pl.program_id(0) * pl.num_programs(1) + ci
    _convert_blocks(grid_step, conv_in, conv_out, conv_blocks)

    @pl.when(ci == 0)
    def _():
        c_ref[...] = c0_ref[...]
        n_ref[...] = n0_ref[...]
        m_ref[...] = m0_ref[...]

    stages = iter(())
    if step:
        stages = _mlstm_step_stages(grid_step % N_HEADS, bias_ref, *step_in, *step_out, bb=step[0], seq=step[1])
    next(stages, None)

    g = g_ref[...]
    for head in range(N_HEADS):
        cols = slice(head * HEAD_DIM, (head + 1) * HEAD_DIM)
        gi, gf = _gate_cols(g, bias_ref, head)
        y, c_new, n_new, m_new = _mlstm_chunk(
            q_ref[:, cols], k_ref[:, cols], v_ref[:, cols], o_ref[:, cols], gi, gf,
            c_ref[0, head], n_ref[0, head], m_ref[0, head], gn_ref[:, cols], chunk=chunk)
        y_ref[:, cols] = y.astype(BF16)
        c_ref[0, head] = c_new
        n_ref[0, head] = n_new
        m_ref[0, head] = m_new
        next(stages, None)
    for _ in stages:
        pass


def _mlstm_seq(qkvo, gates, bias, gnorm, c0, n0, m0, *, batch, seq, chunk, shared_init, convert=(), step=None):
    nc = seq // chunk
    rows = lambda bi, ci: bi * nc + ci
    conv_specs, conv_shapes = _convert_specs(convert, batch * nc, rows)
    step_args, step_in, step_out, step_shapes, step_cfg = [], [], [], [], None
    if step:
        (s_qkvo, s_gates, s_c0, s_n0, s_m0), s_batch, s_seq, s_bb = step
        step_in, step_out, step_shapes = _mlstm_step_specs(s_batch, s_seq, s_bb, batch * nc, rows)
        step_args = [s_qkvo, s_qkvo, s_qkvo, s_qkvo, s_gates, gnorm, s_c0, s_n0, s_m0]
        step_cfg = (s_bb, s_seq)
    st_b = (lambda bi: 0) if shared_init else (lambda bi: bi)

    def col_spec(group):
        return pl.BlockSpec((chunk, MLSTM_W), lambda bi, ci: (rows(bi, ci), group))

    def st_spec(shape, in_b):
        nd = len(shape)
        return pl.BlockSpec((1, N_HEADS) + shape, lambda bi, ci: (in_b(bi), 0) + (0,) * nd)

    ident = lambda bi: bi
    return pl.pallas_call(
        functools.partial(_mlstm_seq_kernel, chunk=chunk, conv_blocks=tuple(n for _, n in convert), step=step_cfg),
        grid=(batch, nc),
        in_specs=[
            pl.BlockSpec(memory_space=pltpu.SMEM),
            col_spec(0), col_spec(1), col_spec(2), col_spec(3),
            pl.BlockSpec((chunk, LANES), lambda bi, ci: (rows(bi, ci), 0)),
            pl.BlockSpec((1, MLSTM_W), lambda bi, ci: (0, 0)),
            st_spec((HEAD_DIM, HEAD_DIM), st_b), st_spec((1, HEAD_DIM), st_b), st_spec((1, 1), st_b),
        ] + conv_specs + step_in,
        out_specs=[
            pl.BlockSpec((chunk, MLSTM_W), lambda bi, ci: (rows(bi, ci), 0)),
            st_spec((HEAD_DIM, HEAD_DIM), ident), st_spec((1, HEAD_DIM), ident), st_spec((1, 1), ident),
        ] + conv_specs + step_out,
        out_shape=[
            jax.ShapeDtypeStruct((batch * seq, MLSTM_W), BF16),
            jax.ShapeDtypeStruct((batch, N_HEADS, HEAD_DIM, HEAD_DIM), F32),
            jax.ShapeDtypeStruct((batch, N_HEADS, 1, HEAD_DIM), F32),
            jax.ShapeDtypeStruct((batch, N_HEADS, 1, 1), F32),
        ] + conv_shapes + step_shapes,
        compiler_params=_params(("arbitrary", "arbitrary")),
        name="mlstm_seq",
    )(bias, qkvo, qkvo, qkvo, qkvo, gates, gnorm, c0, n0, m0, *[w for w, _ in convert], *step_args)


def _mlstm_step_stages(head, bias_ref, q_ref, k_ref, v_ref, o_ref, g_ref, gn_ref, c0_ref, n0_ref, m0_ref,
                       y_ref, c_ref, n_ref, m_ref, *, bb, seq):
    rows = bb * seq
    ti = lax.broadcasted_iota(jnp.int32, (rows, rows), 0)
    si = lax.broadcasted_iota(jnp.int32, (rows, rows), 1)
    same = (ti // seq) == (si // seq)
    tril = same & (si <= ti)
    eye = si == ti

    def to_row(col):
        return jnp.sum(jnp.where(eye, col, 0.0), axis=0, keepdims=True)

    gi, gf = _gate_cols(g_ref[...], bias_ref, head)
    lane = lax.broadcasted_iota(jnp.int32, m0_ref.shape, 1)
    m0 = jnp.sum(jnp.where(lane == head, m0_ref[...], 0.0), axis=1, keepdims=True)

    lf_row = to_row(jax.nn.log_sigmoid(gf))
    b = jnp.sum(jnp.where(tril, lf_row, 0.0), axis=1, keepdims=True)
    b_last = jnp.sum(jnp.where(same, lf_row, 0.0), axis=1, keepdims=True)
    a = gi - b
    a_row = to_row(a)
    cummax_a = jnp.max(jnp.where(tril, a_row, -jnp.inf), axis=1, keepdims=True)
    seqmax_a = jnp.max(jnp.where(same, a_row, -jnp.inf), axis=1, keepdims=True)
    m_t = jnp.maximum(m0 + b, cummax_a + b)
    m_new = jnp.maximum(m0 + b_last, seqmax_a + b_last)
    inter = jnp.exp(m0 + b - m_t)
    dmat = jnp.exp(jnp.where(tril, a_row + (b - m_t), -jnp.inf))
    decay = jnp.exp(m0 + b_last - m_new)
    w = jnp.exp(a + (b_last - m_new))

    q = q_ref[...]
    k = k_ref[...] * (HEAD_DIM ** -0.5)
    v = v_ref[...]
    s = lax.dot_general(q, k, (((1,), (1,)), ((), ())), preferred_element_type=F32) * dmat
    sv = jnp.dot(s.astype(BF16), v, preferred_element_type=F32)

    qf = q.astype(F32)
    kf = k.astype(F32)
    wk = w * kf
    wv = w * v.astype(F32)
    qc, qdn = [], []
    yield
    for i in range(bb):
        r = slice(i * seq, (i + 1) * seq)
        c_i = c0_ref[i, 0]
        n_i = n0_ref[i:i + 1, :]
        d_i = decay[i * seq:i * seq + 1, :]
        qc.append(jnp.dot(qf[r].astype(BF16), c_i.astype(BF16), preferred_element_type=F32))
        qdn.append(jnp.sum(qf[r] * n_i, axis=1, keepdims=True))
        upd = lax.dot_general(kf[r].astype(BF16), wv[r].astype(BF16), (((0,), (0,)), ((), ())),
                              preferred_element_type=F32)
        c_ref[i, 0] = d_i * c_i + upd
        n_ref[i:i + 1, :] = d_i * n_i + jnp.sum(wk[r], axis=0, keepdims=True)
        if (i + 1) % (bb // N_HEADS) == 0:
            yield

    num = sv + inter * jnp.concatenate(qc, axis=0)
    qn = jnp.sum(s, axis=1, keepdims=True) + inter * jnp.concatenate(qdn, axis=0)
    h = num / jnp.maximum(jnp.abs(qn), jnp.exp(-m_t))
    y_ref[...] = (jax.nn.sigmoid(o_ref[...].astype(F32)) * _rms(h, gn_ref[...])).astype(BF16)
    m_ref[0, 0] = to_row(m_new)


def _mlstm_step_specs(batch, seq, bb, n_steps, step_of):
    assert (batch // bb) * N_HEADS == n_steps
    rows = bb * seq
    blk = lambda *idx: step_of(*idx) // N_HEADS
    head = lambda *idx: step_of(*idx) % N_HEADS

    def col_spec(off):
        return pl.BlockSpec((rows, HEAD_DIM), lambda *idx: (blk(*idx), off + head(*idx)))

    c_spec = pl.BlockSpec((bb, 1, HEAD_DIM, HEAD_DIM), lambda *idx: (blk(*idx), head(*idx), 0, 0))
    n_spec = pl.BlockSpec((bb, HEAD_DIM), lambda *idx: (blk(*idx), head(*idx)))
    in_specs = [
        col_spec(0), col_spec(N_HEADS), col_spec(2 * N_HEADS), col_spec(3 * N_HEADS),
        pl.BlockSpec((rows, LANES), lambda *idx: (blk(*idx), 0)),
        pl.BlockSpec((1, HEAD_DIM), lambda *idx: (0, head(*idx))),
        c_spec, n_spec,
        pl.BlockSpec((rows, N_HEADS), lambda *idx: (blk(*idx), 0)),
    ]
    out_specs = [
        pl.BlockSpec((rows, HEAD_DIM), lambda *idx: (blk(*idx), head(*idx))),
        c_spec, n_spec,
        pl.BlockSpec((1, 1, 1, rows), lambda *idx: (blk(*idx), head(*idx), 0, 0)),
    ]
    out_shapes = [
        jax.ShapeDtypeStruct((batch * seq, MLSTM_W), BF16),
        jax.ShapeDtypeStruct((batch, N_HEADS, HEAD_DIM, HEAD_DIM), F32),
        jax.ShapeDtypeStruct((batch, MLSTM_W), F32),
        jax.ShapeDtypeStruct((batch // bb, N_HEADS, 1, rows), F32),
    ]
    return in_specs, out_specs, out_shapes


def _outproj_kernel(x_ref, yp_ref, ym_ref, wo_ref, nw_ref, *rest, conv_blocks):
    nc = len(conv_blocks)
    conv_in, (x2_ref, xn2_ref), conv_out = rest[:nc], rest[nc:nc + 2], rest[nc + 2:]
    _convert_blocks(pl.program_id(0), conv_in, conv_out, conv_blocks)
    x2 = x_ref[...] + jnp.dot(yp_ref[...], wo_ref[0:POOL_W, :], preferred_element_type=F32)
    x2 = x2 + jnp.dot(ym_ref[...], wo_ref[POOL_W:, :], preferred_element_type=F32)
    x2_ref[...] = x2
    xn2_ref[...] = _rms(x2, nw_ref[...]).astype(BF16)


def _outproj(x, y_pool, y_ml, w_out, norm_w, *, tm, convert=()):
    m = x.shape[0]
    row = lambda w: pl.BlockSpec((tm, w), lambda i: (i, 0))
    conv_specs, conv_shapes = _convert_specs(convert, m // tm, lambda i: i)
    return pl.pallas_call(
        functools.partial(_outproj_kernel, conv_blocks=tuple(n for _, n in convert)),
        grid=(m // tm,),
        in_specs=[
            row(D_MODEL), row(POOL_W), row(MLSTM_W),
            pl.BlockSpec((D_MODEL, D_MODEL), lambda i: (0, 0), pipeline_mode=pl.Buffered(1)),
            pl.BlockSpec((1, D_MODEL), lambda i: (0, 0)),
        ] + conv_specs,
        out_specs=[row(D_MODEL), row(D_MODEL)] + conv_specs,
        out_shape=[
            jax.ShapeDtypeStruct((m, D_MODEL), F32),
            jax.ShapeDtypeStruct((m, D_MODEL), BF16),
        ] + conv_shapes,
        compiler_params=_params(("arbitrary",)),
        name="outproj",
    )(x, y_pool, y_ml, w_out, norm_w, *[w for w, _ in convert])


def _ffn_kernel(xn_ref, x2_ref, wg_ref, wu_ref, wd_ref, nw_ref, y_ref, acc_ref):
    f = pl.program_id(1)
    nf = pl.num_programs(1)

    @pl.when(f == 0)
    def _():
        acc_ref[...] = x2_ref[...]

    xn = xn_ref[...]
    g = jnp.dot(xn, wg_ref[...], preferred_element_type=F32)
    u = jnp.dot(xn, wu_ref[...], preferred_element_type=F32)
    h = (jax.nn.silu(g) * u).astype(BF16)
    acc_ref[...] += jnp.dot(h, wd_ref[...], preferred_element_type=F32)

    @pl.when(f == nf - 1)
    def _():
        y_ref[...] = _rms(acc_ref[...], nw_ref[...])


def _ffn(xn2, x2, w_gate, w_up, w_down, norm_w, *, tm, tf):
    m = xn2.shape[0]
    return pl.pallas_call(
        _ffn_kernel,
        grid=(m // tm, D_FF // tf),
        in_specs=[
            pl.BlockSpec((tm, D_MODEL), lambda i, f: (i, 0)),
            pl.BlockSpec((tm, D_MODEL), lambda i, f: (i, 0)),
            pl.BlockSpec((D_MODEL, tf), lambda i, f: (0, f)),
            pl.BlockSpec((D_MODEL, tf), lambda i, f: (0, f)),
            pl.BlockSpec((tf, D_MODEL), lambda i, f: (f, 0)),
            pl.BlockSpec((1, D_MODEL), lambda i, f: (0, 0)),
        ],
        out_specs=pl.BlockSpec((tm, D_MODEL), lambda i, f: (i, 0)),
        out_shape=jax.ShapeDtypeStruct((m, D_MODEL), F32),
        scratch_shapes=[pltpu.VMEM((tm, D_MODEL), F32)],
        compiler_params=_params(("arbitrary", "arbitrary")),
        name="ffn",
    )(xn2, x2, w_gate, w_up, w_down, norm_w)


def kernel(x_prompt, x_sample, state_pool, state_mlstm_C, state_mlstm_n, state_mlstm_m, meta_tokens, norm_mix_w, w_in, b_igate, b_fgate, w_pool, pool_scale, mlstm_norm_w, w_out, norm_ffn_w, w_gate, w_up, w_down, norm_final_w):
    bp, tp, _ = x_prompt.shape
    bs, ts, _ = x_sample.shape

    w_in_t = w_in[0].T
    w_g8 = jnp.pad(w_in_t[MAIN_W:], ((0, LANES - 2 * N_HEADS), (0, 0))).astype(BF16)
    wp = w_pool[0].astype(BF16)
    nmix = norm_mix_w[0].reshape(1, D_MODEL)
    nffn = norm_ffn_w[0].reshape(1, D_MODEL)
    nfin = norm_final_w.reshape(1, D_MODEL)
    scale = pool_scale[0].reshape(1, POOL_W)
    gnorm = mlstm_norm_w[0].reshape(1, MLSTM_W)
    bias = jnp.concatenate([b_igate[0], b_fgate[0]]).astype(F32)

    xp = x_prompt.reshape(bp * tp, D_MODEL)
    xs = x_sample.reshape(bs * ts, D_MODEL)

    u_m, qkvo_m, g_m, w_main = _inproj(meta_tokens, nmix, w_in_t, w_g8, tm=N_META, tn=512)
    u_p, qkvo_p, g_p = _inproj_rows(xp, nmix, w_main, w_g8, tm=512)
    u_s, qkvo_s, g_s = _inproj_rows(xs, nmix, w_main, w_g8, tm=512)

    prev_p = u_m[1:N_META].reshape(1, POOL_HIST, POOL_W)
    yp_p, pool_p = _pool(u_p.reshape(bp, tp, POOL_W), prev_p, wp, scale, bb=1, tt=512, pos0=N_META, shared_prev=True)
    yp_s, pool_s = _pool(u_s.reshape(bs, ts, POOL_W), state_pool[0], wp, scale, bb=16, tt=ts, pos0=PAST_LEN, shared_prev=False)

    zc = jnp.zeros((1, N_HEADS, HEAD_DIM, HEAD_DIM), F32)
    zn = jnp.zeros((1, N_HEADS, 1, HEAD_DIM), F32)
    zm = jnp.zeros((1, N_HEADS, 1, 1), F32)
    _, c_m, n_m, m_m = _mlstm_seq(qkvo_m, g_m, bias, gnorm, zc, zn, zm, batch=1, seq=N_META, chunk=N_META, shared_init=False)
    convert = ((w_out[0], 32), (w_gate[0], 32), (w_down[0], 32))
    step_bb = 16
    step_ops = (qkvo_s, g_s, state_mlstm_C[0], state_mlstm_n[0].reshape(bs, MLSTM_W),
                jnp.repeat(state_mlstm_m[0], ts, axis=0))
    ym_p, c_p, n_p, m_p, wo, wg, wd, ym_s, c_s, n_s, m_s = _mlstm_seq(
        qkvo_p, g_p, bias, gnorm, c_m, n_m, m_m, batch=bp, seq=tp, chunk=256, shared_init=True, convert=convert,
        step=(step_ops, bs, ts, step_bb))
    m_s = m_s.reshape(bs // step_bb, N_HEADS, step_bb, ts)[..., 0].transpose(0, 2, 1)

    x2_p, xn2_p, wu = _outproj(xp, yp_p.reshape(bp * tp, POOL_W), ym_p, wo, nffn, tm=512, convert=((w_up[0], 16),))
    x2_s, xn2_s = _outproj(xs, yp_s.reshape(bs * ts, POOL_W), ym_s, wo, nffn, tm=512)
    y_p = _ffn(xn2_p, x2_p, wg, wu, wd, nfin, tm=512, tf=512)
    y_s = _ffn(xn2_s, x2_s, wg, wu, wd, nfin, tm=512, tf=512)

    return (
        y_p.reshape(bp, tp, D_MODEL),
        y_s.reshape(bs, ts, D_MODEL),
        pool_p[None],
        c_p[None],
        n_p.reshape(1, bp, N_HEADS, HEAD_DIM),
        m_p.reshape(1, bp, N_HEADS),
        pool_s[None],
        c_s[None],
        n_s.reshape(1, bs, N_HEADS, HEAD_DIM),
        m_s.reshape(1, bs, N_HEADS),
    )
```

```python
import functools

import jax
import jax.numpy as jnp
from jax import lax
from jax.experimental import pallas as pl
from jax.experimental.pallas import tpu as pltpu

D_MODEL = 2048
N_META = 16
POOL_W = 1024
MLSTM_W = 1024
POOL_WINDOWS = (2, 4, 8, 16)
N_POOL_GROUPS = 4
POOL_GW = 256
POOL_HIST = 15
N_HEADS = 4
HEAD_DIM = 256
D_FF = 5632
QKVO_W = 4 * MLSTM_W
MAIN_W = POOL_W + QKVO_W
PAST_LEN = 16384
EPS = 1e-6

LANES = 128
HIST_PAD = 16
VMEM_LIMIT = 56 * 1024 * 1024

BF16 = jnp.bfloat16
F32 = jnp.float32


def _params(sem):
    return pltpu.CompilerParams(dimension_semantics=sem, vmem_limit_bytes=VMEM_LIMIT)


def _rms(x, w):
    return x * lax.rsqrt(jnp.mean(x * x, axis=-1, keepdims=True) + EPS) * w


def _inproj_kernel(x_ref, nw_ref, w_ref, wg_ref, *rest, n_u_tiles, emit_copy, conv_blocks):
    nc = len(conv_blocks)
    conv_in, (u_ref, qkvo_ref, g_ref), rest = rest[:nc], rest[nc:nc + 3], rest[nc + 3:]
    conv_out, xn_ref = rest[len(rest) - 1 - nc:-1], rest[-1]
    j = pl.program_id(1)
    _convert_blocks(pl.program_id(0) * pl.num_programs(1) + j, conv_in, conv_out, conv_blocks)

    nt = (((1,), (1,)), ((), ()))

    @pl.when(j == 0)
    def _():
        xn = _rms(x_ref[...], nw_ref[...]).astype(BF16)
        xn_ref[...] = xn
        g_ref[...] = lax.dot_general(xn, wg_ref[...], nt, preferred_element_type=F32)

    w = w_ref[...]
    if emit_copy:
        w = w.astype(BF16)
        rest[0][...] = w
    p = lax.dot_general(xn_ref[...], w, nt, preferred_element_type=F32)

    @pl.when(j < n_u_tiles)
    def _():
        u_ref[...] = p

    @pl.when(j >= n_u_tiles)
    def _():
        qkvo_ref[...] = p.astype(BF16)


def _inproj(x, norm_w, w_t, w_gate_t, *, tm, tn, convert=()):
    m = x.shape[0]
    n_u = POOL_W // tn
    n_j = MAIN_W // tn
    emit_copy = w_t.dtype == F32
    if emit_copy:
        assert m == tm, "the bf16 copy is written once per column tile"
    w_spec = pl.BlockSpec((tn, D_MODEL), lambda i, j: (j, 0))
    copy_spec = [w_spec] if emit_copy else []
    copy_shape = [jax.ShapeDtypeStruct((MAIN_W, D_MODEL), BF16)] if emit_copy else []
    conv_specs, conv_shapes = _convert_specs(convert, (m // tm) * n_j, lambda i, j: i * n_j + j)
    return pl.pallas_call(
        functools.partial(_inproj_kernel, n_u_tiles=n_u, emit_copy=emit_copy,
                          conv_blocks=tuple(n for _, n in convert)),
        grid=(m // tm, n_j),
        in_specs=[
            pl.BlockSpec((tm, D_MODEL), lambda i, j: (i, 0)),
            pl.BlockSpec((1, D_MODEL), lambda i, j: (0, 0)),
            w_spec,
            pl.BlockSpec((LANES, D_MODEL), lambda i, j: (0, 0)),
        ] + conv_specs,
        out_specs=[
            pl.BlockSpec((tm, tn), lambda i, j: (i, jnp.minimum(j, n_u - 1))),
            pl.BlockSpec((tm, tn), lambda i, j: (i, jnp.maximum(j - n_u, 0))),
            pl.BlockSpec((tm, LANES), lambda i, j: (i, 0)),
        ] + copy_spec + conv_specs,
        out_shape=[
            jax.ShapeDtypeStruct((m, POOL_W), F32),
            jax.ShapeDtypeStruct((m, QKVO_W), BF16),
            jax.ShapeDtypeStruct((m, LANES), F32),
        ] + copy_shape + conv_shapes,
        scratch_shapes=[pltpu.VMEM((tm, D_MODEL), BF16)],
        compiler_params=_params(("arbitrary", "arbitrary")),
        name="inproj",
    )(x, norm_w, w_t, w_gate_t, *[w for w, _ in convert])


INPROJ_COL_CHUNK = 1024


def _inproj_rows_kernel(x_ref, nw_ref, w_ref, wg_ref, prev_ref, wp_ref, sc_ref,
                        yp_ref, qkvo_ref, g_ref, st_ref, e_ref, *, bb, tt, pos0, tiles_per_seq):
    nt = (((1,), (1,)), ((), ()))
    ti = pl.program_id(0) % tiles_per_seq

    @pl.when(ti == 0)
    def _():
        e_ref[:, 0:1, :] = jnp.zeros((bb, 1, POOL_W), F32)
        e_ref[:, 1:HIST_PAD, :] = prev_ref[...]

    xn = _rms(x_ref[...], nw_ref[...]).astype(BF16)
    g_ref[...] = lax.dot_general(xn, wg_ref[...], nt, preferred_element_type=F32)
    u = lax.dot_general(xn, w_ref[0:POOL_W, :], nt, preferred_element_type=F32)
    e_ref[:, HIST_PAD:, :] = u.reshape(bb, tt, POOL_W)
    pos = pos0 + ti * tt + lax.broadcasted_iota(jnp.int32, (tt, 1), 0)

    for g, w in enumerate(POOL_WINDOWS):
        c = POOL_W + g * INPROJ_COL_CHUNK
        p = lax.dot_general(xn, w_ref[c:c + INPROJ_COL_CHUNK, :], nt, preferred_element_type=F32)
        qkvo_ref[:, c - POOL_W:c - POOL_W + INPROJ_COL_CHUNK] = p.astype(BF16)

        cols = slice(g * POOL_GW, (g + 1) * POOL_GW)
        cnt = jnp.minimum(w, pos + 1).astype(F32)
        d = []
        for b in range(bb):
            e = e_ref[b, :, cols]
            s = e + pltpu.roll(e, 1, axis=0)
            for k in range(1, g + 1):
                s = s + pltpu.roll(s, 2**k, axis=0)
            d.append(s[HIST_PAD:, :] / cnt - e[HIST_PAD:, :])
        d = jnp.concatenate(d, axis=0).astype(BF16)
        yp_ref[:, cols] = (jnp.dot(d, wp_ref[g], preferred_element_type=F32) * sc_ref[:, cols]).astype(BF16)

    e_ref[:, 0:HIST_PAD, :] = e_ref[:, tt:tt + HIST_PAD, :]

    @pl.when(ti == tiles_per_seq - 1)
    def _():
        st_ref[...] = e_ref[:, 1:HIST_PAD, :]


def _inproj_rows(x, norm_w, w_t, w_gate_t, prev, w_pool, scale, *, bb, tt, pos0, tiles_per_seq):
    assert MAIN_W == POOL_W + N_POOL_GROUPS * INPROJ_COL_CHUNK
    tm = bb * tt
    m = x.shape[0]
    n_seq = m // (tt * tiles_per_seq)
    row = lambda w: pl.BlockSpec((tm, w), lambda i: (i, 0))
    whole = lambda a: pl.BlockSpec(a.shape, lambda i: (0,) * a.ndim, pipeline_mode=pl.Buffered(1))
    seq_blk = lambda i: (i // tiles_per_seq, 0, 0)
    prev_spec = whole(prev) if prev.shape[0] == 1 else pl.BlockSpec((bb, POOL_HIST, POOL_W), seq_blk)
    return pl.pallas_call(
        functools.partial(_inproj_rows_kernel, bb=bb, tt=tt, pos0=pos0, tiles_per_seq=tiles_per_seq),
        grid=(m // tm,),
        in_specs=[row(D_MODEL), whole(norm_w), whole(w_t), whole(w_gate_t), prev_spec, whole(w_pool), whole(scale)],
        out_specs=[row(POOL_W), row(QKVO_W), row(LANES), pl.BlockSpec((bb, POOL_HIST, POOL_W), seq_blk)],
        out_shape=[
            jax.ShapeDtypeStruct((m, POOL_W), BF16),
            jax.ShapeDtypeStruct((m, QKVO_W), BF16),
            jax.ShapeDtypeStruct((m, LANES), F32),
            jax.ShapeDtypeStruct((n_seq, POOL_HIST, POOL_W), F32),
        ],
        scratch_shapes=[pltpu.VMEM((bb, HIST_PAD + tt, POOL_W), F32)],
        compiler_params=_params(("arbitrary",)),
        name="inproj_rows",
    )(x, norm_w, w_t, w_gate_t, prev, w_pool, scale)


def _convert_specs(weights, n_steps, step_of):
    specs, shapes = [], []
    for w, n_blocks in weights:
        assert n_blocks <= n_steps
        rows = w.shape[0] // n_blocks
        specs.append(pl.BlockSpec(
            (rows, w.shape[1]), lambda *idx, n_blocks=n_blocks: (jnp.minimum(step_of(*idx), n_blocks - 1), 0)))
        shapes.append(jax.ShapeDtypeStruct(w.shape, BF16))
    return specs, shapes


def _convert_blocks(step, srcs, dsts, n_blocks):
    for src, dst, n in zip(srcs, dsts, n_blocks):
        @pl.when(step < n)
        def _(src=src, dst=dst):
            dst[...] = src[...].astype(BF16)


def _mlstm_chunk(q, k, v, o, gi, gf, c, n, m, gnorm, *, chunk):
    ti = lax.broadcasted_iota(jnp.int32, (chunk, chunk), 0)
    si = lax.broadcasted_iota(jnp.int32, (chunk, chunk), 1)
    tril = si <= ti
    eye = si == ti

    def to_row(col):
        return jnp.sum(jnp.where(eye, col, 0.0), axis=0, keepdims=True)

    lf = jax.nn.log_sigmoid(gf)
    b = jnp.sum(jnp.where(tril, to_row(lf), 0.0), axis=1, keepdims=True)
    a = gi - b
    a_row = to_row(a)
    cummax_a = jnp.max(jnp.where(tril, a_row, -jnp.inf), axis=1, keepdims=True)
    m_t = jnp.maximum(m + b, cummax_a + b)
    inter = jnp.exp(m + b - m_t)
    dmat = jnp.exp(jnp.where(tril, a_row + (b - m_t), -jnp.inf))

    k = k * (HEAD_DIM ** -0.5)
    s = lax.dot_general(q, k, (((1,), (1,)), ((), ())), preferred_element_type=F32) * dmat
    num = jnp.dot(s.astype(BF16), v, preferred_element_type=F32)
    num = num + inter * jnp.dot(q, c.astype(BF16), preferred_element_type=F32)
    qn = jnp.sum(s, axis=1, keepdims=True) + inter * jnp.sum(q.astype(F32) * n, axis=1, keepdims=True)
    h = num / jnp.maximum(jnp.abs(qn), jnp.exp(-m_t))

    b_last = b[chunk - 1:chunk, :]
    m_new = m_t[chunk - 1:chunk, :]
    decay = jnp.exp(m + b_last - m_new)
    w = jnp.exp(a + (b_last - m_new))
    wv = (w * v.astype(F32)).astype(BF16)
    c_new = decay * c + lax.dot_general(k, wv, (((0,), (0,)), ((), ())), preferred_element_type=F32)
    n_new = decay * n + jnp.sum(w * k.astype(F32), axis=0, keepdims=True)

    y = jax.nn.sigmoid(o.astype(F32)) * _rms(h, gnorm)
    return y, c_new, n_new, m_new


def _gate_cols(g, bias_ref, head):
    lane = lax.broadcasted_iota(jnp.int32, g.shape, 1)
    gi = jnp.sum(jnp.where(lane == head, g, 0.0), axis=1, keepdims=True) + bias_ref[head]
    gf = jnp.sum(jnp.where(lane == head + N_HEADS, g, 0.0), axis=1, keepdims=True) + bias_ref[head + N_HEADS]
    return gi, gf


N_STEP_IN = 9
N_STEP_OUT = 4


def _mlstm_seq_kernel(bias_ref, q_ref, k_ref, v_ref, o_ref, g_ref, gn_ref, c0_ref, n0_ref, m0_ref, *rest,
                      chunk, conv_blocks, step):
    nc = len(conv_blocks)
    ns_in = N_STEP_IN if step else 0
    conv_in, step_in = rest[:nc], rest[nc:nc + ns_in]
    outs = rest[nc + ns_in:]
    (y_ref, c_ref, n_ref, m_ref), conv_out, step_out = outs[:4], outs[4:4 + nc], outs[4 + nc:]
    ci = pl.program_id(1)
    grid_step = pl.program_id(0) * pl.num_programs(1) + ci
    _convert_blocks(grid_step, conv_in, conv_out, conv_blocks)

    @pl.when(ci == 0)
    def _():
        c_ref[...] = c0_ref[...]
        n_ref[...] = n0_ref[...]
        m_ref[...] = m0_ref[...]

    stages = iter(())
    if step:
        stages = _mlstm_step_stages(grid_step % N_HEADS, bias_ref, *step_in, *step_out, bb=step[0], seq=step[1])
    next(stages, None)

    g = g_ref[...]
    for head in range(N_HEADS):
        cols = slice(head * HEAD_DIM, (head + 1) * HEAD_DIM)
        gi, gf = _gate_cols(g, bias_ref, head)
        y, c_new, n_new, m_new = _mlstm_chunk(
            q_ref[:, cols], k_ref[:, cols], v_ref[:, cols], o_ref[:, cols], gi, gf,
            c_ref[0, head], n_ref[0, head], m_ref[0, head], gn_ref[:, cols], chunk=chunk)
        y_ref[:, cols] = y.astype(BF16)
        c_ref[0, head] = c_new
        n_ref[0, head] = n_new
        m_ref[0, head] = m_new
        next(stages, None)
    for _ in stages:
        pass


def _mlstm_seq(qkvo, gates, bias, gnorm, c0, n0, m0, *, batch, seq, chunk, shared_init, convert=(), step=None):
    nc = seq // chunk
    rows = lambda bi, ci: bi * nc + ci
    conv_specs, conv_shapes = _convert_specs(convert, batch * nc, rows)
    step_args, step_in, step_out, step_shapes, step_cfg = [], [], [], [], None
    if step:
        (s_qkvo, s_gates, s_c0, s_n0, s_m0), s_batch, s_seq, s_bb = step
        step_in, step_out, step_shapes = _mlstm_step_specs(s_batch, s_seq, s_bb, batch * nc, rows)
        step_args = [s_qkvo, s_qkvo, s_qkvo, s_qkvo, s_gates, gnorm, s_c0, s_n0, s_m0]
        step_cfg = (s_bb, s_seq)
    st_b = (lambda bi: 0) if shared_init else (lambda bi: bi)

    def col_spec(group):
        return pl.BlockSpec((chunk, MLSTM_W), lambda bi, ci: (rows(bi, ci), group))

    def st_spec(shape, in_b):
        nd = len(shape)
        return pl.BlockSpec((1, N_HEADS) + shape, lambda bi, ci: (in_b(bi), 0) + (0,) * nd)

    ident = lambda bi: bi
    return pl.pallas_call(
        functools.partial(_mlstm_seq_kernel, chunk=chunk, conv_blocks=tuple(n for _, n in convert), step=step_cfg),
        grid=(batch, nc),
        in_specs=[
            pl.BlockSpec(memory_space=pltpu.SMEM),
            col_spec(0), col_spec(1), col_spec(2), col_spec(3),
            pl.BlockSpec((chunk, LANES), lambda bi, ci: (rows(bi, ci), 0)),
            pl.BlockSpec((1, MLSTM_W), lambda bi, ci: (0, 0)),
            st_spec((HEAD_DIM, HEAD_DIM), st_b), st_spec((1, HEAD_DIM), st_b), st_spec((1, 1), st_b),
        ] + conv_specs + step_in,
        out_specs=[
            pl.BlockSpec((chunk, MLSTM_W), lambda bi, ci: (rows(bi, ci), 0)),
            st_spec((HEAD_DIM, HEAD_DIM), ident), st_spec((1, HEAD_DIM), ident), st_spec((1, 1), ident),
        ] + conv_specs + step_out,
        out_shape=[
            jax.ShapeDtypeStruct((batch * seq, MLSTM_W), BF16),
            jax.ShapeDtypeStruct((batch, N_HEADS, HEAD_DIM, HEAD_DIM), F32),
            jax.ShapeDtypeStruct((batch, N_HEADS, 1, HEAD_DIM), F32),
            jax.ShapeDtypeStruct((batch, N_HEADS, 1, 1), F32),
        ] + conv_shapes + step_shapes,
        compiler_params=_params(("arbitrary", "arbitrary")),
        name="mlstm_seq",
    )(bias, qkvo, qkvo, qkvo, qkvo, gates, gnorm, c0, n0, m0, *[w for w, _ in convert], *step_args)


def _mlstm_step_stages(head, bias_ref, q_ref, k_ref, v_ref, o_ref, g_ref, gn_ref, c0_ref, n0_ref, m0_ref,
                       y_ref, c_ref, n_ref, m_ref, *, bb, seq):
    rows = bb * seq
    ti = lax.broadcasted_iota(jnp.int32, (rows, rows), 0)
    si = lax.broadcasted_iota(jnp.int32, (rows, rows), 1)
    same = (ti // seq) == (si // seq)
    tril = same & (si <= ti)
    eye = si == ti

    def to_row(col):
        return jnp.sum(jnp.where(eye, col, 0.0), axis=0, keepdims=True)

    gi, gf = _gate_cols(g_ref[...], bias_ref, head)
    lane = lax.broadcasted_iota(jnp.int32, m0_ref.shape, 1)
    m0 = jnp.sum(jnp.where(lane == head, m0_ref[...], 0.0), axis=1, keepdims=True)

    lf_row = to_row(jax.nn.log_sigmoid(gf))
    b = jnp.sum(jnp.where(tril, lf_row, 0.0), axis=1, keepdims=True)
    b_last = jnp.sum(jnp.where(same, lf_row, 0.0), axis=1, keepdims=True)
    a = gi - b
    a_row = to_row(a)
    cummax_a = jnp.max(jnp.where(tril, a_row, -jnp.inf), axis=1, keepdims=True)
    seqmax_a = jnp.max(jnp.where(same, a_row, -jnp.inf), axis=1, keepdims=True)
    m_t = jnp.maximum(m0 + b, cummax_a + b)
    m_new = jnp.maximum(m0 + b_last, seqmax_a + b_last)
    inter = jnp.exp(m0 + b - m_t)
    dmat = jnp.exp(jnp.where(tril, a_row + (b - m_t), -jnp.inf))
    decay = jnp.exp(m0 + b_last - m_new)
    w = jnp.exp(a + (b_last - m_new))

    q = q_ref[...]
    k = k_ref[...] * (HEAD_DIM ** -0.5)
    v = v_ref[...]
    s = lax.dot_general(q, k, (((1,), (1,)), ((), ())), preferred_element_type=F32) * dmat
    sv = jnp.dot(s.astype(BF16), v, preferred_element_type=F32)

    qf = q.astype(F32)
    kf = k.astype(F32)
    wk = w * kf
    wv = w * v.astype(F32)
    qc, qdn = [], []
    yield
    for i in range(bb):
        r = slice(i * seq, (i + 1) * seq)
        c_i = c0_ref[i, 0]
        n_i = n0_ref[i:i + 1, :]
        d_i = decay[i * seq:i * seq + 1, :]
        qc.append(jnp.dot(qf[r].astype(BF16), c_i.astype(BF16), preferred_element_type=F32))
        qdn.append(jnp.sum(qf[r] * n_i, axis=1, keepdims=True))
        upd = lax.dot_general(kf[r].astype(BF16), wv[r].astype(BF16), (((0,), (0,)), ((), ())),
                              preferred_element_type=F32)
        c_ref[i, 0] = d_i * c_i + upd
        n_ref[i:i + 1, :] = d_i * n_i + jnp.sum(wk[r], axis=0, keepdims=True)
        if (i + 1) % (bb // N_HEADS) == 0:
            yield

    num = sv + inter * jnp.concatenate(qc, axis=0)
    qn = jnp.sum(s, axis=1, keepdims=True) + inter * jnp.concatenate(qdn, axis=0)
    h = num / jnp.maximum(jnp.abs(qn), jnp.exp(-m_t))
    y_ref[...] = (jax.nn.sigmoid(o_ref[...].astype(F32)) * _rms(h, gn_ref[...])).astype(BF16)
    m_ref[0, 0] = to_row(m_new)


def _mlstm_step_specs(batch, seq, bb, n_steps, step_of):
    assert (batch // bb) * N_HEADS == n_steps
    rows = bb * seq
    blk = lambda *idx: step_of(*idx) // N_HEADS
    head = lambda *idx: step_of(*idx) % N_HEADS

    def col_spec(off):
        return pl.BlockSpec((rows, HEAD_DIM), lambda *idx: (blk(*idx), off + head(*idx)))

    c_spec = pl.BlockSpec((bb, 1, HEAD_DIM, HEAD_DIM), lambda *idx: (blk(*idx), head(*idx), 0, 0))
    n_spec = pl.BlockSpec((bb, HEAD_DIM), lambda *idx: (blk(*idx), head(*idx)))
    in_specs = [
        col_spec(0), col_spec(N_HEADS), col_spec(2 * N_HEADS), col_spec(3 * N_HEADS),
        pl.BlockSpec((rows, LANES), lambda *idx: (blk(*idx), 0)),
        pl.BlockSpec((1, HEAD_DIM), lambda *idx: (0, head(*idx))),
        c_spec, n_spec,
        pl.BlockSpec((rows, N_HEADS), lambda *idx: (blk(*idx), 0)),
    ]
    out_specs = [
        pl.BlockSpec((rows, HEAD_DIM), lambda *idx: (blk(*idx), head(*idx))),
        c_spec, n_spec,
        pl.BlockSpec((1, 1, 1, rows), lambda *idx: (blk(*idx), head(*idx), 0, 0)),
    ]
    out_shapes = [
        jax.ShapeDtypeStruct((batch * seq, MLSTM_W), BF16),
        jax.ShapeDtypeStruct((batch, N_HEADS, HEAD_DIM, HEAD_DIM), F32),
        jax.ShapeDtypeStruct((batch, MLSTM_W), F32),
        jax.ShapeDtypeStruct((batch // bb, N_HEADS, 1, rows), F32),
    ]
    return in_specs, out_specs, out_shapes


def _outproj_kernel(x_ref, yp_ref, ym_ref, wo_ref, nw_ref, *rest, conv_blocks):
    nc = len(conv_blocks)
    conv_in, (x2_ref, xn2_ref), conv_out = rest[:nc], rest[nc:nc + 2], rest[nc + 2:]
    _convert_blocks(pl.program_id(0), conv_in, conv_out, conv_blocks)
    x2 = x_ref[...] + jnp.dot(yp_ref[...], wo_ref[0:POOL_W, :], preferred_element_type=F32)
    x2 = x2 + jnp.dot(ym_ref[...], wo_ref[POOL_W:, :], preferred_element_type=F32)
    x2_ref[...] = x2
    xn2_ref[...] = _rms(x2, nw_ref[...]).astype(BF16)


def _outproj(x, y_pool, y_ml, w_out, norm_w, *, tm, convert=()):
    m = x.shape[0]
    row = lambda w: pl.BlockSpec((tm, w), lambda i: (i, 0))
    conv_specs, conv_shapes = _convert_specs(convert, m // tm, lambda i: i)
    return pl.pallas_call(
        functools.partial(_outproj_kernel, conv_blocks=tuple(n for _, n in convert)),
        grid=(m // tm,),
        in_specs=[
            row(D_MODEL), row(POOL_W), row(MLSTM_W),
            pl.BlockSpec((D_MODEL, D_MODEL), lambda i: (0, 0), pipeline_mode=pl.Buffered(1)),
            pl.BlockSpec((1, D_MODEL), lambda i: (0, 0)),
        ] + conv_specs,
        out_specs=[row(D_MODEL), row(D_MODEL)] + conv_specs,
        out_shape=[
            jax.ShapeDtypeStruct((m, D_MODEL), F32),
            jax.ShapeDtypeStruct((m, D_MODEL), BF16),
        ] + conv_shapes,
        compiler_params=_params(("arbitrary",)),
        name="outproj",
    )(x, y_pool, y_ml, w_out, norm_w, *[w for w, _ in convert])


def _ffn_kernel(xn_ref, x2_ref, wg_ref, wu_ref, wd_ref, nw_ref, y_ref, acc_ref):
    f = pl.program_id(1)
    nf = pl.num_programs(1)

    @pl.when(f == 0)
    def _():
        acc_ref[...] = x2_ref[...]

    xn = xn_ref[...]
    g = jnp.dot(xn, wg_ref[...], preferred_element_type=F32)
    u = jnp.dot(xn, wu_ref[...], preferred_element_type=F32)
    h = (jax.nn.silu(g) * u).astype(BF16)
    acc_ref[...] += jnp.dot(h, wd_ref[...], preferred_element_type=F32)

    @pl.when(f == nf - 1)
    def _():
        y_ref[...] = _rms(acc_ref[...], nw_ref[...])


def _ffn(xn2, x2, w_gate, w_up, w_down, norm_w, *, tm, tf):
    m = xn2.shape[0]
    return pl.pallas_call(
        _ffn_kernel,
        grid=(m // tm, D_FF // tf),
        in_specs=[
            pl.BlockSpec((tm, D_MODEL), lambda i, f: (i, 0)),
            pl.BlockSpec((tm, D_MODEL), lambda i, f: (i, 0)),
            pl.BlockSpec((D_MODEL, tf), lambda i, f: (0, f)),
            pl.BlockSpec((D_MODEL, tf), lambda i, f: (0, f)),
            pl.BlockSpec((tf, D_MODEL), lambda i, f: (f, 0)),
            pl.BlockSpec((1, D_MODEL), lambda i, f: (0, 0)),
        ],
        out_specs=pl.BlockSpec((tm, D_MODEL), lambda i, f: (i, 0)),
        out_shape=jax.ShapeDtypeStruct((m, D_MODEL), F32),
        scratch_shapes=[pltpu.VMEM((tm, D_MODEL), F32)],
        compiler_params=_params(("arbitrary", "arbitrary")),
        name="ffn",
    )(xn2, x2, w_gate, w_up, w_down, norm_w)


def kernel(x_prompt, x_sample, state_pool, state_mlstm_C, state_mlstm_n, state_mlstm_m, meta_tokens, norm_mix_w, w_in, b_igate, b_fgate, w_pool, pool_scale, mlstm_norm_w, w_out, norm_ffn_w, w_gate, w_up, w_down, norm_final_w):
    bp, tp, _ = x_prompt.shape
    bs, ts, _ = x_sample.shape

    w_in_t = w_in[0].T
    w_g8 = jnp.pad(w_in_t[MAIN_W:], ((0, LANES - 2 * N_HEADS), (0, 0))).astype(BF16)
    wp = w_pool[0].astype(BF16)
    nmix = norm_mix_w[0].reshape(1, D_MODEL)
    nffn = norm_ffn_w[0].reshape(1, D_MODEL)
    nfin = norm_final_w.reshape(1, D_MODEL)
    scale = pool_scale[0].reshape(1, POOL_W)
    gnorm = mlstm_norm_w[0].reshape(1, MLSTM_W)
    bias = jnp.concatenate([b_igate[0], b_fgate[0]]).astype(F32)

    xp = x_prompt.reshape(bp * tp, D_MODEL)
    xs = x_sample.reshape(bs * ts, D_MODEL)

    u_m, qkvo_m, g_m, w_main = _inproj(meta_tokens, nmix, w_in_t, w_g8, tm=N_META, tn=512)
    prev_p = u_m[1:N_META].reshape(1, POOL_HIST, POOL_W)
    prompt_tt = 512
    yp_p, qkvo_p, g_p, pool_p = _inproj_rows(
        xp, nmix, w_main, w_g8, prev_p, wp, scale, bb=1, tt=prompt_tt, pos0=N_META, tiles_per_seq=tp // prompt_tt)
    yp_s, qkvo_s, g_s, pool_s = _inproj_rows(
        xs, nmix, w_main, w_g8, state_pool[0], wp, scale, bb=32, tt=ts, pos0=PAST_LEN, tiles_per_seq=1)

    zc = jnp.zeros((1, N_HEADS, HEAD_DIM, HEAD_DIM), F32)
    zn = jnp.zeros((1, N_HEADS, 1, HEAD_DIM), F32)
    zm = jnp.zeros((1, N_HEADS, 1, 1), F32)
    _, c_m, n_m, m_m = _mlstm_seq(qkvo_m, g_m, bias, gnorm, zc, zn, zm, batch=1, seq=N_META, chunk=N_META, shared_init=False)
    convert = ((w_out[0], 32), (w_gate[0], 32), (w_down[0], 32))
    step_bb = 16
    step_ops = (qkvo_s, g_s, state_mlstm_C[0], state_mlstm_n[0].reshape(bs, MLSTM_W),
                jnp.repeat(state_mlstm_m[0], ts, axis=0))
    ym_p, c_p, n_p, m_p, wo, wg, wd, ym_s, c_s, n_s, m_s = _mlstm_seq(
        qkvo_p, g_p, bias, gnorm, c_m, n_m, m_m, batch=bp, seq=tp, chunk=256, shared_init=True, convert=convert,
        step=(step_ops, bs, ts, step_bb))
    m_s = m_s.reshape(bs // step_bb, N_HEADS, step_bb, ts)[..., 0].transpose(0, 2, 1)

    x2_p, xn2_p, wu = _outproj(xp, yp_p, ym_p, wo, nffn, tm=512, convert=((w_up[0], 16),))
    x2_s, xn2_s = _outproj(xs, yp_s, ym_s, wo, nffn, tm=512)
    y_p = _ffn(xn2_p, x2_p, wg, wu, wd, nfin, tm=512, tf=512)
    y_s = _ffn(xn2_s, x2_s, wg, wu, wd, nfin, tm=512, tf=512)

    return (
        y_p.reshape(bp, tp, D_MODEL),
        y_s.reshape(bs, ts, D_MODEL),
        pool_p[None],
        c_p[None],
        n_p.reshape(1, bp, N_HEADS, HEAD_DIM),
        m_p.reshape(1, bp, N_HEADS),
        pool_s[None],
        c_s[None],
        n_s.reshape(1, bs, N_HEADS, HEAD_DIM),
        m_s.reshape(1, bs, N_HEADS),
    )
```

```python
import functools

import jax
import jax.numpy as jnp
from jax import lax
from jax.experimental import pallas as pl
from jax.experimental.pallas import tpu as pltpu

D_MODEL = 2048
N_META = 16
POOL_W = 1024
MLSTM_W = 1024
POOL_WINDOWS = (2, 4, 8, 16)
N_POOL_GROUPS = 4
POOL_GW = 256
POOL_HIST = 15
N_HEADS = 4
HEAD_DIM = 256
D_FF = 5632
QKVO_W = 4 * MLSTM_W
MAIN_W = POOL_W + QKVO_W
PAST_LEN = 16384
EPS = 1e-6

LANES = 128
HIST_PAD = 16
VMEM_LIMIT = 56 * 1024 * 1024
VMEM_LIMIT_RESIDENT_WEIGHT = 60 * 1024 * 1024

BF16 = jnp.bfloat16
F32 = jnp.float32


def _params(sem):
    return pltpu.CompilerParams(dimension_semantics=sem, vmem_limit_bytes=VMEM_LIMIT)


def _rms(x, w):
    return x * lax.rsqrt(jnp.mean(x * x, axis=-1, keepdims=True) + EPS) * w


def _inproj_kernel(x_ref, nw_ref, w_ref, wg_ref, *rest, n_u_tiles, emit_copy, conv_blocks):
    nc = len(conv_blocks)
    conv_in, (u_ref, qkvo_ref, g_ref), rest = rest[:nc], rest[nc:nc + 3], rest[nc + 3:]
    conv_out, xn_ref = rest[len(rest) - 1 - nc:-1], rest[-1]
    j = pl.program_id(1)
    _convert_blocks(pl.program_id(0) * pl.num_programs(1) + j, conv_in, conv_out, conv_blocks)

    nt = (((1,), (1,)), ((), ()))

    @pl.when(j == 0)
    def _():
        xn = _rms(x_ref[...], nw_ref[...]).astype(BF16)
        xn_ref[...] = xn
        g_ref[...] = lax.dot_general(xn, wg_ref[...], nt, preferred_element_type=F32)

    w = w_ref[...]
    if emit_copy:
        w = w.astype(BF16)
        rest[0][...] = w
    p = lax.dot_general(xn_ref[...], w, nt, preferred_element_type=F32)

    @pl.when(j < n_u_tiles)
    def _():
        u_ref[...] = p

    @pl.when(j >= n_u_tiles)
    def _():
        qkvo_ref[...] = p.astype(BF16)


def _inproj(x, norm_w, w_t, w_gate_t, *, tm, tn, convert=()):
    m = x.shape[0]
    n_u = POOL_W // tn
    n_j = MAIN_W // tn
    emit_copy = w_t.dtype == F32
    if emit_copy:
        assert m == tm, "the bf16 copy is written once per column tile"
    w_spec = pl.BlockSpec((tn, D_MODEL), lambda i, j: (j, 0))
    copy_spec = [w_spec] if emit_copy else []
    copy_shape = [jax.ShapeDtypeStruct((MAIN_W, D_MODEL), BF16)] if emit_copy else []
    conv_specs, conv_shapes = _convert_specs(convert, (m // tm) * n_j, lambda i, j: i * n_j + j)
    return pl.pallas_call(
        functools.partial(_inproj_kernel, n_u_tiles=n_u, emit_copy=emit_copy,
                          conv_blocks=tuple(n for _, n in convert)),
        grid=(m // tm, n_j),
        in_specs=[
            pl.BlockSpec((tm, D_MODEL), lambda i, j: (i, 0)),
            pl.BlockSpec((1, D_MODEL), lambda i, j: (0, 0)),
            w_spec,
            pl.BlockSpec((LANES, D_MODEL), lambda i, j: (0, 0)),
        ] + conv_specs,
        out_specs=[
            pl.BlockSpec((tm, tn), lambda i, j: (i, jnp.minimum(j, n_u - 1))),
            pl.BlockSpec((tm, tn), lambda i, j: (i, jnp.maximum(j - n_u, 0))),
            pl.BlockSpec((tm, LANES), lambda i, j: (i, 0)),
        ] + copy_spec + conv_specs,
        out_shape=[
            jax.ShapeDtypeStruct((m, POOL_W), F32),
            jax.ShapeDtypeStruct((m, QKVO_W), BF16),
            jax.ShapeDtypeStruct((m, LANES), F32),
        ] + copy_shape + conv_shapes,
        scratch_shapes=[pltpu.VMEM((tm, D_MODEL), BF16)],
        compiler_params=_params(("arbitrary", "arbitrary")),
        name="inproj",
    )(x, norm_w, w_t, w_gate_t, *[w for w, _ in convert])


INPROJ_COL_CHUNK = 1024


def _inproj_rows_kernel(x_ref, nw_ref, w_ref, wg_ref, prev_ref, wp_ref, sc_ref, *rest,
                        bb, tt, pos0, tiles_per_seq, conv_blocks):
    nc = len(conv_blocks)
    conv_in, (yp_ref, qkvo_ref, g_ref, st_ref) = rest[:nc], rest[nc:nc + 4]
    conv_out, e_ref = rest[nc + 4:2 * nc + 4], rest[-1]
    _convert_blocks(pl.program_id(0), conv_in, conv_out, conv_blocks)
    nt = (((1,), (1,)), ((), ()))
    ti = pl.program_id(0) % tiles_per_seq

    @pl.when(ti == 0)
    def _():
        e_ref[:, 0:1, :] = jnp.zeros((bb, 1, POOL_W), F32)
        e_ref[:, 1:HIST_PAD, :] = prev_ref[...]

    xn = _rms(x_ref[...], nw_ref[...]).astype(BF16)
    g_ref[...] = lax.dot_general(xn, wg_ref[...], nt, preferred_element_type=F32)
    u = lax.dot_general(xn, w_ref[0:POOL_W, :], nt, preferred_element_type=F32)
    e_ref[:, HIST_PAD:, :] = u.reshape(bb, tt, POOL_W)
    pos = pos0 + ti * tt + lax.broadcasted_iota(jnp.int32, (tt, 1), 0)

    for g, w in enumerate(POOL_WINDOWS):
        c = POOL_W + g * INPROJ_COL_CHUNK
        p = lax.dot_general(xn, w_ref[c:c + INPROJ_COL_CHUNK, :], nt, preferred_element_type=F32)
        qkvo_ref[:, c - POOL_W:c - POOL_W + INPROJ_COL_CHUNK] = p.astype(BF16)

        cols = slice(g * POOL_GW, (g + 1) * POOL_GW)
        cnt = jnp.minimum(w, pos + 1).astype(F32)
        d = []
        for b in range(bb):
            e = e_ref[b, :, cols]
            s = e + pltpu.roll(e, 1, axis=0)
            for k in range(1, g + 1):
                s = s + pltpu.roll(s, 2**k, axis=0)
            d.append(s[HIST_PAD:, :] / cnt - e[HIST_PAD:, :])
        d = jnp.concatenate(d, axis=0).astype(BF16)
        y = jnp.dot(d, wp_ref[g].astype(BF16), preferred_element_type=F32)
        yp_ref[:, cols] = (y * sc_ref[:, cols]).astype(BF16)

    e_ref[:, 0:HIST_PAD, :] = e_ref[:, tt:tt + HIST_PAD, :]

    @pl.when(ti == tiles_per_seq - 1)
    def _():
        st_ref[...] = e_ref[:, 1:HIST_PAD, :]


def _inproj_rows(x, norm_w, w_t, w_gate_t, prev, w_pool, scale, *, bb, tt, pos0, tiles_per_seq, convert=()):
    assert MAIN_W == POOL_W + N_POOL_GROUPS * INPROJ_COL_CHUNK
    tm = bb * tt
    m = x.shape[0]
    n_seq = m // (tt * tiles_per_seq)
    row = lambda w: pl.BlockSpec((tm, w), lambda i: (i, 0))
    whole = lambda a: pl.BlockSpec(a.shape, lambda i: (0,) * a.ndim, pipeline_mode=pl.Buffered(1))
    seq_blk = lambda i: (i // tiles_per_seq, 0, 0)
    prev_spec = whole(prev) if prev.shape[0] == 1 else pl.BlockSpec((bb, POOL_HIST, POOL_W), seq_blk)
    conv_specs, conv_shapes = _convert_specs(convert, m // tm, lambda i: i)
    return pl.pallas_call(
        functools.partial(_inproj_rows_kernel, bb=bb, tt=tt, pos0=pos0, tiles_per_seq=tiles_per_seq,
                          conv_blocks=tuple(n for _, n in convert)),
        grid=(m // tm,),
        in_specs=[row(D_MODEL), whole(norm_w), whole(w_t), whole(w_gate_t), prev_spec, whole(w_pool),
                  whole(scale)] + conv_specs,
        out_specs=[row(POOL_W), row(QKVO_W), row(LANES),
                   pl.BlockSpec((bb, POOL_HIST, POOL_W), seq_blk)] + conv_specs,
        out_shape=[
            jax.ShapeDtypeStruct((m, POOL_W), BF16),
            jax.ShapeDtypeStruct((m, QKVO_W), BF16),
            jax.ShapeDtypeStruct((m, LANES), F32),
            jax.ShapeDtypeStruct((n_seq, POOL_HIST, POOL_W), F32),
        ] + conv_shapes,
        scratch_shapes=[pltpu.VMEM((bb, HIST_PAD + tt, POOL_W), F32)],
        compiler_params=pltpu.CompilerParams(
            dimension_semantics=("arbitrary",), vmem_limit_bytes=VMEM_LIMIT_RESIDENT_WEIGHT),
        name="inproj_rows",
    )(x, norm_w, w_t, w_gate_t, prev, w_pool, scale, *[w for w, _ in convert])


def _convert_specs(weights, n_steps, step_of):
    specs, shapes = [], []
    for w, n_blocks in weights:
        assert n_blocks <= n_steps
        rows = w.shape[0] // n_blocks
        specs.append(pl.BlockSpec(
            (rows, w.shape[1]), lambda *idx, n_blocks=n_blocks: (jnp.minimum(step_of(*idx), n_blocks - 1), 0)))
        shapes.append(jax.ShapeDtypeStruct(w.shape, BF16))
    return specs, shapes


def _convert_blocks(step, srcs, dsts, n_blocks):
    for src, dst, n in zip(srcs, dsts, n_blocks):
        @pl.when(step < n)
        def _(src=src, dst=dst):
            dst[...] = src[...].astype(BF16)


def _mlstm_chunk(q, k, v, o, gi, gf, c, n, m, gnorm, *, chunk):
    ti = lax.broadcasted_iota(jnp.int32, (chunk, chunk), 0)
    si = lax.broadcasted_iota(jnp.int32, (chunk, chunk), 1)
    tril = si <= ti
    eye = si == ti

    def to_row(col):
        return jnp.sum(jnp.where(eye, col, 0.0), axis=0, keepdims=True)

    lf = jax.nn.log_sigmoid(gf)
    b = jnp.sum(jnp.where(tril, to_row(lf), 0.0), axis=1, keepdims=True)
    a = gi - b
    a_row = to_row(a)
    cummax_a = jnp.max(jnp.where(tril, a_row, -jnp.inf), axis=1, keepdims=True)
    m_t = jnp.maximum(m + b, cummax_a + b)
    inter = jnp.exp(m + b - m_t)
    dmat = jnp.exp(jnp.where(tril, a_row + (b - m_t), -jnp.inf))

    k = k * (HEAD_DIM ** -0.5)
    s = lax.dot_general(q, k, (((1,), (1,)), ((), ())), preferred_element_type=F32) * dmat
    num = jnp.dot(s.astype(BF16), v, preferred_element_type=F32)
    num = num + inter * jnp.dot(q, c.astype(BF16), preferred_element_type=F32)
    qn = jnp.sum(s, axis=1, keepdims=True) + inter * jnp.sum(q.astype(F32) * n, axis=1, keepdims=True)
    h = num / jnp.maximum(jnp.abs(qn), jnp.exp(-m_t))

    b_last = b[chunk - 1:chunk, :]
    m_new = m_t[chunk - 1:chunk, :]
    decay = jnp.exp(m + b_last - m_new)
    w = jnp.exp(a + (b_last - m_new))
    wv = (w * v.astype(F32)).astype(BF16)
    c_new = decay * c + lax.dot_general(k, wv, (((0,), (0,)), ((), ())), preferred_element_type=F32)
    n_new = decay * n + jnp.sum(w * k.astype(F32), axis=0, keepdims=True)

    y = jax.nn.sigmoid(o.astype(F32)) * _rms(h, gnorm)
    return y, c_new, n_new, m_new


def _gate_cols(g, bias_ref, head):
    lane = lax.broadcasted_iota(jnp.int32, g.shape, 1)
    gi = jnp.sum(jnp.where(lane == head, g, 0.0), axis=1, keepdims=True) + bias_ref[head]
    gf = jnp.sum(jnp.where(lane == head + N_HEADS, g, 0.0), axis=1, keepdims=True) + bias_ref[head + N_HEADS]
    return gi, gf


N_STEP_IN = 9
N_STEP_OUT = 4


def _mlstm_seq_kernel(bias_ref, q_ref, k_ref, v_ref, o_ref, g_ref, gn_ref, c0_ref, n0_ref, m0_ref, *rest,
                      chunk, conv_blocks, step):
    nc = len(conv_blocks)
    ns_in = N_STEP_IN if step else 0
    conv_in, step_in = rest[:nc], rest[nc:nc + ns_in]
    outs = rest[nc + ns_in:]
    (y_ref, c_ref, n_ref, m_ref), conv_out, step_out = outs[:4], outs[4:4 + nc], outs[4 + nc:]
    ci = pl.program_id(1)
    grid_step = pl.program_id(0) * pl.num_programs(1) + ci
    _convert_blocks(grid_step, conv_in, conv_out, conv_blocks)

    @pl.when(ci == 0)
    def _():
        c_ref[...] = c0_ref[...]
        n_ref[...] = n0_ref[...]
        m_ref[...] = m0_ref[...]

    stages = iter(())
    if step:
        stages = _mlstm_step_stages(grid_step % N_HEADS, bias_ref, *step_in, *step_out, bb=step[0], seq=step[1])
    next(stages, None)

    g = g_ref[...]
    for head in range(N_HEADS):
        cols = slice(head * HEAD_DIM, (head + 1) * HEAD_DIM)
        gi, gf = _gate_cols(g, bias_ref, head)
        y, c_new, n_new, m_new = _mlstm_chunk(
            q_ref[:, cols], k_ref[:, cols], v_ref[:, cols], o_ref[:, cols], gi, gf,
            c_ref[0, head], n_ref[0, head], m_ref[0, head], gn_ref[:, cols], chunk=chunk)
        y_ref[:, cols] = y.astype(BF16)
        c_ref[0, head] = c_new
        n_ref[0, head] = n_new
        m_ref[0, head] = m_new
        next(stages, None)
    for _ in stages:
        pass


def _mlstm_seq(qkvo, gates, bias, gnorm, c0, n0, m0, *, batch, seq, chunk, shared_init, convert=(), step=None):
    nc = seq // chunk
    rows = lambda bi, ci: bi * nc + ci
    conv_specs, conv_shapes = _convert_specs(convert, batch * nc, rows)
    step_args, step_in, step_out, step_shapes, step_cfg = [], [], [], [], None
    if step:
        (s_qkvo, s_gates, s_c0, s_n0, s_m0), s_batch, s_seq, s_bb = step
        step_in, step_out, step_shapes = _mlstm_step_specs(s_batch, s_seq, s_bb, batch * nc, rows)
        step_args = [s_qkvo, s_qkvo, s_qkvo, s_qkvo, s_gates, gnorm, s_c0, s_n0, s_m0]
        step_cfg = (s_bb, s_seq)
    st_b = (lambda bi: 0) if shared_init else (lambda bi: bi)

    def col_spec(group):
        return pl.BlockSpec((chunk, MLSTM_W), lambda bi, ci: (rows(bi, ci), group))

    def st_spec(shape, in_b):
        nd = len(shape)
        return pl.BlockSpec((1, N_HEADS) + shape, lambda bi, ci: (in_b(bi), 0) + (0,) * nd)

    ident = lambda bi: bi
    return pl.pallas_call(
        functools.partial(_mlstm_seq_kernel, chunk=chunk, conv_blocks=tuple(n for _, n in convert), step=step_cfg),
        grid=(batch, nc),
        in_specs=[
            pl.BlockSpec(memory_space=pltpu.SMEM),
            col_spec(0), col_spec(1), col_spec(2), col_spec(3),
            pl.BlockSpec((chunk, LANES), lambda bi, ci: (rows(bi, ci), 0)),
            pl.BlockSpec((1, MLSTM_W), lambda bi, ci: (0, 0)),
            st_spec((HEAD_DIM, HEAD_DIM), st_b), st_spec((1, HEAD_DIM), st_b), st_spec((1, 1), st_b),
        ] + conv_specs + step_in,
        out_specs=[
            pl.BlockSpec((chunk, MLSTM_W), lambda bi, ci: (rows(bi, ci), 0)),
            st_spec((HEAD_DIM, HEAD_DIM), ident), st_spec((1, HEAD_DIM), ident), st_spec((1, 1), ident),
        ] + conv_specs + step_out,
        out_shape=[
            jax.ShapeDtypeStruct((batch * seq, MLSTM_W), BF16),
            jax.ShapeDtypeStruct((batch, N_HEADS, HEAD_DIM, HEAD_DIM), F32),
            jax.ShapeDtypeStruct((batch, N_HEADS, 1, HEAD_DIM), F32),
            jax.ShapeDtypeStruct((batch, N_HEADS, 1, 1), F32),
        ] + conv_shapes + step_shapes,
        compiler_params=_params(("arbitrary", "arbitrary")),
        name="mlstm_seq",
    )(bias, qkvo, qkvo, qkvo, qkvo, gates, gnorm, c0, n0, m0, *[w for w, _ in convert], *step_args)


def _mlstm_step_stages(head, bias_ref, q_ref, k_ref, v_ref, o_ref, g_ref, gn_ref, c0_ref, n0_ref, m0_ref,
                       y_ref, c_ref, n_ref, m_ref, *, bb, seq):
    rows = bb * seq
    ti = lax.broadcasted_iota(jnp.int32, (rows, rows), 0)
    si = lax.broadcasted_iota(jnp.int32, (rows, rows), 1)
    same = (ti // seq) == (si // seq)
    tril = same & (si <= ti)
    eye = si == ti

    def to_row(col):
        return jnp.sum(jnp.where(eye, col, 0.0), axis=0, keepdims=True)

    gi, gf = _gate_cols(g_ref[...], bias_ref, head)
    lane = lax.broadcasted_iota(jnp.int32, m0_ref.shape, 1)
    m0 = jnp.sum(jnp.where(lane == head, m0_ref[...], 0.0), axis=1, keepdims=True)

    lf_row = to_row(jax.nn.log_sigmoid(gf))
    b = jnp.sum(jnp.where(tril, lf_row, 0.0), axis=1, keepdims=True)
    b_last = jnp.sum(jnp.where(same, lf_row, 0.0), axis=1, keepdims=True)
    a = gi - b
    a_row = to_row(a)
    cummax_a = jnp.max(jnp.where(tril, a_row, -jnp.inf), axis=1, keepdims=True)
    seqmax_a = jnp.max(jnp.where(same, a_row, -jnp.inf), axis=1, keepdims=True)
    m_t = jnp.maximum(m0 + b, cummax_a + b)
    m_new = jnp.maximum(m0 + b_last, seqmax_a + b_last)
    inter = jnp.exp(m0 + b - m_t)
    dmat = jnp.exp(jnp.where(tril, a_row + (b - m_t), -jnp.inf))
    decay = jnp.exp(m0 + b_last - m_new)
    w = jnp.exp(a + (b_last - m_new))

    q = q_ref[...]
    k = k_ref[...] * (HEAD_DIM ** -0.5)
    v = v_ref[...]
    s = lax.dot_general(q, k, (((1,), (1,)), ((), ())), preferred_element_type=F32) * dmat
    sv = jnp.dot(s.astype(BF16), v, preferred_element_type=F32)

    qf = q.astype(F32)
    kf = k.astype(F32)
    wk = w * kf
    wv = w * v.astype(F32)
    qc, qdn = [], []
    yield
    for i in range(bb):
        r = slice(i * seq, (i + 1) * seq)
        c_i = c0_ref[i, 0]
        n_i = n0_ref[i:i + 1, :]
        d_i = decay[i * seq:i * seq + 1, :]
        qc.append(jnp.dot(qf[r].astype(BF16), c_i.astype(BF16), preferred_element_type=F32))
        qdn.append(jnp.sum(qf[r] * n_i, axis=1, keepdims=True))
        upd = lax.dot_general(kf[r].astype(BF16), wv[r].astype(BF16), (((0,), (0,)), ((), ())),
                              preferred_element_type=F32)
        c_ref[i, 0] = d_i * c_i + upd
        n_ref[i:i + 1, :] = d_i * n_i + jnp.sum(wk[r], axis=0, keepdims=True)
        if (i + 1) % (bb // N_HEADS) == 0:
            yield

    num = sv + inter * jnp.concatenate(qc, axis=0)
    qn = jnp.sum(s, axis=1, keepdims=True) + inter * jnp.concatenate(qdn, axis=0)
    h = num / jnp.maximum(jnp.abs(qn), jnp.exp(-m_t))
    y_ref[...] = (jax.nn.sigmoid(o_ref[...].astype(F32)) * _rms(h, gn_ref[...])).astype(BF16)
    m_ref[0, 0] = to_row(m_new)


def _mlstm_step_specs(batch, seq, bb, n_steps, step_of):
    assert (batch // bb) * N_HEADS == n_steps
    rows = bb * seq
    blk = lambda *idx: step_of(*idx) // N_HEADS
    head = lambda *idx: step_of(*idx) % N_HEADS

    def col_spec(off):
        return pl.BlockSpec((rows, HEAD_DIM), lambda *idx: (blk(*idx), off + head(*idx)))

    c_spec = pl.BlockSpec((bb, 1, HEAD_DIM, HEAD_DIM), lambda *idx: (blk(*idx), head(*idx), 0, 0))
    n_spec = pl.BlockSpec((bb, HEAD_DIM), lambda *idx: (blk(*idx), head(*idx)))
    in_specs = [
        col_spec(0), col_spec(N_HEADS), col_spec(2 * N_HEADS), col_spec(3 * N_HEADS),
        pl.BlockSpec((rows, LANES), lambda *idx: (blk(*idx), 0)),
        pl.BlockSpec((1, HEAD_DIM), lambda *idx: (0, head(*idx))),
        c_spec, n_spec,
        pl.BlockSpec((rows, N_HEADS), lambda *idx: (blk(*idx), 0)),
    ]
    out_specs = [
        pl.BlockSpec((rows, HEAD_DIM), lambda *idx: (blk(*idx), head(*idx))),
        c_spec, n_spec,
        pl.BlockSpec((1, 1, 1, rows), lambda *idx: (blk(*idx), head(*idx), 0, 0)),
    ]
    out_shapes = [
        jax.ShapeDtypeStruct((batch * seq, MLSTM_W), BF16),
        jax.ShapeDtypeStruct((batch, N_HEADS, HEAD_DIM, HEAD_DIM), F32),
        jax.ShapeDtypeStruct((batch, MLSTM_W), F32),
        jax.ShapeDtypeStruct((batch // bb, N_HEADS, 1, rows), F32),
    ]
    return in_specs, out_specs, out_shapes


def _outproj_kernel(x_ref, yp_ref, ym_ref, wo_ref, nw_ref, *rest, conv_blocks):
    nc = len(conv_blocks)
    conv_in, (x2_ref, xn2_ref), conv_out = rest[:nc], rest[nc:nc + 2], rest[nc + 2:]
    _convert_blocks(pl.program_id(0), conv_in, conv_out, conv_blocks)
    x2 = x_ref[...] + jnp.dot(yp_ref[...], wo_ref[0:POOL_W, :], preferred_element_type=F32)
    x2 = x2 + jnp.dot(ym_ref[...], wo_ref[POOL_W:, :], preferred_element_type=F32)
    x2_ref[...] = x2
    xn2_ref[...] = _rms(x2, nw_ref[...]).astype(BF16)


def _outproj(x, y_pool, y_ml, w_out, norm_w, *, tm, convert=()):
    m = x.shape[0]
    row = lambda w: pl.BlockSpec((tm, w), lambda i: (i, 0))
    conv_specs, conv_shapes = _convert_specs(convert, m // tm, lambda i: i)
    return pl.pallas_call(
        functools.partial(_outproj_kernel, conv_blocks=tuple(n for _, n in convert)),
        grid=(m // tm,),
        in_specs=[
            row(D_MODEL), row(POOL_W), row(MLSTM_W),
            pl.BlockSpec((D_MODEL, D_MODEL), lambda i: (0, 0), pipeline_mode=pl.Buffered(1)),
            pl.BlockSpec((1, D_MODEL), lambda i: (0, 0)),
        ] + conv_specs,
        out_specs=[row(D_MODEL), row(D_MODEL)] + conv_specs,
        out_shape=[
            jax.ShapeDtypeStruct((m, D_MODEL), F32),
            jax.ShapeDtypeStruct((m, D_MODEL), BF16),
        ] + conv_shapes,
        compiler_params=_params(("arbitrary",)),
        name="outproj",
    )(x, y_pool, y_ml, w_out, norm_w, *[w for w, _ in convert])


def _ffn_kernel(xn_ref, x2_ref, wg_ref, wu_ref, wd_ref, nw_ref, y_ref, acc_ref):
    f = pl.program_id(1)
    nf = pl.num_programs(1)

    @pl.when(f == 0)
    def _():
        acc_ref[...] = x2_ref[...]

    xn = xn_ref[...]
    g = jnp.dot(xn, wg_ref[...], preferred_element_type=F32)
    u = jnp.dot(xn, wu_ref[...], preferred_element_type=F32)
    h = (jax.nn.silu(g) * u).astype(BF16)
    acc_ref[...] += jnp.dot(h, wd_ref[...], preferred_element_type=F32)

    @pl.when(f == nf - 1)
    def _():
        y_ref[...] = _rms(acc_ref[...], nw_ref[...])


def _ffn(xn2, x2, w_gate, w_up, w_down, norm_w, *, tm, tf):
    m = xn2.shape[0]
    return pl.pallas_call(
        _ffn_kernel,
        grid=(m // tm, D_FF // tf),
        in_specs=[
            pl.BlockSpec((tm, D_MODEL), lambda i, f: (i, 0)),
            pl.BlockSpec((tm, D_MODEL), lambda i, f: (i, 0)),
            pl.BlockSpec((D_MODEL, tf), lambda i, f: (0, f)),
            pl.BlockSpec((D_MODEL, tf), lambda i, f: (0, f)),
            pl.BlockSpec((tf, D_MODEL), lambda i, f: (f, 0)),
            pl.BlockSpec((1, D_MODEL), lambda i, f: (0, 0)),
        ],
        out_specs=pl.BlockSpec((tm, D_MODEL), lambda i, f: (i, 0)),
        out_shape=jax.ShapeDtypeStruct((m, D_MODEL), F32),
        scratch_shapes=[pltpu.VMEM((tm, D_MODEL), F32)],
        compiler_params=_params(("arbitrary", "arbitrary")),
        name="ffn",
    )(xn2, x2, w_gate, w_up, w_down, norm_w)


def kernel(x_prompt, x_sample, state_pool, state_mlstm_C, state_mlstm_n, state_mlstm_m, meta_tokens, norm_mix_w, w_in, b_igate, b_fgate, w_pool, pool_scale, mlstm_norm_w, w_out, norm_ffn_w, w_gate, w_up, w_down, norm_final_w):
    bp, tp, _ = x_prompt.shape
    bs, ts, _ = x_sample.shape

    w_in_t = w_in[0].T
    w_g8 = jnp.pad(w_in_t[MAIN_W:], ((0, LANES - 2 * N_HEADS), (0, 0))).astype(BF16)
    wp = w_pool[0]
    nmix = norm_mix_w[0].reshape(1, D_MODEL)
    nffn = norm_ffn_w[0].reshape(1, D_MODEL)
    nfin = norm_final_w.reshape(1, D_MODEL)
    scale = pool_scale[0].reshape(1, POOL_W)
    gnorm = mlstm_norm_w[0].reshape(1, MLSTM_W)
    bias = jnp.concatenate([b_igate[0], b_fgate[0]]).astype(F32)

    xp = x_prompt.reshape(bp * tp, D_MODEL)
    xs = x_sample.reshape(bs * ts, D_MODEL)

    u_m, qkvo_m, g_m, w_main = _inproj(meta_tokens, nmix, w_in_t, w_g8, tm=N_META, tn=512)
    prev_p = u_m[1:N_META].reshape(1, POOL_HIST, POOL_W)
    prompt_tt = 512
    yp_p, qkvo_p, g_p, pool_p, wu = _inproj_rows(
        xp, nmix, w_main, w_g8, prev_p, wp, scale, bb=1, tt=prompt_tt, pos0=N_META, tiles_per_seq=tp // prompt_tt,
        convert=((w_up[0], 16),))
    yp_s, qkvo_s, g_s, pool_s = _inproj_rows(
        xs, nmix, w_main, w_g8, state_pool[0], wp, scale, bb=32, tt=ts, pos0=PAST_LEN, tiles_per_seq=1)

    zc = jnp.zeros((1, N_HEADS, HEAD_DIM, HEAD_DIM), F32)
    zn = jnp.zeros((1, N_HEADS, 1, HEAD_DIM), F32)
    zm = jnp.zeros((1, N_HEADS, 1, 1), F32)
    _, c_m, n_m, m_m = _mlstm_seq(qkvo_m, g_m, bias, gnorm, zc, zn, zm, batch=1, seq=N_META, chunk=N_META, shared_init=False)
    convert = ((w_out[0], 32), (w_gate[0], 32), (w_down[0], 32))
    step_bb = 16
    step_ops = (qkvo_s, g_s, state_mlstm_C[0], state_mlstm_n[0].reshape(bs, MLSTM_W),
                jnp.repeat(state_mlstm_m[0], ts, axis=0))
    ym_p, c_p, n_p, m_p, wo, wg, wd, ym_s, c_s, n_s, m_s = _mlstm_seq(
        qkvo_p, g_p, bias, gnorm, c_m, n_m, m_m, batch=bp, seq=tp, chunk=256, shared_init=True, convert=convert,
        step=(step_ops, bs, ts, step_bb))
    m_s = m_s.reshape(bs // step_bb, N_HEADS, step_bb, ts)[..., 0].transpose(0, 2, 1)

    x2_p, xn2_p = _outproj(xp, yp_p, ym_p, wo, nffn, tm=512)
    x2_s, xn2_s = _outproj(xs, yp_s, ym_s, wo, nffn, tm=512)
    y_p = _ffn(xn2_p, x2_p, wg, wu, wd, nfin, tm=512, tf=512)
    y_s = _ffn(xn2_s, x2_s, wg, wu, wd, nfin, tm=512, tf=512)

    return (
        y_p.reshape(bp, tp, D_MODEL),
        y_s.reshape(bs, ts, D_MODEL),
        pool_p[None],
        c_p[None],
        n_p.reshape(1, bp, N_HEADS, HEAD_DIM),
        m_p.reshape(1, bp, N_HEADS),
        pool_s[None],
        c_s[None],
        n_s.reshape(1, bs, N_HEADS, HEAD_DIM),
        m_s.reshape(1, bs, N_HEADS),
    )
```

```python
import functools

import jax
import jax.numpy as jnp
from jax import lax
from jax.experimental import pallas as pl
from jax.experimental.pallas import tpu as pltpu

D_MODEL = 2048
N_META = 16
POOL_W = 1024
MLSTM_W = 1024
POOL_WINDOWS = (2, 4, 8, 16)
N_POOL_GROUPS = 4
POOL_GW = 256
POOL_HIST = 15
N_HEADS = 4
HEAD_DIM = 256
D_FF = 5632
QKVO_W = 4 * MLSTM_W
MAIN_W = POOL_W + QKVO_W
PAST_LEN = 16384
EPS = 1e-6

LANES = 128
HIST_PAD = 16
VMEM_LIMIT = 56 * 1024 * 1024
VMEM_LIMIT_RESIDENT_WEIGHT = 60 * 1024 * 1024

BF16 = jnp.bfloat16
F32 = jnp.float32


def _params(sem):
    return pltpu.CompilerParams(dimension_semantics=sem, vmem_limit_bytes=VMEM_LIMIT)


def _rms(x, w):
    return x * lax.rsqrt(jnp.mean(x * x, axis=-1, keepdims=True) + EPS) * w


def _inproj_kernel(x_ref, nw_ref, w_ref, wg_ref, *rest, n_u_tiles, emit_copy, conv_blocks):
    nc = len(conv_blocks)
    conv_in, (u_ref, qkvo_ref, g_ref), rest = rest[:nc], rest[nc:nc + 3], rest[nc + 3:]
    conv_out, xn_ref = rest[len(rest) - 1 - nc:-1], rest[-1]
    j = pl.program_id(1)
    _convert_blocks(pl.program_id(0) * pl.num_programs(1) + j, conv_in, conv_out, conv_blocks)

    nt = (((1,), (1,)), ((), ()))

    @pl.when(j == 0)
    def _():
        xn = _rms(x_ref[...], nw_ref[...]).astype(BF16)
        xn_ref[...] = xn
        g_ref[...] = lax.dot_general(xn, wg_ref[...], nt, preferred_element_type=F32)

    w = w_ref[...]
    if emit_copy:
        w = w.astype(BF16)
        rest[0][...] = w
    p = lax.dot_general(xn_ref[...], w, nt, preferred_element_type=F32)

    @pl.when(j < n_u_tiles)
    def _():
        u_ref[...] = p

    @pl.when(j >= n_u_tiles)
    def _():
        qkvo_ref[...] = p.astype(BF16)


def _inproj(x, norm_w, w_t, w_gate_t, *, tm, tn, convert=()):
    m = x.shape[0]
    n_u = POOL_W // tn
    n_j = MAIN_W // tn
    emit_copy = w_t.dtype == F32
    if emit_copy:
        assert m == tm, "the bf16 copy is written once per column tile"
    w_spec = pl.BlockSpec((tn, D_MODEL), lambda i, j: (j, 0))
    copy_spec = [w_spec] if emit_copy else []
    copy_shape = [jax.ShapeDtypeStruct((MAIN_W, D_MODEL), BF16)] if emit_copy else []
    conv_specs, conv_shapes = _convert_specs(convert, (m // tm) * n_j, lambda i, j: i * n_j + j)
    return pl.pallas_call(
        functools.partial(_inproj_kernel, n_u_tiles=n_u, emit_copy=emit_copy,
                          conv_blocks=tuple(n for _, n in convert)),
        grid=(m // tm, n_j),
        in_specs=[
            pl.BlockSpec((tm, D_MODEL), lambda i, j: (i, 0)),
            pl.BlockSpec((1, D_MODEL), lambda i, j: (0, 0)),
            w_spec,
            pl.BlockSpec((LANES, D_MODEL), lambda i, j: (0, 0)),
        ] + conv_specs,
        out_specs=[
            pl.BlockSpec((tm, tn), lambda i, j: (i, jnp.minimum(j, n_u - 1))),
            pl.BlockSpec((tm, tn), lambda i, j: (i, jnp.maximum(j - n_u, 0))),
            pl.BlockSpec((tm, LANES), lambda i, j: (i, 0)),
        ] + copy_spec + conv_specs,
        out_shape=[
            jax.ShapeDtypeStruct((m, POOL_W), F32),
            jax.ShapeDtypeStruct((m, QKVO_W), BF16),
            jax.ShapeDtypeStruct((m, LANES), F32),
        ] + copy_shape + conv_shapes,
        scratch_shapes=[pltpu.VMEM((tm, D_MODEL), BF16)],
        compiler_params=_params(("arbitrary", "arbitrary")),
        name="inproj",
    )(x, norm_w, w_t, w_gate_t, *[w for w, _ in convert])


INPROJ_COL_CHUNK = 1024


def _inproj_rows_kernel(x_ref, nw_ref, w_ref, wg_ref, prev_ref, wp_ref, sc_ref, *rest,
                        bb, tt, pos0, tiles_per_seq, conv_blocks):
    nc = len(conv_blocks)
    conv_in, (yp_ref, qkvo_ref, g_ref, st_ref) = rest[:nc], rest[nc:nc + 4]
    conv_out, e_ref = rest[nc + 4:2 * nc + 4], rest[-1]
    _convert_blocks(pl.program_id(0), conv_in, conv_out, conv_blocks)
    nt = (((1,), (1,)), ((), ()))
    ti = pl.program_id(0) % tiles_per_seq

    @pl.when(ti == 0)
    def _():
        e_ref[:, 0:1, :] = jnp.zeros((bb, 1, POOL_W), F32)
        e_ref[:, 1:HIST_PAD, :] = prev_ref[...]

    xn = _rms(x_ref[...], nw_ref[...]).astype(BF16)
    g_ref[...] = lax.dot_general(xn, wg_ref[...], nt, preferred_element_type=F32)
    u = lax.dot_general(xn, w_ref[0:POOL_W, :], nt, preferred_element_type=F32)
    e_ref[:, HIST_PAD:, :] = u.reshape(bb, tt, POOL_W)
    pos = pos0 + ti * tt + lax.broadcasted_iota(jnp.int32, (tt, 1), 0)

    for g, w in enumerate(POOL_WINDOWS):
        c = POOL_W + g * INPROJ_COL_CHUNK
        p = lax.dot_general(xn, w_ref[c:c + INPROJ_COL_CHUNK, :], nt, preferred_element_type=F32)
        qkvo_ref[:, c - POOL_W:c - POOL_W + INPROJ_COL_CHUNK] = p.astype(BF16)

        cols = slice(g * POOL_GW, (g + 1) * POOL_GW)
        cnt = jnp.minimum(w, pos + 1).astype(F32)
        d = []
        for b in range(bb):
            e = e_ref[b, :, cols]
            s = e + pltpu.roll(e, 1, axis=0)
            for k in range(1, g + 1):
                s = s + pltpu.roll(s, 2**k, axis=0)
            d.append(s[HIST_PAD:, :] / cnt - e[HIST_PAD:, :])
        d = jnp.concatenate(d, axis=0).astype(BF16)
        y = jnp.dot(d, wp_ref[g].astype(BF16), preferred_element_type=F32)
        yp_ref[:, cols] = (y * sc_ref[:, cols]).astype(BF16)

    e_ref[:, 0:HIST_PAD, :] = e_ref[:, tt:tt + HIST_PAD, :]

    @pl.when(ti == tiles_per_seq - 1)
    def _():
        st_ref[...] = e_ref[:, 1:HIST_PAD, :]


def _inproj_rows(x, norm_w, w_t, w_gate_t, prev, w_pool, scale, *, bb, tt, pos0, tiles_per_seq, convert=()):
    assert MAIN_W == POOL_W + N_POOL_GROUPS * INPROJ_COL_CHUNK
    tm = bb * tt
    m = x.shape[0]
    n_seq = m // (tt * tiles_per_seq)
    row = lambda w: pl.BlockSpec((tm, w), lambda i: (i, 0))
    whole = lambda a: pl.BlockSpec(a.shape, lambda i: (0,) * a.ndim, pipeline_mode=pl.Buffered(1))
    seq_blk = lambda i: (i // tiles_per_seq, 0, 0)
    prev_spec = whole(prev) if prev.shape[0] == 1 else pl.BlockSpec((bb, POOL_HIST, POOL_W), seq_blk)
    conv_specs, conv_shapes = _convert_specs(convert, m // tm, lambda i: i)
    return pl.pallas_call(
        functools.partial(_inproj_rows_kernel, bb=bb, tt=tt, pos0=pos0, tiles_per_seq=tiles_per_seq,
                          conv_blocks=tuple(n for _, n in convert)),
        grid=(m // tm,),
        in_specs=[row(D_MODEL), whole(norm_w), whole(w_t), whole(w_gate_t), prev_spec, whole(w_pool),
                  whole(scale)] + conv_specs,
        out_specs=[row(POOL_W), row(QKVO_W), row(LANES),
                   pl.BlockSpec((bb, POOL_HIST, POOL_W), seq_blk)] + conv_specs,
        out_shape=[
            jax.ShapeDtypeStruct((m, POOL_W), BF16),
            jax.ShapeDtypeStruct((m, QKVO_W), BF16),
            jax.ShapeDtypeStruct((m, LANES), F32),
            jax.ShapeDtypeStruct((n_seq, POOL_HIST, POOL_W), F32),
        ] + conv_shapes,
        scratch_shapes=[pltpu.VMEM((bb, HIST_PAD + tt, POOL_W), F32)],
        compiler_params=pltpu.CompilerParams(
            dimension_semantics=("arbitrary",), vmem_limit_bytes=VMEM_LIMIT_RESIDENT_WEIGHT),
        name="inproj_rows",
    )(x, norm_w, w_t, w_gate_t, prev, w_pool, scale, *[w for w, _ in convert])


def _convert_specs(weights, n_steps, step_of):
    specs, shapes = [], []
    for w, n_blocks in weights:
        assert n_blocks <= n_steps
        rows = w.shape[0] // n_blocks
        specs.append(pl.BlockSpec(
            (rows, w.shape[1]), lambda *idx, n_blocks=n_blocks: (jnp.minimum(step_of(*idx), n_blocks - 1), 0)))
        shapes.append(jax.ShapeDtypeStruct(w.shape, BF16))
    return specs, shapes


def _convert_blocks(step, srcs, dsts, n_blocks):
    for src, dst, n in zip(srcs, dsts, n_blocks):
        @pl.when(step < n)
        def _(src=src, dst=dst):
            dst[...] = src[...].astype(BF16)


def _mlstm_chunk(q, k, v, gi, gf, c, n, m, *, chunk):
    ti = lax.broadcasted_iota(jnp.int32, (chunk, chunk), 0)
    si = lax.broadcasted_iota(jnp.int32, (chunk, chunk), 1)
    tril = si <= ti
    eye = si == ti

    def to_row(col):
        return jnp.sum(jnp.where(eye, col, 0.0), axis=0, keepdims=True)

    lf = jax.nn.log_sigmoid(gf)
    b = jnp.sum(jnp.where(tril, to_row(lf), 0.0), axis=1, keepdims=True)
    a = gi - b
    a_row = to_row(a)
    cummax_a = jnp.max(jnp.where(tril, a_row, -jnp.inf), axis=1, keepdims=True)
    m_t = jnp.maximum(m + b, cummax_a + b)
    inter = jnp.exp(m + b - m_t)
    dmat = jnp.exp(jnp.where(tril, a_row + (b - m_t), -jnp.inf))

    k = k * (HEAD_DIM ** -0.5)
    s = lax.dot_general(q, k, (((1,), (1,)), ((), ())), preferred_element_type=F32) * dmat
    num = jnp.dot(s.astype(BF16), v, preferred_element_type=F32)
    num = num + inter * jnp.dot(q, c.astype(BF16), preferred_element_type=F32)
    qn = jnp.sum(s, axis=1, keepdims=True) + inter * jnp.sum(q.astype(F32) * n, axis=1, keepdims=True)
    h = num / jnp.maximum(jnp.abs(qn), jnp.exp(-m_t))

    b_last = b[chunk - 1:chunk, :]
    m_new = m_t[chunk - 1:chunk, :]
    decay = jnp.exp(m + b_last - m_new)
    w = jnp.exp(a + (b_last - m_new))
    wv = (w * v.astype(F32)).astype(BF16)
    c_new = decay * c + lax.dot_general(k, wv, (((0,), (0,)), ((), ())), preferred_element_type=F32)
    n_new = decay * n + jnp.sum(w * k.astype(F32), axis=0, keepdims=True)

    return h, c_new, n_new, m_new


def _gate_cols(g, bias_ref, head):
    lane = lax.broadcasted_iota(jnp.int32, g.shape, 1)
    gi = jnp.sum(jnp.where(lane == head, g, 0.0), axis=1, keepdims=True) + bias_ref[head]
    gf = jnp.sum(jnp.where(lane == head + N_HEADS, g, 0.0), axis=1, keepdims=True) + bias_ref[head + N_HEADS]
    return gi, gf


N_STEP_IN = 7


def _mlstm_seq_kernel(bias_ref, q_ref, k_ref, v_ref, g_ref, c0_ref, n0_ref, m0_ref, *rest,
                      chunk, conv_blocks, step):
    nc = len(conv_blocks)
    ns_in = N_STEP_IN if step else 0
    conv_in, step_in = rest[:nc], rest[nc:nc + ns_in]
    outs = rest[nc + ns_in:]
    (y_ref, c_ref, n_ref, m_ref), conv_out, step_out = outs[:4], outs[4:4 + nc], outs[4 + nc:]
    ci = pl.program_id(1)
    grid_step = pl.program_id(0) * pl.num_programs(1) + ci
    _convert_blocks(grid_step, conv_in, conv_out, conv_blocks)

    @pl.when(ci == 0)
    def _():
        c_ref[...] = c0_ref[...]
        n_ref[...] = n0_ref[...]
        m_ref[...] = m0_ref[...]

    stages = iter(())
    if step:
        stages = _mlstm_step_stages(grid_step % N_HEADS, bias_ref, *step_in, *step_out, bb=step[0], seq=step[1])
    next(stages, None)

    g = g_ref[...]
    for head in range(N_HEADS):
        cols = slice(head * HEAD_DIM, (head + 1) * HEAD_DIM)
        gi, gf = _gate_cols(g, bias_ref, head)
        h, c_new, n_new, m_new = _mlstm_chunk(
            q_ref[:, cols], k_ref[:, cols], v_ref[:, cols], gi, gf,
            c_ref[0, head], n_ref[0, head], m_ref[0, head], chunk=chunk)
        y_ref[:, cols] = h
        c_ref[0, head] = c_new
        n_ref[0, head] = n_new
        m_ref[0, head] = m_new
        next(stages, None)
    for _ in stages:
        pass


def _mlstm_seq(qkvo, gates, bias, c0, n0, m0, *, batch, seq, chunk, shared_init, convert=(), step=None):
    nc = seq // chunk
    rows = lambda bi, ci: bi * nc + ci
    conv_specs, conv_shapes = _convert_specs(convert, batch * nc, rows)
    step_args, step_in, step_out, step_shapes, step_cfg = [], [], [], [], None
    if step:
        (s_qkvo, s_gates, s_c0, s_n0, s_m0), s_batch, s_seq, s_bb = step
        step_in, step_out, step_shapes = _mlstm_step_specs(s_batch, s_seq, s_bb, batch * nc, rows)
        step_args = [s_qkvo, s_qkvo, s_qkvo, s_gates, s_c0, s_n0, s_m0]
        step_cfg = (s_bb, s_seq)
    st_b = (lambda bi: 0) if shared_init else (lambda bi: bi)

    def col_spec(group):
        return pl.BlockSpec((chunk, MLSTM_W), lambda bi, ci: (rows(bi, ci), group))

    def st_spec(shape, in_b):
        nd = len(shape)
        return pl.BlockSpec((1, N_HEADS) + shape, lambda bi, ci: (in_b(bi), 0) + (0,) * nd)

    ident = lambda bi: bi
    return pl.pallas_call(
        functools.partial(_mlstm_seq_kernel, chunk=chunk, conv_blocks=tuple(n for _, n in convert), step=step_cfg),
        grid=(batch, nc),
        in_specs=[
            pl.BlockSpec(memory_space=pltpu.SMEM),
            col_spec(0), col_spec(1), col_spec(2),
            pl.BlockSpec((chunk, LANES), lambda bi, ci: (rows(bi, ci), 0)),
            st_spec((HEAD_DIM, HEAD_DIM), st_b), st_spec((1, HEAD_DIM), st_b), st_spec((1, 1), st_b),
        ] + conv_specs + step_in,
        out_specs=[
            pl.BlockSpec((chunk, MLSTM_W), lambda bi, ci: (rows(bi, ci), 0)),
            st_spec((HEAD_DIM, HEAD_DIM), ident), st_spec((1, HEAD_DIM), ident), st_spec((1, 1), ident),
        ] + conv_specs + step_out,
        out_shape=[
            jax.ShapeDtypeStruct((batch * seq, MLSTM_W), F32),
            jax.ShapeDtypeStruct((batch, N_HEADS, HEAD_DIM, HEAD_DIM), F32),
            jax.ShapeDtypeStruct((batch, N_HEADS, 1, HEAD_DIM), F32),
            jax.ShapeDtypeStruct((batch, N_HEADS, 1, 1), F32),
        ] + conv_shapes + step_shapes,
        compiler_params=_params(("arbitrary", "arbitrary")),
        name="mlstm_seq",
    )(bias, qkvo, qkvo, qkvo, gates, c0, n0, m0, *[w for w, _ in convert], *step_args)


def _mlstm_step_stages(head, bias_ref, q_ref, k_ref, v_ref, g_ref, c0_ref, n0_ref, m0_ref,
                       y_ref, c_ref, n_ref, m_ref, *, bb, seq):
    rows = bb * seq
    ti = lax.broadcasted_iota(jnp.int32, (rows, rows), 0)
    si = lax.broadcasted_iota(jnp.int32, (rows, rows), 1)
    same = (ti // seq) == (si // seq)
    tril = same & (si <= ti)
    eye = si == ti

    def to_row(col):
        return jnp.sum(jnp.where(eye, col, 0.0), axis=0, keepdims=True)

    gi, gf = _gate_cols(g_ref[...], bias_ref, head)
    lane = lax.broadcasted_iota(jnp.int32, m0_ref.shape, 1)
    m0 = jnp.sum(jnp.where(lane == head, m0_ref[...], 0.0), axis=1, keepdims=True)

    lf_row = to_row(jax.nn.log_sigmoid(gf))
    b = jnp.sum(jnp.where(tril, lf_row, 0.0), axis=1, keepdims=True)
    b_last = jnp.sum(jnp.where(same, lf_row, 0.0), axis=1, keepdims=True)
    a = gi - b
    a_row = to_row(a)
    cummax_a = jnp.max(jnp.where(tril, a_row, -jnp.inf), axis=1, keepdims=True)
    seqmax_a = jnp.max(jnp.where(same, a_row, -jnp.inf), axis=1, keepdims=True)
    m_t = jnp.maximum(m0 + b, cummax_a + b)
    m_new = jnp.maximum(m0 + b_last, seqmax_a + b_last)
    inter = jnp.exp(m0 + b - m_t)
    dmat = jnp.exp(jnp.where(tril, a_row + (b - m_t), -jnp.inf))
    decay = jnp.exp(m0 + b_last - m_new)
    w = jnp.exp(a + (b_last - m_new))

    q = q_ref[...]
    k = k_ref[...] * (HEAD_DIM ** -0.5)
    v = v_ref[...]
    s = lax.dot_general(q, k, (((1,), (1,)), ((), ())), preferred_element_type=F32) * dmat
    sv = jnp.dot(s.astype(BF16), v, preferred_element_type=F32)

    qf = q.astype(F32)
    kf = k.astype(F32)
    wk = w * kf
    wv = w * v.astype(F32)
    qc, qdn = [], []
    yield
    for i in range(bb):
        r = slice(i * seq, (i + 1) * seq)
        c_i = c0_ref[i, 0]
        n_i = n0_ref[i:i + 1, :]
        d_i = decay[i * seq:i * seq + 1, :]
        qc.append(jnp.dot(qf[r].astype(BF16), c_i.astype(BF16), preferred_element_type=F32))
        qdn.append(jnp.sum(qf[r] * n_i, axis=1, keepdims=True))
        upd = lax.dot_general(kf[r].astype(BF16), wv[r].astype(BF16), (((0,), (0,)), ((), ())),
                              preferred_element_type=F32)
        c_ref[i, 0] = d_i * c_i + upd
        n_ref[i:i + 1, :] = d_i * n_i + jnp.sum(wk[r], axis=0, keepdims=True)
        if (i + 1) % (bb // N_HEADS) == 0:
            yield

    num = sv + inter * jnp.concatenate(qc, axis=0)
    qn = jnp.sum(s, axis=1, keepdims=True) + inter * jnp.concatenate(qdn, axis=0)
    h = num / jnp.maximum(jnp.abs(qn), jnp.exp(-m_t))
    y_ref[...] = h
    m_ref[0, 0] = to_row(m_new)


def _mlstm_step_specs(batch, seq, bb, n_steps, step_of):
    assert (batch // bb) * N_HEADS == n_steps
    rows = bb * seq
    blk = lambda *idx: step_of(*idx) // N_HEADS
    head = lambda *idx: step_of(*idx) % N_HEADS

    def col_spec(off):
        return pl.BlockSpec((rows, HEAD_DIM), lambda *idx: (blk(*idx), off + head(*idx)))

    c_spec = pl.BlockSpec((bb, 1, HEAD_DIM, HEAD_DIM), lambda *idx: (blk(*idx), head(*idx), 0, 0))
    n_spec = pl.BlockSpec((bb, HEAD_DIM), lambda *idx: (blk(*idx), head(*idx)))
    in_specs = [
        col_spec(0), col_spec(N_HEADS), col_spec(2 * N_HEADS),
        pl.BlockSpec((rows, LANES), lambda *idx: (blk(*idx), 0)),
        c_spec, n_spec,
        pl.BlockSpec((rows, N_HEADS), lambda *idx: (blk(*idx), 0)),
    ]
    out_specs = [
        pl.BlockSpec((rows, HEAD_DIM), lambda *idx: (blk(*idx), head(*idx))),
        c_spec, n_spec,
        pl.BlockSpec((1, 1, 1, rows), lambda *idx: (blk(*idx), head(*idx), 0, 0)),
    ]
    out_shapes = [
        jax.ShapeDtypeStruct((batch * seq, MLSTM_W), F32),
        jax.ShapeDtypeStruct((batch, N_HEADS, HEAD_DIM, HEAD_DIM), F32),
        jax.ShapeDtypeStruct((batch, MLSTM_W), F32),
        jax.ShapeDtypeStruct((batch // bb, N_HEADS, 1, rows), F32),
    ]
    return in_specs, out_specs, out_shapes


def _outproj_kernel(x_ref, yp_ref, h_ref, o_ref, gn_ref, wo_ref, nw_ref, *rest, conv_blocks):
    nc = len(conv_blocks)
    conv_in, (x2_ref, xn2_ref), conv_out = rest[:nc], rest[nc:nc + 2], rest[nc + 2:]
    _convert_blocks(pl.program_id(0), conv_in, conv_out, conv_blocks)
    x2 = x_ref[...] + jnp.dot(yp_ref[...], wo_ref[0:POOL_W, :], preferred_element_type=F32)
    ym = []
    for head in range(N_HEADS):
        cols = slice(head * HEAD_DIM, (head + 1) * HEAD_DIM)
        y = jax.nn.sigmoid(o_ref[:, cols].astype(F32)) * _rms(h_ref[:, cols], gn_ref[:, cols])
        ym.append(y.astype(BF16))
    x2 = x2 + jnp.dot(jnp.concatenate(ym, axis=1), wo_ref[POOL_W:, :], preferred_element_type=F32)
    x2_ref[...] = x2
    xn2_ref[...] = _rms(x2, nw_ref[...]).astype(BF16)


def _outproj(x, y_pool, h_ml, qkvo, gnorm, w_out, norm_w, *, tm, convert=()):
    m = x.shape[0]
    row = lambda w: pl.BlockSpec((tm, w), lambda i: (i, 0))
    conv_specs, conv_shapes = _convert_specs(convert, m // tm, lambda i: i)
    return pl.pallas_call(
        functools.partial(_outproj_kernel, conv_blocks=tuple(n for _, n in convert)),
        grid=(m // tm,),
        in_specs=[
            row(D_MODEL), row(POOL_W), row(MLSTM_W),
            pl.BlockSpec((tm, MLSTM_W), lambda i: (i, 3)),
            pl.BlockSpec((1, MLSTM_W), lambda i: (0, 0)),
            pl.BlockSpec((D_MODEL, D_MODEL), lambda i: (0, 0), pipeline_mode=pl.Buffered(1)),
            pl.BlockSpec((1, D_MODEL), lambda i: (0, 0)),
        ] + conv_specs,
        out_specs=[row(D_MODEL), row(D_MODEL)] + conv_specs,
        out_shape=[
            jax.ShapeDtypeStruct((m, D_MODEL), F32),
            jax.ShapeDtypeStruct((m, D_MODEL), BF16),
        ] + conv_shapes,
        compiler_params=_params(("arbitrary",)),
        name="outproj",
    )(x, y_pool, h_ml, qkvo, gnorm, w_out, norm_w, *[w for w, _ in convert])


def _ffn_kernel(xn_ref, x2_ref, wg_ref, wu_ref, wd_ref, nw_ref, y_ref, acc_ref):
    f = pl.program_id(1)
    nf = pl.num_programs(1)

    @pl.when(f == 0)
    def _():
        acc_ref[...] = x2_ref[...]

    xn = xn_ref[...]
    g = jnp.dot(xn, wg_ref[...], preferred_element_type=F32)
    u = jnp.dot(xn, wu_ref[...], preferred_element_type=F32)
    h = (jax.nn.silu(g) * u).astype(BF16)
    acc_ref[...] += jnp.dot(h, wd_ref[...], preferred_element_type=F32)

    @pl.when(f == nf - 1)
    def _():
        y_ref[...] = _rms(acc_ref[...], nw_ref[...])


def _ffn(xn2, x2, w_gate, w_up, w_down, norm_w, *, tm, tf):
    m = xn2.shape[0]
    return pl.pallas_call(
        _ffn_kernel,
        grid=(m // tm, D_FF // tf),
        in_specs=[
            pl.BlockSpec((tm, D_MODEL), lambda i, f: (i, 0)),
            pl.BlockSpec((tm, D_MODEL), lambda i, f: (i, 0)),
            pl.BlockSpec((D_MODEL, tf), lambda i, f: (0, f)),
            pl.BlockSpec((D_MODEL, tf), lambda i, f: (0, f)),
            pl.BlockSpec((tf, D_MODEL), lambda i, f: (f, 0)),
            pl.BlockSpec((1, D_MODEL), lambda i, f: (0, 0)),
        ],
        out_specs=pl.BlockSpec((tm, D_MODEL), lambda i, f: (i, 0)),
        out_shape=jax.ShapeDtypeStruct((m, D_MODEL), F32),
        scratch_shapes=[pltpu.VMEM((tm, D_MODEL), F32)],
        compiler_params=_params(("arbitrary", "arbitrary")),
        name="ffn",
    )(xn2, x2, w_gate, w_up, w_down, norm_w)


def kernel(x_prompt, x_sample, state_pool, state_mlstm_C, state_mlstm_n, state_mlstm_m, meta_tokens, norm_mix_w, w_in, b_igate, b_fgate, w_pool, pool_scale, mlstm_norm_w, w_out, norm_ffn_w, w_gate, w_up, w_down, norm_final_w):
    bp, tp, _ = x_prompt.shape
    bs, ts, _ = x_sample.shape

    w_in_t = w_in[0].T
    w_g8 = jnp.pad(w_in_t[MAIN_W:], ((0, LANES - 2 * N_HEADS), (0, 0))).astype(BF16)
    wp = w_pool[0]
    nmix = norm_mix_w[0].reshape(1, D_MODEL)
    nffn = norm_ffn_w[0].reshape(1, D_MODEL)
    nfin = norm_final_w.reshape(1, D_MODEL)
    scale = pool_scale[0].reshape(1, POOL_W)
    gnorm = mlstm_norm_w[0].reshape(1, MLSTM_W)
    bias = jnp.concatenate([b_igate[0], b_fgate[0]]).astype(F32)

    xp = x_prompt.reshape(bp * tp, D_MODEL)
    xs = x_sample.reshape(bs * ts, D_MODEL)

    u_m, qkvo_m, g_m, w_main = _inproj(meta_tokens, nmix, w_in_t, w_g8, tm=N_META, tn=512)
    prev_p = u_m[1:N_META].reshape(1, POOL_HIST, POOL_W)
    prompt_tt = 256
    convert = ((w_out[0], 32), (w_gate[0], 32), (w_up[0], 32), (w_down[0], 32))
    yp_p, qkvo_p, g_p, pool_p, wo, wg, wu, wd = _inproj_rows(
        xp, nmix, w_main, w_g8, prev_p, wp, scale, bb=1, tt=prompt_tt, pos0=N_META, tiles_per_seq=tp // prompt_tt,
        convert=convert)
    yp_s, qkvo_s, g_s, pool_s = _inproj_rows(
        xs, nmix, w_main, w_g8, state_pool[0], wp, scale, bb=32, tt=ts, pos0=PAST_LEN, tiles_per_seq=1)

    zc = jnp.zeros((1, N_HEADS, HEAD_DIM, HEAD_DIM), F32)
    zn = jnp.zeros((1, N_HEADS, 1, HEAD_DIM), F32)
    zm = jnp.zeros((1, N_HEADS, 1, 1), F32)
    _, c_m, n_m, m_m = _mlstm_seq(qkvo_m, g_m, bias, zc, zn, zm, batch=1, seq=N_META, chunk=N_META, shared_init=False)
    step_bb = 16
    step_ops = (qkvo_s, g_s, state_mlstm_C[0], state_mlstm_n[0].reshape(bs, MLSTM_W),
                jnp.repeat(state_mlstm_m[0], ts, axis=0))
    h_p, c_p, n_p, m_p, h_s, c_s, n_s, m_s = _mlstm_seq(
        qkvo_p, g_p, bias, c_m, n_m, m_m, batch=bp, seq=tp, chunk=256, shared_init=True,
        step=(step_ops, bs, ts, step_bb))
    m_s = m_s.reshape(bs // step_bb, N_HEADS, step_bb, ts)[..., 0].transpose(0, 2, 1)

    x2_p, xn2_p = _outproj(xp, yp_p, h_p, qkvo_p, gnorm, wo, nffn, tm=512)
    x2_s, xn2_s = _outproj(xs, yp_s, h_s, qkvo_s, gnorm, wo, nffn, tm=512)
    y_p = _ffn(xn2_p, x2_p, wg, wu, wd, nfin, tm=512, tf=512)
    y_s = _ffn(xn2_s, x2_s, wg, wu, wd, nfin, tm=512, tf=512)

    return (
        y_p.reshape(bp, tp, D_MODEL),
        y_s.reshape(bs, ts, D_MODEL),
        pool_p[None],
        c_p[None],
        n_p.reshape(1, bp, N_HEADS, HEAD_DIM),
        m_p.reshape(1, bp, N_HEADS),
        pool_s[None],
        c_s[None],
        n_s.reshape(1, bs, N_HEADS, HEAD_DIM),
        m_s.reshape(1, bs, N_HEADS),
    )
```

```python
import functools

import jax
import jax.numpy as jnp
from jax import lax
from jax.experimental import pallas as pl
from jax.experimental.pallas import tpu as pltpu

D_MODEL = 2048
N_META = 16
POOL_W = 1024
MLSTM_W = 1024
POOL_WINDOWS = (2, 4, 8, 16)
N_POOL_GROUPS = 4
POOL_GW = 256
POOL_HIST = 15
N_HEADS = 4
HEAD_DIM = 256
D_FF = 5632
QKVO_W = 4 * MLSTM_W
MAIN_W = POOL_W + QKVO_W
PAST_LEN = 16384
EPS = 1e-6

LANES = 128
HIST_PAD = 16
VMEM_LIMIT = 56 * 1024 * 1024
VMEM_LIMIT_RESIDENT_WEIGHT = 60 * 1024 * 1024

BF16 = jnp.bfloat16
F32 = jnp.float32


def _params(sem):
    return pltpu.CompilerParams(dimension_semantics=sem, vmem_limit_bytes=VMEM_LIMIT)


def _rms(x, w):
    return x * lax.rsqrt(jnp.mean(x * x, axis=-1, keepdims=True) + EPS) * w


def _inproj_kernel(x_ref, nw_ref, w_ref, wg_ref, *rest, n_u_tiles, emit_copy, conv_blocks):
    nc = len(conv_blocks)
    conv_in, (u_ref, qkvo_ref, g_ref), rest = rest[:nc], rest[nc:nc + 3], rest[nc + 3:]
    conv_out, xn_ref = rest[len(rest) - 1 - nc:-1], rest[-1]
    j = pl.program_id(1)
    _convert_blocks(pl.program_id(0) * pl.num_programs(1) + j, conv_in, conv_out, conv_blocks)

    nt = (((1,), (1,)), ((), ()))

    @pl.when(j == 0)
    def _():
        xn = _rms(x_ref[...], nw_ref[...]).astype(BF16)
        xn_ref[...] = xn
        g_ref[...] = lax.dot_general(xn, wg_ref[...], nt, preferred_element_type=F32)

    w = w_ref[...]
    if emit_copy:
        w = w.astype(BF16)
        rest[0][...] = w
    p = lax.dot_general(xn_ref[...], w, nt, preferred_element_type=F32)

    @pl.when(j < n_u_tiles)
    def _():
        u_ref[...] = p

    @pl.when(j >= n_u_tiles)
    def _():
        qkvo_ref[...] = p.astype(BF16)


def _inproj(x, norm_w, w_t, w_gate_t, *, tm, tn, convert=()):
    m = x.shape[0]
    n_u = POOL_W // tn
    n_j = MAIN_W // tn
    emit_copy = w_t.dtype == F32
    if emit_copy:
        assert m == tm, "the bf16 copy is written once per column tile"
    w_spec = pl.BlockSpec((tn, D_MODEL), lambda i, j: (j, 0))
    copy_spec = [w_spec] if emit_copy else []
    copy_shape = [jax.ShapeDtypeStruct((MAIN_W, D_MODEL), BF16)] if emit_copy else []
    conv_specs, conv_shapes = _convert_specs(convert, (m // tm) * n_j, lambda i, j: i * n_j + j)
    return pl.pallas_call(
        functools.partial(_inproj_kernel, n_u_tiles=n_u, emit_copy=emit_copy,
                          conv_blocks=tuple(n for _, n in convert)),
        grid=(m // tm, n_j),
        in_specs=[
            pl.BlockSpec((tm, D_MODEL), lambda i, j: (i, 0)),
            pl.BlockSpec((1, D_MODEL), lambda i, j: (0, 0)),
            w_spec,
            pl.BlockSpec((LANES, D_MODEL), lambda i, j: (0, 0)),
        ] + conv_specs,
        out_specs=[
            pl.BlockSpec((tm, tn), lambda i, j: (i, jnp.minimum(j, n_u - 1))),
            pl.BlockSpec((tm, tn), lambda i, j: (i, jnp.maximum(j - n_u, 0))),
            pl.BlockSpec((tm, LANES), lambda i, j: (i, 0)),
        ] + copy_spec + conv_specs,
        out_shape=[
            jax.ShapeDtypeStruct((m, POOL_W), F32),
            jax.ShapeDtypeStruct((m, QKVO_W), BF16),
            jax.ShapeDtypeStruct((m, LANES), F32),
        ] + copy_shape + conv_shapes,
        scratch_shapes=[pltpu.VMEM((tm, D_MODEL), BF16)],
        compiler_params=_params(("arbitrary", "arbitrary")),
        name="inproj",
    )(x, norm_w, w_t, w_gate_t, *[w for w, _ in convert])


INPROJ_COL_CHUNK = 1024


def _inproj_rows_kernel(x_ref, nw_ref, w_ref, wg_ref, prev_ref, wp_ref, sc_ref, *rest,
                        bb, tt, pos0, tiles_per_seq, conv_blocks):
    nc = len(conv_blocks)
    conv_in, (yp_ref, qkvo_ref, g_ref, st_ref) = rest[:nc], rest[nc:nc + 4]
    conv_out, e_ref = rest[nc + 4:2 * nc + 4], rest[-1]
    _convert_blocks(pl.program_id(0), conv_in, conv_out, conv_blocks)
    nt = (((1,), (1,)), ((), ()))
    ti = pl.program_id(0) % tiles_per_seq

    @pl.when(ti == 0)
    def _():
        e_ref[:, 0:1, :] = jnp.zeros((bb, 1, POOL_W), F32)
        e_ref[:, 1:HIST_PAD, :] = prev_ref[...]

    xn = _rms(x_ref[...], nw_ref[...]).astype(BF16)
    g_ref[...] = lax.dot_general(xn, wg_ref[...], nt, preferred_element_type=F32)
    u = lax.dot_general(xn, w_ref[0:POOL_W, :], nt, preferred_element_type=F32)
    e_ref[:, HIST_PAD:, :] = u.reshape(bb, tt, POOL_W)
    pos = pos0 + ti * tt + lax.broadcasted_iota(jnp.int32, (tt, 1), 0)

    for g, w in enumerate(POOL_WINDOWS):
        c = POOL_W + g * INPROJ_COL_CHUNK
        p = lax.dot_general(xn, w_ref[c:c + INPROJ_COL_CHUNK, :], nt, preferred_element_type=F32)
        qkvo_ref[:, c - POOL_W:c - POOL_W + INPROJ_COL_CHUNK] = p.astype(BF16)

        cols = slice(g * POOL_GW, (g + 1) * POOL_GW)
        cnt = jnp.minimum(w, pos + 1).astype(F32)
        d = []
        for b in range(bb):
            e = e_ref[b, :, cols]
            s = e + pltpu.roll(e, 1, axis=0)
            for k in range(1, g + 1):
                s = s + pltpu.roll(s, 2**k, axis=0)
            d.append(s[HIST_PAD:, :] / cnt - e[HIST_PAD:, :])
        d = jnp.concatenate(d, axis=0).astype(BF16)
        y = jnp.dot(d, wp_ref[g].astype(BF16), preferred_element_type=F32)
        yp_ref[:, cols] = (y * sc_ref[:, cols]).astype(BF16)

    e_ref[:, 0:HIST_PAD, :] = e_ref[:, tt:tt + HIST_PAD, :]

    @pl.when(ti == tiles_per_seq - 1)
    def _():
        st_ref[...] = e_ref[:, 1:HIST_PAD, :]


def _inproj_rows(x, norm_w, w_t, w_gate_t, prev, w_pool, scale, *, bb, tt, pos0, tiles_per_seq, convert=()):
    assert MAIN_W == POOL_W + N_POOL_GROUPS * INPROJ_COL_CHUNK
    tm = bb * tt
    m = x.shape[0]
    n_seq = m // (tt * tiles_per_seq)
    row = lambda w: pl.BlockSpec((tm, w), lambda i: (i, 0))
    whole = lambda a: pl.BlockSpec(a.shape, lambda i: (0,) * a.ndim, pipeline_mode=pl.Buffered(1))
    seq_blk = lambda i: (i // tiles_per_seq, 0, 0)
    prev_spec = whole(prev) if prev.shape[0] == 1 else pl.BlockSpec((bb, POOL_HIST, POOL_W), seq_blk)
    conv_specs, conv_shapes = _convert_specs(convert, m // tm, lambda i: i)
    return pl.pallas_call(
        functools.partial(_inproj_rows_kernel, bb=bb, tt=tt, pos0=pos0, tiles_per_seq=tiles_per_seq,
                          conv_blocks=tuple(n for _, n in convert)),
        grid=(m // tm,),
        in_specs=[row(D_MODEL), whole(norm_w), whole(w_t), whole(w_gate_t), prev_spec, whole(w_pool),
                  whole(scale)] + conv_specs,
        out_specs=[row(POOL_W), row(QKVO_W), row(LANES),
                   pl.BlockSpec((bb, POOL_HIST, POOL_W), seq_blk)] + conv_specs,
        out_shape=[
            jax.ShapeDtypeStruct((m, POOL_W), BF16),
            jax.ShapeDtypeStruct((m, QKVO_W), BF16),
            jax.ShapeDtypeStruct((m, LANES), F32),
            jax.ShapeDtypeStruct((n_seq, POOL_HIST, POOL_W), F32),
        ] + conv_shapes,
        scratch_shapes=[pltpu.VMEM((bb, HIST_PAD + tt, POOL_W), F32)],
        compiler_params=pltpu.CompilerParams(
            dimension_semantics=("arbitrary",), vmem_limit_bytes=VMEM_LIMIT_RESIDENT_WEIGHT),
        name="inproj_rows",
    )(x, norm_w, w_t, w_gate_t, prev, w_pool, scale, *[w for w, _ in convert])


def _convert_specs(weights, n_steps, step_of):
    specs, shapes = [], []
    for w, n_blocks in weights:
        assert n_blocks <= n_steps
        rows = w.shape[0] // n_blocks
        specs.append(pl.BlockSpec(
            (rows, w.shape[1]), lambda *idx, n_blocks=n_blocks: (jnp.minimum(step_of(*idx), n_blocks - 1), 0)))
        shapes.append(jax.ShapeDtypeStruct(w.shape, BF16))
    return specs, shapes


def _convert_blocks(step, srcs, dsts, n_blocks):
    for src, dst, n in zip(srcs, dsts, n_blocks):
        @pl.when(step < n)
        def _(src=src, dst=dst):
            dst[...] = src[...].astype(BF16)


def _mlstm_chunk(q, k, v, gi, gf, c, n, m, *, chunk):
    ti = lax.broadcasted_iota(jnp.int32, (chunk, chunk), 0)
    si = lax.broadcasted_iota(jnp.int32, (chunk, chunk), 1)
    tril = si <= ti
    eye = si == ti

    def to_row(col):
        return jnp.sum(jnp.where(eye, col, 0.0), axis=0, keepdims=True)

    lf = jax.nn.log_sigmoid(gf)
    b = jnp.sum(jnp.where(tril, to_row(lf), 0.0), axis=1, keepdims=True)
    a = gi - b
    a_row = to_row(a)
    cummax_a = jnp.max(jnp.where(tril, a_row, -jnp.inf), axis=1, keepdims=True)
    m_t = jnp.maximum(m + b, cummax_a + b)
    inter = jnp.exp(m + b - m_t)
    dmat = jnp.exp(jnp.where(tril, a_row + (b - m_t), -jnp.inf))

    k = k * (HEAD_DIM ** -0.5)
    s = lax.dot_general(q, k, (((1,), (1,)), ((), ())), preferred_element_type=F32) * dmat
    num = jnp.dot(s.astype(BF16), v, preferred_element_type=F32)
    num = num + inter * jnp.dot(q, c.astype(BF16), preferred_element_type=F32)
    qn = jnp.sum(s, axis=1, keepdims=True) + inter * jnp.sum(q.astype(F32) * n, axis=1, keepdims=True)
    h = num / jnp.maximum(jnp.abs(qn), jnp.exp(-m_t))

    b_last = b[chunk - 1:chunk, :]
    m_new = m_t[chunk - 1:chunk, :]
    decay = jnp.exp(m + b_last - m_new)
    w = jnp.exp(a + (b_last - m_new))
    wv = (w * v.astype(F32)).astype(BF16)
    c_new = decay * c + lax.dot_general(k, wv, (((0,), (0,)), ((), ())), preferred_element_type=F32)
    n_new = decay * n + jnp.sum(w * k.astype(F32), axis=0, keepdims=True)

    return h, c_new, n_new, m_new


def _gate_cols(g, bias_ref, head):
    lane = lax.broadcasted_iota(jnp.int32, g.shape, 1)
    gi = jnp.sum(jnp.where(lane == head, g, 0.0), axis=1, keepdims=True) + bias_ref[head]
    gf = jnp.sum(jnp.where(lane == head + N_HEADS, g, 0.0), axis=1, keepdims=True) + bias_ref[head + N_HEADS]
    return gi, gf


N_STEP_IN = 7


N_OUTPROJ_IN = 6


def _gated_head(h, o, gnorm):
    return (jax.nn.sigmoid(o.astype(F32)) * _rms(h, gnorm)).astype(BF16)


def _mlstm_seq_kernel(bias_ref, q_ref, k_ref, v_ref, g_ref, c0_ref, n0_ref, m0_ref, *rest,
                      chunk, conv_blocks, step, outproj):
    nc = len(conv_blocks)
    no_in = N_OUTPROJ_IN if outproj else 0
    ns_in = N_STEP_IN if step else 0
    op_in, conv_in, step_in = rest[:no_in], rest[no_in:no_in + nc], rest[no_in + nc:no_in + nc + ns_in]
    outs = rest[no_in + nc + ns_in:]
    n_rows_out = 2 if outproj else 1
    rows_out, (c_ref, n_ref, m_ref) = outs[:n_rows_out], outs[n_rows_out:n_rows_out + 3]
    conv_out, step_out = outs[n_rows_out + 3:n_rows_out + 3 + nc], outs[n_rows_out + 3 + nc:]
    ci = pl.program_id(1)
    grid_step = pl.program_id(0) * pl.num_programs(1) + ci
    _convert_blocks(grid_step, conv_in, conv_out, conv_blocks)

    @pl.when(ci == 0)
    def _():
        c_ref[...] = c0_ref[...]
        n_ref[...] = n0_ref[...]
        m_ref[...] = m0_ref[...]

    stages = iter(())
    if step:
        stages = _mlstm_step_stages(grid_step % N_HEADS, bias_ref, *step_in, *step_out, bb=step[0], seq=step[1])
    next(stages, None)

    if outproj:
        x_ref, yp_ref, o_ref, gn_ref, wo_ref, nw_ref = op_in
        x2 = x_ref[...] + jnp.dot(yp_ref[...], wo_ref[0:POOL_W, :], preferred_element_type=F32)
        ym = []

    g = g_ref[...]
    for head in range(N_HEADS):
        cols = slice(head * HEAD_DIM, (head + 1) * HEAD_DIM)
        gi, gf = _gate_cols(g, bias_ref, head)
        h, c_new, n_new, m_new = _mlstm_chunk(
            q_ref[:, cols], k_ref[:, cols], v_ref[:, cols], gi, gf,
            c_ref[0, head], n_ref[0, head], m_ref[0, head], chunk=chunk)
        if outproj:
            ym.append(_gated_head(h, o_ref[:, cols], gn_ref[:, cols]))
        else:
            rows_out[0][:, cols] = h
        c_ref[0, head] = c_new
        n_ref[0, head] = n_new
        m_ref[0, head] = m_new
        next(stages, None)
    for _ in stages:
        pass

    if outproj:
        x2_ref, xn2_ref = rows_out
        x2 = x2 + jnp.dot(jnp.concatenate(ym, axis=1), wo_ref[POOL_W:, :], preferred_element_type=F32)
        x2_ref[...] = x2
        xn2_ref[...] = _rms(x2, nw_ref[...]).astype(BF16)


def _mlstm_seq(qkvo, gates, bias, c0, n0, m0, *, batch, seq, chunk, shared_init, convert=(), step=None,
               outproj=None):
    nc = seq // chunk
    rows = lambda bi, ci: bi * nc + ci
    conv_specs, conv_shapes = _convert_specs(convert, batch * nc, rows)
    row_spec = lambda w: pl.BlockSpec((chunk, w), lambda bi, ci: (rows(bi, ci), 0))
    whole = lambda a: pl.BlockSpec(a.shape, lambda bi, ci: (0,) * a.ndim, pipeline_mode=pl.Buffered(1))
    if outproj:
        x, y_pool, gnorm, w_out, norm_w = outproj
        op_args = [x, y_pool, qkvo, gnorm, w_out, norm_w]
        op_in = [row_spec(D_MODEL), row_spec(POOL_W),
                 pl.BlockSpec((chunk, MLSTM_W), lambda bi, ci: (rows(bi, ci), 3)),
                 whole(gnorm), whole(w_out), whole(norm_w)]
        rows_out = [row_spec(D_MODEL), row_spec(D_MODEL)]
        rows_shapes = [jax.ShapeDtypeStruct((batch * seq, D_MODEL), F32),
                       jax.ShapeDtypeStruct((batch * seq, D_MODEL), BF16)]
    else:
        op_args, op_in = [], []
        rows_out = [row_spec(MLSTM_W)]
        rows_shapes = [jax.ShapeDtypeStruct((batch * seq, MLSTM_W), F32)]
    step_args, step_in, step_out, step_shapes, step_cfg = [], [], [], [], None
    if step:
        (s_qkvo, s_gates, s_c0, s_n0, s_m0), s_batch, s_seq, s_bb = step
        step_in, step_out, step_shapes = _mlstm_step_specs(s_batch, s_seq, s_bb, batch * nc, rows)
        step_args = [s_qkvo, s_qkvo, s_qkvo, s_gates, s_c0, s_n0, s_m0]
        step_cfg = (s_bb, s_seq)
    st_b = (lambda bi: 0) if shared_init else (lambda bi: bi)

    def col_spec(group):
        return pl.BlockSpec((chunk, MLSTM_W), lambda bi, ci: (rows(bi, ci), group))

    def st_spec(shape, in_b):
        nd = len(shape)
        return pl.BlockSpec((1, N_HEADS) + shape, lambda bi, ci: (in_b(bi), 0) + (0,) * nd)

    ident = lambda bi: bi
    return pl.pallas_call(
        functools.partial(_mlstm_seq_kernel, chunk=chunk, conv_blocks=tuple(n for _, n in convert), step=step_cfg,
                          outproj=bool(outproj)),
        grid=(batch, nc),
        in_specs=[
            pl.BlockSpec(memory_space=pltpu.SMEM),
            col_spec(0), col_spec(1), col_spec(2),
            pl.BlockSpec((chunk, LANES), lambda bi, ci: (rows(bi, ci), 0)),
            st_spec((HEAD_DIM, HEAD_DIM), st_b), st_spec((1, HEAD_DIM), st_b), st_spec((1, 1), st_b),
        ] + op_in + conv_specs + step_in,
        out_specs=rows_out + [
            st_spec((HEAD_DIM, HEAD_DIM), ident), st_spec((1, HEAD_DIM), ident), st_spec((1, 1), ident),
        ] + conv_specs + step_out,
        out_shape=rows_shapes + [
            jax.ShapeDtypeStruct((batch, N_HEADS, HEAD_DIM, HEAD_DIM), F32),
            jax.ShapeDtypeStruct((batch, N_HEADS, 1, HEAD_DIM), F32),
            jax.ShapeDtypeStruct((batch, N_HEADS, 1, 1), F32),
        ] + conv_shapes + step_shapes,
        compiler_params=pltpu.CompilerParams(
            dimension_semantics=("arbitrary", "arbitrary"),
            vmem_limit_bytes=VMEM_LIMIT_RESIDENT_WEIGHT if outproj else VMEM_LIMIT),
        name="mlstm_seq",
    )(bias, qkvo, qkvo, qkvo, gates, c0, n0, m0, *op_args, *[w for w, _ in convert], *step_args)


def _mlstm_step_stages(head, bias_ref, q_ref, k_ref, v_ref, g_ref, c0_ref, n0_ref, m0_ref,
                       y_ref, c_ref, n_ref, m_ref, *, bb, seq):
    rows = bb * seq
    ti = lax.broadcasted_iota(jnp.int32, (rows, rows), 0)
    si = lax.broadcasted_iota(jnp.int32, (rows, rows), 1)
    same = (ti // seq) == (si // seq)
    tril = same & (si <= ti)
    eye = si == ti

    def to_row(col):
        return jnp.sum(jnp.where(eye, col, 0.0), axis=0, keepdims=True)

    gi, gf = _gate_cols(g_ref[...], bias_ref, head)
    lane = lax.broadcasted_iota(jnp.int32, m0_ref.shape, 1)
    m0 = jnp.sum(jnp.where(lane == head, m0_ref[...], 0.0), axis=1, keepdims=True)

    lf_row = to_row(jax.nn.log_sigmoid(gf))
    b = jnp.sum(jnp.where(tril, lf_row, 0.0), axis=1, keepdims=True)
    b_last = jnp.sum(jnp.where(same, lf_row, 0.0), axis=1, keepdims=True)
    a = gi - b
    a_row = to_row(a)
    cummax_a = jnp.max(jnp.where(tril, a_row, -jnp.inf), axis=1, keepdims=True)
    seqmax_a = jnp.max(jnp.where(same, a_row, -jnp.inf), axis=1, keepdims=True)
    m_t = jnp.maximum(m0 + b, cummax_a + b)
    m_new = jnp.maximum(m0 + b_last, seqmax_a + b_last)
    inter = jnp.exp(m0 + b - m_t)
    dmat = jnp.exp(jnp.where(tril, a_row + (b - m_t), -jnp.inf))
    decay = jnp.exp(m0 + b_last - m_new)
    w = jnp.exp(a + (b_last - m_new))

    q = q_ref[...]
    k = k_ref[...] * (HEAD_DIM ** -0.5)
    v = v_ref[...]
    s = lax.dot_general(q, k, (((1,), (1,)), ((), ())), preferred_element_type=F32) * dmat
    sv = jnp.dot(s.astype(BF16), v, preferred_element_type=F32)

    qf = q.astype(F32)
    kf = k.astype(F32)
    wk = w * kf
    wv = w * v.astype(F32)
    qc, qdn = [], []
    yield
    for i in range(bb):
        r = slice(i * seq, (i + 1) * seq)
        c_i = c0_ref[i, 0]
        n_i = n0_ref[i:i + 1, :]
        d_i = decay[i * seq:i * seq + 1, :]
        qc.append(jnp.dot(qf[r].astype(BF16), c_i.astype(BF16), preferred_element_type=F32))
        qdn.append(jnp.sum(qf[r] * n_i, axis=1, keepdims=True))
        upd = lax.dot_general(kf[r].astype(BF16), wv[r].astype(BF16), (((0,), (0,)), ((), ())),
                              preferred_element_type=F32)
        c_ref[i, 0] = d_i * c_i + upd
        n_ref[i:i + 1, :] = d_i * n_i + jnp.sum(wk[r], axis=0, keepdims=True)
        if (i + 1) % (bb // N_HEADS) == 0:
            yield

    num = sv + inter * jnp.concatenate(qc, axis=0)
    qn = jnp.sum(s, axis=1, keepdims=True) + inter * jnp.concatenate(qdn, axis=0)
    h = num / jnp.maximum(jnp.abs(qn), jnp.exp(-m_t))
    y_ref[...] = h
    m_ref[0, 0] = to_row(m_new)


def _mlstm_step_specs(batch, seq, bb, n_steps, step_of):
    assert (batch // bb) * N_HEADS == n_steps
    rows = bb * seq
    blk = lambda *idx: step_of(*idx) // N_HEADS
    head = lambda *idx: step_of(*idx) % N_HEADS

    def col_spec(off):
        return pl.BlockSpec((rows, HEAD_DIM), lambda *idx: (blk(*idx), off + head(*idx)))

    c_spec = pl.BlockSpec((bb, 1, HEAD_DIM, HEAD_DIM), lambda *idx: (blk(*idx), head(*idx), 0, 0))
    n_spec = pl.BlockSpec((bb, HEAD_DIM), lambda *idx: (blk(*idx), head(*idx)))
    in_specs = [
        col_spec(0), col_spec(N_HEADS), col_spec(2 * N_HEADS),
        pl.BlockSpec((rows, LANES), lambda *idx: (blk(*idx), 0)),
        c_spec, n_spec,
        pl.BlockSpec((rows, N_HEADS), lambda *idx: (blk(*idx), 0)),
    ]
    out_specs = [
        pl.BlockSpec((rows, HEAD_DIM), lambda *idx: (blk(*idx), head(*idx))),
        c_spec, n_spec,
        pl.BlockSpec((1, 1, 1, rows), lambda *idx: (blk(*idx), head(*idx), 0, 0)),
    ]
    out_shapes = [
        jax.ShapeDtypeStruct((batch * seq, MLSTM_W), F32),
        jax.ShapeDtypeStruct((batch, N_HEADS, HEAD_DIM, HEAD_DIM), F32),
        jax.ShapeDtypeStruct((batch, MLSTM_W), F32),
        jax.ShapeDtypeStruct((batch // bb, N_HEADS, 1, rows), F32),
    ]
    return in_specs, out_specs, out_shapes


def _outproj_kernel(x_ref, yp_ref, h_ref, o_ref, gn_ref, wo_ref, nw_ref, *rest, conv_blocks):
    nc = len(conv_blocks)
    conv_in, (x2_ref, xn2_ref), conv_out = rest[:nc], rest[nc:nc + 2], rest[nc + 2:]
    _convert_blocks(pl.program_id(0), conv_in, conv_out, conv_blocks)
    x2 = x_ref[...] + jnp.dot(yp_ref[...], wo_ref[0:POOL_W, :], preferred_element_type=F32)
    ym = []
    for head in range(N_HEADS):
        cols = slice(head * HEAD_DIM, (head + 1) * HEAD_DIM)
        ym.append(_gated_head(h_ref[:, cols], o_ref[:, cols], gn_ref[:, cols]))
    x2 = x2 + jnp.dot(jnp.concatenate(ym, axis=1), wo_ref[POOL_W:, :], preferred_element_type=F32)
    x2_ref[...] = x2
    xn2_ref[...] = _rms(x2, nw_ref[...]).astype(BF16)


def _outproj(x, y_pool, h_ml, qkvo, gnorm, w_out, norm_w, *, tm, convert=()):
    m = x.shape[0]
    row = lambda w: pl.BlockSpec((tm, w), lambda i: (i, 0))
    conv_specs, conv_shapes = _convert_specs(convert, m // tm, lambda i: i)
    return pl.pallas_call(
        functools.partial(_outproj_kernel, conv_blocks=tuple(n for _, n in convert)),
        grid=(m // tm,),
        in_specs=[
            row(D_MODEL), row(POOL_W), row(MLSTM_W),
            pl.BlockSpec((tm, MLSTM_W), lambda i: (i, 3)),
            pl.BlockSpec((1, MLSTM_W), lambda i: (0, 0)),
            pl.BlockSpec((D_MODEL, D_MODEL), lambda i: (0, 0), pipeline_mode=pl.Buffered(1)),
            pl.BlockSpec((1, D_MODEL), lambda i: (0, 0)),
        ] + conv_specs,
        out_specs=[row(D_MODEL), row(D_MODEL)] + conv_specs,
        out_shape=[
            jax.ShapeDtypeStruct((m, D_MODEL), F32),
            jax.ShapeDtypeStruct((m, D_MODEL), BF16),
        ] + conv_shapes,
        compiler_params=_params(("arbitrary",)),
        name="outproj",
    )(x, y_pool, h_ml, qkvo, gnorm, w_out, norm_w, *[w for w, _ in convert])


def _ffn_kernel(xn_ref, x2_ref, wg_ref, wu_ref, wd_ref, nw_ref, y_ref, acc_ref):
    f = pl.program_id(1)
    nf = pl.num_programs(1)

    @pl.when(f == 0)
    def _():
        acc_ref[...] = x2_ref[...]

    xn = xn_ref[...]
    g = jnp.dot(xn, wg_ref[...], preferred_element_type=F32)
    u = jnp.dot(xn, wu_ref[...], preferred_element_type=F32)
    h = (jax.nn.silu(g) * u).astype(BF16)
    acc_ref[...] += jnp.dot(h, wd_ref[...], preferred_element_type=F32)

    @pl.when(f == nf - 1)
    def _():
        y_ref[...] = _rms(acc_ref[...], nw_ref[...])


def _ffn(xn2, x2, w_gate, w_up, w_down, norm_w, *, tm, tf):
    m = xn2.shape[0]
    return pl.pallas_call(
        _ffn_kernel,
        grid=(m // tm, D_FF // tf),
        in_specs=[
            pl.BlockSpec((tm, D_MODEL), lambda i, f: (i, 0)),
            pl.BlockSpec((tm, D_MODEL), lambda i, f: (i, 0)),
            pl.BlockSpec((D_MODEL, tf), lambda i, f: (0, f)),
            pl.BlockSpec((D_MODEL, tf), lambda i, f: (0, f)),
            pl.BlockSpec((tf, D_MODEL), lambda i, f: (f, 0)),
            pl.BlockSpec((1, D_MODEL), lambda i, f: (0, 0)),
        ],
        out_specs=pl.BlockSpec((tm, D_MODEL), lambda i, f: (i, 0)),
        out_shape=jax.ShapeDtypeStruct((m, D_MODEL), F32),
        scratch_shapes=[pltpu.VMEM((tm, D_MODEL), F32)],
        compiler_params=_params(("arbitrary", "arbitrary")),
        name="ffn",
    )(xn2, x2, w_gate, w_up, w_down, norm_w)


def kernel(x_prompt, x_sample, state_pool, state_mlstm_C, state_mlstm_n, state_mlstm_m, meta_tokens, norm_mix_w, w_in, b_igate, b_fgate, w_pool, pool_scale, mlstm_norm_w, w_out, norm_ffn_w, w_gate, w_up, w_down, norm_final_w):
    bp, tp, _ = x_prompt.shape
    bs, ts, _ = x_sample.shape

    w_in_t = w_in[0].T
    w_g8 = jnp.pad(w_in_t[MAIN_W:], ((0, LANES - 2 * N_HEADS), (0, 0))).astype(BF16)
    wp = w_pool[0]
    nmix = norm_mix_w[0].reshape(1, D_MODEL)
    nffn = norm_ffn_w[0].reshape(1, D_MODEL)
    nfin = norm_final_w.reshape(1, D_MODEL)
    scale = pool_scale[0].reshape(1, POOL_W)
    gnorm = mlstm_norm_w[0].reshape(1, MLSTM_W)
    bias = jnp.concatenate([b_igate[0], b_fgate[0]]).astype(F32)

    xp = x_prompt.reshape(bp * tp, D_MODEL)
    xs = x_sample.reshape(bs * ts, D_MODEL)

    u_m, qkvo_m, g_m, w_main = _inproj(meta_tokens, nmix, w_in_t, w_g8, tm=N_META, tn=512)
    prev_p = u_m[1:N_META].reshape(1, POOL_HIST, POOL_W)
    prompt_tt = 256
    convert = ((w_out[0], 32), (w_gate[0], 32), (w_up[0], 32), (w_down[0], 32))
    yp_p, qkvo_p, g_p, pool_p, wo, wg, wu, wd = _inproj_rows(
        xp, nmix, w_main, w_g8, prev_p, wp, scale, bb=1, tt=prompt_tt, pos0=N_META, tiles_per_seq=tp // prompt_tt,
        convert=convert)
    yp_s, qkvo_s, g_s, pool_s = _inproj_rows(
        xs, nmix, w_main, w_g8, state_pool[0], wp, scale, bb=32, tt=ts, pos0=PAST_LEN, tiles_per_seq=1)

    zc = jnp.zeros((1, N_HEADS, HEAD_DIM, HEAD_DIM), F32)
    zn = jnp.zeros((1, N_HEADS, 1, HEAD_DIM), F32)
    zm = jnp.zeros((1, N_HEADS, 1, 1), F32)
    _, c_m, n_m, m_m = _mlstm_seq(qkvo_m, g_m, bias, zc, zn, zm, batch=1, seq=N_META, chunk=N_META, shared_init=False)
    step_bb = 16
    step_ops = (qkvo_s, g_s, state_mlstm_C[0], state_mlstm_n[0].reshape(bs, MLSTM_W),
                jnp.repeat(state_mlstm_m[0], ts, axis=0))
    x2_p, xn2_p, c_p, n_p, m_p, h_s, c_s, n_s, m_s = _mlstm_seq(
        qkvo_p, g_p, bias, c_m, n_m, m_m, batch=bp, seq=tp, chunk=256, shared_init=True,
        step=(step_ops, bs, ts, step_bb), outproj=(xp, yp_p, gnorm, wo, nffn))
    m_s = m_s.reshape(bs // step_bb, N_HEADS, step_bb, ts)[..., 0].transpose(0, 2, 1)

    x2_s, xn2_s = _outproj(xs, yp_s, h_s, qkvo_s, gnorm, wo, nffn, tm=512)
    y_p = _ffn(xn2_p, x2_p, wg, wu, wd, nfin, tm=512, tf=512)
    y_s = _ffn(xn2_s, x2_s, wg, wu, wd, nfin, tm=512, tf=512)

    return (
        y_p.reshape(bp, tp, D_MODEL),
        y_s.reshape(bs, ts, D_MODEL),
        pool_p[None],
        c_p[None],
        n_p.reshape(1, bp, N_HEADS, HEAD_DIM),
        m_p.reshape(1, bp, N_HEADS),
        pool_s[None],
        c_s[None],
        n_s.reshape(1, bs, N_HEADS, HEAD_DIM),
        m_s.reshape(1, bs, N_HEADS),
    )
```

```python
import functools

import jax
import jax.numpy as jnp
from jax import lax
from jax.experimental import pallas as pl
from jax.experimental.pallas import tpu as pltpu

D_MODEL = 2048
N_META = 16
POOL_W = 1024
MLSTM_W = 1024
POOL_WINDOWS = (2, 4, 8, 16)
N_POOL_GROUPS = 4
POOL_GW = 256
POOL_HIST = 15
N_HEADS = 4
HEAD_DIM = 256
D_FF = 5632
QKVO_W = 4 * MLSTM_W
MAIN_W = POOL_W + QKVO_W
PAST_LEN = 16384
EPS = 1e-6

LANES = 128
HIST_PAD = 16
VMEM_LIMIT = 56 * 1024 * 1024
VMEM_LIMIT_RESIDENT_WEIGHT = 60 * 1024 * 1024

BF16 = jnp.bfloat16
F32 = jnp.float32


def _params(sem):
    return pltpu.CompilerParams(dimension_semantics=sem, vmem_limit_bytes=VMEM_LIMIT)


def _rms(x, w):
    return x * lax.rsqrt(jnp.mean(x * x, axis=-1, keepdims=True) + EPS) * w


def _inproj_kernel(x_ref, nw_ref, w_ref, wg_ref, *rest, n_u_tiles, emit_copy, conv_blocks):
    nc = len(conv_blocks)
    conv_in, (u_ref, qkvo_ref, g_ref), rest = rest[:nc], rest[nc:nc + 3], rest[nc + 3:]
    conv_out, xn_ref = rest[len(rest) - 1 - nc:-1], rest[-1]
    j = pl.program_id(1)
    _convert_blocks(pl.program_id(0) * pl.num_programs(1) + j, conv_in, conv_out, conv_blocks)

    nt = (((1,), (1,)), ((), ()))

    @pl.when(j == 0)
    def _():
        xn = _rms(x_ref[...], nw_ref[...]).astype(BF16)
        xn_ref[...] = xn
        g_ref[...] = lax.dot_general(xn, wg_ref[...], nt, preferred_element_type=F32)

    w = w_ref[...]
    if emit_copy:
        w = w.astype(BF16)
        rest[0][...] = w
    p = lax.dot_general(xn_ref[...], w, nt, preferred_element_type=F32)

    @pl.when(j < n_u_tiles)
    def _():
        u_ref[...] = p

    @pl.when(j >= n_u_tiles)
    def _():
        qkvo_ref[...] = p.astype(BF16)


def _inproj(x, norm_w, w_t, w_gate_t, *, tm, tn, convert=()):
    m = x.shape[0]
    n_u = POOL_W // tn
    n_j = MAIN_W // tn
    emit_copy = w_t.dtype == F32
    if emit_copy:
        assert m == tm, "the bf16 copy is written once per column tile"
    w_spec = pl.BlockSpec((tn, D_MODEL), lambda i, j: (j, 0))
    copy_spec = [w_spec] if emit_copy else []
    copy_shape = [jax.ShapeDtypeStruct((MAIN_W, D_MODEL), BF16)] if emit_copy else []
    conv_specs, conv_shapes = _convert_specs(convert, (m // tm) * n_j, lambda i, j: i * n_j + j)
    return pl.pallas_call(
        functools.partial(_inproj_kernel, n_u_tiles=n_u, emit_copy=emit_copy,
                          conv_blocks=tuple(n for _, n in convert)),
        grid=(m // tm, n_j),
        in_specs=[
            pl.BlockSpec((tm, D_MODEL), lambda i, j: (i, 0)),
            pl.BlockSpec((1, D_MODEL), lambda i, j: (0, 0)),
            w_spec,
            pl.BlockSpec((LANES, D_MODEL), lambda i, j: (0, 0)),
        ] + conv_specs,
        out_specs=[
            pl.BlockSpec((tm, tn), lambda i, j: (i, jnp.minimum(j, n_u - 1))),
            pl.BlockSpec((tm, tn), lambda i, j: (i, jnp.maximum(j - n_u, 0))),
            pl.BlockSpec((tm, LANES), lambda i, j: (i, 0)),
        ] + copy_spec + conv_specs,
        out_shape=[
            jax.ShapeDtypeStruct((m, POOL_W), F32),
            jax.ShapeDtypeStruct((m, QKVO_W), BF16),
            jax.ShapeDtypeStruct((m, LANES), F32),
        ] + copy_shape + conv_shapes,
        scratch_shapes=[pltpu.VMEM((tm, D_MODEL), BF16)],
        compiler_params=_params(("arbitrary", "arbitrary")),
        name="inproj",
    )(x, norm_w, w_t, w_gate_t, *[w for w, _ in convert])


INPROJ_COL_CHUNK = 1024


def _inproj_rows_kernel(x_ref, nw_ref, w_ref, wg_ref, prev_ref, wp_ref, sc_ref, *rest,
                        bb, tt, pos0, tiles_per_seq, conv_blocks):
    nc = len(conv_blocks)
    conv_in, (yp_ref, qkvo_ref, g_ref, st_ref) = rest[:nc], rest[nc:nc + 4]
    conv_out, e_ref = rest[nc + 4:2 * nc + 4], rest[-1]
    nt = (((1,), (1,)), ((), ()))
    ti = pl.program_id(0) % tiles_per_seq

    @pl.when(ti == 0)
    def _():
        e_ref[:, 0:1, :] = jnp.zeros((bb, 1, POOL_W), F32)
        e_ref[:, 1:HIST_PAD, :] = prev_ref[...]

    xn = _rms(x_ref[...], nw_ref[...]).astype(BF16)
    g_ref[...] = lax.dot_general(xn, wg_ref[...], nt, preferred_element_type=F32)
    u = lax.dot_general(xn, w_ref[0:POOL_W, :], nt, preferred_element_type=F32)
    e_ref[:, HIST_PAD:, :] = u.reshape(bb, tt, POOL_W)
    pos = pos0 + ti * tt + lax.broadcasted_iota(jnp.int32, (tt, 1), 0)

    for g, w in enumerate(POOL_WINDOWS):
        c = POOL_W + g * INPROJ_COL_CHUNK
        p = lax.dot_general(xn, w_ref[c:c + INPROJ_COL_CHUNK, :], nt, preferred_element_type=F32)
        qkvo_ref[:, c - POOL_W:c - POOL_W + INPROJ_COL_CHUNK] = p.astype(BF16)

        cols = slice(g * POOL_GW, (g + 1) * POOL_GW)
        cnt = jnp.minimum(w, pos + 1).astype(F32)
        d = []
        for b in range(bb):
            e = e_ref[b, :, cols]
            s = e + pltpu.roll(e, 1, axis=0)
            for k in range(1, g + 1):
                s = s + pltpu.roll(s, 2**k, axis=0)
            d.append(s[HIST_PAD:, :] / cnt - e[HIST_PAD:, :])
        d = jnp.concatenate(d, axis=0).astype(BF16)
        y = jnp.dot(d, wp_ref[g].astype(BF16), preferred_element_type=F32)
        yp_ref[:, cols] = (y * sc_ref[:, cols]).astype(BF16)

        for src, dst in list(zip(conv_in, conv_out))[g::N_POOL_GROUPS]:
            dst[...] = src[...].astype(BF16)

    e_ref[:, 0:HIST_PAD, :] = e_ref[:, tt:tt + HIST_PAD, :]

    @pl.when(ti == tiles_per_seq - 1)
    def _():
        st_ref[...] = e_ref[:, 1:HIST_PAD, :]


def _inproj_rows(x, norm_w, w_t, w_gate_t, prev, w_pool, scale, *, bb, tt, pos0, tiles_per_seq, convert=()):
    assert MAIN_W == POOL_W + N_POOL_GROUPS * INPROJ_COL_CHUNK
    tm = bb * tt
    m = x.shape[0]
    n_seq = m // (tt * tiles_per_seq)
    row = lambda w: pl.BlockSpec((tm, w), lambda i: (i, 0))
    whole = lambda a: pl.BlockSpec(a.shape, lambda i: (0,) * a.ndim, pipeline_mode=pl.Buffered(1))
    seq_blk = lambda i: (i // tiles_per_seq, 0, 0)
    prev_spec = whole(prev) if prev.shape[0] == 1 else pl.BlockSpec((bb, POOL_HIST, POOL_W), seq_blk)
    assert all(n == m // tm for _, n in convert), "the kernel converts one block of each weight at every step"
    conv_specs, conv_shapes = _convert_specs(convert, m // tm, lambda i: i)
    return pl.pallas_call(
        functools.partial(_inproj_rows_kernel, bb=bb, tt=tt, pos0=pos0, tiles_per_seq=tiles_per_seq,
                          conv_blocks=tuple(n for _, n in convert)),
        grid=(m // tm,),
        in_specs=[row(D_MODEL), whole(norm_w), whole(w_t), whole(w_gate_t), prev_spec, whole(w_pool),
                  whole(scale)] + conv_specs,
        out_specs=[row(POOL_W), row(QKVO_W), row(LANES),
                   pl.BlockSpec((bb, POOL_HIST, POOL_W), seq_blk)] + conv_specs,
        out_shape=[
            jax.ShapeDtypeStruct((m, POOL_W), BF16),
            jax.ShapeDtypeStruct((m, QKVO_W), BF16),
            jax.ShapeDtypeStruct((m, LANES), F32),
            jax.ShapeDtypeStruct((n_seq, POOL_HIST, POOL_W), F32),
        ] + conv_shapes,
        scratch_shapes=[pltpu.VMEM((bb, HIST_PAD + tt, POOL_W), F32)],
        compiler_params=pltpu.CompilerParams(
            dimension_semantics=("arbitrary",), vmem_limit_bytes=VMEM_LIMIT_RESIDENT_WEIGHT),
        name="inproj_rows",
    )(x, norm_w, w_t, w_gate_t, prev, w_pool, scale, *[w for w, _ in convert])


def _convert_specs(weights, n_steps, step_of):
    specs, shapes = [], []
    for w, n_blocks in weights:
        assert n_blocks <= n_steps
        rows = w.shape[0] // n_blocks
        specs.append(pl.BlockSpec(
            (rows, w.shape[1]), lambda *idx, n_blocks=n_blocks: (jnp.minimum(step_of(*idx), n_blocks - 1), 0)))
        shapes.append(jax.ShapeDtypeStruct(w.shape, BF16))
    return specs, shapes


def _convert_blocks(step, srcs, dsts, n_blocks):
    for src, dst, n in zip(srcs, dsts, n_blocks):
        @pl.when(step < n)
        def _(src=src, dst=dst):
            dst[...] = src[...].astype(BF16)


def _mlstm_chunk(q, k, v, gi, gf, c, n, m, *, chunk):
    ti = lax.broadcasted_iota(jnp.int32, (chunk, chunk), 0)
    si = lax.broadcasted_iota(jnp.int32, (chunk, chunk), 1)
    tril = si <= ti
    eye = si == ti

    def to_row(col):
        return jnp.sum(jnp.where(eye, col, 0.0), axis=0, keepdims=True)

    lf = jax.nn.log_sigmoid(gf)
    b = jnp.sum(jnp.where(tril, to_row(lf), 0.0), axis=1, keepdims=True)
    a = gi - b
    a_row = to_row(a)
    cummax_a = jnp.max(jnp.where(tril, a_row, -jnp.inf), axis=1, keepdims=True)
    m_t = jnp.maximum(m + b, cummax_a + b)
    inter = jnp.exp(m + b - m_t)
    dmat = jnp.exp(jnp.where(tril, a_row + (b - m_t), -jnp.inf))

    k = k * (HEAD_DIM ** -0.5)
    s = lax.dot_general(q, k, (((1,), (1,)), ((), ())), preferred_element_type=F32) * dmat
    num = jnp.dot(s.astype(BF16), v, preferred_element_type=F32)
    num = num + inter * jnp.dot(q, c.astype(BF16), preferred_element_type=F32)
    qn = jnp.sum(s, axis=1, keepdims=True) + inter * jnp.sum(q.astype(F32) * n, axis=1, keepdims=True)
    h = num / jnp.maximum(jnp.abs(qn), jnp.exp(-m_t))

    b_last = b[chunk - 1:chunk, :]
    m_new = m_t[chunk - 1:chunk, :]
    decay = jnp.exp(m + b_last - m_new)
    w = jnp.exp(a + (b_last - m_new))
    wv = (w * v.astype(F32)).astype(BF16)
    c_new = decay * c + lax.dot_general(k, wv, (((0,), (0,)), ((), ())), preferred_element_type=F32)
    n_new = decay * n + jnp.sum(w * k.astype(F32), axis=0, keepdims=True)

    return h, c_new, n_new, m_new


def _gate_cols(g, bias_ref, head):
    lane = lax.broadcasted_iota(jnp.int32, g.shape, 1)
    gi = jnp.sum(jnp.where(lane == head, g, 0.0), axis=1, keepdims=True) + bias_ref[head]
    gf = jnp.sum(jnp.where(lane == head + N_HEADS, g, 0.0), axis=1, keepdims=True) + bias_ref[head + N_HEADS]
    return gi, gf


N_STEP_IN = 7


N_OUTPROJ_IN = 6


def _gated_head(h, o, gnorm):
    return (jax.nn.sigmoid(o.astype(F32)) * _rms(h, gnorm)).astype(BF16)


N_STEP_OUT = 4


def _mlstm_seq_kernel(bias_ref, q_ref, k_ref, v_ref, g_ref, c0_ref, n0_ref, m0_ref, *rest,
                      chunk, n_chunks, n_steps, step, outproj):
    no_in = N_OUTPROJ_IN if outproj else 0
    ns_in = N_STEP_IN if step else 0
    op_in, step_in = rest[:no_in], rest[no_in:no_in + ns_in]
    outs = rest[no_in + ns_in:]
    n_rows_out = 2 if outproj else 1
    rows_out, (c_ref, n_ref, m_ref) = outs[:n_rows_out], outs[n_rows_out:n_rows_out + 3]
    step_out = outs[n_rows_out + 3:n_rows_out + 3 + N_STEP_OUT] if step else ()
    s = pl.program_id(0)
    active = s < n_steps
    cur = jnp.minimum(s, n_steps - 1)

    @pl.when((cur % n_chunks == 0) & active)
    def _():
        c_ref[...] = c0_ref[...]
        n_ref[...] = n0_ref[...]
        m_ref[...] = m0_ref[...]

    if outproj:
        ym_ref = outs[-1]

        @pl.when(s == 0)
        def _():
            ym_ref[...] = jnp.zeros(ym_ref.shape, BF16)

    stages = iter(())
    if step:
        stages = _mlstm_step_stages(cur % N_HEADS, bias_ref, *step_in, *step_out, bb=step[0], seq=step[1])
    next(stages, None)

    if outproj:
        x_ref, yp_ref, o_ref, gn_ref, wo_ref, nw_ref = op_in
        x2_ref, xn2_ref = rows_out
        slot = s % 2
        ssq = jnp.zeros((chunk, 1), F32)
        out_cols = D_MODEL // N_HEADS

    g = g_ref[...]
    for head in range(N_HEADS):
        if outproj:
            oc = slice(head * out_cols, (head + 1) * out_cols)
            x2 = x_ref[:, oc] + jnp.dot(yp_ref[...], wo_ref[0:POOL_W, oc], preferred_element_type=F32)
            x2 = x2 + jnp.dot(ym_ref[1 - slot], wo_ref[POOL_W:, oc], preferred_element_type=F32)
            x2_ref[:, oc] = x2
            ssq = ssq + jnp.sum(x2 * x2, axis=1, keepdims=True)
        cols = slice(head * HEAD_DIM, (head + 1) * HEAD_DIM)
        gi, gf = _gate_cols(g, bias_ref, head)
        c_old, n_old, m_old = c_ref[0, head], n_ref[0, head], m_ref[0, head]
        h, c_new, n_new, m_new = _mlstm_chunk(
            q_ref[:, cols], k_ref[:, cols], v_ref[:, cols], gi, gf, c_old, n_old, m_old, chunk=chunk)
        if outproj:
            ym_ref[slot, :, cols] = _gated_head(h, o_ref[:, cols], gn_ref[:, cols])
            c_new, n_new, m_new = (jnp.where(active, new, old)
                                   for new, old in ((c_new, c_old), (n_new, n_old), (m_new, m_old)))
        else:
            rows_out[0][:, cols] = h
        c_ref[0, head] = c_new
        n_ref[0, head] = n_new
        m_ref[0, head] = m_new
        next(stages, None)
    for _ in stages:
        pass

    if outproj:
        scale = lax.rsqrt(ssq * (1.0 / D_MODEL) + EPS)
        xn2_ref[...] = (x2_ref[...] * scale * nw_ref[...]).astype(BF16)


def _mlstm_seq(qkvo, gates, bias, c0, n0, m0, *, batch, seq, chunk, shared_init, step=None, outproj=None):
    nc = seq // chunk
    n_steps = batch * nc
    cur = lambda s: jnp.minimum(s, n_steps - 1)
    prev = lambda s: jnp.maximum(s - 1, 0)
    whole = lambda a: pl.BlockSpec(a.shape, lambda s: (0,) * a.ndim, pipeline_mode=pl.Buffered(1))
    if outproj:
        x, y_pool, gnorm, w_out, norm_w = outproj
        op_args = [x, y_pool, qkvo, gnorm, w_out, norm_w]
        late_rows = lambda w: pl.BlockSpec((chunk, w), lambda s: (prev(s), 0))
        op_in = [late_rows(D_MODEL), late_rows(POOL_W),
                 pl.BlockSpec((chunk, MLSTM_W), lambda s: (cur(s), 3)),
                 whole(gnorm), whole(w_out), whole(norm_w)]
        rows_out = [late_rows(D_MODEL), late_rows(D_MODEL)]
        rows_shapes = [jax.ShapeDtypeStruct((batch * seq, D_MODEL), F32),
                       jax.ShapeDtypeStruct((batch * seq, D_MODEL), BF16)]
        scratch = [pltpu.VMEM((2, chunk, MLSTM_W), BF16)]
    else:
        op_args, op_in, scratch = [], [], []
        rows_out = [pl.BlockSpec((chunk, MLSTM_W), lambda s: (cur(s), 0))]
        rows_shapes = [jax.ShapeDtypeStruct((batch * seq, MLSTM_W), F32)]
    step_args, step_in, step_out, step_shapes, step_cfg = [], [], [], [], None
    if step:
        (s_qkvo, s_gates, s_c0, s_n0, s_m0), s_batch, s_seq, s_bb = step
        step_in, step_out, step_shapes = _mlstm_step_specs(s_batch, s_seq, s_bb, n_steps, cur)
        step_args = [s_qkvo, s_qkvo, s_qkvo, s_gates, s_c0, s_n0, s_m0]
        step_cfg = (s_bb, s_seq)
    st_b = (lambda s: 0) if shared_init else (lambda s: cur(s) // nc)

    def col_spec(group):
        return pl.BlockSpec((chunk, MLSTM_W), lambda s: (cur(s), group))

    def st_spec(shape, in_b):
        nd = len(shape)
        return pl.BlockSpec((1, N_HEADS) + shape, lambda s: (in_b(s), 0) + (0,) * nd)

    seq_of = lambda s: cur(s) // nc
    return pl.pallas_call(
        functools.partial(_mlstm_seq_kernel, chunk=chunk, n_chunks=nc, n_steps=n_steps, step=step_cfg,
                          outproj=bool(outproj)),
        grid=(n_steps + (1 if outproj else 0),),
        in_specs=[
            pl.BlockSpec(memory_space=pltpu.SMEM),
            col_spec(0), col_spec(1), col_spec(2),
            pl.BlockSpec((chunk, LANES), lambda s: (cur(s), 0)),
            st_spec((HEAD_DIM, HEAD_DIM), st_b), st_spec((1, HEAD_DIM), st_b), st_spec((1, 1), st_b),
        ] + op_in + step_in,
        out_specs=rows_out + [
            st_spec((HEAD_DIM, HEAD_DIM), seq_of), st_spec((1, HEAD_DIM), seq_of), st_spec((1, 1), seq_of),
        ] + step_out,
        out_shape=rows_shapes + [
            jax.ShapeDtypeStruct((batch, N_HEADS, HEAD_DIM, HEAD_DIM), F32),
            jax.ShapeDtypeStruct((batch, N_HEADS, 1, HEAD_DIM), F32),
            jax.ShapeDtypeStruct((batch, N_HEADS, 1, 1), F32),
        ] + step_shapes,
        scratch_shapes=scratch,
        compiler_params=pltpu.CompilerParams(
            dimension_semantics=("arbitrary",),
            vmem_limit_bytes=VMEM_LIMIT_RESIDENT_WEIGHT if outproj else VMEM_LIMIT),
        name="mlstm_seq",
    )(bias, qkvo, qkvo, qkvo, gates, c0, n0, m0, *op_args, *step_args)


def _mlstm_step_stages(head, bias_ref, q_ref, k_ref, v_ref, g_ref, c0_ref, n0_ref, m0_ref,
                       y_ref, c_ref, n_ref, m_ref, *, bb, seq):
    rows = bb * seq
    ti = lax.broadcasted_iota(jnp.int32, (rows, rows), 0)
    si = lax.broadcasted_iota(jnp.int32, (rows, rows), 1)
    same = (ti // seq) == (si // seq)
    tril = same & (si <= ti)
    eye = si == ti

    def to_row(col):
        return jnp.sum(jnp.where(eye, col, 0.0), axis=0, keepdims=True)

    gi, gf = _gate_cols(g_ref[...], bias_ref, head)
    lane = lax.broadcasted_iota(jnp.int32, m0_ref.shape, 1)
    m0 = jnp.sum(jnp.where(lane == head, m0_ref[...], 0.0), axis=1, keepdims=True)

    lf_row = to_row(jax.nn.log_sigmoid(gf))
    b = jnp.sum(jnp.where(tril, lf_row, 0.0), axis=1, keepdims=True)
    b_last = jnp.sum(jnp.where(same, lf_row, 0.0), axis=1, keepdims=True)
    a = gi - b
    a_row = to_row(a)
    cummax_a = jnp.max(jnp.where(tril, a_row, -jnp.inf), axis=1, keepdims=True)
    seqmax_a = jnp.max(jnp.where(same, a_row, -jnp.inf), axis=1, keepdims=True)
    m_t = jnp.maximum(m0 + b, cummax_a + b)
    m_new = jnp.maximum(m0 + b_last, seqmax_a + b_last)
    inter = jnp.exp(m0 + b - m_t)
    dmat = jnp.exp(jnp.where(tril, a_row + (b - m_t), -jnp.inf))
    decay = jnp.exp(m0 + b_last - m_new)
    w = jnp.exp(a + (b_last - m_new))

    q = q_ref[...]
    k = k_ref[...] * (HEAD_DIM ** -0.5)
    v = v_ref[...]
    s = lax.dot_general(q, k, (((1,), (1,)), ((), ())), preferred_element_type=F32) * dmat
    sv = jnp.dot(s.astype(BF16), v, preferred_element_type=F32)

    qf = q.astype(F32)
    kf = k.astype(F32)
    wk = w * kf
    wv = w * v.astype(F32)
    qc, qdn = [], []
    yield
    for i in range(bb):
        r = slice(i * seq, (i + 1) * seq)
        c_i = c0_ref[i, 0]
        n_i = n0_ref[i:i + 1, :]
        d_i = decay[i * seq:i * seq + 1, :]
        qc.append(jnp.dot(qf[r].astype(BF16), c_i.astype(BF16), preferred_element_type=F32))
        qdn.append(jnp.sum(qf[r] * n_i, axis=1, keepdims=True))
        upd = lax.dot_general(kf[r].astype(BF16), wv[r].astype(BF16), (((0,), (0,)), ((), ())),
                              preferred_element_type=F32)
        c_ref[i, 0] = d_i * c_i + upd
        n_ref[i:i + 1, :] = d_i * n_i + jnp.sum(wk[r], axis=0, keepdims=True)
        if (i + 1) % (bb // N_HEADS) == 0:
            yield

    num = sv + inter * jnp.concatenate(qc, axis=0)
    qn = jnp.sum(s, axis=1, keepdims=True) + inter * jnp.concatenate(qdn, axis=0)
    h = num / jnp.maximum(jnp.abs(qn), jnp.exp(-m_t))
    y_ref[...] = h
    m_ref[0, 0] = to_row(m_new)


def _mlstm_step_specs(batch, seq, bb, n_steps, step_of):
    assert (batch // bb) * N_HEADS == n_steps
    rows = bb * seq
    blk = lambda *idx: step_of(*idx) // N_HEADS
    head = lambda *idx: step_of(*idx) % N_HEADS

    def col_spec(off):
        return pl.BlockSpec((rows, HEAD_DIM), lambda *idx: (blk(*idx), off + head(*idx)))

    c_spec = pl.BlockSpec((bb, 1, HEAD_DIM, HEAD_DIM), lambda *idx: (blk(*idx), head(*idx), 0, 0))
    n_spec = pl.BlockSpec((bb, HEAD_DIM), lambda *idx: (blk(*idx), head(*idx)))
    in_specs = [
        col_spec(0), col_spec(N_HEADS), col_spec(2 * N_HEADS),
        pl.BlockSpec((rows, LANES), lambda *idx: (blk(*idx), 0)),
        c_spec, n_spec,
        pl.BlockSpec((rows, N_HEADS), lambda *idx: (blk(*idx), 0)),
    ]
    out_specs = [
        pl.BlockSpec((rows, HEAD_DIM), lambda *idx: (blk(*idx), head(*idx))),
        c_spec, n_spec,
        pl.BlockSpec((1, 1, 1, rows), lambda *idx: (blk(*idx), head(*idx), 0, 0)),
    ]
    out_shapes = [
        jax.ShapeDtypeStruct((batch * seq, MLSTM_W), F32),
        jax.ShapeDtypeStruct((batch, N_HEADS, HEAD_DIM, HEAD_DIM), F32),
        jax.ShapeDtypeStruct((batch, MLSTM_W), F32),
        jax.ShapeDtypeStruct((batch // bb, N_HEADS, 1, rows), F32),
    ]
    return in_specs, out_specs, out_shapes


def _outproj_kernel(x_ref, yp_ref, h_ref, o_ref, gn_ref, wo_ref, nw_ref, *rest, conv_blocks):
    nc = len(conv_blocks)
    conv_in, (x2_ref, xn2_ref), conv_out = rest[:nc], rest[nc:nc + 2], rest[nc + 2:]
    _convert_blocks(pl.program_id(0), conv_in, conv_out, conv_blocks)
    x2 = x_ref[...] + jnp.dot(yp_ref[...], wo_ref[0:POOL_W, :], preferred_element_type=F32)
    ym = []
    for head in range(N_HEADS):
        cols = slice(head * HEAD_DIM, (head + 1) * HEAD_DIM)
        ym.append(_gated_head(h_ref[:, cols], o_ref[:, cols], gn_ref[:, cols]))
    x2 = x2 + jnp.dot(jnp.concatenate(ym, axis=1), wo_ref[POOL_W:, :], preferred_element_type=F32)
    x2_ref[...] = x2
    xn2_ref[...] = _rms(x2, nw_ref[...]).astype(BF16)


def _outproj(x, y_pool, h_ml, qkvo, gnorm, w_out, norm_w, *, tm, convert=()):
    m = x.shape[0]
    row = lambda w: pl.BlockSpec((tm, w), lambda i: (i, 0))
    conv_specs, conv_shapes = _convert_specs(convert, m // tm, lambda i: i)
    return pl.pallas_call(
        functools.partial(_outproj_kernel, conv_blocks=tuple(n for _, n in convert)),
        grid=(m // tm,),
        in_specs=[
            row(D_MODEL), row(POOL_W), row(MLSTM_W),
            pl.BlockSpec((tm, MLSTM_W), lambda i: (i, 3)),
            pl.BlockSpec((1, MLSTM_W), lambda i: (0, 0)),
            pl.BlockSpec((D_MODEL, D_MODEL), lambda i: (0, 0), pipeline_mode=pl.Buffered(1)),
            pl.BlockSpec((1, D_MODEL), lambda i: (0, 0)),
        ] + conv_specs,
        out_specs=[row(D_MODEL), row(D_MODEL)] + conv_specs,
        out_shape=[
            jax.ShapeDtypeStruct((m, D_MODEL), F32),
            jax.ShapeDtypeStruct((m, D_MODEL), BF16),
        ] + conv_shapes,
        compiler_params=_params(("arbitrary",)),
        name="outproj",
    )(x, y_pool, h_ml, qkvo, gnorm, w_out, norm_w, *[w for w, _ in convert])


def _ffn_kernel(xn_ref, x2_ref, wg_ref, wu_ref, wd_ref, nw_ref, y_ref, acc_ref):
    f = pl.program_id(1)
    nf = pl.num_programs(1)

    @pl.when(f == 0)
    def _():
        acc_ref[...] = x2_ref[...]

    xn = xn_ref[...]
    g = jnp.dot(xn, wg_ref[...], preferred_element_type=F32)
    u = jnp.dot(xn, wu_ref[...], preferred_element_type=F32)
    h = (jax.nn.silu(g) * u).astype(BF16)
    acc_ref[...] += jnp.dot(h, wd_ref[...], preferred_element_type=F32)

    @pl.when(f == nf - 1)
    def _():
        y_ref[...] = _rms(acc_ref[...], nw_ref[...])


def _ffn(xn2, x2, w_gate, w_up, w_down, norm_w, *, tm, tf):
    m = xn2.shape[0]
    return pl.pallas_call(
        _ffn_kernel,
        grid=(m // tm, D_FF // tf),
        in_specs=[
            pl.BlockSpec((tm, D_MODEL), lambda i, f: (i, 0)),
            pl.BlockSpec((tm, D_MODEL), lambda i, f: (i, 0)),
            pl.BlockSpec((D_MODEL, tf), lambda i, f: (0, f)),
            pl.BlockSpec((D_MODEL, tf), lambda i, f: (0, f)),
            pl.BlockSpec((tf, D_MODEL), lambda i, f: (f, 0)),
            pl.BlockSpec((1, D_MODEL), lambda i, f: (0, 0)),
        ],
        out_specs=pl.BlockSpec((tm, D_MODEL), lambda i, f: (i, 0)),
        out_shape=jax.ShapeDtypeStruct((m, D_MODEL), F32),
        scratch_shapes=[pltpu.VMEM((tm, D_MODEL), F32)],
        compiler_params=_params(("arbitrary", "arbitrary")),
        name="ffn",
    )(xn2, x2, w_gate, w_up, w_down, norm_w)


def kernel(x_prompt, x_sample, state_pool, state_mlstm_C, state_mlstm_n, state_mlstm_m, meta_tokens, norm_mix_w, w_in, b_igate, b_fgate, w_pool, pool_scale, mlstm_norm_w, w_out, norm_ffn_w, w_gate, w_up, w_down, norm_final_w):
    bp, tp, _ = x_prompt.shape
    bs, ts, _ = x_sample.shape

    w_in_t = w_in[0].T
    w_g8 = jnp.pad(w_in_t[MAIN_W:], ((0, LANES - 2 * N_HEADS), (0, 0))).astype(BF16)
    wp = w_pool[0]
    nmix = norm_mix_w[0].reshape(1, D_MODEL)
    nffn = norm_ffn_w[0].reshape(1, D_MODEL)
    nfin = norm_final_w.reshape(1, D_MODEL)
    scale = pool_scale[0].reshape(1, POOL_W)
    gnorm = mlstm_norm_w[0].reshape(1, MLSTM_W)
    bias = jnp.concatenate([b_igate[0], b_fgate[0]]).astype(F32)

    xp = x_prompt.reshape(bp * tp, D_MODEL)
    xs = x_sample.reshape(bs * ts, D_MODEL)

    u_m, qkvo_m, g_m, w_main = _inproj(meta_tokens, nmix, w_in_t, w_g8, tm=N_META, tn=512)
    prev_p = u_m[1:N_META].reshape(1, POOL_HIST, POOL_W)
    prompt_tt = 256
    convert = ((w_out[0], 32), (w_gate[0], 32), (w_up[0], 32), (w_down[0], 32))
    yp_p, qkvo_p, g_p, pool_p, wo, wg, wu, wd = _inproj_rows(
        xp, nmix, w_main, w_g8, prev_p, wp, scale, bb=1, tt=prompt_tt, pos0=N_META, tiles_per_seq=tp // prompt_tt,
        convert=convert)
    yp_s, qkvo_s, g_s, pool_s = _inproj_rows(
        xs, nmix, w_main, w_g8, state_pool[0], wp, scale, bb=32, tt=ts, pos0=PAST_LEN, tiles_per_seq=1)

    zc = jnp.zeros((1, N_HEADS, HEAD_DIM, HEAD_DIM), F32)
    zn = jnp.zeros((1, N_HEADS, 1, HEAD_DIM), F32)
    zm = jnp.zeros((1, N_HEADS, 1, 1), F32)
    _, c_m, n_m, m_m = _mlstm_seq(qkvo_m, g_m, bias, zc, zn, zm, batch=1, seq=N_META, chunk=N_META, shared_init=False)
    step_bb = 16
    step_ops = (qkvo_s, g_s, state_mlstm_C[0], state_mlstm_n[0].reshape(bs, MLSTM_W),
                jnp.repeat(state_mlstm_m[0], ts, axis=0))
    x2_p, xn2_p, c_p, n_p, m_p, h_s, c_s, n_s, m_s = _mlstm_seq(
        qkvo_p, g_p, bias, c_m, n_m, m_m, batch=bp, seq=tp, chunk=256, shared_init=True,
        step=(step_ops, bs, ts, step_bb), outproj=(xp, yp_p, gnorm, wo, nffn))
    m_s = m_s.reshape(bs // step_bb, N_HEADS, step_bb, ts)[..., 0].transpose(0, 2, 1)

    x2_s, xn2_s = _outproj(xs, yp_s, h_s, qkvo_s, gnorm, wo, nffn, tm=512)
    y_p = _ffn(xn2_p, x2_p, wg, wu, wd, nfin, tm=512, tf=512)
    y_s = _ffn(xn2_s, x2_s, wg, wu, wd, nfin, tm=512, tf=512)

    return (
        y_p.reshape(bp, tp, D_MODEL),
        y_s.reshape(bs, ts, D_MODEL),
        pool_p[None],
        c_p[None],
        n_p.reshape(1, bp, N_HEADS, HEAD_DIM),
        m_p.reshape(1, bp, N_HEADS),
        pool_s[None],
        c_s[None],
        n_s.reshape(1, bs, N_HEADS, HEAD_DIM),
        m_s.reshape(1, bs, N_HEADS),
    )
```

```python
import functools

import jax
import jax.numpy as jnp
from jax import lax
from jax.experimental import pallas as pl
from jax.experimental.pallas import tpu as pltpu

D_MODEL = 2048
N_META = 16
POOL_W = 1024
MLSTM_W = 1024
POOL_WINDOWS = (2, 4, 8, 16)
N_POOL_GROUPS = 4
POOL_GW = 256
POOL_HIST = 15
N_HEADS = 4
HEAD_DIM = 256
D_FF = 5632
QKVO_W = 4 * MLSTM_W
MAIN_W = POOL_W + QKVO_W
PAST_LEN = 16384
EPS = 1e-6

LANES = 128
HIST_PAD = 16
VMEM_LIMIT = 56 * 1024 * 1024
VMEM_LIMIT_RESIDENT_WEIGHT = 60 * 1024 * 1024

META_TN = 512
PROMPT_TT = 256
DECODE_BB = 32
MLSTM_CHUNK = 256
STEP_BB = 16
OUTPROJ_TM = 512
FFN_TM = 512
FFN_TF = 512

BF16 = jnp.bfloat16
F32 = jnp.float32


def _params(sem):
    return pltpu.CompilerParams(dimension_semantics=sem, vmem_limit_bytes=VMEM_LIMIT)


def _rms(x, w):
    return x * lax.rsqrt(jnp.mean(x * x, axis=-1, keepdims=True) + EPS) * w


def _inproj_kernel(x_ref, nw_ref, w_ref, wg_ref, u_ref, qkvo_ref, g_ref, wcopy_ref, xn_ref, *, n_u_tiles):
    j = pl.program_id(0)
    nt = (((1,), (1,)), ((), ()))

    @pl.when(j == 0)
    def _():
        xn = _rms(x_ref[...], nw_ref[...]).astype(BF16)
        xn_ref[...] = xn
        g_ref[...] = lax.dot_general(xn, wg_ref[...], nt, preferred_element_type=F32)

    w = w_ref[...].astype(BF16)
    wcopy_ref[...] = w
    p = lax.dot_general(xn_ref[...], w, nt, preferred_element_type=F32)

    @pl.when(j < n_u_tiles)
    def _():
        u_ref[...] = p

    @pl.when(j >= n_u_tiles)
    def _():
        qkvo_ref[...] = p.astype(BF16)


def _inproj(x, norm_w, w_t, w_gate_t, *, tn):
    tm = x.shape[0]
    n_u = POOL_W // tn
    whole = lambda a: pl.BlockSpec(a.shape, lambda j: (0,) * a.ndim)
    w_spec = pl.BlockSpec((tn, D_MODEL), lambda j: (j, 0))
    return pl.pallas_call(
        functools.partial(_inproj_kernel, n_u_tiles=n_u),
        grid=(MAIN_W // tn,),
        in_specs=[whole(x), whole(norm_w), w_spec, whole(w_gate_t)],
        out_specs=[
            pl.BlockSpec((tm, tn), lambda j: (0, jnp.minimum(j, n_u - 1))),
            pl.BlockSpec((tm, tn), lambda j: (0, jnp.maximum(j - n_u, 0))),
            pl.BlockSpec((tm, LANES), lambda j: (0, 0)),
            w_spec,
        ],
        out_shape=[
            jax.ShapeDtypeStruct((tm, POOL_W), F32),
            jax.ShapeDtypeStruct((tm, QKVO_W), BF16),
            jax.ShapeDtypeStruct((tm, LANES), F32),
            jax.ShapeDtypeStruct((MAIN_W, D_MODEL), BF16),
        ],
        scratch_shapes=[pltpu.VMEM((tm, D_MODEL), BF16)],
        compiler_params=_params(("arbitrary",)),
        name="inproj",
    )(x, norm_w, w_t, w_gate_t)


INPROJ_COL_CHUNK = 1024


def _inproj_rows_kernel(x_ref, nw_ref, w_ref, wg_ref, prev_ref, wp_ref, sc_ref, *rest,
                        bb, tt, pos0, tiles_per_seq, conv_blocks):
    nc = len(conv_blocks)
    conv_in, (yp_ref, qkvo_ref, g_ref, st_ref) = rest[:nc], rest[nc:nc + 4]
    conv_out, e_ref = rest[nc + 4:2 * nc + 4], rest[-1]
    nt = (((1,), (1,)), ((), ()))
    ti = pl.program_id(0) % tiles_per_seq

    @pl.when(ti == 0)
    def _():
        e_ref[:, 0:1, :] = jnp.zeros((bb, 1, POOL_W), F32)
        e_ref[:, 1:HIST_PAD, :] = prev_ref[...]

    xn = _rms(x_ref[...], nw_ref[...]).astype(BF16)
    g_ref[...] = lax.dot_general(xn, wg_ref[...], nt, preferred_element_type=F32)
    u = lax.dot_general(xn, w_ref[0:POOL_W, :], nt, preferred_element_type=F32)
    e_ref[:, HIST_PAD:, :] = u.reshape(bb, tt, POOL_W)
    pos = pos0 + ti * tt + lax.broadcasted_iota(jnp.int32, (tt, 1), 0)

    for g, w in enumerate(POOL_WINDOWS):
        c = POOL_W + g * INPROJ_COL_CHUNK
        p = lax.dot_general(xn, w_ref[c:c + INPROJ_COL_CHUNK, :], nt, preferred_element_type=F32)
        qkvo_ref[:, c - POOL_W:c - POOL_W + INPROJ_COL_CHUNK] = p.astype(BF16)

        cols = slice(g * POOL_GW, (g + 1) * POOL_GW)
        cnt = jnp.minimum(w, pos + 1).astype(F32)
        d = []
        for b in range(bb):
            e = e_ref[b, :, cols]
            s = e + pltpu.roll(e, 1, axis=0)
            for k in range(1, g + 1):
                s = s + pltpu.roll(s, 2**k, axis=0)
            d.append(s[HIST_PAD:, :] / cnt - e[HIST_PAD:, :])
        d = jnp.concatenate(d, axis=0).astype(BF16)
        y = jnp.dot(d, wp_ref[g].astype(BF16), preferred_element_type=F32)
        yp_ref[:, cols] = (y * sc_ref[:, cols]).astype(BF16)

        for src, dst in list(zip(conv_in, conv_out))[g::N_POOL_GROUPS]:
            dst[...] = src[...].astype(BF16)

    e_ref[:, 0:HIST_PAD, :] = e_ref[:, tt:tt + HIST_PAD, :]

    @pl.when(ti == tiles_per_seq - 1)
    def _():
        st_ref[...] = e_ref[:, 1:HIST_PAD, :]


def _inproj_rows(x, norm_w, w_t, w_gate_t, prev, w_pool, scale, *, bb, tt, pos0, tiles_per_seq, convert=()):
    assert MAIN_W == POOL_W + N_POOL_GROUPS * INPROJ_COL_CHUNK
    tm = bb * tt
    m = x.shape[0]
    n_seq = m // (tt * tiles_per_seq)
    row = lambda w: pl.BlockSpec((tm, w), lambda i: (i, 0))
    whole = lambda a: pl.BlockSpec(a.shape, lambda i: (0,) * a.ndim, pipeline_mode=pl.Buffered(1))
    seq_blk = lambda i: (i // tiles_per_seq, 0, 0)
    prev_spec = whole(prev) if prev.shape[0] == 1 else pl.BlockSpec((bb, POOL_HIST, POOL_W), seq_blk)
    assert all(n == m // tm for _, n in convert), "the kernel converts one block of each weight at every step"
    conv_specs, conv_shapes = _convert_specs(convert, m // tm, lambda i: i)
    return pl.pallas_call(
        functools.partial(_inproj_rows_kernel, bb=bb, tt=tt, pos0=pos0, tiles_per_seq=tiles_per_seq,
                          conv_blocks=tuple(n for _, n in convert)),
        grid=(m // tm,),
        in_specs=[row(D_MODEL), whole(norm_w), whole(w_t), whole(w_gate_t), prev_spec, whole(w_pool),
                  whole(scale)] + conv_specs,
        out_specs=[row(POOL_W), row(QKVO_W), row(LANES),
                   pl.BlockSpec((bb, POOL_HIST, POOL_W), seq_blk)] + conv_specs,
        out_shape=[
            jax.ShapeDtypeStruct((m, POOL_W), BF16),
            jax.ShapeDtypeStruct((m, QKVO_W), BF16),
            jax.ShapeDtypeStruct((m, LANES), F32),
            jax.ShapeDtypeStruct((n_seq, POOL_HIST, POOL_W), F32),
        ] + conv_shapes,
        scratch_shapes=[pltpu.VMEM((bb, HIST_PAD + tt, POOL_W), F32)],
        compiler_params=pltpu.CompilerParams(
            dimension_semantics=("arbitrary",), vmem_limit_bytes=VMEM_LIMIT_RESIDENT_WEIGHT),
        name="inproj_rows",
    )(x, norm_w, w_t, w_gate_t, prev, w_pool, scale, *[w for w, _ in convert])


def _convert_specs(weights, n_steps, step_of):
    specs, shapes = [], []
    for w, n_blocks in weights:
        assert n_blocks <= n_steps
        rows = w.shape[0] // n_blocks
        specs.append(pl.BlockSpec(
            (rows, w.shape[1]), lambda *idx, n_blocks=n_blocks: (jnp.minimum(step_of(*idx), n_blocks - 1), 0)))
        shapes.append(jax.ShapeDtypeStruct(w.shape, BF16))
    return specs, shapes


def _mlstm_chunk(q, k, v, gi, gf, c, n, m, *, chunk):
    ti = lax.broadcasted_iota(jnp.int32, (chunk, chunk), 0)
    si = lax.broadcasted_iota(jnp.int32, (chunk, chunk), 1)
    tril = si <= ti
    eye = si == ti

    def to_row(col):
        return jnp.sum(jnp.where(eye, col, 0.0), axis=0, keepdims=True)

    lf = jax.nn.log_sigmoid(gf)
    b = jnp.sum(jnp.where(tril, to_row(lf), 0.0), axis=1, keepdims=True)
    a = gi - b
    a_row = to_row(a)
    cummax_a = jnp.max(jnp.where(tril, a_row, -jnp.inf), axis=1, keepdims=True)
    m_t = jnp.maximum(m + b, cummax_a + b)
    inter = jnp.exp(m + b - m_t)
    dmat = jnp.exp(jnp.where(tril, a_row + (b - m_t), -jnp.inf))

    k = k * (HEAD_DIM ** -0.5)
    s = lax.dot_general(q, k, (((1,), (1,)), ((), ())), preferred_element_type=F32) * dmat
    num = jnp.dot(s.astype(BF16), v, preferred_element_type=F32)
    num = num + inter * jnp.dot(q, c.astype(BF16), preferred_element_type=F32)
    qn = jnp.sum(s, axis=1, keepdims=True) + inter * jnp.sum(q.astype(F32) * n, axis=1, keepdims=True)
    h = num / jnp.maximum(jnp.abs(qn), jnp.exp(-m_t))

    b_last = b[chunk - 1:chunk, :]
    m_new = m_t[chunk - 1:chunk, :]
    decay = jnp.exp(m + b_last - m_new)
    w = jnp.exp(a + (b_last - m_new))
    wv = (w * v.astype(F32)).astype(BF16)
    c_new = decay * c + lax.dot_general(k, wv, (((0,), (0,)), ((), ())), preferred_element_type=F32)
    n_new = decay * n + jnp.sum(w * k.astype(F32), axis=0, keepdims=True)

    return h, c_new, n_new, m_new


def _gate_cols(g, bias_refs, head):
    bi_ref, bf_ref = bias_refs
    lane = lax.broadcasted_iota(jnp.int32, g.shape, 1)
    gi = jnp.sum(jnp.where(lane == head, g, 0.0), axis=1, keepdims=True) + bi_ref[head]
    gf = jnp.sum(jnp.where(lane == head + N_HEADS, g, 0.0), axis=1, keepdims=True) + bf_ref[head]
    return gi, gf


N_STEP_IN = 7


N_OUTPROJ_IN = 6


def _gated_head(h, o, gnorm):
    return (jax.nn.sigmoid(o.astype(F32)) * _rms(h, gnorm)).astype(BF16)


N_STEP_OUT = 4


def _mlstm_seq_kernel(bi_ref, bf_ref, q_ref, k_ref, v_ref, g_ref, *rest,
                      chunk, n_chunks, n_steps, zero_init, step, outproj):
    bias_refs = (bi_ref, bf_ref)
    n_init = 0 if zero_init else 3
    init_refs, rest = rest[:n_init], rest[n_init:]
    no_in = N_OUTPROJ_IN if outproj else 0
    ns_in = N_STEP_IN if step else 0
    op_in, step_in = rest[:no_in], rest[no_in:no_in + ns_in]
    outs = rest[no_in + ns_in:]
    n_rows_out = 2 if outproj else 1
    rows_out, (c_ref, n_ref, m_ref) = outs[:n_rows_out], outs[n_rows_out:n_rows_out + 3]
    step_out = outs[n_rows_out + 3:n_rows_out + 3 + N_STEP_OUT] if step else ()
    s = pl.program_id(0)
    active = s < n_steps
    cur = jnp.minimum(s, n_steps - 1)

    @pl.when((cur % n_chunks == 0) & active)
    def _():
        for dst, src in zip((c_ref, n_ref, m_ref), init_refs or (None,) * 3):
            dst[...] = jnp.zeros(dst.shape, F32) if src is None else src[...]

    if outproj:
        ym_ref = outs[-1]

        @pl.when(s == 0)
        def _():
            ym_ref[...] = jnp.zeros(ym_ref.shape, BF16)

    stages = iter(())
    if step:
        stages = _mlstm_step_stages(cur % N_HEADS, bias_refs, *step_in, *step_out, bb=step[0], seq=step[1])
    next(stages, None)

    if outproj:
        x_ref, yp_ref, o_ref, gn_ref, wo_ref, nw_ref = op_in
        x2_ref, xn2_ref = rows_out
        slot = s % 2
        ssq = jnp.zeros((chunk, 1), F32)
        out_cols = D_MODEL // N_HEADS

    g = g_ref[...]
    for head in range(N_HEADS):
        if outproj:
            oc = slice(head * out_cols, (head + 1) * out_cols)
            x2 = x_ref[:, oc] + jnp.dot(yp_ref[...], wo_ref[0:POOL_W, oc], preferred_element_type=F32)
            x2 = x2 + jnp.dot(ym_ref[1 - slot], wo_ref[POOL_W:, oc], preferred_element_type=F32)
            x2_ref[:, oc] = x2
            ssq = ssq + jnp.sum(x2 * x2, axis=1, keepdims=True)
        cols = slice(head * HEAD_DIM, (head + 1) * HEAD_DIM)
        gi, gf = _gate_cols(g, bias_refs, head)
        c_old, n_old, m_old = c_ref[0, head], n_ref[0, head], m_ref[0, head]
        h, c_new, n_new, m_new = _mlstm_chunk(
            q_ref[:, cols], k_ref[:, cols], v_ref[:, cols], gi, gf, c_old, n_old, m_old, chunk=chunk)
        if outproj:
            ym_ref[slot, :, cols] = _gated_head(h, o_ref[:, cols], gn_ref[:, cols])
            c_new, n_new, m_new = (jnp.where(active, new, old)
                                   for new, old in ((c_new, c_old), (n_new, n_old), (m_new, m_old)))
        else:
            rows_out[0][:, cols] = h
        c_ref[0, head] = c_new
        n_ref[0, head] = n_new
        m_ref[0, head] = m_new
        next(stages, None)
    for _ in stages:
        pass

    if outproj:
        scale = lax.rsqrt(ssq * (1.0 / D_MODEL) + EPS)
        xn2_ref[...] = (x2_ref[...] * scale * nw_ref[...]).astype(BF16)


def _mlstm_seq(qkvo, gates, b_i, b_f, init, *, batch, seq, chunk, step=None, outproj=None):
    nc = seq // chunk
    n_steps = batch * nc
    cur = lambda s: jnp.minimum(s, n_steps - 1)
    prev = lambda s: jnp.maximum(s - 1, 0)
    whole = lambda a: pl.BlockSpec(a.shape, lambda s: (0,) * a.ndim, pipeline_mode=pl.Buffered(1))
    if outproj:
        x, y_pool, gnorm, w_out, norm_w = outproj
        op_args = [x, y_pool, qkvo, gnorm, w_out, norm_w]
        late_rows = lambda w: pl.BlockSpec((chunk, w), lambda s: (prev(s), 0))
        op_in = [late_rows(D_MODEL), late_rows(POOL_W),
                 pl.BlockSpec((chunk, MLSTM_W), lambda s: (cur(s), 3)),
                 whole(gnorm), whole(w_out), whole(norm_w)]
        rows_out = [late_rows(D_MODEL), late_rows(D_MODEL)]
        rows_shapes = [jax.ShapeDtypeStruct((batch * seq, D_MODEL), F32),
                       jax.ShapeDtypeStruct((batch * seq, D_MODEL), BF16)]
        scratch = [pltpu.VMEM((2, chunk, MLSTM_W), BF16)]
    else:
        op_args, op_in, scratch = [], [], []
        rows_out = [pl.BlockSpec((chunk, MLSTM_W), lambda s: (cur(s), 0))]
        rows_shapes = [jax.ShapeDtypeStruct((batch * seq, MLSTM_W), F32)]
    step_args, step_in, step_out, step_shapes, step_cfg = [], [], [], [], None
    if step:
        (s_qkvo, s_gates, s_c0, s_n0, s_m0), s_batch, s_seq, s_bb = step
        step_in, step_out, step_shapes = _mlstm_step_specs(s_batch, s_seq, s_bb, n_steps, cur)
        step_args = [s_qkvo, s_qkvo, s_qkvo, s_gates, s_c0, s_n0, s_m0]
        step_cfg = (s_bb, s_seq)
    def col_spec(group):
        return pl.BlockSpec((chunk, MLSTM_W), lambda s: (cur(s), group))

    def st_spec(shape, in_b):
        nd = len(shape)
        return pl.BlockSpec((1, N_HEADS) + shape, lambda s: (in_b(s), 0) + (0,) * nd)

    state_shapes = ((HEAD_DIM, HEAD_DIM), (1, HEAD_DIM), (1, 1))
    init_in = [st_spec(shape, lambda s: 0) for shape in state_shapes] if init else []
    seq_of = lambda s: cur(s) // nc
    smem = pl.BlockSpec(memory_space=pltpu.SMEM)
    return pl.pallas_call(
        functools.partial(_mlstm_seq_kernel, chunk=chunk, n_chunks=nc, n_steps=n_steps, zero_init=init is None,
                          step=step_cfg, outproj=bool(outproj)),
        grid=(n_steps + (1 if outproj else 0),),
        in_specs=[
            smem, smem,
            col_spec(0), col_spec(1), col_spec(2),
            pl.BlockSpec((chunk, LANES), lambda s: (cur(s), 0)),
        ] + init_in + op_in + step_in,
        out_specs=rows_out + [st_spec(shape, seq_of) for shape in state_shapes] + step_out,
        out_shape=rows_shapes + [
            jax.ShapeDtypeStruct((batch, N_HEADS, HEAD_DIM, HEAD_DIM), F32),
            jax.ShapeDtypeStruct((batch, N_HEADS, 1, HEAD_DIM), F32),
            jax.ShapeDtypeStruct((batch, N_HEADS, 1, 1), F32),
        ] + step_shapes,
        scratch_shapes=scratch,
        compiler_params=pltpu.CompilerParams(
            dimension_semantics=("arbitrary",),
            vmem_limit_bytes=VMEM_LIMIT_RESIDENT_WEIGHT if outproj else VMEM_LIMIT),
        name="mlstm_seq",
    )(b_i, b_f, qkvo, qkvo, qkvo, gates, *(init or ()), *op_args, *step_args)


def _mlstm_step_stages(head, bias_refs, q_ref, k_ref, v_ref, g_ref, c0_ref, n0_ref, m0_ref,
                       y_ref, c_ref, n_ref, m_ref, *, bb, seq):
    rows = bb * seq
    ti = lax.broadcasted_iota(jnp.int32, (rows, rows), 0)
    si = lax.broadcasted_iota(jnp.int32, (rows, rows), 1)
    same = (ti // seq) == (si // seq)
    tril = same & (si <= ti)
    eye = si == ti

    def to_row(col):
        return jnp.sum(jnp.where(eye, col, 0.0), axis=0, keepdims=True)

    gi, gf = _gate_cols(g_ref[...], bias_refs, head)
    lane = lax.broadcasted_iota(jnp.int32, m0_ref.shape, 1)
    m0 = jnp.sum(jnp.where(lane == head, m0_ref[...], 0.0), axis=1, keepdims=True)

    lf_row = to_row(jax.nn.log_sigmoid(gf))
    b = jnp.sum(jnp.where(tril, lf_row, 0.0), axis=1, keepdims=True)
    b_last = jnp.sum(jnp.where(same, lf_row, 0.0), axis=1, keepdims=True)
    a = gi - b
    a_row = to_row(a)
    cummax_a = jnp.max(jnp.where(tril, a_row, -jnp.inf), axis=1, keepdims=True)
    seqmax_a = jnp.max(jnp.where(same, a_row, -jnp.inf), axis=1, keepdims=True)
    m_t = jnp.maximum(m0 + b, cummax_a + b)
    m_new = jnp.maximum(m0 + b_last, seqmax_a + b_last)
    inter = jnp.exp(m0 + b - m_t)
    dmat = jnp.exp(jnp.where(tril, a_row + (b - m_t), -jnp.inf))
    decay = jnp.exp(m0 + b_last - m_new)
    w = jnp.exp(a + (b_last - m_new))

    q = q_ref[...]
    k = k_ref[...] * (HEAD_DIM ** -0.5)
    v = v_ref[...]
    s = lax.dot_general(q, k, (((1,), (1,)), ((), ())), preferred_element_type=F32) * dmat
    sv = jnp.dot(s.astype(BF16), v, preferred_element_type=F32)

    qf = q.astype(F32)
    kf = k.astype(F32)
    wk = w * kf
    wv = w * v.astype(F32)
    qc, qdn = [], []
    yield
    for i in range(bb):
        r = slice(i * seq, (i + 1) * seq)
        c_i = c0_ref[i, 0]
        n_i = n0_ref[i:i + 1, :]
        d_i = decay[i * seq:i * seq + 1, :]
        qc.append(jnp.dot(qf[r].astype(BF16), c_i.astype(BF16), preferred_element_type=F32))
        qdn.append(jnp.sum(qf[r] * n_i, axis=1, keepdims=True))
        upd = lax.dot_general(kf[r].astype(BF16), wv[r].astype(BF16), (((0,), (0,)), ((), ())),
                              preferred_element_type=F32)
        c_ref[i, 0] = d_i * c_i + upd
        n_ref[i:i + 1, :] = d_i * n_i + jnp.sum(wk[r], axis=0, keepdims=True)
        if (i + 1) % (bb // N_HEADS) == 0:
            yield

    num = sv + inter * jnp.concatenate(qc, axis=0)
    qn = jnp.sum(s, axis=1, keepdims=True) + inter * jnp.concatenate(qdn, axis=0)
    h = num / jnp.maximum(jnp.abs(qn), jnp.exp(-m_t))
    y_ref[...] = h
    m_ref[0, 0] = to_row(m_new)


def _mlstm_step_specs(batch, seq, bb, n_steps, step_of):
    assert (batch // bb) * N_HEADS == n_steps
    rows = bb * seq
    blk = lambda *idx: step_of(*idx) // N_HEADS
    head = lambda *idx: step_of(*idx) % N_HEADS

    def col_spec(off):
        return pl.BlockSpec((rows, HEAD_DIM), lambda *idx: (blk(*idx), off + head(*idx)))

    c_spec = pl.BlockSpec((bb, 1, HEAD_DIM, HEAD_DIM), lambda *idx: (blk(*idx), head(*idx), 0, 0))
    n_spec = pl.BlockSpec((bb, HEAD_DIM), lambda *idx: (blk(*idx), head(*idx)))
    in_specs = [
        col_spec(0), col_spec(N_HEADS), col_spec(2 * N_HEADS),
        pl.BlockSpec((rows, LANES), lambda *idx: (blk(*idx), 0)),
        c_spec, n_spec,
        pl.BlockSpec((rows, N_HEADS), lambda *idx: (blk(*idx), 0)),
    ]
    out_specs = [
        pl.BlockSpec((rows, HEAD_DIM), lambda *idx: (blk(*idx), head(*idx))),
        c_spec, n_spec,
        pl.BlockSpec((1, 1, 1, rows), lambda *idx: (blk(*idx), head(*idx), 0, 0)),
    ]
    out_shapes = [
        jax.ShapeDtypeStruct((batch * seq, MLSTM_W), F32),
        jax.ShapeDtypeStruct((batch, N_HEADS, HEAD_DIM, HEAD_DIM), F32),
        jax.ShapeDtypeStruct((batch, MLSTM_W), F32),
        jax.ShapeDtypeStruct((batch // bb, N_HEADS, 1, rows), F32),
    ]
    return in_specs, out_specs, out_shapes


def _outproj_kernel(x_ref, yp_ref, h_ref, o_ref, gn_ref, wo_ref, nw_ref, x2_ref, xn2_ref):
    x2 = x_ref[...] + jnp.dot(yp_ref[...], wo_ref[0:POOL_W, :], preferred_element_type=F32)
    ym = []
    for head in range(N_HEADS):
        cols = slice(head * HEAD_DIM, (head + 1) * HEAD_DIM)
        ym.append(_gated_head(h_ref[:, cols], o_ref[:, cols], gn_ref[:, cols]))
    x2 = x2 + jnp.dot(jnp.concatenate(ym, axis=1), wo_ref[POOL_W:, :], preferred_element_type=F32)
    x2_ref[...] = x2
    xn2_ref[...] = _rms(x2, nw_ref[...]).astype(BF16)


def _outproj(x, y_pool, h_ml, qkvo, gnorm, w_out, norm_w, *, tm):
    m = x.shape[0]
    row = lambda w: pl.BlockSpec((tm, w), lambda i: (i, 0))
    return pl.pallas_call(
        _outproj_kernel,
        grid=(m // tm,),
        in_specs=[
            row(D_MODEL), row(POOL_W), row(MLSTM_W),
            pl.BlockSpec((tm, MLSTM_W), lambda i: (i, 3)),
            pl.BlockSpec((1, MLSTM_W), lambda i: (0, 0)),
            pl.BlockSpec((D_MODEL, D_MODEL), lambda i: (0, 0), pipeline_mode=pl.Buffered(1)),
            pl.BlockSpec((1, D_MODEL), lambda i: (0, 0)),
        ],
        out_specs=[row(D_MODEL), row(D_MODEL)],
        out_shape=[
            jax.ShapeDtypeStruct((m, D_MODEL), F32),
            jax.ShapeDtypeStruct((m, D_MODEL), BF16),
        ],
        compiler_params=_params(("arbitrary",)),
        name="outproj",
    )(x, y_pool, h_ml, qkvo, gnorm, w_out, norm_w)


def _ffn_kernel(xn_ref, x2_ref, wg_ref, wu_ref, wd_ref, nw_ref, y_ref, acc_ref):
    f = pl.program_id(1)
    nf = pl.num_programs(1)

    @pl.when(f == 0)
    def _():
        acc_ref[...] = x2_ref[...]

    xn = xn_ref[...]
    g = jnp.dot(xn, wg_ref[...], preferred_element_type=F32)
    u = jnp.dot(xn, wu_ref[...], preferred_element_type=F32)
    h = (jax.nn.silu(g) * u).astype(BF16)
    acc_ref[...] += jnp.dot(h, wd_ref[...], preferred_element_type=F32)

    @pl.when(f == nf - 1)
    def _():
        y_ref[...] = _rms(acc_ref[...], nw_ref[...])


def _ffn(xn2, x2, w_gate, w_up, w_down, norm_w, *, tm, tf):
    m = xn2.shape[0]
    return pl.pallas_call(
        _ffn_kernel,
        grid=(m // tm, D_FF // tf),
        in_specs=[
            pl.BlockSpec((tm, D_MODEL), lambda i, f: (i, 0)),
            pl.BlockSpec((tm, D_MODEL), lambda i, f: (i, 0)),
            pl.BlockSpec((D_MODEL, tf), lambda i, f: (0, f)),
            pl.BlockSpec((D_MODEL, tf), lambda i, f: (0, f)),
            pl.BlockSpec((tf, D_MODEL), lambda i, f: (f, 0)),
            pl.BlockSpec((1, D_MODEL), lambda i, f: (0, 0)),
        ],
        out_specs=pl.BlockSpec((tm, D_MODEL), lambda i, f: (i, 0)),
        out_shape=jax.ShapeDtypeStruct((m, D_MODEL), F32),
        scratch_shapes=[pltpu.VMEM((tm, D_MODEL), F32)],
        compiler_params=_params(("arbitrary", "arbitrary")),
        name="ffn",
    )(xn2, x2, w_gate, w_up, w_down, norm_w)


def kernel(x_prompt, x_sample, state_pool, state_mlstm_C, state_mlstm_n, state_mlstm_m, meta_tokens, norm_mix_w, w_in, b_igate, b_fgate, w_pool, pool_scale, mlstm_norm_w, w_out, norm_ffn_w, w_gate, w_up, w_down, norm_final_w):
    bp, tp, _ = x_prompt.shape
    bs, ts, _ = x_sample.shape

    w_in_t = w_in[0].T
    w_g8 = jnp.pad(w_in_t[MAIN_W:], ((0, LANES - 2 * N_HEADS), (0, 0))).astype(BF16)
    wp = w_pool[0]
    nmix = norm_mix_w[0].reshape(1, D_MODEL)
    nffn = norm_ffn_w[0].reshape(1, D_MODEL)
    nfin = norm_final_w.reshape(1, D_MODEL)
    scale = pool_scale[0].reshape(1, POOL_W)
    gnorm = mlstm_norm_w[0].reshape(1, MLSTM_W)
    b_i, b_f = b_igate[0], b_fgate[0]

    xp = x_prompt.reshape(bp * tp, D_MODEL)
    xs = x_sample.reshape(bs * ts, D_MODEL)

    u_m, qkvo_m, g_m, w_main = _inproj(meta_tokens, nmix, w_in_t, w_g8, tn=META_TN)
    prev_p = u_m[1:N_META].reshape(1, POOL_HIST, POOL_W)
    n_prompt_tiles = bp * tp // PROMPT_TT
    convert = tuple((w[0], n_prompt_tiles) for w in (w_out, w_gate, w_up, w_down))
    yp_p, qkvo_p, g_p, pool_p, wo, wg, wu, wd = _inproj_rows(
        xp, nmix, w_main, w_g8, prev_p, wp, scale, bb=1, tt=PROMPT_TT, pos0=N_META, tiles_per_seq=tp // PROMPT_TT,
        convert=convert)
    yp_s, qkvo_s, g_s, pool_s = _inproj_rows(
        xs, nmix, w_main, w_g8, state_pool[0], wp, scale, bb=DECODE_BB, tt=ts, pos0=PAST_LEN, tiles_per_seq=1)

    _, *state_m = _mlstm_seq(qkvo_m, g_m, b_i, b_f, None, batch=1, seq=N_META, chunk=N_META)
    step_ops = (qkvo_s, g_s, state_mlstm_C[0], state_mlstm_n[0].reshape(bs, MLSTM_W),
                jnp.repeat(state_mlstm_m[0], ts, axis=0))
    x2_p, xn2_p, c_p, n_p, m_p, h_s, c_s, n_s, m_s = _mlstm_seq(
        qkvo_p, g_p, b_i, b_f, state_m, batch=bp, seq=tp, chunk=MLSTM_CHUNK,
        step=(step_ops, bs, ts, STEP_BB), outproj=(xp, yp_p, gnorm, wo, nffn))
    m_s = m_s.reshape(bs // STEP_BB, N_HEADS, STEP_BB, ts)[..., 0].transpose(0, 2, 1)

    x2_s, xn2_s = _outproj(xs, yp_s, h_s, qkvo_s, gnorm, wo, nffn, tm=OUTPROJ_TM)
    y_p = _ffn(xn2_p, x2_p, wg, wu, wd, nfin, tm=FFN_TM, tf=FFN_TF)
    y_s = _ffn(xn2_s, x2_s, wg, wu, wd, nfin, tm=FFN_TM, tf=FFN_TF)

    return (
        y_p.reshape(bp, tp, D_MODEL),
        y_s.reshape(bs, ts, D_MODEL),
        pool_p[None],
        c_p[None],
        n_p.reshape(1, bp, N_HEADS, HEAD_DIM),
        m_p.reshape(1, bp, N_HEADS),
        pool_s[None],
        c_s[None],
        n_s.reshape(1, bs, N_HEADS, HEAD_DIM),
        m_s.reshape(1, bs, N_HEADS),
    )
```

```python
import functools

import jax
import jax.numpy as jnp
from jax import lax
from jax.experimental import pallas as pl
from jax.experimental.pallas import tpu as pltpu

D_MODEL = 2048
N_META = 16
POOL_W = 1024
MLSTM_W = 1024
POOL_WINDOWS = (2, 4, 8, 16)
N_POOL_GROUPS = 4
POOL_GW = 256
POOL_HIST = 15
N_HEADS = 4
HEAD_DIM = 256
D_FF = 5632
QKVO_W = 4 * MLSTM_W
MAIN_W = POOL_W + QKVO_W
PAST_LEN = 16384
EPS = 1e-6

LANES = 128
HIST_PAD = 16
VMEM_LIMIT = 56 * 1024 * 1024
VMEM_LIMIT_RESIDENT_WEIGHT = 60 * 1024 * 1024

META_TN = 512
PROMPT_TT = 256
DECODE_BB = 32
MLSTM_CHUNK = 256
STEP_BB = 16
OUTPROJ_TM = 512
FFN_TM = 512
FFN_TF = 512

BF16 = jnp.bfloat16
F32 = jnp.float32


def _params(sem):
    return pltpu.CompilerParams(dimension_semantics=sem, vmem_limit_bytes=VMEM_LIMIT)


def _rms(x, w):
    return x * lax.rsqrt(jnp.mean(x * x, axis=-1, keepdims=True) + EPS) * w


def _inproj_kernel(*refs, n_groups, n_u_tiles):
    x_refs, (nw_ref, w_ref, wg_ref) = refs[:n_groups], refs[n_groups:n_groups + 3]
    outs = refs[n_groups + 3:]
    wcopy_ref, xn_refs = outs[3 * n_groups], outs[3 * n_groups + 1:]
    j = pl.program_id(0)
    nt = (((1,), (1,)), ((), ()))

    @pl.when(j == 0)
    def _():
        for x_ref, xn_ref, g_ref in zip(x_refs, xn_refs, outs[2:3 * n_groups:3]):
            xn = _rms(x_ref[...], nw_ref[...]).astype(BF16)
            xn_ref[...] = xn
            g_ref[...] = lax.dot_general(xn, wg_ref[...], nt, preferred_element_type=F32)

    w = w_ref[...].astype(BF16)
    wcopy_ref[...] = w
    ps = [lax.dot_general(xn_ref[...], w, nt, preferred_element_type=F32) for xn_ref in xn_refs]

    @pl.when(j < n_u_tiles)
    def _():
        for p, u_ref in zip(ps, outs[0:3 * n_groups:3]):
            u_ref[...] = p

    @pl.when(j >= n_u_tiles)
    def _():
        for p, qkvo_ref in zip(ps, outs[1:3 * n_groups:3]):
            qkvo_ref[...] = p.astype(BF16)


def _inproj(xs, norm_w, w_t, w_gate_t, *, tn):
    n_u = POOL_W // tn
    whole = lambda a: pl.BlockSpec(a.shape, lambda j: (0,) * a.ndim, pipeline_mode=pl.Buffered(1))
    w_spec = pl.BlockSpec((tn, D_MODEL), lambda j: (j, 0))
    out_specs, out_shapes = [], []
    for x in xs:
        tm = x.shape[0]
        out_specs += [pl.BlockSpec((tm, tn), lambda j: (0, jnp.minimum(j, n_u - 1))),
                      pl.BlockSpec((tm, tn), lambda j: (0, jnp.maximum(j - n_u, 0))),
                      pl.BlockSpec((tm, LANES), lambda j: (0, 0))]
        out_shapes += [jax.ShapeDtypeStruct((tm, POOL_W), F32),
                       jax.ShapeDtypeStruct((tm, QKVO_W), BF16),
                       jax.ShapeDtypeStruct((tm, LANES), F32)]
    return pl.pallas_call(
        functools.partial(_inproj_kernel, n_groups=len(xs), n_u_tiles=n_u),
        grid=(MAIN_W // tn,),
        in_specs=[whole(x) for x in xs] + [whole(norm_w), w_spec, whole(w_gate_t)],
        out_specs=out_specs + [w_spec],
        out_shape=out_shapes + [jax.ShapeDtypeStruct((MAIN_W, D_MODEL), BF16)],
        scratch_shapes=[pltpu.VMEM((x.shape[0], D_MODEL), BF16) for x in xs],
        compiler_params=_params(("arbitrary",)),
        name="inproj",
    )(*xs, norm_w, w_t, w_gate_t)


def _pool_group(e_ref, wp_ref, sc_ref, g, w, pos, bb):
    cols = slice(g * POOL_GW, (g + 1) * POOL_GW)
    cnt = jnp.minimum(w, pos + 1).astype(F32)
    d = []
    for b in range(bb):
        e = e_ref[b, :, cols]
        s = e + pltpu.roll(e, 1, axis=0)
        for k in range(1, g + 1):
            s = s + pltpu.roll(s, 2**k, axis=0)
        d.append(s[HIST_PAD:, :] / cnt - e[HIST_PAD:, :])
    d = jnp.concatenate(d, axis=0).astype(BF16)
    y = jnp.dot(d, wp_ref[g].astype(BF16), preferred_element_type=F32)
    return (y * sc_ref[:, cols]).astype(BF16)


def _pool_kernel(u_ref, prev_ref, wp_ref, sc_ref, y_ref, st_ref, e_ref, *, bb, tt, pos0):
    e_ref[:, 0:1, :] = jnp.zeros((bb, 1, POOL_W), F32)
    e_ref[:, 1:HIST_PAD, :] = prev_ref[...]
    e_ref[:, HIST_PAD:, :] = u_ref[...]
    pos = pos0 + lax.broadcasted_iota(jnp.int32, (tt, 1), 0)
    for g, w in enumerate(POOL_WINDOWS):
        y_ref[:, g * POOL_GW:(g + 1) * POOL_GW] = _pool_group(e_ref, wp_ref, sc_ref, g, w, pos, bb)
    st_ref[...] = e_ref[:, tt + 1:tt + HIST_PAD, :]


def _pool(u, prev, w_pool, scale, *, bb, pos0):
    b, tt, _ = u.shape
    whole = lambda a: pl.BlockSpec(a.shape, lambda i: (0,) * a.ndim)
    return pl.pallas_call(
        functools.partial(_pool_kernel, bb=bb, tt=tt, pos0=pos0),
        grid=(b // bb,),
        in_specs=[
            pl.BlockSpec((bb, tt, POOL_W), lambda i: (i, 0, 0)),
            pl.BlockSpec((bb, POOL_HIST, POOL_W), lambda i: (i, 0, 0)),
            whole(w_pool), whole(scale),
        ],
        out_specs=[
            pl.BlockSpec((bb * tt, POOL_W), lambda i: (i, 0)),
            pl.BlockSpec((bb, POOL_HIST, POOL_W), lambda i: (i, 0, 0)),
        ],
        out_shape=[
            jax.ShapeDtypeStruct((b * tt, POOL_W), BF16),
            jax.ShapeDtypeStruct((b, POOL_HIST, POOL_W), F32),
        ],
        scratch_shapes=[pltpu.VMEM((bb, HIST_PAD + tt, POOL_W), F32)],
        compiler_params=_params(("arbitrary",)),
        name="pool",
    )(u, prev, w_pool, scale)


INPROJ_COL_CHUNK = 1024


def _inproj_rows_kernel(x_ref, nw_ref, w_ref, wg_ref, prev_ref, wp_ref, sc_ref, *rest,
                        bb, tt, pos0, tiles_per_seq, conv_blocks):
    nc = len(conv_blocks)
    conv_in, (yp_ref, qkvo_ref, g_ref, st_ref) = rest[:nc], rest[nc:nc + 4]
    conv_out, e_ref = rest[nc + 4:2 * nc + 4], rest[-1]
    nt = (((1,), (1,)), ((), ()))
    ti = pl.program_id(0) % tiles_per_seq

    @pl.when(ti == 0)
    def _():
        e_ref[:, 0:1, :] = jnp.zeros((bb, 1, POOL_W), F32)
        e_ref[:, 1:HIST_PAD, :] = prev_ref[...]

    xn = _rms(x_ref[...], nw_ref[...]).astype(BF16)
    g_ref[...] = lax.dot_general(xn, wg_ref[...], nt, preferred_element_type=F32)
    u = lax.dot_general(xn, w_ref[0:POOL_W, :], nt, preferred_element_type=F32)
    e_ref[:, HIST_PAD:, :] = u.reshape(bb, tt, POOL_W)
    pos = pos0 + ti * tt + lax.broadcasted_iota(jnp.int32, (tt, 1), 0)

    for g, w in enumerate(POOL_WINDOWS):
        c = POOL_W + g * INPROJ_COL_CHUNK
        p = lax.dot_general(xn, w_ref[c:c + INPROJ_COL_CHUNK, :], nt, preferred_element_type=F32)
        qkvo_ref[:, c - POOL_W:c - POOL_W + INPROJ_COL_CHUNK] = p.astype(BF16)

        cols = slice(g * POOL_GW, (g + 1) * POOL_GW)
        yp_ref[:, cols] = _pool_group(e_ref, wp_ref, sc_ref, g, w, pos, bb)

        for src, dst in list(zip(conv_in, conv_out))[g::N_POOL_GROUPS]:
            dst[...] = src[...].astype(BF16)

    e_ref[:, 0:HIST_PAD, :] = e_ref[:, tt:tt + HIST_PAD, :]

    @pl.when(ti == tiles_per_seq - 1)
    def _():
        st_ref[...] = e_ref[:, 1:HIST_PAD, :]


def _inproj_rows(x, norm_w, w_t, w_gate_t, prev, w_pool, scale, *, bb, tt, pos0, tiles_per_seq, convert=()):
    assert MAIN_W == POOL_W + N_POOL_GROUPS * INPROJ_COL_CHUNK
    tm = bb * tt
    m = x.shape[0]
    n_seq = m // (tt * tiles_per_seq)
    row = lambda w: pl.BlockSpec((tm, w), lambda i: (i, 0))
    whole = lambda a: pl.BlockSpec(a.shape, lambda i: (0,) * a.ndim, pipeline_mode=pl.Buffered(1))
    seq_blk = lambda i: (i // tiles_per_seq, 0, 0)
    prev_spec = whole(prev) if prev.shape[0] == 1 else pl.BlockSpec((bb, POOL_HIST, POOL_W), seq_blk)
    assert all(n == m // tm for _, n in convert), "the kernel converts one block of each weight at every step"
    conv_specs, conv_shapes = _convert_specs(convert, m // tm, lambda i: i)
    return pl.pallas_call(
        functools.partial(_inproj_rows_kernel, bb=bb, tt=tt, pos0=pos0, tiles_per_seq=tiles_per_seq,
                          conv_blocks=tuple(n for _, n in convert)),
        grid=(m // tm,),
        in_specs=[row(D_MODEL), whole(norm_w), whole(w_t), whole(w_gate_t), prev_spec, whole(w_pool),
                  whole(scale)] + conv_specs,
        out_specs=[row(POOL_W), row(QKVO_W), row(LANES),
                   pl.BlockSpec((bb, POOL_HIST, POOL_W), seq_blk)] + conv_specs,
        out_shape=[
            jax.ShapeDtypeStruct((m, POOL_W), BF16),
            jax.ShapeDtypeStruct((m, QKVO_W), BF16),
            jax.ShapeDtypeStruct((m, LANES), F32),
            jax.ShapeDtypeStruct((n_seq, POOL_HIST, POOL_W), F32),
        ] + conv_shapes,
        scratch_shapes=[pltpu.VMEM((bb, HIST_PAD + tt, POOL_W), F32)],
        compiler_params=pltpu.CompilerParams(
            dimension_semantics=("arbitrary",), vmem_limit_bytes=VMEM_LIMIT_RESIDENT_WEIGHT),
        name="inproj_rows",
    )(x, norm_w, w_t, w_gate_t, prev, w_pool, scale, *[w for w, _ in convert])


def _convert_specs(weights, n_steps, step_of):
    specs, shapes = [], []
    for w, n_blocks in weights:
        assert n_blocks <= n_steps
        rows = w.shape[0] // n_blocks
        specs.append(pl.BlockSpec(
            (rows, w.shape[1]), lambda *idx, n_blocks=n_blocks: (jnp.minimum(step_of(*idx), n_blocks - 1), 0)))
        shapes.append(jax.ShapeDtypeStruct(w.shape, BF16))
    return specs, shapes


def _mlstm_chunk(q, k, v, gi, gf, c, n, m, *, chunk):
    ti = lax.broadcasted_iota(jnp.int32, (chunk, chunk), 0)
    si = lax.broadcasted_iota(jnp.int32, (chunk, chunk), 1)
    tril = si <= ti
    eye = si == ti

    def to_row(col):
        return jnp.sum(jnp.where(eye, col, 0.0), axis=0, keepdims=True)

    lf = jax.nn.log_sigmoid(gf)
    b = jnp.sum(jnp.where(tril, to_row(lf), 0.0), axis=1, keepdims=True)
    a = gi - b
    a_row = to_row(a)
    cummax_a = jnp.max(jnp.where(tril, a_row, -jnp.inf), axis=1, keepdims=True)
    m_t = jnp.maximum(m + b, cummax_a + b)
    inter = jnp.exp(m + b - m_t)
    dmat = jnp.exp(jnp.where(tril, a_row + (b - m_t), -jnp.inf))

    k = k * (HEAD_DIM ** -0.5)
    s = lax.dot_general(q, k, (((1,), (1,)), ((), ())), preferred_element_type=F32) * dmat
    num = jnp.dot(s.astype(BF16), v, preferred_element_type=F32)
    num = num + inter * jnp.dot(q, c.astype(BF16), preferred_element_type=F32)
    qn = jnp.sum(s, axis=1, keepdims=True) + inter * jnp.sum(q.astype(F32) * n, axis=1, keepdims=True)
    h = num / jnp.maximum(jnp.abs(qn), jnp.exp(-m_t))

    b_last = b[chunk - 1:chunk, :]
    m_new = m_t[chunk - 1:chunk, :]
    decay = jnp.exp(m + b_last - m_new)
    w = jnp.exp(a + (b_last - m_new))
    wv = (w * v.astype(F32)).astype(BF16)
    c_new = decay * c + lax.dot_general(k, wv, (((0,), (0,)), ((), ())), preferred_element_type=F32)
    n_new = decay * n + jnp.sum(w * k.astype(F32), axis=0, keepdims=True)

    return h, c_new, n_new, m_new


def _gate_cols(g, bias_refs, head):
    bi_ref, bf_ref = bias_refs
    lane = lax.broadcasted_iota(jnp.int32, g.shape, 1)
    gi = jnp.sum(jnp.where(lane == head, g, 0.0), axis=1, keepdims=True) + bi_ref[head]
    gf = jnp.sum(jnp.where(lane == head + N_HEADS, g, 0.0), axis=1, keepdims=True) + bf_ref[head]
    return gi, gf


N_STEP_IN = 7


N_OUTPROJ_IN = 6


def _gated_head(h, o, gnorm):
    return (jax.nn.sigmoid(o.astype(F32)) * _rms(h, gnorm)).astype(BF16)


N_STEP_OUT = 4


def _mlstm_seq_kernel(bi_ref, bf_ref, q_ref, k_ref, v_ref, g_ref, *rest,
                      chunk, n_chunks, n_steps, zero_init, step, outproj):
    bias_refs = (bi_ref, bf_ref)
    n_init = 0 if zero_init else 3
    init_refs, rest = rest[:n_init], rest[n_init:]
    no_in = N_OUTPROJ_IN if outproj else 0
    ns_in = N_STEP_IN if step else 0
    op_in, step_in = rest[:no_in], rest[no_in:no_in + ns_in]
    outs = rest[no_in + ns_in:]
    n_rows_out = 2 if outproj else 1
    rows_out, (c_ref, n_ref, m_ref) = outs[:n_rows_out], outs[n_rows_out:n_rows_out + 3]
    step_out = outs[n_rows_out + 3:n_rows_out + 3 + N_STEP_OUT] if step else ()
    s = pl.program_id(0)
    active = s < n_steps
    cur = jnp.minimum(s, n_steps - 1)

    @pl.when((cur % n_chunks == 0) & active)
    def _():
        for dst, src in zip((c_ref, n_ref, m_ref), init_refs or (None,) * 3):
            dst[...] = jnp.zeros(dst.shape, F32) if src is None else src[...]

    if outproj:
        ym_ref = outs[-1]

        @pl.when(s == 0)
        def _():
            ym_ref[...] = jnp.zeros(ym_ref.shape, BF16)

    stages = iter(())
    if step:
        stages = _mlstm_step_stages(cur % N_HEADS, bias_refs, *step_in, *step_out, bb=step[0], seq=step[1])
    next(stages, None)

    if outproj:
        x_ref, yp_ref, o_ref, gn_ref, wo_ref, nw_ref = op_in
        x2_ref, xn2_ref = rows_out
        slot = s % 2
        ssq = jnp.zeros((chunk, 1), F32)
        out_cols = D_MODEL // N_HEADS

    g = g_ref[...]
    for head in range(N_HEADS):
        if outproj:
            oc = slice(head * out_cols, (head + 1) * out_cols)
            x2 = x_ref[:, oc] + jnp.dot(yp_ref[...], wo_ref[0:POOL_W, oc], preferred_element_type=F32)
            x2 = x2 + jnp.dot(ym_ref[1 - slot], wo_ref[POOL_W:, oc], preferred_element_type=F32)
            x2_ref[:, oc] = x2
            ssq = ssq + jnp.sum(x2 * x2, axis=1, keepdims=True)
        cols = slice(head * HEAD_DIM, (head + 1) * HEAD_DIM)
        gi, gf = _gate_cols(g, bias_refs, head)
        c_old, n_old, m_old = c_ref[0, head], n_ref[0, head], m_ref[0, head]
        h, c_new, n_new, m_new = _mlstm_chunk(
            q_ref[:, cols], k_ref[:, cols], v_ref[:, cols], gi, gf, c_old, n_old, m_old, chunk=chunk)
        if outproj:
            ym_ref[slot, :, cols] = _gated_head(h, o_ref[:, cols], gn_ref[:, cols])
            c_new, n_new, m_new = (jnp.where(active, new, old)
                                   for new, old in ((c_new, c_old), (n_new, n_old), (m_new, m_old)))
        else:
            rows_out[0][:, cols] = h
        c_ref[0, head] = c_new
        n_ref[0, head] = n_new
        m_ref[0, head] = m_new
        next(stages, None)
    for _ in stages:
        pass

    if outproj:
        scale = lax.rsqrt(ssq * (1.0 / D_MODEL) + EPS)
        xn2_ref[...] = (x2_ref[...] * scale * nw_ref[...]).astype(BF16)


def _mlstm_seq(qkvo, gates, b_i, b_f, init, *, batch, seq, chunk, step=None, outproj=None):
    nc = seq // chunk
    n_steps = batch * nc
    cur = lambda s: jnp.minimum(s, n_steps - 1)
    prev = lambda s: jnp.maximum(s - 1, 0)
    whole = lambda a: pl.BlockSpec(a.shape, lambda s: (0,) * a.ndim, pipeline_mode=pl.Buffered(1))
    if outproj:
        x, y_pool, gnorm, w_out, norm_w = outproj
        op_args = [x, y_pool, qkvo, gnorm, w_out, norm_w]
        late_rows = lambda w: pl.BlockSpec((chunk, w), lambda s: (prev(s), 0))
        op_in = [late_rows(D_MODEL), late_rows(POOL_W),
                 pl.BlockSpec((chunk, MLSTM_W), lambda s: (cur(s), 3)),
                 whole(gnorm), whole(w_out), whole(norm_w)]
        rows_out = [late_rows(D_MODEL), late_rows(D_MODEL)]
        rows_shapes = [jax.ShapeDtypeStruct((batch * seq, D_MODEL), F32),
                       jax.ShapeDtypeStruct((batch * seq, D_MODEL), BF16)]
        scratch = [pltpu.VMEM((2, chunk, MLSTM_W), BF16)]
    else:
        op_args, op_in, scratch = [], [], []
        rows_out = [pl.BlockSpec((chunk, MLSTM_W), lambda s: (cur(s), 0))]
        rows_shapes = [jax.ShapeDtypeStruct((batch * seq, MLSTM_W), F32)]
    step_args, step_in, step_out, step_shapes, step_cfg = [], [], [], [], None
    if step:
        (s_qkvo, s_gates, s_c0, s_n0, s_m0), s_batch, s_seq, s_bb = step
        step_in, step_out, step_shapes = _mlstm_step_specs(s_batch, s_seq, s_bb, n_steps, cur)
        step_args = [s_qkvo, s_qkvo, s_qkvo, s_gates, s_c0, s_n0, s_m0]
        step_cfg = (s_bb, s_seq)
    def col_spec(group):
        return pl.BlockSpec((chunk, MLSTM_W), lambda s: (cur(s), group))

    def st_spec(shape, in_b):
        nd = len(shape)
        return pl.BlockSpec((1, N_HEADS) + shape, lambda s: (in_b(s), 0) + (0,) * nd)

    state_shapes = ((HEAD_DIM, HEAD_DIM), (1, HEAD_DIM), (1, 1))
    init_in = [st_spec(shape, lambda s: 0) for shape in state_shapes] if init else []
    seq_of = lambda s: cur(s) // nc
    smem = pl.BlockSpec(memory_space=pltpu.SMEM)
    return pl.pallas_call(
        functools.partial(_mlstm_seq_kernel, chunk=chunk, n_chunks=nc, n_steps=n_steps, zero_init=init is None,
                          step=step_cfg, outproj=bool(outproj)),
        grid=(n_steps + (1 if outproj else 0),),
        in_specs=[
            smem, smem,
            col_spec(0), col_spec(1), col_spec(2),
            pl.BlockSpec((chunk, LANES), lambda s: (cur(s), 0)),
        ] + init_in + op_in + step_in,
        out_specs=rows_out + [st_spec(shape, seq_of) for shape in state_shapes] + step_out,
        out_shape=rows_shapes + [
            jax.ShapeDtypeStruct((batch, N_HEADS, HEAD_DIM, HEAD_DIM), F32),
            jax.ShapeDtypeStruct((batch, N_HEADS, 1, HEAD_DIM), F32),
            jax.ShapeDtypeStruct((batch, N_HEADS, 1, 1), F32),
        ] + step_shapes,
        scratch_shapes=scratch,
        compiler_params=pltpu.CompilerParams(
            dimension_semantics=("arbitrary",),
            vmem_limit_bytes=VMEM_LIMIT_RESIDENT_WEIGHT if outproj else VMEM_LIMIT),
        name="mlstm_seq",
    )(b_i, b_f, qkvo, qkvo, qkvo, gates, *(init or ()), *op_args, *step_args)


def _mlstm_step_stages(head, bias_refs, q_ref, k_ref, v_ref, g_ref, c0_ref, n0_ref, m0_ref,
                       y_ref, c_ref, n_ref, m_ref, *, bb, seq):
    rows = bb * seq
    ti = lax.broadcasted_iota(jnp.int32, (rows, rows), 0)
    si = lax.broadcasted_iota(jnp.int32, (rows, rows), 1)
    same = (ti // seq) == (si // seq)
    tril = same & (si <= ti)
    eye = si == ti

    def to_row(col):
        return jnp.sum(jnp.where(eye, col, 0.0), axis=0, keepdims=True)

    gi, gf = _gate_cols(g_ref[...], bias_refs, head)
    lane = lax.broadcasted_iota(jnp.int32, m0_ref.shape, 1)
    m0 = jnp.sum(jnp.where(lane == head, m0_ref[...], 0.0), axis=1, keepdims=True)

    lf_row = to_row(jax.nn.log_sigmoid(gf))
    b = jnp.sum(jnp.where(tril, lf_row, 0.0), axis=1, keepdims=True)
    b_last = jnp.sum(jnp.where(same, lf_row, 0.0), axis=1, keepdims=True)
    a = gi - b
    a_row = to_row(a)
    cummax_a = jnp.max(jnp.where(tril, a_row, -jnp.inf), axis=1, keepdims=True)
    seqmax_a = jnp.max(jnp.where(same, a_row, -jnp.inf), axis=1, keepdims=True)
    m_t = jnp.maximum(m0 + b, cummax_a + b)
    m_new = jnp.maximum(m0 + b_last, seqmax_a + b_last)
    inter = jnp.exp(m0 + b - m_t)
    dmat = jnp.exp(jnp.where(tril, a_row + (b - m_t), -jnp.inf))
    decay = jnp.exp(m0 + b_last - m_new)
    w = jnp.exp(a + (b_last - m_new))

    q = q_ref[...]
    k = k_ref[...] * (HEAD_DIM ** -0.5)
    v = v_ref[...]
    s = lax.dot_general(q, k, (((1,), (1,)), ((), ())), preferred_element_type=F32) * dmat
    sv = jnp.dot(s.astype(BF16), v, preferred_element_type=F32)

    qf = q.astype(F32)
    kf = k.astype(F32)
    wk = w * kf
    wv = w * v.astype(F32)
    qc, qdn = [], []
    yield
    for i in range(bb):
        r = slice(i * seq, (i + 1) * seq)
        c_i = c0_ref[i, 0]
        n_i = n0_ref[i:i + 1, :]
        d_i = decay[i * seq:i * seq + 1, :]
        qc.append(jnp.dot(qf[r].astype(BF16), c_i.astype(BF16), preferred_element_type=F32))
        qdn.append(jnp.sum(qf[r] * n_i, axis=1, keepdims=True))
        upd = lax.dot_general(kf[r].astype(BF16), wv[r].astype(BF16), (((0,), (0,)), ((), ())),
                              preferred_element_type=F32)
        c_ref[i, 0] = d_i * c_i + upd
        n_ref[i:i + 1, :] = d_i * n_i + jnp.sum(wk[r], axis=0, keepdims=True)
        if (i + 1) % (bb // N_HEADS) == 0:
            yield

    num = sv + inter * jnp.concatenate(qc, axis=0)
    qn = jnp.sum(s, axis=1, keepdims=True) + inter * jnp.concatenate(qdn, axis=0)
    h = num / jnp.maximum(jnp.abs(qn), jnp.exp(-m_t))
    y_ref[...] = h
    m_ref[0, 0] = to_row(m_new)


def _mlstm_step_specs(batch, seq, bb, n_steps, step_of):
    assert (batch // bb) * N_HEADS == n_steps
    rows = bb * seq
    blk = lambda *idx: step_of(*idx) // N_HEADS
    head = lambda *idx: step_of(*idx) % N_HEADS

    def col_spec(off):
        return pl.BlockSpec((rows, HEAD_DIM), lambda *idx: (blk(*idx), off + head(*idx)))

    c_spec = pl.BlockSpec((bb, 1, HEAD_DIM, HEAD_DIM), lambda *idx: (blk(*idx), head(*idx), 0, 0))
    n_spec = pl.BlockSpec((bb, HEAD_DIM), lambda *idx: (blk(*idx), head(*idx)))
    in_specs = [
        col_spec(0), col_spec(N_HEADS), col_spec(2 * N_HEADS),
        pl.BlockSpec((rows, LANES), lambda *idx: (blk(*idx), 0)),
        c_spec, n_spec,
        pl.BlockSpec((rows, N_HEADS), lambda *idx: (blk(*idx), 0)),
    ]
    out_specs = [
        pl.BlockSpec((rows, HEAD_DIM), lambda *idx: (blk(*idx), head(*idx))),
        c_spec, n_spec,
        pl.BlockSpec((1, 1, 1, rows), lambda *idx: (blk(*idx), head(*idx), 0, 0)),
    ]
    out_shapes = [
        jax.ShapeDtypeStruct((batch * seq, MLSTM_W), F32),
        jax.ShapeDtypeStruct((batch, N_HEADS, HEAD_DIM, HEAD_DIM), F32),
        jax.ShapeDtypeStruct((batch, MLSTM_W), F32),
        jax.ShapeDtypeStruct((batch // bb, N_HEADS, 1, rows), F32),
    ]
    return in_specs, out_specs, out_shapes


def _outproj_kernel(x_ref, yp_ref, h_ref, o_ref, gn_ref, wo_ref, nw_ref, x2_ref, xn2_ref):
    x2 = x_ref[...] + jnp.dot(yp_ref[...], wo_ref[0:POOL_W, :], preferred_element_type=F32)
    ym = []
    for head in range(N_HEADS):
        cols = slice(head * HEAD_DIM, (head + 1) * HEAD_DIM)
        ym.append(_gated_head(h_ref[:, cols], o_ref[:, cols], gn_ref[:, cols]))
    x2 = x2 + jnp.dot(jnp.concatenate(ym, axis=1), wo_ref[POOL_W:, :], preferred_element_type=F32)
    x2_ref[...] = x2
    xn2_ref[...] = _rms(x2, nw_ref[...]).astype(BF16)


def _outproj(x, y_pool, h_ml, qkvo, gnorm, w_out, norm_w, *, tm):
    m = x.shape[0]
    row = lambda w: pl.BlockSpec((tm, w), lambda i: (i, 0))
    return pl.pallas_call(
        _outproj_kernel,
        grid=(m // tm,),
        in_specs=[
            row(D_MODEL), row(POOL_W), row(MLSTM_W),
            pl.BlockSpec((tm, MLSTM_W), lambda i: (i, 3)),
            pl.BlockSpec((1, MLSTM_W), lambda i: (0, 0)),
            pl.BlockSpec((D_MODEL, D_MODEL), lambda i: (0, 0), pipeline_mode=pl.Buffered(1)),
            pl.BlockSpec((1, D_MODEL), lambda i: (0, 0)),
        ],
        out_specs=[row(D_MODEL), row(D_MODEL)],
        out_shape=[
            jax.ShapeDtypeStruct((m, D_MODEL), F32),
            jax.ShapeDtypeStruct((m, D_MODEL), BF16),
        ],
        compiler_params=_params(("arbitrary",)),
        name="outproj",
    )(x, y_pool, h_ml, qkvo, gnorm, w_out, norm_w)


def _ffn_kernel(xn_ref, x2_ref, wg_ref, wu_ref, wd_ref, nw_ref, y_ref, acc_ref):
    f = pl.program_id(1)
    nf = pl.num_programs(1)

    @pl.when(f == 0)
    def _():
        acc_ref[...] = x2_ref[...]

    xn = xn_ref[...]
    g = jnp.dot(xn, wg_ref[...], preferred_element_type=F32)
    u = jnp.dot(xn, wu_ref[...], preferred_element_type=F32)
    h = (jax.nn.silu(g) * u).astype(BF16)
    acc_ref[...] += jnp.dot(h, wd_ref[...], preferred_element_type=F32)

    @pl.when(f == nf - 1)
    def _():
        y_ref[...] = _rms(acc_ref[...], nw_ref[...])


def _ffn(xn2, x2, w_gate, w_up, w_down, norm_w, *, tm, tf):
    m = xn2.shape[0]
    return pl.pallas_call(
        _ffn_kernel,
        grid=(m // tm, D_FF // tf),
        in_specs=[
            pl.BlockSpec((tm, D_MODEL), lambda i, f: (i, 0)),
            pl.BlockSpec((tm, D_MODEL), lambda i, f: (i, 0)),
            pl.BlockSpec((D_MODEL, tf), lambda i, f: (0, f)),
            pl.BlockSpec((D_MODEL, tf), lambda i, f: (0, f)),
            pl.BlockSpec((tf, D_MODEL), lambda i, f: (f, 0)),
            pl.BlockSpec((1, D_MODEL), lambda i, f: (0, 0)),
        ],
        out_specs=pl.BlockSpec((tm, D_MODEL), lambda i, f: (i, 0)),
        out_shape=jax.ShapeDtypeStruct((m, D_MODEL), F32),
        scratch_shapes=[pltpu.VMEM((tm, D_MODEL), F32)],
        compiler_params=_params(("arbitrary", "arbitrary")),
        name="ffn",
    )(xn2, x2, w_gate, w_up, w_down, norm_w)


def kernel(x_prompt, x_sample, state_pool, state_mlstm_C, state_mlstm_n, state_mlstm_m, meta_tokens, norm_mix_w, w_in, b_igate, b_fgate, w_pool, pool_scale, mlstm_norm_w, w_out, norm_ffn_w, w_gate, w_up, w_down, norm_final_w):
    bp, tp, _ = x_prompt.shape
    bs, ts, _ = x_sample.shape

    w_in_t = w_in[0].T
    w_g8 = jnp.pad(w_in_t[MAIN_W:], ((0, LANES - 2 * N_HEADS), (0, 0))).astype(BF16)
    wp = w_pool[0]
    nmix = norm_mix_w[0].reshape(1, D_MODEL)
    nffn = norm_ffn_w[0].reshape(1, D_MODEL)
    nfin = norm_final_w.reshape(1, D_MODEL)
    scale = pool_scale[0].reshape(1, POOL_W)
    gnorm = mlstm_norm_w[0].reshape(1, MLSTM_W)
    b_i, b_f = b_igate[0], b_fgate[0]

    xp = x_prompt.reshape(bp * tp, D_MODEL)
    xs = x_sample.reshape(bs * ts, D_MODEL)

    u_m, qkvo_m, g_m, u_s, qkvo_s, g_s, w_main = _inproj([meta_tokens, xs], nmix, w_in_t, w_g8, tn=META_TN)
    yp_s, pool_s = _pool(u_s.reshape(bs, ts, POOL_W), state_pool[0], wp, scale, bb=DECODE_BB, pos0=PAST_LEN)
    prev_p = u_m[1:N_META].reshape(1, POOL_HIST, POOL_W)
    n_prompt_tiles = bp * tp // PROMPT_TT
    convert = tuple((w[0], n_prompt_tiles) for w in (w_out, w_gate, w_up, w_down))
    yp_p, qkvo_p, g_p, pool_p, wo, wg, wu, wd = _inproj_rows(
        xp, nmix, w_main, w_g8, prev_p, wp, scale, bb=1, tt=PROMPT_TT, pos0=N_META, tiles_per_seq=tp // PROMPT_TT,
        convert=convert)

    _, *state_m = _mlstm_seq(qkvo_m, g_m, b_i, b_f, None, batch=1, seq=N_META, chunk=N_META)
    step_ops = (qkvo_s, g_s, state_mlstm_C[0], state_mlstm_n[0].reshape(bs, MLSTM_W),
                jnp.repeat(state_mlstm_m[0], ts, axis=0))
    x2_p, xn2_p, c_p, n_p, m_p, h_s, c_s, n_s, m_s = _mlstm_seq(
        qkvo_p, g_p, b_i, b_f, state_m, batch=bp, seq=tp, chunk=MLSTM_CHUNK,
        step=(step_ops, bs, ts, STEP_BB), outproj=(xp, yp_p, gnorm, wo, nffn))
    m_s = m_s.reshape(bs // STEP_BB, N_HEADS, STEP_BB, ts)[..., 0].transpose(0, 2, 1)

    x2_s, xn2_s = _outproj(xs, yp_s, h_s, qkvo_s, gnorm, wo, nffn, tm=OUTPROJ_TM)
    y_p = _ffn(xn2_p, x2_p, wg, wu, wd, nfin, tm=FFN_TM, tf=FFN_TF)
    y_s = _ffn(xn2_s, x2_s, wg, wu, wd, nfin, tm=FFN_TM, tf=FFN_TF)

    return (
        y_p.reshape(bp, tp, D_MODEL),
        y_s.reshape(bs, ts, D_MODEL),
        pool_p[None],
        c_p[None],
        n_p.reshape(1, bp, N_HEADS, HEAD_DIM),
        m_p.reshape(1, bp, N_HEADS),
        pool_s[None],
        c_s[None],
        n_s.reshape(1, bs, N_HEADS, HEAD_DIM),
        m_s.reshape(1, bs, N_HEADS),
    )
```

```python
import functools

import jax
import jax.numpy as jnp
from jax import lax
from jax.experimental import pallas as pl
from jax.experimental.pallas import tpu as pltpu

D_MODEL = 2048
N_META = 16
POOL_W = 1024
MLSTM_W = 1024
POOL_WINDOWS = (2, 4, 8, 16)
N_POOL_GROUPS = 4
POOL_GW = 256
POOL_HIST = 15
N_HEADS = 4
HEAD_DIM = 256
D_FF = 5632
QKVO_W = 4 * MLSTM_W
MAIN_W = POOL_W + QKVO_W
PAST_LEN = 16384
EPS = 1e-6

LANES = 128
HIST_PAD = 16
VMEM_LIMIT = 56 * 1024 * 1024
VMEM_LIMIT_RESIDENT_WEIGHT = 60 * 1024 * 1024

META_TN = 512
PROMPT_TT = 256
DECODE_BB = 32
MLSTM_CHUNK = 256
STEP_BB = 16
FFN_TM = 512
FFN_TF = 512

BF16 = jnp.bfloat16
F32 = jnp.float32


def _params(sem):
    return pltpu.CompilerParams(dimension_semantics=sem, vmem_limit_bytes=VMEM_LIMIT)


def _rms(x, w):
    return x * lax.rsqrt(jnp.mean(x * x, axis=-1, keepdims=True) + EPS) * w


def _inproj_kernel(*refs, n_groups, n_u_tiles):
    x_refs, (nw_ref, w_ref, wg_ref) = refs[:n_groups], refs[n_groups:n_groups + 3]
    outs = refs[n_groups + 3:]
    wcopy_ref, xn_refs = outs[3 * n_groups], outs[3 * n_groups + 1:]
    j = pl.program_id(0)
    nt = (((1,), (1,)), ((), ()))

    @pl.when(j == 0)
    def _():
        for x_ref, xn_ref, g_ref in zip(x_refs, xn_refs, outs[2:3 * n_groups:3]):
            xn = _rms(x_ref[...], nw_ref[...]).astype(BF16)
            xn_ref[...] = xn
            g_ref[...] = lax.dot_general(xn, wg_ref[...], nt, preferred_element_type=F32)

    w = w_ref[...].astype(BF16)
    wcopy_ref[...] = w
    ps = [lax.dot_general(xn_ref[...], w, nt, preferred_element_type=F32) for xn_ref in xn_refs]

    @pl.when(j < n_u_tiles)
    def _():
        for p, u_ref in zip(ps, outs[0:3 * n_groups:3]):
            u_ref[...] = p

    @pl.when(j >= n_u_tiles)
    def _():
        for p, qkvo_ref in zip(ps, outs[1:3 * n_groups:3]):
            qkvo_ref[...] = p.astype(BF16)


def _inproj(xs, norm_w, w_t, w_gate_t, *, tn):
    n_u = POOL_W // tn
    whole = lambda a: pl.BlockSpec(a.shape, lambda j: (0,) * a.ndim, pipeline_mode=pl.Buffered(1))
    w_spec = pl.BlockSpec((tn, D_MODEL), lambda j: (j, 0))
    out_specs, out_shapes = [], []
    for x in xs:
        tm = x.shape[0]
        out_specs += [pl.BlockSpec((tm, tn), lambda j: (0, jnp.minimum(j, n_u - 1))),
                      pl.BlockSpec((tm, tn), lambda j: (0, jnp.maximum(j - n_u, 0))),
                      pl.BlockSpec((tm, LANES), lambda j: (0, 0))]
        out_shapes += [jax.ShapeDtypeStruct((tm, POOL_W), F32),
                       jax.ShapeDtypeStruct((tm, QKVO_W), BF16),
                       jax.ShapeDtypeStruct((tm, LANES), F32)]
    return pl.pallas_call(
        functools.partial(_inproj_kernel, n_groups=len(xs), n_u_tiles=n_u),
        grid=(MAIN_W // tn,),
        in_specs=[whole(x) for x in xs] + [whole(norm_w), w_spec, whole(w_gate_t)],
        out_specs=out_specs + [w_spec],
        out_shape=out_shapes + [jax.ShapeDtypeStruct((MAIN_W, D_MODEL), BF16)],
        scratch_shapes=[pltpu.VMEM((x.shape[0], D_MODEL), BF16) for x in xs],
        compiler_params=_params(("arbitrary",)),
        name="inproj",
    )(*xs, norm_w, w_t, w_gate_t)


def _pool_group(e_ref, wp_ref, sc_ref, g, w, pos, bb):
    cols = slice(g * POOL_GW, (g + 1) * POOL_GW)
    cnt = jnp.minimum(w, pos + 1).astype(F32)
    d = []
    for b in range(bb):
        e = e_ref[b, :, cols]
        s = e + pltpu.roll(e, 1, axis=0)
        for k in range(1, g + 1):
            s = s + pltpu.roll(s, 2**k, axis=0)
        d.append(s[HIST_PAD:, :] / cnt - e[HIST_PAD:, :])
    d = jnp.concatenate(d, axis=0).astype(BF16)
    y = jnp.dot(d, wp_ref[g].astype(BF16), preferred_element_type=F32)
    return (y * sc_ref[:, cols]).astype(BF16)


INPROJ_COL_CHUNK = 1024


def _inproj_rows_kernel(x_ref, nw_ref, w_ref, wg_ref, prev_ref, wp_ref, sc_ref, *rest,
                        bb, tt, pos0, tiles_per_seq, conv_blocks):
    nc = len(conv_blocks)
    conv_in, (yp_ref, qkvo_ref, g_ref, st_ref) = rest[:nc], rest[nc:nc + 4]
    conv_out, e_ref = rest[nc + 4:2 * nc + 4], rest[-1]
    nt = (((1,), (1,)), ((), ()))
    ti = pl.program_id(0) % tiles_per_seq

    @pl.when(ti == 0)
    def _():
        e_ref[:, 0:1, :] = jnp.zeros((bb, 1, POOL_W), F32)
        e_ref[:, 1:HIST_PAD, :] = prev_ref[...]

    xn = _rms(x_ref[...], nw_ref[...]).astype(BF16)
    g_ref[...] = lax.dot_general(xn, wg_ref[...], nt, preferred_element_type=F32)
    u = lax.dot_general(xn, w_ref[0:POOL_W, :], nt, preferred_element_type=F32)
    e_ref[:, HIST_PAD:, :] = u.reshape(bb, tt, POOL_W)
    pos = pos0 + ti * tt + lax.broadcasted_iota(jnp.int32, (tt, 1), 0)

    for g, w in enumerate(POOL_WINDOWS):
        c = POOL_W + g * INPROJ_COL_CHUNK
        p = lax.dot_general(xn, w_ref[c:c + INPROJ_COL_CHUNK, :], nt, preferred_element_type=F32)
        qkvo_ref[:, c - POOL_W:c - POOL_W + INPROJ_COL_CHUNK] = p.astype(BF16)

        cols = slice(g * POOL_GW, (g + 1) * POOL_GW)
        yp_ref[:, cols] = _pool_group(e_ref, wp_ref, sc_ref, g, w, pos, bb)

        for src, dst in list(zip(conv_in, conv_out))[g::N_POOL_GROUPS]:
            dst[...] = src[...].astype(BF16)

    e_ref[:, 0:HIST_PAD, :] = e_ref[:, tt:tt + HIST_PAD, :]

    @pl.when(ti == tiles_per_seq - 1)
    def _():
        st_ref[...] = e_ref[:, 1:HIST_PAD, :]


def _inproj_rows(x, norm_w, w_t, w_gate_t, prev, w_pool, scale, *, bb, tt, pos0, tiles_per_seq, convert=()):
    assert MAIN_W == POOL_W + N_POOL_GROUPS * INPROJ_COL_CHUNK
    tm = bb * tt
    m = x.shape[0]
    n_seq = m // (tt * tiles_per_seq)
    row = lambda w: pl.BlockSpec((tm, w), lambda i: (i, 0))
    whole = lambda a: pl.BlockSpec(a.shape, lambda i: (0,) * a.ndim, pipeline_mode=pl.Buffered(1))
    seq_blk = lambda i: (i // tiles_per_seq, 0, 0)
    prev_spec = whole(prev) if prev.shape[0] == 1 else pl.BlockSpec((bb, POOL_HIST, POOL_W), seq_blk)
    assert all(n == m // tm for _, n in convert), "the kernel converts one block of each weight at every step"
    conv_specs, conv_shapes = _convert_specs(convert, m // tm, lambda i: i)
    return pl.pallas_call(
        functools.partial(_inproj_rows_kernel, bb=bb, tt=tt, pos0=pos0, tiles_per_seq=tiles_per_seq,
                          conv_blocks=tuple(n for _, n in convert)),
        grid=(m // tm,),
        in_specs=[row(D_MODEL), whole(norm_w), whole(w_t), whole(w_gate_t), prev_spec, whole(w_pool),
                  whole(scale)] + conv_specs,
        out_specs=[row(POOL_W), row(QKVO_W), row(LANES),
                   pl.BlockSpec((bb, POOL_HIST, POOL_W), seq_blk)] + conv_specs,
        out_shape=[
            jax.ShapeDtypeStruct((m, POOL_W), BF16),
            jax.ShapeDtypeStruct((m, QKVO_W), BF16),
            jax.ShapeDtypeStruct((m, LANES), F32),
            jax.ShapeDtypeStruct((n_seq, POOL_HIST, POOL_W), F32),
        ] + conv_shapes,
        scratch_shapes=[pltpu.VMEM((bb, HIST_PAD + tt, POOL_W), F32)],
        compiler_params=pltpu.CompilerParams(
            dimension_semantics=("arbitrary",), vmem_limit_bytes=VMEM_LIMIT_RESIDENT_WEIGHT),
        name="inproj_rows",
    )(x, norm_w, w_t, w_gate_t, prev, w_pool, scale, *[w for w, _ in convert])


def _convert_specs(weights, n_steps, step_of):
    specs, shapes = [], []
    for w, n_blocks in weights:
        assert n_blocks <= n_steps
        rows = w.shape[0] // n_blocks
        specs.append(pl.BlockSpec(
            (rows, w.shape[1]), lambda *idx, n_blocks=n_blocks: (jnp.minimum(step_of(*idx), n_blocks - 1), 0)))
        shapes.append(jax.ShapeDtypeStruct(w.shape, BF16))
    return specs, shapes


def _mlstm_chunk(q, k, v, gi, gf, c, n, m, *, chunk):
    ti = lax.broadcasted_iota(jnp.int32, (chunk, chunk), 0)
    si = lax.broadcasted_iota(jnp.int32, (chunk, chunk), 1)
    tril = si <= ti
    eye = si == ti

    def to_row(col):
        return jnp.sum(jnp.where(eye, col, 0.0), axis=0, keepdims=True)

    lf = jax.nn.log_sigmoid(gf)
    b = jnp.sum(jnp.where(tril, to_row(lf), 0.0), axis=1, keepdims=True)
    a = gi - b
    a_row = to_row(a)
    cummax_a = jnp.max(jnp.where(tril, a_row, -jnp.inf), axis=1, keepdims=True)
    m_t = jnp.maximum(m + b, cummax_a + b)
    inter = jnp.exp(m + b - m_t)
    dmat = jnp.exp(jnp.where(tril, a_row + (b - m_t), -jnp.inf))

    k = k * (HEAD_DIM ** -0.5)
    s = lax.dot_general(q, k, (((1,), (1,)), ((), ())), preferred_element_type=F32) * dmat
    num = jnp.dot(s.astype(BF16), v, preferred_element_type=F32)
    num = num + inter * jnp.dot(q, c.astype(BF16), preferred_element_type=F32)
    qn = jnp.sum(s, axis=1, keepdims=True) + inter * jnp.sum(q.astype(F32) * n, axis=1, keepdims=True)
    h = num / jnp.maximum(jnp.abs(qn), jnp.exp(-m_t))

    b_last = b[chunk - 1:chunk, :]
    m_new = m_t[chunk - 1:chunk, :]
    decay = jnp.exp(m + b_last - m_new)
    w = jnp.exp(a + (b_last - m_new))
    wv = (w * v.astype(F32)).astype(BF16)
    c_new = decay * c + lax.dot_general(k, wv, (((0,), (0,)), ((), ())), preferred_element_type=F32)
    n_new = decay * n + jnp.sum(w * k.astype(F32), axis=0, keepdims=True)

    return h, c_new, n_new, m_new


def _gate_cols(g, bias_refs, head):
    bi_ref, bf_ref = bias_refs
    lane = lax.broadcasted_iota(jnp.int32, g.shape, 1)
    gi = jnp.sum(jnp.where(lane == head, g, 0.0), axis=1, keepdims=True) + bi_ref[head]
    gf = jnp.sum(jnp.where(lane == head + N_HEADS, g, 0.0), axis=1, keepdims=True) + bf_ref[head]
    return gi, gf


N_STEP_IN = 7


N_OUTPROJ_IN = 6


def _gated_head(h, o, gnorm):
    return (jax.nn.sigmoid(o.astype(F32)) * _rms(h, gnorm)).astype(BF16)


N_STEP_OUT = 4


def _mlstm_seq_kernel(bi_ref, bf_ref, q_ref, k_ref, v_ref, g_ref, *rest,
                      chunk, n_chunks, n_steps, zero_init, step, outproj):
    bias_refs = (bi_ref, bf_ref)
    n_init = 0 if zero_init else 3
    init_refs, rest = rest[:n_init], rest[n_init:]
    no_in = N_OUTPROJ_IN if outproj else 0
    ns_in = N_STEP_IN if step else 0
    op_in, step_in = rest[:no_in], rest[no_in:no_in + ns_in]
    outs = rest[no_in + ns_in:]
    n_rows_out = 2 if outproj else 1
    rows_out, (c_ref, n_ref, m_ref) = outs[:n_rows_out], outs[n_rows_out:n_rows_out + 3]
    step_out = outs[n_rows_out + 3:n_rows_out + 3 + N_STEP_OUT] if step else ()
    s = pl.program_id(0)
    active = s < n_steps
    cur = jnp.minimum(s, n_steps - 1)

    @pl.when((cur % n_chunks == 0) & active)
    def _():
        for dst, src in zip((c_ref, n_ref, m_ref), init_refs or (None,) * 3):
            dst[...] = jnp.zeros(dst.shape, F32) if src is None else src[...]

    if outproj:
        ym_ref = outs[-1]

        @pl.when(s == 0)
        def _():
            ym_ref[...] = jnp.zeros(ym_ref.shape, BF16)

    stages = iter(())
    if step:
        stages = _mlstm_step_stages(cur % N_HEADS, bias_refs, *step_in, *step_out, bb=step[0], seq=step[1])
    next(stages, None)

    if outproj:
        x_ref, yp_ref, o_ref, gn_ref, wo_ref, nw_ref = op_in
        x2_ref, xn2_ref = rows_out
        slot = s % 2
        ssq = jnp.zeros((chunk, 1), F32)
        out_cols = D_MODEL // N_HEADS

    g = g_ref[...]
    for head in range(N_HEADS):
        if outproj:
            oc = slice(head * out_cols, (head + 1) * out_cols)
            x2 = x_ref[:, oc] + jnp.dot(yp_ref[...], wo_ref[0:POOL_W, oc], preferred_element_type=F32)
            x2 = x2 + jnp.dot(ym_ref[1 - slot], wo_ref[POOL_W:, oc], preferred_element_type=F32)
            x2_ref[:, oc] = x2
            ssq = ssq + jnp.sum(x2 * x2, axis=1, keepdims=True)
        cols = slice(head * HEAD_DIM, (head + 1) * HEAD_DIM)
        gi, gf = _gate_cols(g, bias_refs, head)
        c_old, n_old, m_old = c_ref[0, head], n_ref[0, head], m_ref[0, head]
        h, c_new, n_new, m_new = _mlstm_chunk(
            q_ref[:, cols], k_ref[:, cols], v_ref[:, cols], gi, gf, c_old, n_old, m_old, chunk=chunk)
        if outproj:
            ym_ref[slot, :, cols] = _gated_head(h, o_ref[:, cols], gn_ref[:, cols])
            c_new, n_new, m_new = (jnp.where(active, new, old)
                                   for new, old in ((c_new, c_old), (n_new, n_old), (m_new, m_old)))
        else:
            rows_out[0][:, cols] = h
        c_ref[0, head] = c_new
        n_ref[0, head] = n_new
        m_ref[0, head] = m_new
        next(stages, None)
    for _ in stages:
        pass

    if outproj:
        scale = lax.rsqrt(ssq * (1.0 / D_MODEL) + EPS)
        xn2_ref[...] = (x2_ref[...] * scale * nw_ref[...]).astype(BF16)


def _mlstm_seq(qkvo, gates, b_i, b_f, init, *, batch, seq, chunk, step=None, outproj=None):
    nc = seq // chunk
    n_steps = batch * nc
    cur = lambda s: jnp.minimum(s, n_steps - 1)
    prev = lambda s: jnp.maximum(s - 1, 0)
    whole = lambda a: pl.BlockSpec(a.shape, lambda s: (0,) * a.ndim, pipeline_mode=pl.Buffered(1))
    if outproj:
        x, y_pool, gnorm, w_out, norm_w = outproj
        op_args = [x, y_pool, qkvo, gnorm, w_out, norm_w]
        late_rows = lambda w: pl.BlockSpec((chunk, w), lambda s: (prev(s), 0))
        op_in = [late_rows(D_MODEL), late_rows(POOL_W),
                 pl.BlockSpec((chunk, MLSTM_W), lambda s: (cur(s), 3)),
                 whole(gnorm), whole(w_out), whole(norm_w)]
        rows_out = [late_rows(D_MODEL), late_rows(D_MODEL)]
        rows_shapes = [jax.ShapeDtypeStruct((batch * seq, D_MODEL), F32),
                       jax.ShapeDtypeStruct((batch * seq, D_MODEL), BF16)]
        scratch = [pltpu.VMEM((2, chunk, MLSTM_W), BF16)]
    else:
        op_args, op_in, scratch = [], [], []
        rows_out = [pl.BlockSpec((chunk, MLSTM_W), lambda s: (cur(s), 0))]
        rows_shapes = [jax.ShapeDtypeStruct((batch * seq, MLSTM_W), F32)]
    step_args, step_in, step_out, step_shapes, step_cfg = [], [], [], [], None
    if step:
        (s_qkvo, s_gates, s_c0, s_n0, s_m0), s_batch, s_seq, s_bb = step
        step_in, step_out, step_shapes = _mlstm_step_specs(s_batch, s_seq, s_bb, n_steps, cur)
        step_args = [s_qkvo, s_qkvo, s_qkvo, s_gates, s_c0, s_n0, s_m0]
        step_cfg = (s_bb, s_seq)
    def col_spec(group):
        return pl.BlockSpec((chunk, MLSTM_W), lambda s: (cur(s), group))

    def st_spec(shape, in_b):
        nd = len(shape)
        return pl.BlockSpec((1, N_HEADS) + shape, lambda s: (in_b(s), 0) + (0,) * nd)

    state_shapes = ((HEAD_DIM, HEAD_DIM), (1, HEAD_DIM), (1, 1))
    init_in = [st_spec(shape, lambda s: 0) for shape in state_shapes] if init else []
    seq_of = lambda s: cur(s) // nc
    smem = pl.BlockSpec(memory_space=pltpu.SMEM)
    return pl.pallas_call(
        functools.partial(_mlstm_seq_kernel, chunk=chunk, n_chunks=nc, n_steps=n_steps, zero_init=init is None,
                          step=step_cfg, outproj=bool(outproj)),
        grid=(n_steps + (1 if outproj else 0),),
        in_specs=[
            smem, smem,
            col_spec(0), col_spec(1), col_spec(2),
            pl.BlockSpec((chunk, LANES), lambda s: (cur(s), 0)),
        ] + init_in + op_in + step_in,
        out_specs=rows_out + [st_spec(shape, seq_of) for shape in state_shapes] + step_out,
        out_shape=rows_shapes + [
            jax.ShapeDtypeStruct((batch, N_HEADS, HEAD_DIM, HEAD_DIM), F32),
            jax.ShapeDtypeStruct((batch, N_HEADS, 1, HEAD_DIM), F32),
            jax.ShapeDtypeStruct((batch, N_HEADS, 1, 1), F32),
        ] + step_shapes,
        scratch_shapes=scratch,
        compiler_params=pltpu.CompilerParams(
            dimension_semantics=("arbitrary",),
            vmem_limit_bytes=VMEM_LIMIT_RESIDENT_WEIGHT if outproj else VMEM_LIMIT),
        name="mlstm_seq",
    )(b_i, b_f, qkvo, qkvo, qkvo, gates, *(init or ()), *op_args, *step_args)


def _mlstm_step_stages(head, bias_refs, q_ref, k_ref, v_ref, g_ref, c0_ref, n0_ref, m0_ref,
                       y_ref, c_ref, n_ref, m_ref, *, bb, seq):
    rows = bb * seq
    ti = lax.broadcasted_iota(jnp.int32, (rows, rows), 0)
    si = lax.broadcasted_iota(jnp.int32, (rows, rows), 1)
    same = (ti // seq) == (si // seq)
    tril = same & (si <= ti)
    eye = si == ti

    def to_row(col):
        return jnp.sum(jnp.where(eye, col, 0.0), axis=0, keepdims=True)

    gi, gf = _gate_cols(g_ref[...], bias_refs, head)
    lane = lax.broadcasted_iota(jnp.int32, m0_ref.shape, 1)
    m0 = jnp.sum(jnp.where(lane == head, m0_ref[...], 0.0), axis=1, keepdims=True)

    lf_row = to_row(jax.nn.log_sigmoid(gf))
    b = jnp.sum(jnp.where(tril, lf_row, 0.0), axis=1, keepdims=True)
    b_last = jnp.sum(jnp.where(same, lf_row, 0.0), axis=1, keepdims=True)
    a = gi - b
    a_row = to_row(a)
    cummax_a = jnp.max(jnp.where(tril, a_row, -jnp.inf), axis=1, keepdims=True)
    seqmax_a = jnp.max(jnp.where(same, a_row, -jnp.inf), axis=1, keepdims=True)
    m_t = jnp.maximum(m0 + b, cummax_a + b)
    m_new = jnp.maximum(m0 + b_last, seqmax_a + b_last)
    inter = jnp.exp(m0 + b - m_t)
    dmat = jnp.exp(jnp.where(tril, a_row + (b - m_t), -jnp.inf))
    decay = jnp.exp(m0 + b_last - m_new)
    w = jnp.exp(a + (b_last - m_new))

    q = q_ref[...]
    k = k_ref[...] * (HEAD_DIM ** -0.5)
    v = v_ref[...]
    s = lax.dot_general(q, k, (((1,), (1,)), ((), ())), preferred_element_type=F32) * dmat
    sv = jnp.dot(s.astype(BF16), v, preferred_element_type=F32)

    qf = q.astype(F32)
    kf = k.astype(F32)
    wk = w * kf
    wv = w * v.astype(F32)
    qc, qdn = [], []
    yield
    for i in range(bb):
        r = slice(i * seq, (i + 1) * seq)
        c_i = c0_ref[i, 0]
        n_i = n0_ref[i:i + 1, :]
        d_i = decay[i * seq:i * seq + 1, :]
        qc.append(jnp.dot(qf[r].astype(BF16), c_i.astype(BF16), preferred_element_type=F32))
        qdn.append(jnp.sum(qf[r] * n_i, axis=1, keepdims=True))
        upd = lax.dot_general(kf[r].astype(BF16), wv[r].astype(BF16), (((0,), (0,)), ((), ())),
                              preferred_element_type=F32)
        c_ref[i, 0] = d_i * c_i + upd
        n_ref[i:i + 1, :] = d_i * n_i + jnp.sum(wk[r], axis=0, keepdims=True)
        if (i + 1) % (bb // N_HEADS) == 0:
            yield

    num = sv + inter * jnp.concatenate(qc, axis=0)
    qn = jnp.sum(s, axis=1, keepdims=True) + inter * jnp.concatenate(qdn, axis=0)
    h = num / jnp.maximum(jnp.abs(qn), jnp.exp(-m_t))
    y_ref[...] = h
    m_ref[0, 0] = to_row(m_new)


def _mlstm_step_specs(batch, seq, bb, n_steps, step_of):
    assert (batch // bb) * N_HEADS == n_steps
    rows = bb * seq
    blk = lambda *idx: step_of(*idx) // N_HEADS
    head = lambda *idx: step_of(*idx) % N_HEADS

    def col_spec(off):
        return pl.BlockSpec((rows, HEAD_DIM), lambda *idx: (blk(*idx), off + head(*idx)))

    c_spec = pl.BlockSpec((bb, 1, HEAD_DIM, HEAD_DIM), lambda *idx: (blk(*idx), head(*idx), 0, 0))
    n_spec = pl.BlockSpec((bb, HEAD_DIM), lambda *idx: (blk(*idx), head(*idx)))
    in_specs = [
        col_spec(0), col_spec(N_HEADS), col_spec(2 * N_HEADS),
        pl.BlockSpec((rows, LANES), lambda *idx: (blk(*idx), 0)),
        c_spec, n_spec,
        pl.BlockSpec((rows, N_HEADS), lambda *idx: (blk(*idx), 0)),
    ]
    out_specs = [
        pl.BlockSpec((rows, HEAD_DIM), lambda *idx: (blk(*idx), head(*idx))),
        c_spec, n_spec,
        pl.BlockSpec((1, 1, 1, rows), lambda *idx: (blk(*idx), head(*idx), 0, 0)),
    ]
    out_shapes = [
        jax.ShapeDtypeStruct((batch * seq, MLSTM_W), F32),
        jax.ShapeDtypeStruct((batch, N_HEADS, HEAD_DIM, HEAD_DIM), F32),
        jax.ShapeDtypeStruct((batch, MLSTM_W), F32),
        jax.ShapeDtypeStruct((batch // bb, N_HEADS, 1, rows), F32),
    ]
    return in_specs, out_specs, out_shapes


def _mixer_out_kernel(x_ref, u_ref, prev_ref, wp_ref, sc_ref, h_ref, o_ref, gn_ref, wo_ref, nw_ref,
                      x2_ref, xn2_ref, st_ref, e_ref, *, bb, tt, pos0):
    e_ref[:, 0:1, :] = jnp.zeros((bb, 1, POOL_W), F32)
    e_ref[:, 1:HIST_PAD, :] = prev_ref[...]
    e_ref[:, HIST_PAD:, :] = u_ref[...]
    st_ref[...] = e_ref[:, tt + 1:tt + HIST_PAD, :]
    pos = pos0 + lax.broadcasted_iota(jnp.int32, (tt, 1), 0)
    ym = [_gated_head(h_ref[:, c], o_ref[:, c], gn_ref[:, c])
          for c in (slice(hd * HEAD_DIM, (hd + 1) * HEAD_DIM) for hd in range(N_HEADS))]
    x2 = x_ref[...] + jnp.dot(jnp.concatenate(ym, axis=1), wo_ref[POOL_W:, :], preferred_element_type=F32)
    yp = [_pool_group(e_ref, wp_ref, sc_ref, g, w, pos, bb) for g, w in enumerate(POOL_WINDOWS)]
    x2 = x2 + jnp.dot(jnp.concatenate(yp, axis=1), wo_ref[0:POOL_W, :], preferred_element_type=F32)
    x2_ref[...] = x2
    xn2_ref[...] = _rms(x2, nw_ref[...]).astype(BF16)


def _mixer_out(x, u, prev, w_pool, scale, h_ml, qkvo, gnorm, w_out, norm_w, *, bb, pos0):
    b, tt, _ = u.shape
    tm = bb * tt
    row = lambda w: pl.BlockSpec((tm, w), lambda i: (i, 0))
    whole = lambda a: pl.BlockSpec(a.shape, lambda i: (0,) * a.ndim, pipeline_mode=pl.Buffered(1))
    seq_blk = lambda w: pl.BlockSpec((bb, w, POOL_W), lambda i: (i, 0, 0))
    return pl.pallas_call(
        functools.partial(_mixer_out_kernel, bb=bb, tt=tt, pos0=pos0),
        grid=(b // bb,),
        in_specs=[
            row(D_MODEL), seq_blk(tt), seq_blk(POOL_HIST), whole(w_pool), whole(scale),
            row(MLSTM_W), pl.BlockSpec((tm, MLSTM_W), lambda i: (i, 3)), whole(gnorm), whole(w_out), whole(norm_w),
        ],
        out_specs=[row(D_MODEL), row(D_MODEL), seq_blk(POOL_HIST)],
        out_shape=[
            jax.ShapeDtypeStruct((b * tt, D_MODEL), F32),
            jax.ShapeDtypeStruct((b * tt, D_MODEL), BF16),
            jax.ShapeDtypeStruct((b, POOL_HIST, POOL_W), F32),
        ],
        scratch_shapes=[pltpu.VMEM((bb, HIST_PAD + tt, POOL_W), F32)],
        compiler_params=_params(("arbitrary",)),
        name="mixer_out",
    )(x, u, prev, w_pool, scale, h_ml, qkvo, gnorm, w_out, norm_w)


def _ffn_kernel(xn_ref, x2_ref, wg_ref, wu_ref, wd_ref, nw_ref, y_ref, acc_ref):
    f = pl.program_id(1)
    nf = pl.num_programs(1)

    @pl.when(f == 0)
    def _():
        acc_ref[...] = x2_ref[...]

    xn = xn_ref[...]
    g = jnp.dot(xn, wg_ref[...], preferred_element_type=F32)
    u = jnp.dot(xn, wu_ref[...], preferred_element_type=F32)
    h = (jax.nn.silu(g) * u).astype(BF16)
    acc_ref[...] += jnp.dot(h, wd_ref[...], preferred_element_type=F32)

    @pl.when(f == nf - 1)
    def _():
        y_ref[...] = _rms(acc_ref[...], nw_ref[...])


def _ffn(xn2, x2, w_gate, w_up, w_down, norm_w, *, tm, tf):
    m = xn2.shape[0]
    return pl.pallas_call(
        _ffn_kernel,
        grid=(m // tm, D_FF // tf),
        in_specs=[
            pl.BlockSpec((tm, D_MODEL), lambda i, f: (i, 0)),
            pl.BlockSpec((tm, D_MODEL), lambda i, f: (i, 0)),
            pl.BlockSpec((D_MODEL, tf), lambda i, f: (0, f)),
            pl.BlockSpec((D_MODEL, tf), lambda i, f: (0, f)),
            pl.BlockSpec((tf, D_MODEL), lambda i, f: (f, 0)),
            pl.BlockSpec((1, D_MODEL), lambda i, f: (0, 0)),
        ],
        out_specs=pl.BlockSpec((tm, D_MODEL), lambda i, f: (i, 0)),
        out_shape=jax.ShapeDtypeStruct((m, D_MODEL), F32),
        scratch_shapes=[pltpu.VMEM((tm, D_MODEL), F32)],
        compiler_params=_params(("arbitrary", "arbitrary")),
        name="ffn",
    )(xn2, x2, w_gate, w_up, w_down, norm_w)


def kernel(x_prompt, x_sample, state_pool, state_mlstm_C, state_mlstm_n, state_mlstm_m, meta_tokens, norm_mix_w, w_in, b_igate, b_fgate, w_pool, pool_scale, mlstm_norm_w, w_out, norm_ffn_w, w_gate, w_up, w_down, norm_final_w):
    bp, tp, _ = x_prompt.shape
    bs, ts, _ = x_sample.shape

    w_in_t = w_in[0].T
    w_g8 = jnp.pad(w_in_t[MAIN_W:], ((0, LANES - 2 * N_HEADS), (0, 0))).astype(BF16)
    wp = w_pool[0]
    nmix = norm_mix_w[0].reshape(1, D_MODEL)
    nffn = norm_ffn_w[0].reshape(1, D_MODEL)
    nfin = norm_final_w.reshape(1, D_MODEL)
    scale = pool_scale[0].reshape(1, POOL_W)
    gnorm = mlstm_norm_w[0].reshape(1, MLSTM_W)
    b_i, b_f = b_igate[0], b_fgate[0]

    xp = x_prompt.reshape(bp * tp, D_MODEL)
    xs = x_sample.reshape(bs * ts, D_MODEL)

    u_m, qkvo_m, g_m, u_s, qkvo_s, g_s, w_main = _inproj([meta_tokens, xs], nmix, w_in_t, w_g8, tn=META_TN)
    prev_p = u_m[1:N_META].reshape(1, POOL_HIST, POOL_W)
    n_prompt_tiles = bp * tp // PROMPT_TT
    convert = tuple((w[0], n_prompt_tiles) for w in (w_out, w_gate, w_up, w_down))
    yp_p, qkvo_p, g_p, pool_p, wo, wg, wu, wd = _inproj_rows(
        xp, nmix, w_main, w_g8, prev_p, wp, scale, bb=1, tt=PROMPT_TT, pos0=N_META, tiles_per_seq=tp // PROMPT_TT,
        convert=convert)

    _, *state_m = _mlstm_seq(qkvo_m, g_m, b_i, b_f, None, batch=1, seq=N_META, chunk=N_META)
    step_ops = (qkvo_s, g_s, state_mlstm_C[0], state_mlstm_n[0].reshape(bs, MLSTM_W),
                jnp.repeat(state_mlstm_m[0], ts, axis=0))
    x2_p, xn2_p, c_p, n_p, m_p, h_s, c_s, n_s, m_s = _mlstm_seq(
        qkvo_p, g_p, b_i, b_f, state_m, batch=bp, seq=tp, chunk=MLSTM_CHUNK,
        step=(step_ops, bs, ts, STEP_BB), outproj=(xp, yp_p, gnorm, wo, nffn))
    m_s = m_s.reshape(bs // STEP_BB, N_HEADS, STEP_BB, ts)[..., 0].transpose(0, 2, 1)

    x2_s, xn2_s, pool_s = _mixer_out(xs, u_s.reshape(bs, ts, POOL_W), state_pool[0], wp, scale, h_s, qkvo_s, gnorm,
                                     wo, nffn, bb=DECODE_BB, pos0=PAST_LEN)
    y_p = _ffn(xn2_p, x2_p, wg, wu, wd, nfin, tm=FFN_TM, tf=FFN_TF)
    y_s = _ffn(xn2_s, x2_s, wg, wu, wd, nfin, tm=FFN_TM, tf=FFN_TF)

    return (
        y_p.reshape(bp, tp, D_MODEL),
        y_s.reshape(bs, ts, D_MODEL),
        pool_p[None],
        c_p[None],
        n_p.reshape(1, bp, N_HEADS, HEAD_DIM),
        m_p.reshape(1, bp, N_HEADS),
        pool_s[None],
        c_s[None],
        n_s.reshape(1, bs, N_HEADS, HEAD_DIM),
        m_s.reshape(1, bs, N_HEADS),
    )
```

```python
import functools

import jax
import jax.numpy as jnp
from jax import lax
from jax.experimental import pallas as pl
from jax.experimental.pallas import tpu as pltpu

D_MODEL = 2048
N_META = 16
POOL_W = 1024
MLSTM_W = 1024
POOL_WINDOWS = (2, 4, 8, 16)
N_POOL_GROUPS = 4
POOL_GW = 256
POOL_HIST = 15
N_HEADS = 4
HEAD_DIM = 256
D_FF = 5632
QKVO_W = 4 * MLSTM_W
MAIN_W = POOL_W + QKVO_W
PAST_LEN = 16384
EPS = 1e-6

LANES = 128
HIST_PAD = 16
VMEM_LIMIT = 56 * 1024 * 1024
VMEM_LIMIT_RESIDENT_WEIGHT = 60 * 1024 * 1024

META_TN = 512
PROMPT_TT = 256
DECODE_BB = 32
MLSTM_CHUNK = 256
STEP_BB = 16
FFN_TM = 512
FFN_TF = 512

BF16 = jnp.bfloat16
F32 = jnp.float32


def _params(sem):
    return pltpu.CompilerParams(dimension_semantics=sem, vmem_limit_bytes=VMEM_LIMIT)


def _rms(x, w):
    return x * lax.rsqrt(jnp.mean(x * x, axis=-1, keepdims=True) + EPS) * w


def _inproj_kernel(*refs, n_groups, n_u_tiles):
    x_refs, (nw_ref, w_ref, wg_ref) = refs[:n_groups], refs[n_groups:n_groups + 3]
    outs = refs[n_groups + 3:]
    wcopy_ref, xn_ref = outs[3 * n_groups], outs[3 * n_groups + 1]
    rows, start = [], 0
    for x_ref in x_refs:
        rows.append(slice(start, start + x_ref.shape[0]))
        start += x_ref.shape[0]
    j = pl.program_id(0)
    nt = (((1,), (1,)), ((), ()))

    @pl.when(j == 0)
    def _():
        for x_ref, r in zip(x_refs, rows):
            xn_ref[r, :] = _rms(x_ref[...], nw_ref[...]).astype(BF16)
        g = lax.dot_general(xn_ref[...], wg_ref[...], nt, preferred_element_type=F32)
        for g_ref, r in zip(outs[2:3 * n_groups:3], rows):
            g_ref[...] = g[r]

    w = w_ref[...].astype(BF16)
    wcopy_ref[...] = w
    p = lax.dot_general(xn_ref[...], w, nt, preferred_element_type=F32)
    ps = [p[r] for r in rows]

    @pl.when(j < n_u_tiles)
    def _():
        for p, u_ref in zip(ps, outs[0:3 * n_groups:3]):
            u_ref[...] = p

    @pl.when(j >= n_u_tiles)
    def _():
        for p, qkvo_ref in zip(ps, outs[1:3 * n_groups:3]):
            qkvo_ref[...] = p.astype(BF16)


def _inproj(xs, norm_w, w_t, w_gate_t, *, tn):
    n_u = POOL_W // tn
    whole = lambda a: pl.BlockSpec(a.shape, lambda j: (0,) * a.ndim, pipeline_mode=pl.Buffered(1))
    w_spec = pl.BlockSpec((tn, D_MODEL), lambda j: (j, 0))
    out_specs, out_shapes = [], []
    for x in xs:
        tm = x.shape[0]
        out_specs += [pl.BlockSpec((tm, tn), lambda j: (0, jnp.minimum(j, n_u - 1))),
                      pl.BlockSpec((tm, tn), lambda j: (0, jnp.maximum(j - n_u, 0))),
                      pl.BlockSpec((tm, LANES), lambda j: (0, 0))]
        out_shapes += [jax.ShapeDtypeStruct((tm, POOL_W), F32),
                       jax.ShapeDtypeStruct((tm, QKVO_W), BF16),
                       jax.ShapeDtypeStruct((tm, LANES), F32)]
    return pl.pallas_call(
        functools.partial(_inproj_kernel, n_groups=len(xs), n_u_tiles=n_u),
        grid=(MAIN_W // tn,),
        in_specs=[whole(x) for x in xs] + [whole(norm_w), w_spec, whole(w_gate_t)],
        out_specs=out_specs + [w_spec],
        out_shape=out_shapes + [jax.ShapeDtypeStruct((MAIN_W, D_MODEL), BF16)],
        scratch_shapes=[pltpu.VMEM((sum(x.shape[0] for x in xs), D_MODEL), BF16)],
        compiler_params=_params(("arbitrary",)),
        name="inproj",
    )(*xs, norm_w, w_t, w_gate_t)


def _pool_group(e_ref, wp_ref, sc_ref, g, w, pos, bb):
    cols = slice(g * POOL_GW, (g + 1) * POOL_GW)
    cnt = jnp.minimum(w, pos + 1).astype(F32)
    d = []
    for b in range(bb):
        e = e_ref[b, :, cols]
        s = e + pltpu.roll(e, 1, axis=0)
        for k in range(1, g + 1):
            s = s + pltpu.roll(s, 2**k, axis=0)
        d.append(s[HIST_PAD:, :] / cnt - e[HIST_PAD:, :])
    d = jnp.concatenate(d, axis=0).astype(BF16)
    y = jnp.dot(d, wp_ref[g].astype(BF16), preferred_element_type=F32)
    return (y * sc_ref[:, cols]).astype(BF16)


INPROJ_COL_CHUNK = 1024


def _inproj_rows_kernel(x_ref, nw_ref, w_ref, wg_ref, prev_ref, wp_ref, sc_ref, *rest,
                        bb, tt, pos0, tiles_per_seq, conv_blocks):
    nc = len(conv_blocks)
    conv_in, (yp_ref, qkvo_ref, g_ref, st_ref) = rest[:nc], rest[nc:nc + 4]
    conv_out, e_ref = rest[nc + 4:2 * nc + 4], rest[-1]
    nt = (((1,), (1,)), ((), ()))
    ti = pl.program_id(0) % tiles_per_seq

    @pl.when(ti == 0)
    def _():
        e_ref[:, 0:1, :] = jnp.zeros((bb, 1, POOL_W), F32)
        e_ref[:, 1:HIST_PAD, :] = prev_ref[...]

    xn = _rms(x_ref[...], nw_ref[...]).astype(BF16)
    g_ref[...] = lax.dot_general(xn, wg_ref[...], nt, preferred_element_type=F32)
    u = lax.dot_general(xn, w_ref[0:POOL_W, :], nt, preferred_element_type=F32)
    e_ref[:, HIST_PAD:, :] = u.reshape(bb, tt, POOL_W)
    pos = pos0 + ti * tt + lax.broadcasted_iota(jnp.int32, (tt, 1), 0)

    for g, w in enumerate(POOL_WINDOWS):
        c = POOL_W + g * INPROJ_COL_CHUNK
        p = lax.dot_general(xn, w_ref[c:c + INPROJ_COL_CHUNK, :], nt, preferred_element_type=F32)
        qkvo_ref[:, c - POOL_W:c - POOL_W + INPROJ_COL_CHUNK] = p.astype(BF16)

        cols = slice(g * POOL_GW, (g + 1) * POOL_GW)
        yp_ref[:, cols] = _pool_group(e_ref, wp_ref, sc_ref, g, w, pos, bb)

        for src, dst in list(zip(conv_in, conv_out))[g::N_POOL_GROUPS]:
            dst[...] = src[...].astype(BF16)

    e_ref[:, 0:HIST_PAD, :] = e_ref[:, tt:tt + HIST_PAD, :]

    @pl.when(ti == tiles_per_seq - 1)
    def _():
        st_ref[...] = e_ref[:, 1:HIST_PAD, :]


def _inproj_rows(x, norm_w, w_t, w_gate_t, prev, w_pool, scale, *, bb, tt, pos0, tiles_per_seq, convert=()):
    assert MAIN_W == POOL_W + N_POOL_GROUPS * INPROJ_COL_CHUNK
    tm = bb * tt
    m = x.shape[0]
    n_seq = m // (tt * tiles_per_seq)
    row = lambda w: pl.BlockSpec((tm, w), lambda i: (i, 0))
    whole = lambda a: pl.BlockSpec(a.shape, lambda i: (0,) * a.ndim, pipeline_mode=pl.Buffered(1))
    seq_blk = lambda i: (i // tiles_per_seq, 0, 0)
    prev_spec = whole(prev) if prev.shape[0] == 1 else pl.BlockSpec((bb, POOL_HIST, POOL_W), seq_blk)
    assert all(n == m // tm for _, n in convert), "the kernel converts one block of each weight at every step"
    conv_specs, conv_shapes = _convert_specs(convert, m // tm, lambda i: i)
    return pl.pallas_call(
        functools.partial(_inproj_rows_kernel, bb=bb, tt=tt, pos0=pos0, tiles_per_seq=tiles_per_seq,
                          conv_blocks=tuple(n for _, n in convert)),
        grid=(m // tm,),
        in_specs=[row(D_MODEL), whole(norm_w), whole(w_t), whole(w_gate_t), prev_spec, whole(w_pool),
                  whole(scale)] + conv_specs,
        out_specs=[row(POOL_W), row(QKVO_W), row(LANES),
                   pl.BlockSpec((bb, POOL_HIST, POOL_W), seq_blk)] + conv_specs,
        out_shape=[
            jax.ShapeDtypeStruct((m, POOL_W), BF16),
            jax.ShapeDtypeStruct((m, QKVO_W), BF16),
            jax.ShapeDtypeStruct((m, LANES), F32),
            jax.ShapeDtypeStruct((n_seq, POOL_HIST, POOL_W), F32),
        ] + conv_shapes,
        scratch_shapes=[pltpu.VMEM((bb, HIST_PAD + tt, POOL_W), F32)],
        compiler_params=pltpu.CompilerParams(
            dimension_semantics=("arbitrary",), vmem_limit_bytes=VMEM_LIMIT_RESIDENT_WEIGHT),
        name="inproj_rows",
    )(x, norm_w, w_t, w_gate_t, prev, w_pool, scale, *[w for w, _ in convert])


def _convert_specs(weights, n_steps, step_of):
    specs, shapes = [], []
    for w, n_blocks in weights:
        assert n_blocks <= n_steps
        rows = w.shape[0] // n_blocks
        specs.append(pl.BlockSpec(
            (rows, w.shape[1]), lambda *idx, n_blocks=n_blocks: (jnp.minimum(step_of(*idx), n_blocks - 1), 0)))
        shapes.append(jax.ShapeDtypeStruct(w.shape, BF16))
    return specs, shapes


def _mlstm_chunk(q, k, v, gi, gf, c, n, m, *, chunk):
    ti = lax.broadcasted_iota(jnp.int32, (chunk, chunk), 0)
    si = lax.broadcasted_iota(jnp.int32, (chunk, chunk), 1)
    tril = si <= ti
    eye = si == ti

    def to_row(col):
        return jnp.sum(jnp.where(eye, col, 0.0), axis=0, keepdims=True)

    lf = jax.nn.log_sigmoid(gf)
    b = jnp.sum(jnp.where(tril, to_row(lf), 0.0), axis=1, keepdims=True)
    a = gi - b
    a_row = to_row(a)
    cummax_a = jnp.max(jnp.where(tril, a_row, -jnp.inf), axis=1, keepdims=True)
    m_t = jnp.maximum(m + b, cummax_a + b)
    inter = jnp.exp(m + b - m_t)
    dmat = jnp.exp(jnp.where(tril, a_row + (b - m_t), -jnp.inf))

    k = k * (HEAD_DIM ** -0.5)
    s = lax.dot_general(q, k, (((1,), (1,)), ((), ())), preferred_element_type=F32) * dmat
    num = jnp.dot(s.astype(BF16), v, preferred_element_type=F32)
    num = num + inter * jnp.dot(q, c.astype(BF16), preferred_element_type=F32)
    qn = jnp.sum(s, axis=1, keepdims=True) + inter * jnp.sum(q.astype(F32) * n, axis=1, keepdims=True)
    h = num / jnp.maximum(jnp.abs(qn), jnp.exp(-m_t))

    b_last = b[chunk - 1:chunk, :]
    m_new = m_t[chunk - 1:chunk, :]
    decay = jnp.exp(m + b_last - m_new)
    w = jnp.exp(a + (b_last - m_new))
    wv = (w * v.astype(F32)).astype(BF16)
    c_new = decay * c + lax.dot_general(k, wv, (((0,), (0,)), ((), ())), preferred_element_type=F32)
    n_new = decay * n + jnp.sum(w * k.astype(F32), axis=0, keepdims=True)

    return h, c_new, n_new, m_new


def _gate_cols(g, bias_refs, head):
    bi_ref, bf_ref = bias_refs
    lane = lax.broadcasted_iota(jnp.int32, g.shape, 1)
    gi = jnp.sum(jnp.where(lane == head, g, 0.0), axis=1, keepdims=True) + bi_ref[head]
    gf = jnp.sum(jnp.where(lane == head + N_HEADS, g, 0.0), axis=1, keepdims=True) + bf_ref[head]
    return gi, gf


N_STEP_IN = 7


N_OUTPROJ_IN = 6


def _gated_head(h, o, gnorm):
    return (jax.nn.sigmoid(o.astype(F32)) * _rms(h, gnorm)).astype(BF16)


N_STEP_OUT = 4


def _mlstm_seq_kernel(bi_ref, bf_ref, q_ref, k_ref, v_ref, g_ref, *rest,
                      chunk, n_chunks, n_steps, zero_init, step, outproj):
    bias_refs = (bi_ref, bf_ref)
    n_init = 0 if zero_init else 3
    init_refs, rest = rest[:n_init], rest[n_init:]
    no_in = N_OUTPROJ_IN if outproj else 0
    ns_in = N_STEP_IN if step else 0
    op_in, step_in = rest[:no_in], rest[no_in:no_in + ns_in]
    outs = rest[no_in + ns_in:]
    n_rows_out = 2 if outproj else 1
    rows_out, (c_ref, n_ref, m_ref) = outs[:n_rows_out], outs[n_rows_out:n_rows_out + 3]
    step_out = outs[n_rows_out + 3:n_rows_out + 3 + N_STEP_OUT] if step else ()
    s = pl.program_id(0)
    active = s < n_steps
    cur = jnp.minimum(s, n_steps - 1)

    @pl.when((cur % n_chunks == 0) & active)
    def _():
        for dst, src in zip((c_ref, n_ref, m_ref), init_refs or (None,) * 3):
            dst[...] = jnp.zeros(dst.shape, F32) if src is None else src[...]

    if outproj:
        ym_ref = outs[-1]

        @pl.when(s == 0)
        def _():
            ym_ref[...] = jnp.zeros(ym_ref.shape, BF16)

    stages = iter(())
    if step:
        stages = _mlstm_step_stages(cur % N_HEADS, bias_refs, *step_in, *step_out, bb=step[0], seq=step[1])
    next(stages, None)

    if outproj:
        x_ref, yp_ref, o_ref, gn_ref, wo_ref, nw_ref = op_in
        x2_ref, xn2_ref = rows_out
        slot = s % 2
        ssq = jnp.zeros((chunk, 1), F32)
        out_cols = D_MODEL // N_HEADS

    g = g_ref[...]
    for head in range(N_HEADS):
        if outproj:
            oc = slice(head * out_cols, (head + 1) * out_cols)
            x2 = x_ref[:, oc] + jnp.dot(yp_ref[...], wo_ref[0:POOL_W, oc], preferred_element_type=F32)
            x2 = x2 + jnp.dot(ym_ref[1 - slot], wo_ref[POOL_W:, oc], preferred_element_type=F32)
            x2_ref[:, oc] = x2
            ssq = ssq + jnp.sum(x2 * x2, axis=1, keepdims=True)
        cols = slice(head * HEAD_DIM, (head + 1) * HEAD_DIM)
        gi, gf = _gate_cols(g, bias_refs, head)
        c_old, n_old, m_old = c_ref[0, head], n_ref[0, head], m_ref[0, head]
        h, c_new, n_new, m_new = _mlstm_chunk(
            q_ref[:, cols], k_ref[:, cols], v_ref[:, cols], gi, gf, c_old, n_old, m_old, chunk=chunk)
        if outproj:
            ym_ref[slot, :, cols] = _gated_head(h, o_ref[:, cols], gn_ref[:, cols])
            c_new, n_new, m_new = (jnp.where(active, new, old)
                                   for new, old in ((c_new, c_old), (n_new, n_old), (m_new, m_old)))
        else:
            rows_out[0][:, cols] = h
        c_ref[0, head] = c_new
        n_ref[0, head] = n_new
        m_ref[0, head] = m_new
        next(stages, None)
    for _ in stages:
        pass

    if outproj:
        scale = lax.rsqrt(ssq * (1.0 / D_MODEL) + EPS)
        xn2_ref[...] = (x2_ref[...] * scale * nw_ref[...]).astype(BF16)


def _mlstm_seq(qkvo, gates, b_i, b_f, init, *, batch, seq, chunk, step=None, outproj=None):
    nc = seq // chunk
    n_steps = batch * nc
    cur = lambda s: jnp.minimum(s, n_steps - 1)
    prev = lambda s: jnp.maximum(s - 1, 0)
    whole = lambda a: pl.BlockSpec(a.shape, lambda s: (0,) * a.ndim, pipeline_mode=pl.Buffered(1))
    if outproj:
        x, y_pool, gnorm, w_out, norm_w = outproj
        op_args = [x, y_pool, qkvo, gnorm, w_out, norm_w]
        late_rows = lambda w: pl.BlockSpec((chunk, w), lambda s: (prev(s), 0))
        op_in = [late_rows(D_MODEL), late_rows(POOL_W),
                 pl.BlockSpec((chunk, MLSTM_W), lambda s: (cur(s), 3)),
                 whole(gnorm), whole(w_out), whole(norm_w)]
        rows_out = [late_rows(D_MODEL), late_rows(D_MODEL)]
        rows_shapes = [jax.ShapeDtypeStruct((batch * seq, D_MODEL), F32),
                       jax.ShapeDtypeStruct((batch * seq, D_MODEL), BF16)]
        scratch = [pltpu.VMEM((2, chunk, MLSTM_W), BF16)]
    else:
        op_args, op_in, scratch = [], [], []
        rows_out = [pl.BlockSpec((chunk, MLSTM_W), lambda s: (cur(s), 0))]
        rows_shapes = [jax.ShapeDtypeStruct((batch * seq, MLSTM_W), F32)]
    step_args, step_in, step_out, step_shapes, step_cfg = [], [], [], [], None
    if step:
        (s_qkvo, s_gates, s_c0, s_n0, s_m0), s_batch, s_seq, s_bb = step
        step_in, step_out, step_shapes = _mlstm_step_specs(s_batch, s_seq, s_bb, n_steps, cur)
        step_args = [s_qkvo, s_qkvo, s_qkvo, s_gates, s_c0, s_n0, s_m0]
        step_cfg = (s_bb, s_seq)
    def col_spec(group):
        return pl.BlockSpec((chunk, MLSTM_W), lambda s: (cur(s), group))

    def st_spec(shape, in_b):
        nd = len(shape)
        return pl.BlockSpec((1, N_HEADS) + shape, lambda s: (in_b(s), 0) + (0,) * nd)

    state_shapes = ((HEAD_DIM, HEAD_DIM), (1, HEAD_DIM), (1, 1))
    init_in = [st_spec(shape, lambda s: 0) for shape in state_shapes] if init else []
    seq_of = lambda s: cur(s) // nc
    smem = pl.BlockSpec(memory_space=pltpu.SMEM)
    return pl.pallas_call(
        functools.partial(_mlstm_seq_kernel, chunk=chunk, n_chunks=nc, n_steps=n_steps, zero_init=init is None,
                          step=step_cfg, outproj=bool(outproj)),
        grid=(n_steps + (1 if outproj else 0),),
        in_specs=[
            smem, smem,
            col_spec(0), col_spec(1), col_spec(2),
            pl.BlockSpec((chunk, LANES), lambda s: (cur(s), 0)),
        ] + init_in + op_in + step_in,
        out_specs=rows_out + [st_spec(shape, seq_of) for shape in state_shapes] + step_out,
        out_shape=rows_shapes + [
            jax.ShapeDtypeStruct((batch, N_HEADS, HEAD_DIM, HEAD_DIM), F32),
            jax.ShapeDtypeStruct((batch, N_HEADS, 1, HEAD_DIM), F32),
            jax.ShapeDtypeStruct((batch, N_HEADS, 1, 1), F32),
        ] + step_shapes,
        scratch_shapes=scratch,
        compiler_params=pltpu.CompilerParams(
            dimension_semantics=("arbitrary",),
            vmem_limit_bytes=VMEM_LIMIT_RESIDENT_WEIGHT if outproj else VMEM_LIMIT),
        name="mlstm_seq",
    )(b_i, b_f, qkvo, qkvo, qkvo, gates, *(init or ()), *op_args, *step_args)


def _mlstm_step_stages(head, bias_refs, q_ref, k_ref, v_ref, g_ref, c0_ref, n0_ref, m0_ref,
                       y_ref, c_ref, n_ref, m_ref, *, bb, seq):
    rows = bb * seq
    ti = lax.broadcasted_iota(jnp.int32, (rows, rows), 0)
    si = lax.broadcasted_iota(jnp.int32, (rows, rows), 1)
    same = (ti // seq) == (si // seq)
    tril = same & (si <= ti)
    eye = si == ti

    def to_row(col):
        return jnp.sum(jnp.where(eye, col, 0.0), axis=0, keepdims=True)

    gi, gf = _gate_cols(g_ref[...], bias_refs, head)
    lane = lax.broadcasted_iota(jnp.int32, m0_ref.shape, 1)
    m0 = jnp.sum(jnp.where(lane == head, m0_ref[...], 0.0), axis=1, keepdims=True)

    lf_row = to_row(jax.nn.log_sigmoid(gf))
    b = jnp.sum(jnp.where(tril, lf_row, 0.0), axis=1, keepdims=True)
    b_last = jnp.sum(jnp.where(same, lf_row, 0.0), axis=1, keepdims=True)
    a = gi - b
    a_row = to_row(a)
    cummax_a = jnp.max(jnp.where(tril, a_row, -jnp.inf), axis=1, keepdims=True)
    seqmax_a = jnp.max(jnp.where(same, a_row, -jnp.inf), axis=1, keepdims=True)
    m_t = jnp.maximum(m0 + b, cummax_a + b)
    m_new = jnp.maximum(m0 + b_last, seqmax_a + b_last)
    inter = jnp.exp(m0 + b - m_t)
    dmat = jnp.exp(jnp.where(tril, a_row + (b - m_t), -jnp.inf))
    decay = jnp.exp(m0 + b_last - m_new)
    w = jnp.exp(a + (b_last - m_new))

    q = q_ref[...]
    k = k_ref[...] * (HEAD_DIM ** -0.5)
    v = v_ref[...]
    s = lax.dot_general(q, k, (((1,), (1,)), ((), ())), preferred_element_type=F32) * dmat
    sv = jnp.dot(s.astype(BF16), v, preferred_element_type=F32)

    qf = q.astype(F32)
    kf = k.astype(F32)
    wk = w * kf
    wv = w * v.astype(F32)
    qc, qdn = [], []
    yield
    for i in range(bb):
        r = slice(i * seq, (i + 1) * seq)
        c_i = c0_ref[i, 0]
        n_i = n0_ref[i:i + 1, :]
        d_i = decay[i * seq:i * seq + 1, :]
        qc.append(jnp.dot(qf[r].astype(BF16), c_i.astype(BF16), preferred_element_type=F32))
        qdn.append(jnp.sum(qf[r] * n_i, axis=1, keepdims=True))
        upd = lax.dot_general(kf[r].astype(BF16), wv[r].astype(BF16), (((0,), (0,)), ((), ())),
                              preferred_element_type=F32)
        c_ref[i, 0] = d_i * c_i + upd
        n_ref[i:i + 1, :] = d_i * n_i + jnp.sum(wk[r], axis=0, keepdims=True)
        if (i + 1) % (bb // N_HEADS) == 0:
            yield

    num = sv + inter * jnp.concatenate(qc, axis=0)
    qn = jnp.sum(s, axis=1, keepdims=True) + inter * jnp.concatenate(qdn, axis=0)
    h = num / jnp.maximum(jnp.abs(qn), jnp.exp(-m_t))
    y_ref[...] = h
    m_ref[0, 0] = to_row(m_new)


def _mlstm_step_specs(batch, seq, bb, n_steps, step_of):
    assert (batch // bb) * N_HEADS == n_steps
    rows = bb * seq
    blk = lambda *idx: step_of(*idx) // N_HEADS
    head = lambda *idx: step_of(*idx) % N_HEADS

    def col_spec(off):
        return pl.BlockSpec((rows, HEAD_DIM), lambda *idx: (blk(*idx), off + head(*idx)))

    c_spec = pl.BlockSpec((bb, 1, HEAD_DIM, HEAD_DIM), lambda *idx: (blk(*idx), head(*idx), 0, 0))
    n_spec = pl.BlockSpec((bb, HEAD_DIM), lambda *idx: (blk(*idx), head(*idx)))
    in_specs = [
        col_spec(0), col_spec(N_HEADS), col_spec(2 * N_HEADS),
        pl.BlockSpec((rows, LANES), lambda *idx: (blk(*idx), 0)),
        c_spec, n_spec,
        pl.BlockSpec((rows, N_HEADS), lambda *idx: (blk(*idx), 0)),
    ]
    out_specs = [
        pl.BlockSpec((rows, HEAD_DIM), lambda *idx: (blk(*idx), head(*idx))),
        c_spec, n_spec,
        pl.BlockSpec((1, 1, 1, rows), lambda *idx: (blk(*idx), head(*idx), 0, 0)),
    ]
    out_shapes = [
        jax.ShapeDtypeStruct((batch * seq, MLSTM_W), F32),
        jax.ShapeDtypeStruct((batch, N_HEADS, HEAD_DIM, HEAD_DIM), F32),
        jax.ShapeDtypeStruct((batch, MLSTM_W), F32),
        jax.ShapeDtypeStruct((batch // bb, N_HEADS, 1, rows), F32),
    ]
    return in_specs, out_specs, out_shapes


def _mixer_out_kernel(x_ref, u_ref, prev_ref, wp_ref, sc_ref, h_ref, o_ref, gn_ref, wo_ref, nw_ref,
                      x2_ref, xn2_ref, st_ref, e_ref, *, bb, tt, pos0):
    e_ref[:, 0:1, :] = jnp.zeros((bb, 1, POOL_W), F32)
    for t in range(POOL_HIST):
        e_ref[:, 1 + t, :] = prev_ref[t]
    e_ref[:, HIST_PAD:, :] = u_ref[...]
    st_ref[...] = e_ref[:, tt + 1:tt + HIST_PAD, :]
    pos = pos0 + lax.broadcasted_iota(jnp.int32, (tt, 1), 0)
    ym = [_gated_head(h_ref[:, c], o_ref[:, c], gn_ref[:, c])
          for c in (slice(hd * HEAD_DIM, (hd + 1) * HEAD_DIM) for hd in range(N_HEADS))]
    x2 = x_ref[...] + jnp.dot(jnp.concatenate(ym, axis=1), wo_ref[POOL_W:, :], preferred_element_type=F32)
    yp = [_pool_group(e_ref, wp_ref, sc_ref, g, w, pos, bb) for g, w in enumerate(POOL_WINDOWS)]
    x2 = x2 + jnp.dot(jnp.concatenate(yp, axis=1), wo_ref[0:POOL_W, :], preferred_element_type=F32)
    x2_ref[...] = x2
    xn2_ref[...] = _rms(x2, nw_ref[...]).astype(BF16)


def _mixer_out(x, u, prev, w_pool, scale, h_ml, qkvo, gnorm, w_out, norm_w, *, bb, pos0):
    b, tt, _ = u.shape
    tm = bb * tt
    row = lambda w: pl.BlockSpec((tm, w), lambda i: (i, 0))
    whole = lambda a: pl.BlockSpec(a.shape, lambda i: (0,) * a.ndim, pipeline_mode=pl.Buffered(1))
    seq_blk = lambda w: pl.BlockSpec((bb, w, POOL_W), lambda i: (i, 0, 0))
    return pl.pallas_call(
        functools.partial(_mixer_out_kernel, bb=bb, tt=tt, pos0=pos0),
        grid=(b // bb,),
        in_specs=[
            row(D_MODEL), seq_blk(tt), pl.BlockSpec((POOL_HIST, bb, POOL_W), lambda i: (0, i, 0)),
            whole(w_pool), whole(scale),
            row(MLSTM_W), pl.BlockSpec((tm, MLSTM_W), lambda i: (i, 3)), whole(gnorm), whole(w_out), whole(norm_w),
        ],
        out_specs=[row(D_MODEL), row(D_MODEL), seq_blk(POOL_HIST)],
        out_shape=[
            jax.ShapeDtypeStruct((b * tt, D_MODEL), F32),
            jax.ShapeDtypeStruct((b * tt, D_MODEL), BF16),
            jax.ShapeDtypeStruct((b, POOL_HIST, POOL_W), F32),
        ],
        scratch_shapes=[pltpu.VMEM((bb, HIST_PAD + tt, POOL_W), F32)],
        compiler_params=_params(("arbitrary",)),
        name="mixer_out",
    )(x, u, prev, w_pool, scale, h_ml, qkvo, gnorm, w_out, norm_w)


def _ffn_kernel(xn_ref, x2_ref, wg_ref, wu_ref, wd_ref, nw_ref, y_ref, acc_ref):
    f = pl.program_id(1)
    nf = pl.num_programs(1)

    @pl.when(f == 0)
    def _():
        acc_ref[...] = x2_ref[...]

    xn = xn_ref[...]
    g = jnp.dot(xn, wg_ref[...], preferred_element_type=F32)
    u = jnp.dot(xn, wu_ref[...], preferred_element_type=F32)
    h = (jax.nn.silu(g) * u).astype(BF16)
    acc_ref[...] += jnp.dot(h, wd_ref[...], preferred_element_type=F32)

    @pl.when(f == nf - 1)
    def _():
        y_ref[...] = _rms(acc_ref[...], nw_ref[...])


def _ffn(xn2, x2, w_gate, w_up, w_down, norm_w, *, tm, tf):
    m = xn2.shape[0]
    return pl.pallas_call(
        _ffn_kernel,
        grid=(m // tm, D_FF // tf),
        in_specs=[
            pl.BlockSpec((tm, D_MODEL), lambda i, f: (i, 0)),
            pl.BlockSpec((tm, D_MODEL), lambda i, f: (i, 0)),
            pl.BlockSpec((D_MODEL, tf), lambda i, f: (0, f)),
            pl.BlockSpec((D_MODEL, tf), lambda i, f: (0, f)),
            pl.BlockSpec((tf, D_MODEL), lambda i, f: (f, 0)),
            pl.BlockSpec((1, D_MODEL), lambda i, f: (0, 0)),
        ],
        out_specs=pl.BlockSpec((tm, D_MODEL), lambda i, f: (i, 0)),
        out_shape=jax.ShapeDtypeStruct((m, D_MODEL), F32),
        scratch_shapes=[pltpu.VMEM((tm, D_MODEL), F32)],
        compiler_params=_params(("arbitrary", "arbitrary")),
        name="ffn",
    )(xn2, x2, w_gate, w_up, w_down, norm_w)


def kernel(x_prompt, x_sample, state_pool, state_mlstm_C, state_mlstm_n, state_mlstm_m, meta_tokens, norm_mix_w, w_in, b_igate, b_fgate, w_pool, pool_scale, mlstm_norm_w, w_out, norm_ffn_w, w_gate, w_up, w_down, norm_final_w):
    bp, tp, _ = x_prompt.shape
    bs, ts, _ = x_sample.shape

    w_in_t = w_in[0].T
    w_g8 = jnp.pad(w_in_t[MAIN_W:], ((0, LANES - 2 * N_HEADS), (0, 0))).astype(BF16)
    wp = w_pool[0]
    nmix = norm_mix_w[0].reshape(1, D_MODEL)
    nffn = norm_ffn_w[0].reshape(1, D_MODEL)
    nfin = norm_final_w.reshape(1, D_MODEL)
    scale = pool_scale[0].reshape(1, POOL_W)
    gnorm = mlstm_norm_w[0].reshape(1, MLSTM_W)
    b_i, b_f = b_igate[0], b_fgate[0]

    xp = x_prompt.reshape(bp * tp, D_MODEL)
    xs = x_sample.reshape(bs * ts, D_MODEL)

    u_s, qkvo_s, g_s, u_m, qkvo_m, g_m, w_main = _inproj([xs, meta_tokens], nmix, w_in_t, w_g8, tn=META_TN)
    prev_p = u_m[1:N_META].reshape(1, POOL_HIST, POOL_W)
    n_prompt_tiles = bp * tp // PROMPT_TT
    convert = tuple((w[0], n_prompt_tiles) for w in (w_out, w_gate, w_up, w_down))
    yp_p, qkvo_p, g_p, pool_p, wo, wg, wu, wd = _inproj_rows(
        xp, nmix, w_main, w_g8, prev_p, wp, scale, bb=1, tt=PROMPT_TT, pos0=N_META, tiles_per_seq=tp // PROMPT_TT,
        convert=convert)

    _, *state_m = _mlstm_seq(qkvo_m, g_m, b_i, b_f, None, batch=1, seq=N_META, chunk=N_META)
    step_ops = (qkvo_s, g_s, state_mlstm_C[0], state_mlstm_n[0].reshape(bs, MLSTM_W),
                jnp.repeat(state_mlstm_m[0], ts, axis=0))
    x2_p, xn2_p, c_p, n_p, m_p, h_s, c_s, n_s, m_s = _mlstm_seq(
        qkvo_p, g_p, b_i, b_f, state_m, batch=bp, seq=tp, chunk=MLSTM_CHUNK,
        step=(step_ops, bs, ts, STEP_BB), outproj=(xp, yp_p, gnorm, wo, nffn))
    m_s = m_s.reshape(bs // STEP_BB, N_HEADS, STEP_BB, ts)[..., 0].transpose(0, 2, 1)

    prev_s = state_pool[0].transpose(1, 0, 2)
    x2_s, xn2_s, pool_s = _mixer_out(xs, u_s.reshape(bs, ts, POOL_W), prev_s, wp, scale, h_s, qkvo_s, gnorm,
                                     wo, nffn, bb=DECODE_BB, pos0=PAST_LEN)
    y_p = _ffn(xn2_p, x2_p, wg, wu, wd, nfin, tm=FFN_TM, tf=FFN_TF)
    y_s = _ffn(xn2_s, x2_s, wg, wu, wd, nfin, tm=FFN_TM, tf=FFN_TF)

    return (
        y_p.reshape(bp, tp, D_MODEL),
        y_s.reshape(bs, ts, D_MODEL),
        pool_p[None],
        c_p[None],
        n_p.reshape(1, bp, N_HEADS, HEAD_DIM),
        m_p.reshape(1, bp, N_HEADS),
        pool_s[None],
        c_s[None],
        n_s.reshape(1, bs, N_HEADS, HEAD_DIM),
        m_s.reshape(1, bs, N_HEADS),
    )
```

```python
import functools

import jax
import jax.numpy as jnp
from jax import lax
from jax.experimental import pallas as pl
from jax.experimental.pallas import tpu as pltpu

D_MODEL = 2048
N_META = 16
POOL_W = 1024
MLSTM_W = 1024
POOL_WINDOWS = (2, 4, 8, 16)
N_POOL_GROUPS = 4
POOL_GW = 256
POOL_HIST = 15
N_HEADS = 4
HEAD_DIM = 256
D_FF = 5632
QKVO_W = 4 * MLSTM_W
MAIN_W = POOL_W + QKVO_W
PAST_LEN = 16384
EPS = 1e-6

LANES = 128
HIST_PAD = 16
VMEM_LIMIT = 56 * 1024 * 1024
VMEM_LIMIT_RESIDENT_WEIGHT = 60 * 1024 * 1024

META_TN = 512
PROMPT_TT = 256
DECODE_BB = 32
MLSTM_CHUNK = 256
STEP_BB = 16
FFN_TM = 512
FFN_TF = 512

BF16 = jnp.bfloat16
F32 = jnp.float32


def _params(sem):
    return pltpu.CompilerParams(dimension_semantics=sem, vmem_limit_bytes=VMEM_LIMIT)


def _rms(x, w):
    return x * lax.rsqrt(jnp.mean(x * x, axis=-1, keepdims=True) + EPS) * w


W_RING = 3


def _inproj_kernel(*refs, n_groups, n_u_tiles, n_tiles, tn):
    x_refs, (nw_ref, w_hbm, wg_ref) = refs[:n_groups], refs[n_groups:n_groups + 3]
    outs = refs[n_groups + 3:]
    wcopy_ref, xn_ref, wbuf_ref, sem = outs[3 * n_groups:3 * n_groups + 4]

    def tile_copy(tile, slot):
        return pltpu.make_async_copy(w_hbm.at[pl.ds(tile * tn, tn), :], wbuf_ref.at[slot], sem.at[slot])

    rows, start = [], 0
    for x_ref in x_refs:
        rows.append(slice(start, start + x_ref.shape[0]))
        start += x_ref.shape[0]
    j = pl.program_id(0)
    nt = (((1,), (1,)), ((), ()))

    @pl.when(j == 0)
    def _():
        for t in range(min(W_RING, n_tiles)):
            tile_copy(t, t).start()
        for x_ref, r in zip(x_refs, rows):
            xn_ref[r, :] = _rms(x_ref[...], nw_ref[...]).astype(BF16)
        g = lax.dot_general(xn_ref[...], wg_ref[...], nt, preferred_element_type=F32)
        for g_ref, r in zip(outs[2:3 * n_groups:3], rows):
            g_ref[...] = g[r]

    slot = j % W_RING
    tile_copy(j, slot).wait()
    w = wbuf_ref[slot].astype(BF16)
    wcopy_ref[...] = w
    p = lax.dot_general(xn_ref[...], w, nt, preferred_element_type=F32)
    ps = [p[r] for r in rows]

    @pl.when(j + W_RING < n_tiles)
    def _():
        tile_copy(j + W_RING, slot).start()

    @pl.when(j < n_u_tiles)
    def _():
        for p, u_ref in zip(ps, outs[0:3 * n_groups:3]):
            u_ref[...] = p

    @pl.when(j >= n_u_tiles)
    def _():
        for p, qkvo_ref in zip(ps, outs[1:3 * n_groups:3]):
            qkvo_ref[...] = p.astype(BF16)


def _inproj(xs, norm_w, w_t, w_gate_t, *, tn):
    n_u = POOL_W // tn
    whole = lambda a: pl.BlockSpec(a.shape, lambda j: (0,) * a.ndim, pipeline_mode=pl.Buffered(1))
    w_spec = pl.BlockSpec((tn, D_MODEL), lambda j: (j, 0))
    out_specs, out_shapes = [], []
    for x in xs:
        tm = x.shape[0]
        out_specs += [pl.BlockSpec((tm, tn), lambda j: (0, jnp.minimum(j, n_u - 1))),
                      pl.BlockSpec((tm, tn), lambda j: (0, jnp.maximum(j - n_u, 0))),
                      pl.BlockSpec((tm, LANES), lambda j: (0, 0))]
        out_shapes += [jax.ShapeDtypeStruct((tm, POOL_W), F32),
                       jax.ShapeDtypeStruct((tm, QKVO_W), BF16),
                       jax.ShapeDtypeStruct((tm, LANES), F32)]
    n_tiles = MAIN_W // tn
    return pl.pallas_call(
        functools.partial(_inproj_kernel, n_groups=len(xs), n_u_tiles=n_u, n_tiles=n_tiles, tn=tn),
        grid=(n_tiles,),
        in_specs=[whole(x) for x in xs] + [whole(norm_w), pl.BlockSpec(memory_space=pl.ANY), whole(w_gate_t)],
        out_specs=out_specs + [w_spec],
        out_shape=out_shapes + [jax.ShapeDtypeStruct((MAIN_W, D_MODEL), BF16)],
        scratch_shapes=[pltpu.VMEM((sum(x.shape[0] for x in xs), D_MODEL), BF16),
                        pltpu.VMEM((W_RING, tn, D_MODEL), F32),
                        pltpu.SemaphoreType.DMA((W_RING,))],
        compiler_params=_params(("arbitrary",)),
        name="inproj",
    )(*xs, norm_w, w_t, w_gate_t)


def _pool_group(e_ref, wp_ref, sc_ref, g, w, pos, bb):
    cols = slice(g * POOL_GW, (g + 1) * POOL_GW)
    cnt = jnp.minimum(w, pos + 1).astype(F32)
    d = []
    for b in range(bb):
        e = e_ref[b, :, cols]
        s = e + pltpu.roll(e, 1, axis=0)
        for k in range(1, g + 1):
            s = s + pltpu.roll(s, 2**k, axis=0)
        d.append(s[HIST_PAD:, :] / cnt - e[HIST_PAD:, :])
    d = jnp.concatenate(d, axis=0).astype(BF16)
    y = jnp.dot(d, wp_ref[g].astype(BF16), preferred_element_type=F32)
    return (y * sc_ref[:, cols]).astype(BF16)


INPROJ_COL_CHUNK = 1024


def _inproj_rows_kernel(x_ref, nw_ref, w_ref, wg_ref, prev_ref, wp_ref, sc_ref, *rest,
                        bb, tt, pos0, tiles_per_seq, conv_blocks):
    nc = len(conv_blocks)
    conv_in, (yp_ref, qkvo_ref, g_ref, st_ref) = rest[:nc], rest[nc:nc + 4]
    conv_out, e_ref = rest[nc + 4:2 * nc + 4], rest[-1]
    nt = (((1,), (1,)), ((), ()))
    ti = pl.program_id(0) % tiles_per_seq

    @pl.when(ti == 0)
    def _():
        e_ref[:, 0:1, :] = jnp.zeros((bb, 1, POOL_W), F32)
        e_ref[:, 1:HIST_PAD, :] = prev_ref[...]

    xn = _rms(x_ref[...], nw_ref[...]).astype(BF16)
    g_ref[...] = lax.dot_general(xn, wg_ref[...], nt, preferred_element_type=F32)
    u = lax.dot_general(xn, w_ref[0:POOL_W, :], nt, preferred_element_type=F32)
    e_ref[:, HIST_PAD:, :] = u.reshape(bb, tt, POOL_W)
    pos = pos0 + ti * tt + lax.broadcasted_iota(jnp.int32, (tt, 1), 0)

    for g, w in enumerate(POOL_WINDOWS):
        c = POOL_W + g * INPROJ_COL_CHUNK
        p = lax.dot_general(xn, w_ref[c:c + INPROJ_COL_CHUNK, :], nt, preferred_element_type=F32)
        qkvo_ref[:, c - POOL_W:c - POOL_W + INPROJ_COL_CHUNK] = p.astype(BF16)

        cols = slice(g * POOL_GW, (g + 1) * POOL_GW)
        yp_ref[:, cols] = _pool_group(e_ref, wp_ref, sc_ref, g, w, pos, bb)

        for src, dst in list(zip(conv_in, conv_out))[g::N_POOL_GROUPS]:
            dst[...] = src[...].astype(BF16)

    e_ref[:, 0:HIST_PAD, :] = e_ref[:, tt:tt + HIST_PAD, :]

    @pl.when(ti == tiles_per_seq - 1)
    def _():
        st_ref[...] = e_ref[:, 1:HIST_PAD, :]


def _inproj_rows(x, norm_w, w_t, w_gate_t, prev, w_pool, scale, *, bb, tt, pos0, tiles_per_seq, convert=()):
    assert MAIN_W == POOL_W + N_POOL_GROUPS * INPROJ_COL_CHUNK
    tm = bb * tt
    m = x.shape[0]
    n_seq = m // (tt * tiles_per_seq)
    row = lambda w: pl.BlockSpec((tm, w), lambda i: (i, 0))
    whole = lambda a: pl.BlockSpec(a.shape, lambda i: (0,) * a.ndim, pipeline_mode=pl.Buffered(1))
    seq_blk = lambda i: (i // tiles_per_seq, 0, 0)
    prev_spec = whole(prev) if prev.shape[0] == 1 else pl.BlockSpec((bb, POOL_HIST, POOL_W), seq_blk)
    assert all(n == m // tm for _, n in convert), "the kernel converts one block of each weight at every step"
    conv_specs, conv_shapes = _convert_specs(convert, m // tm, lambda i: i)
    return pl.pallas_call(
        functools.partial(_inproj_rows_kernel, bb=bb, tt=tt, pos0=pos0, tiles_per_seq=tiles_per_seq,
                          conv_blocks=tuple(n for _, n in convert)),
        grid=(m // tm,),
        in_specs=[row(D_MODEL), whole(norm_w), whole(w_t), whole(w_gate_t), prev_spec, whole(w_pool),
                  whole(scale)] + conv_specs,
        out_specs=[row(POOL_W), row(QKVO_W), row(LANES),
                   pl.BlockSpec((bb, POOL_HIST, POOL_W), seq_blk)] + conv_specs,
        out_shape=[
            jax.ShapeDtypeStruct((m, POOL_W), BF16),
            jax.ShapeDtypeStruct((m, QKVO_W), BF16),
            jax.ShapeDtypeStruct((m, LANES), F32),
            jax.ShapeDtypeStruct((n_seq, POOL_HIST, POOL_W), F32),
        ] + conv_shapes,
        scratch_shapes=[pltpu.VMEM((bb, HIST_PAD + tt, POOL_W), F32)],
        compiler_params=pltpu.CompilerParams(
            dimension_semantics=("arbitrary",), vmem_limit_bytes=VMEM_LIMIT_RESIDENT_WEIGHT),
        name="inproj_rows",
    )(x, norm_w, w_t, w_gate_t, prev, w_pool, scale, *[w for w, _ in convert])


def _convert_specs(weights, n_steps, step_of):
    specs, shapes = [], []
    for w, n_blocks in weights:
        assert n_blocks <= n_steps
        rows = w.shape[0] // n_blocks
        specs.append(pl.BlockSpec(
            (rows, w.shape[1]), lambda *idx, n_blocks=n_blocks: (jnp.minimum(step_of(*idx), n_blocks - 1), 0)))
        shapes.append(jax.ShapeDtypeStruct(w.shape, BF16))
    return specs, shapes


def _mlstm_chunk(q, k, v, gi, gf, c, n, m, *, chunk):
    ti = lax.broadcasted_iota(jnp.int32, (chunk, chunk), 0)
    si = lax.broadcasted_iota(jnp.int32, (chunk, chunk), 1)
    tril = si <= ti
    eye = si == ti

    def to_row(col):
        return jnp.sum(jnp.where(eye, col, 0.0), axis=0, keepdims=True)

    lf = jax.nn.log_sigmoid(gf)
    b = jnp.sum(jnp.where(tril, to_row(lf), 0.0), axis=1, keepdims=True)
    a = gi - b
    a_row = to_row(a)
    cummax_a = jnp.max(jnp.where(tril, a_row, -jnp.inf), axis=1, keepdims=True)
    m_t = jnp.maximum(m + b, cummax_a + b)
    inter = jnp.exp(m + b - m_t)
    dmat = jnp.exp(jnp.where(tril, a_row + (b - m_t), -jnp.inf))

    k = k * (HEAD_DIM ** -0.5)
    s = lax.dot_general(q, k, (((1,), (1,)), ((), ())), preferred_element_type=F32) * dmat
    num = jnp.dot(s.astype(BF16), v, preferred_element_type=F32)
    num = num + inter * jnp.dot(q, c.astype(BF16), preferred_element_type=F32)
    qn = jnp.sum(s, axis=1, keepdims=True) + inter * jnp.sum(q.astype(F32) * n, axis=1, keepdims=True)
    h = num / jnp.maximum(jnp.abs(qn), jnp.exp(-m_t))

    b_last = b[chunk - 1:chunk, :]
    m_new = m_t[chunk - 1:chunk, :]
    decay = jnp.exp(m + b_last - m_new)
    w = jnp.exp(a + (b_last - m_new))
    wv = (w * v.astype(F32)).astype(BF16)
    c_new = decay * c + lax.dot_general(k, wv, (((0,), (0,)), ((), ())), preferred_element_type=F32)
    n_new = decay * n + jnp.sum(w * k.astype(F32), axis=0, keepdims=True)

    return h, c_new, n_new, m_new


def _gate_cols(g, bias_refs, head):
    bi_ref, bf_ref = bias_refs
    lane = lax.broadcasted_iota(jnp.int32, g.shape, 1)
    gi = jnp.sum(jnp.where(lane == head, g, 0.0), axis=1, keepdims=True) + bi_ref[head]
    gf = jnp.sum(jnp.where(lane == head + N_HEADS, g, 0.0), axis=1, keepdims=True) + bf_ref[head]
    return gi, gf


N_STEP_IN = 7


N_OUTPROJ_IN = 6


def _gated_head(h, o, gnorm):
    return (jax.nn.sigmoid(o.astype(F32)) * _rms(h, gnorm)).astype(BF16)


N_STEP_OUT = 4


def _mlstm_seq_kernel(bi_ref, bf_ref, q_ref, k_ref, v_ref, g_ref, *rest,
                      chunk, n_chunks, n_steps, zero_init, step, outproj):
    bias_refs = (bi_ref, bf_ref)
    n_init = 0 if zero_init else 3
    init_refs, rest = rest[:n_init], rest[n_init:]
    no_in = N_OUTPROJ_IN if outproj else 0
    ns_in = N_STEP_IN if step else 0
    op_in, step_in = rest[:no_in], rest[no_in:no_in + ns_in]
    outs = rest[no_in + ns_in:]
    n_rows_out = 2 if outproj else 1
    rows_out, (c_ref, n_ref, m_ref) = outs[:n_rows_out], outs[n_rows_out:n_rows_out + 3]
    step_out = outs[n_rows_out + 3:n_rows_out + 3 + N_STEP_OUT] if step else ()
    s = pl.program_id(0)
    active = s < n_steps
    cur = jnp.minimum(s, n_steps - 1)

    @pl.when((cur % n_chunks == 0) & active)
    def _():
        for dst, src in zip((c_ref, n_ref, m_ref), init_refs or (None,) * 3):
            dst[...] = jnp.zeros(dst.shape, F32) if src is None else src[...]

    if outproj:
        ym_ref = outs[-1]

        @pl.when(s == 0)
        def _():
            ym_ref[...] = jnp.zeros(ym_ref.shape, BF16)

    stages = iter(())
    if step:
        stages = _mlstm_step_stages(cur % N_HEADS, bias_refs, *step_in, *step_out, bb=step[0], seq=step[1])
    next(stages, None)

    if outproj:
        x_ref, yp_ref, o_ref, gn_ref, wo_ref, nw_ref = op_in
        x2_ref, xn2_ref = rows_out
        slot = s % 2
        ssq = jnp.zeros((chunk, 1), F32)
        out_cols = D_MODEL // N_HEADS

    g = g_ref[...]
    for head in range(N_HEADS):
        if outproj:
            oc = slice(head * out_cols, (head + 1) * out_cols)
            x2 = x_ref[:, oc] + jnp.dot(yp_ref[...], wo_ref[0:POOL_W, oc], preferred_element_type=F32)
            x2 = x2 + jnp.dot(ym_ref[1 - slot], wo_ref[POOL_W:, oc], preferred_element_type=F32)
            x2_ref[:, oc] = x2
            ssq = ssq + jnp.sum(x2 * x2, axis=1, keepdims=True)
        cols = slice(head * HEAD_DIM, (head + 1) * HEAD_DIM)
        gi, gf = _gate_cols(g, bias_refs, head)
        c_old, n_old, m_old = c_ref[0, head], n_ref[0, head], m_ref[0, head]
        h, c_new, n_new, m_new = _mlstm_chunk(
            q_ref[:, cols], k_ref[:, cols], v_ref[:, cols], gi, gf, c_old, n_old, m_old, chunk=chunk)
        if outproj:
            ym_ref[slot, :, cols] = _gated_head(h, o_ref[:, cols], gn_ref[:, cols])
            c_new, n_new, m_new = (jnp.where(active, new, old)
                                   for new, old in ((c_new, c_old), (n_new, n_old), (m_new, m_old)))
        else:
            rows_out[0][:, cols] = h
        c_ref[0, head] = c_new
        n_ref[0, head] = n_new
        m_ref[0, head] = m_new
        next(stages, None)
    for _ in stages:
        pass

    if outproj:
        scale = lax.rsqrt(ssq * (1.0 / D_MODEL) + EPS)
        xn2_ref[...] = (x2_ref[...] * scale * nw_ref[...]).astype(BF16)


def _mlstm_seq(qkvo, gates, b_i, b_f, init, *, batch, seq, chunk, step=None, outproj=None):
    nc = seq // chunk
    n_steps = batch * nc
    cur = lambda s: jnp.minimum(s, n_steps - 1)
    prev = lambda s: jnp.maximum(s - 1, 0)
    whole = lambda a: pl.BlockSpec(a.shape, lambda s: (0,) * a.ndim, pipeline_mode=pl.Buffered(1))
    if outproj:
        x, y_pool, gnorm, w_out, norm_w = outproj
        op_args = [x, y_pool, qkvo, gnorm, w_out, norm_w]
        late_rows = lambda w: pl.BlockSpec((chunk, w), lambda s: (prev(s), 0))
        op_in = [late_rows(D_MODEL), late_rows(POOL_W),
                 pl.BlockSpec((chunk, MLSTM_W), lambda s: (cur(s), 3)),
                 whole(gnorm), whole(w_out), whole(norm_w)]
        rows_out = [late_rows(D_MODEL), late_rows(D_MODEL)]
        rows_shapes = [jax.ShapeDtypeStruct((batch * seq, D_MODEL), F32),
                       jax.ShapeDtypeStruct((batch * seq, D_MODEL), BF16)]
        scratch = [pltpu.VMEM((2, chunk, MLSTM_W), BF16)]
    else:
        op_args, op_in, scratch = [], [], []
        rows_out = [pl.BlockSpec((chunk, MLSTM_W), lambda s: (cur(s), 0))]
        rows_shapes = [jax.ShapeDtypeStruct((batch * seq, MLSTM_W), F32)]
    step_args, step_in, step_out, step_shapes, step_cfg = [], [], [], [], None
    if step:
        (s_qkvo, s_gates, s_c0, s_n0, s_m0), s_batch, s_seq, s_bb = step
        step_in, step_out, step_shapes = _mlstm_step_specs(s_batch, s_seq, s_bb, n_steps, cur)
        step_args = [s_qkvo, s_qkvo, s_qkvo, s_gates, s_c0, s_n0, s_m0]
        step_cfg = (s_bb, s_seq)
    def col_spec(group):
        return pl.BlockSpec((chunk, MLSTM_W), lambda s: (cur(s), group))

    def st_spec(shape, in_b):
        nd = len(shape)
        return pl.BlockSpec((1, N_HEADS) + shape, lambda s: (in_b(s), 0) + (0,) * nd)

    state_shapes = ((HEAD_DIM, HEAD_DIM), (1, HEAD_DIM), (1, 1))
    init_in = [st_spec(shape, lambda s: 0) for shape in state_shapes] if init else []
    seq_of = lambda s: cur(s) // nc
    smem = pl.BlockSpec(memory_space=pltpu.SMEM)
    return pl.pallas_call(
        functools.partial(_mlstm_seq_kernel, chunk=chunk, n_chunks=nc, n_steps=n_steps, zero_init=init is None,
                          step=step_cfg, outproj=bool(outproj)),
        grid=(n_steps + (1 if outproj else 0),),
        in_specs=[
            smem, smem,
            col_spec(0), col_spec(1), col_spec(2),
            pl.BlockSpec((chunk, LANES), lambda s: (cur(s), 0)),
        ] + init_in + op_in + step_in,
        out_specs=rows_out + [st_spec(shape, seq_of) for shape in state_shapes] + step_out,
        out_shape=rows_shapes + [
            jax.ShapeDtypeStruct((batch, N_HEADS, HEAD_DIM, HEAD_DIM), F32),
            jax.ShapeDtypeStruct((batch, N_HEADS, 1, HEAD_DIM), F32),
            jax.ShapeDtypeStruct((batch, N_HEADS, 1, 1), F32),
        ] + step_shapes,
        scratch_shapes=scratch,
        compiler_params=pltpu.CompilerParams(
            dimension_semantics=("arbitrary",),
            vmem_limit_bytes=VMEM_LIMIT_RESIDENT_WEIGHT if outproj else VMEM_LIMIT),
        name="mlstm_seq",
    )(b_i, b_f, qkvo, qkvo, qkvo, gates, *(init or ()), *op_args, *step_args)


def _mlstm_step_stages(head, bias_refs, q_ref, k_ref, v_ref, g_ref, c0_ref, n0_ref, m0_ref,
                       y_ref, c_ref, n_ref, m_ref, *, bb, seq):
    rows = bb * seq
    ti = lax.broadcasted_iota(jnp.int32, (rows, rows), 0)
    si = lax.broadcasted_iota(jnp.int32, (rows, rows), 1)
    same = (ti // seq) == (si // seq)
    tril = same & (si <= ti)
    eye = si == ti

    def to_row(col):
        return jnp.sum(jnp.where(eye, col, 0.0), axis=0, keepdims=True)

    gi, gf = _gate_cols(g_ref[...], bias_refs, head)
    lane = lax.broadcasted_iota(jnp.int32, m0_ref.shape, 1)
    m0 = jnp.sum(jnp.where(lane == head, m0_ref[...], 0.0), axis=1, keepdims=True)

    lf_row = to_row(jax.nn.log_sigmoid(gf))
    b = jnp.sum(jnp.where(tril, lf_row, 0.0), axis=1, keepdims=True)
    b_last = jnp.sum(jnp.where(same, lf_row, 0.0), axis=1, keepdims=True)
    a = gi - b
    a_row = to_row(a)
    cummax_a = jnp.max(jnp.where(tril, a_row, -jnp.inf), axis=1, keepdims=True)
    seqmax_a = jnp.max(jnp.where(same, a_row, -jnp.inf), axis=1, keepdims=True)
    m_t = jnp.maximum(m0 + b, cummax_a + b)
    m_new = jnp.maximum(m0 + b_last, seqmax_a + b_last)
    inter = jnp.exp(m0 + b - m_t)
    dmat = jnp.exp(jnp.where(tril, a_row + (b - m_t), -jnp.inf))
    decay = jnp.exp(m0 + b_last - m_new)
    w = jnp.exp(a + (b_last - m_new))

    q = q_ref[...]
    k = k_ref[...] * (HEAD_DIM ** -0.5)
    v = v_ref[...]
    s = lax.dot_general(q, k, (((1,), (1,)), ((), ())), preferred_element_type=F32) * dmat
    sv = jnp.dot(s.astype(BF16), v, preferred_element_type=F32)

    qf = q.astype(F32)
    kf = k.astype(F32)
    wk = w * kf
    wv = w * v.astype(F32)
    qc, qdn = [], []
    yield
    for i in range(bb):
        r = slice(i * seq, (i + 1) * seq)
        c_i = c0_ref[i, 0]
        n_i = n0_ref[i:i + 1, :]
        d_i = decay[i * seq:i * seq + 1, :]
        qc.append(jnp.dot(qf[r].astype(BF16), c_i.astype(BF16), preferred_element_type=F32))
        qdn.append(jnp.sum(qf[r] * n_i, axis=1, keepdims=True))
        upd = lax.dot_general(kf[r].astype(BF16), wv[r].astype(BF16), (((0,), (0,)), ((), ())),
                              preferred_element_type=F32)
        c_ref[i, 0] = d_i * c_i + upd
        n_ref[i:i + 1, :] = d_i * n_i + jnp.sum(wk[r], axis=0, keepdims=True)
        if (i + 1) % (bb // N_HEADS) == 0:
            yield

    num = sv + inter * jnp.concatenate(qc, axis=0)
    qn = jnp.sum(s, axis=1, keepdims=True) + inter * jnp.concatenate(qdn, axis=0)
    h = num / jnp.maximum(jnp.abs(qn), jnp.exp(-m_t))
    y_ref[...] = h
    m_ref[0, 0] = to_row(m_new)


def _mlstm_step_specs(batch, seq, bb, n_steps, step_of):
    assert (batch // bb) * N_HEADS == n_steps
    rows = bb * seq
    blk = lambda *idx: step_of(*idx) // N_HEADS
    head = lambda *idx: step_of(*idx) % N_HEADS

    def col_spec(off):
        return pl.BlockSpec((rows, HEAD_DIM), lambda *idx: (blk(*idx), off + head(*idx)))

    c_spec = pl.BlockSpec((bb, 1, HEAD_DIM, HEAD_DIM), lambda *idx: (blk(*idx), head(*idx), 0, 0))
    n_spec = pl.BlockSpec((bb, HEAD_DIM), lambda *idx: (blk(*idx), head(*idx)))
    in_specs = [
        col_spec(0), col_spec(N_HEADS), col_spec(2 * N_HEADS),
        pl.BlockSpec((rows, LANES), lambda *idx: (blk(*idx), 0)),
        c_spec, n_spec,
        pl.BlockSpec((rows, N_HEADS), lambda *idx: (blk(*idx), 0)),
    ]
    out_specs = [
        pl.BlockSpec((rows, HEAD_DIM), lambda *idx: (blk(*idx), head(*idx))),
        c_spec, n_spec,
        pl.BlockSpec((1, 1, 1, rows), lambda *idx: (blk(*idx), head(*idx), 0, 0)),
    ]
    out_shapes = [
        jax.ShapeDtypeStruct((batch * seq, MLSTM_W), F32),
        jax.ShapeDtypeStruct((batch, N_HEADS, HEAD_DIM, HEAD_DIM), F32),
        jax.ShapeDtypeStruct((batch, MLSTM_W), F32),
        jax.ShapeDtypeStruct((batch // bb, N_HEADS, 1, rows), F32),
    ]
    return in_specs, out_specs, out_shapes


def _mixer_out_kernel(x_ref, u_ref, prev_ref, wp_ref, sc_ref, h_ref, o_ref, gn_ref, wo_ref, nw_ref,
                      x2_ref, xn2_ref, st_ref, e_ref, *, bb, tt, pos0):
    e_ref[:, 0:1, :] = jnp.zeros((bb, 1, POOL_W), F32)
    for t in range(POOL_HIST):
        e_ref[:, 1 + t, :] = prev_ref[t]
    e_ref[:, HIST_PAD:, :] = u_ref[...]
    st_ref[...] = e_ref[:, tt + 1:tt + HIST_PAD, :]
    pos = pos0 + lax.broadcasted_iota(jnp.int32, (tt, 1), 0)
    ym = [_gated_head(h_ref[:, c], o_ref[:, c], gn_ref[:, c])
          for c in (slice(hd * HEAD_DIM, (hd + 1) * HEAD_DIM) for hd in range(N_HEADS))]
    x2 = x_ref[...] + jnp.dot(jnp.concatenate(ym, axis=1), wo_ref[POOL_W:, :], preferred_element_type=F32)
    yp = [_pool_group(e_ref, wp_ref, sc_ref, g, w, pos, bb) for g, w in enumerate(POOL_WINDOWS)]
    x2 = x2 + jnp.dot(jnp.concatenate(yp, axis=1), wo_ref[0:POOL_W, :], preferred_element_type=F32)
    x2_ref[...] = x2
    xn2_ref[...] = _rms(x2, nw_ref[...]).astype(BF16)


def _mixer_out(x, u, prev, w_pool, scale, h_ml, qkvo, gnorm, w_out, norm_w, *, bb, pos0):
    b, tt, _ = u.shape
    tm = bb * tt
    row = lambda w: pl.BlockSpec((tm, w), lambda i: (i, 0))
    whole = lambda a: pl.BlockSpec(a.shape, lambda i: (0,) * a.ndim, pipeline_mode=pl.Buffered(1))
    seq_blk = lambda w: pl.BlockSpec((bb, w, POOL_W), lambda i: (i, 0, 0))
    return pl.pallas_call(
        functools.partial(_mixer_out_kernel, bb=bb, tt=tt, pos0=pos0),
        grid=(b // bb,),
        in_specs=[
            row(D_MODEL), seq_blk(tt), pl.BlockSpec((POOL_HIST, bb, POOL_W), lambda i: (0, i, 0)),
            whole(w_pool), whole(scale),
            row(MLSTM_W), pl.BlockSpec((tm, MLSTM_W), lambda i: (i, 3)), whole(gnorm), whole(w_out), whole(norm_w),
        ],
        out_specs=[row(D_MODEL), row(D_MODEL), seq_blk(POOL_HIST)],
        out_shape=[
            jax.ShapeDtypeStruct((b * tt, D_MODEL), F32),
            jax.ShapeDtypeStruct((b * tt, D_MODEL), BF16),
            jax.ShapeDtypeStruct((b, POOL_HIST, POOL_W), F32),
        ],
        scratch_shapes=[pltpu.VMEM((bb, HIST_PAD + tt, POOL_W), F32)],
        compiler_params=_params(("arbitrary",)),
        name="mixer_out",
    )(x, u, prev, w_pool, scale, h_ml, qkvo, gnorm, w_out, norm_w)


def _ffn_kernel(xn_ref, x2_ref, wg_ref, wu_ref, wd_ref, nw_ref, y_ref, acc_ref):
    f = pl.program_id(1)
    nf = pl.num_programs(1)

    @pl.when(f == 0)
    def _():
        acc_ref[...] = x2_ref[...]

    xn = xn_ref[...]
    g = jnp.dot(xn, wg_ref[...], preferred_element_type=F32)
    u = jnp.dot(xn, wu_ref[...], preferred_element_type=F32)
    h = (jax.nn.silu(g) * u).astype(BF16)
    acc_ref[...] += jnp.dot(h, wd_ref[...], preferred_element_type=F32)

    @pl.when(f == nf - 1)
    def _():
        y_ref[...] = _rms(acc_ref[...], nw_ref[...])


def _ffn(xn2, x2, w_gate, w_up, w_down, norm_w, *, tm, tf):
    m = xn2.shape[0]
    return pl.pallas_call(
        _ffn_kernel,
        grid=(m // tm, D_FF // tf),
        in_specs=[
            pl.BlockSpec((tm, D_MODEL), lambda i, f: (i, 0)),
            pl.BlockSpec((tm, D_MODEL), lambda i, f: (i, 0)),
            pl.BlockSpec((D_MODEL, tf), lambda i, f: (0, f)),
            pl.BlockSpec((D_MODEL, tf), lambda i, f: (0, f)),
            pl.BlockSpec((tf, D_MODEL), lambda i, f: (f, 0)),
            pl.BlockSpec((1, D_MODEL), lambda i, f: (0, 0)),
        ],
        out_specs=pl.BlockSpec((tm, D_MODEL), lambda i, f: (i, 0)),
        out_shape=jax.ShapeDtypeStruct((m, D_MODEL), F32),
        scratch_shapes=[pltpu.VMEM((tm, D_MODEL), F32)],
        compiler_params=_params(("arbitrary", "arbitrary")),
        name="ffn",
    )(xn2, x2, w_gate, w_up, w_down, norm_w)


def kernel(x_prompt, x_sample, state_pool, state_mlstm_C, state_mlstm_n, state_mlstm_m, meta_tokens, norm_mix_w, w_in, b_igate, b_fgate, w_pool, pool_scale, mlstm_norm_w, w_out, norm_ffn_w, w_gate, w_up, w_down, norm_final_w):
    bp, tp, _ = x_prompt.shape
    bs, ts, _ = x_sample.shape

    w_in_t = w_in[0].T
    w_g8 = jnp.pad(w_in_t[MAIN_W:], ((0, LANES - 2 * N_HEADS), (0, 0))).astype(BF16)
    wp = w_pool[0]
    nmix = norm_mix_w[0].reshape(1, D_MODEL)
    nffn = norm_ffn_w[0].reshape(1, D_MODEL)
    nfin = norm_final_w.reshape(1, D_MODEL)
    scale = pool_scale[0].reshape(1, POOL_W)
    gnorm = mlstm_norm_w[0].reshape(1, MLSTM_W)
    b_i, b_f = b_igate[0], b_fgate[0]

    xp = x_prompt.reshape(bp * tp, D_MODEL)
    xs = x_sample.reshape(bs * ts, D_MODEL)

    u_s, qkvo_s, g_s, u_m, qkvo_m, g_m, w_main = _inproj([xs, meta_tokens], nmix, w_in_t, w_g8, tn=META_TN)
    prev_p = u_m[1:N_META].reshape(1, POOL_HIST, POOL_W)
    n_prompt_tiles = bp * tp // PROMPT_TT
    convert = tuple((w[0], n_prompt_tiles) for w in (w_out, w_gate, w_up, w_down))
    yp_p, qkvo_p, g_p, pool_p, wo, wg, wu, wd = _inproj_rows(
        xp, nmix, w_main, w_g8, prev_p, wp, scale, bb=1, tt=PROMPT_TT, pos0=N_META, tiles_per_seq=tp // PROMPT_TT,
        convert=convert)

    _, *state_m = _mlstm_seq(qkvo_m, g_m, b_i, b_f, None, batch=1, seq=N_META, chunk=N_META)
    step_ops = (qkvo_s, g_s, state_mlstm_C[0], state_mlstm_n[0].reshape(bs, MLSTM_W),
                jnp.repeat(state_mlstm_m[0], ts, axis=0))
    x2_p, xn2_p, c_p, n_p, m_p, h_s, c_s, n_s, m_s = _mlstm_seq(
        qkvo_p, g_p, b_i, b_f, state_m, batch=bp, seq=tp, chunk=MLSTM_CHUNK,
        step=(step_ops, bs, ts, STEP_BB), outproj=(xp, yp_p, gnorm, wo, nffn))
    m_s = m_s.reshape(bs // STEP_BB, N_HEADS, STEP_BB, ts)[..., 0].transpose(0, 2, 1)

    prev_s = state_pool[0].transpose(1, 0, 2)
    x2_s, xn2_s, pool_s = _mixer_out(xs, u_s.reshape(bs, ts, POOL_W), prev_s, wp, scale, h_s, qkvo_s, gnorm,
                                     wo, nffn, bb=DECODE_BB, pos0=PAST_LEN)
    y_p = _ffn(xn2_p, x2_p, wg, wu, wd, nfin, tm=FFN_TM, tf=FFN_TF)
    y_s = _ffn(xn2_s, x2_s, wg, wu, wd, nfin, tm=FFN_TM, tf=FFN_TF)

    return (
        y_p.reshape(bp, tp, D_MODEL),
        y_s.reshape(bs, ts, D_MODEL),
        pool_p[None],
        c_p[None],
        n_p.reshape(1, bp, N_HEADS, HEAD_DIM),
        m_p.reshape(1, bp, N_HEADS),
        pool_s[None],
        c_s[None],
        n_s.reshape(1, bs, N_HEADS, HEAD_DIM),
        m_s.reshape(1, bs, N_HEADS),
    )
```

```python
import functools

import jax
import jax.numpy as jnp
from jax import lax
from jax.experimental import pallas as pl
from jax.experimental.pallas import tpu as pltpu

D_MODEL = 2048
N_META = 16
POOL_W = 1024
MLSTM_W = 1024
POOL_WINDOWS = (2, 4, 8, 16)
N_POOL_GROUPS = 4
POOL_GW = 256
POOL_HIST = 15
N_HEADS = 4
HEAD_DIM = 256
D_FF = 5632
QKVO_W = 4 * MLSTM_W
MAIN_W = POOL_W + QKVO_W
PAST_LEN = 16384
EPS = 1e-6

LANES = 128
HIST_PAD = 16
VMEM_LIMIT = 56 * 1024 * 1024
VMEM_LIMIT_RESIDENT_WEIGHT = 60 * 1024 * 1024

META_TN = 512
PROMPT_TT = 256
DECODE_BB = 32
MLSTM_CHUNK = 256
STEP_BB = 16
FFN_TM = 512
FFN_TF = 512

BF16 = jnp.bfloat16
F32 = jnp.float32


def _params(sem):
    return pltpu.CompilerParams(dimension_semantics=sem, vmem_limit_bytes=VMEM_LIMIT)


def _rms(x, w):
    return x * lax.rsqrt(jnp.mean(x * x, axis=-1, keepdims=True) + EPS) * w


def _inproj_kernel(*refs, n_groups, n_u_tiles):
    x_refs, (nw_ref, w_ref, wg_ref) = refs[:n_groups], refs[n_groups:n_groups + 3]
    outs = refs[n_groups + 3:]
    wcopy_ref, xn_ref = outs[3 * n_groups], outs[3 * n_groups + 1]
    rows, start = [], 0
    for x_ref in x_refs:
        rows.append(slice(start, start + x_ref.shape[0]))
        start += x_ref.shape[0]
    j = pl.program_id(0)
    nt = (((1,), (1,)), ((), ()))

    @pl.when(j == 0)
    def _():
        for x_ref, r in zip(x_refs, rows):
            xn_ref[r, :] = _rms(x_ref[...], nw_ref[...]).astype(BF16)
        g = lax.dot_general(xn_ref[...], wg_ref[...], nt, preferred_element_type=F32)
        for g_ref, r in zip(outs[2:3 * n_groups:3], rows):
            g_ref[...] = g[r]

    w = w_ref[...].astype(BF16)
    wcopy_ref[...] = w
    p = lax.dot_general(xn_ref[...], w, nt, preferred_element_type=F32)
    ps = [p[r] for r in rows]

    @pl.when(j < n_u_tiles)
    def _():
        for p, u_ref in zip(ps, outs[0:3 * n_groups:3]):
            u_ref[...] = p

    @pl.when(j >= n_u_tiles)
    def _():
        for p, qkvo_ref in zip(ps, outs[1:3 * n_groups:3]):
            qkvo_ref[...] = p.astype(BF16)


def _inproj(xs, norm_w, w_t, w_gate_t, *, tn):
    n_u = POOL_W // tn
    whole = lambda a: pl.BlockSpec(a.shape, lambda j: (0,) * a.ndim, pipeline_mode=pl.Buffered(1))
    w_spec = pl.BlockSpec((tn, D_MODEL), lambda j: (j, 0))
    out_specs, out_shapes = [], []
    for x in xs:
        tm = x.shape[0]
        out_specs += [pl.BlockSpec((tm, tn), lambda j: (0, jnp.minimum(j, n_u - 1))),
                      pl.BlockSpec((tm, tn), lambda j: (0, jnp.maximum(j - n_u, 0))),
                      pl.BlockSpec((tm, LANES), lambda j: (0, 0))]
        out_shapes += [jax.ShapeDtypeStruct((tm, POOL_W), F32),
                       jax.ShapeDtypeStruct((tm, QKVO_W), BF16),
                       jax.ShapeDtypeStruct((tm, LANES), F32)]
    return pl.pallas_call(
        functools.partial(_inproj_kernel, n_groups=len(xs), n_u_tiles=n_u),
        grid=(MAIN_W // tn,),
        in_specs=[whole(x) for x in xs] + [whole(norm_w), w_spec, whole(w_gate_t)],
        out_specs=out_specs + [w_spec],
        out_shape=out_shapes + [jax.ShapeDtypeStruct((MAIN_W, D_MODEL), BF16)],
        scratch_shapes=[pltpu.VMEM((sum(x.shape[0] for x in xs), D_MODEL), BF16)],
        compiler_params=pltpu.CompilerParams(
            dimension_semantics=("arbitrary",), vmem_limit_bytes=VMEM_LIMIT,
            allow_input_fusion=[False] * (len(xs) + 2) + [True]),
        name="inproj",
    )(*xs, norm_w, w_t, w_gate_t)


def _pool_group(e_ref, wp_ref, sc_ref, g, w, pos, bb):
    cols = slice(g * POOL_GW, (g + 1) * POOL_GW)
    cnt = jnp.minimum(w, pos + 1).astype(F32)
    d = []
    for b in range(bb):
        e = e_ref[b, :, cols]
        s = e + pltpu.roll(e, 1, axis=0)
        for k in range(1, g + 1):
            s = s + pltpu.roll(s, 2**k, axis=0)
        d.append(s[HIST_PAD:, :] / cnt - e[HIST_PAD:, :])
    d = jnp.concatenate(d, axis=0).astype(BF16)
    y = jnp.dot(d, wp_ref[g].astype(BF16), preferred_element_type=F32)
    return (y * sc_ref[:, cols]).astype(BF16)


INPROJ_COL_CHUNK = 1024


def _inproj_rows_kernel(x_ref, nw_ref, w_ref, wg_ref, prev_ref, wp_ref, sc_ref, *rest,
                        bb, tt, pos0, tiles_per_seq, conv_blocks):
    nc = len(conv_blocks)
    conv_in, (yp_ref, qkvo_ref, g_ref, st_ref) = rest[:nc], rest[nc:nc + 4]
    conv_out, e_ref = rest[nc + 4:2 * nc + 4], rest[-1]
    nt = (((1,), (1,)), ((), ()))
    ti = pl.program_id(0) % tiles_per_seq

    @pl.when(ti == 0)
    def _():
        e_ref[:, 0:1, :] = jnp.zeros((bb, 1, POOL_W), F32)
        e_ref[:, 1:HIST_PAD, :] = prev_ref[...]

    xn = _rms(x_ref[...], nw_ref[...]).astype(BF16)
    g_ref[...] = lax.dot_general(xn, wg_ref[...], nt, preferred_element_type=F32)
    u = lax.dot_general(xn, w_ref[0:POOL_W, :], nt, preferred_element_type=F32)
    e_ref[:, HIST_PAD:, :] = u.reshape(bb, tt, POOL_W)
    pos = pos0 + ti * tt + lax.broadcasted_iota(jnp.int32, (tt, 1), 0)

    for g, w in enumerate(POOL_WINDOWS):
        c = POOL_W + g * INPROJ_COL_CHUNK
        p = lax.dot_general(xn, w_ref[c:c + INPROJ_COL_CHUNK, :], nt, preferred_element_type=F32)
        qkvo_ref[:, c - POOL_W:c - POOL_W + INPROJ_COL_CHUNK] = p.astype(BF16)

        cols = slice(g * POOL_GW, (g + 1) * POOL_GW)
        yp_ref[:, cols] = _pool_group(e_ref, wp_ref, sc_ref, g, w, pos, bb)

        for src, dst in list(zip(conv_in, conv_out))[g::N_POOL_GROUPS]:
            dst[...] = src[...].astype(BF16)

    e_ref[:, 0:HIST_PAD, :] = e_ref[:, tt:tt + HIST_PAD, :]

    @pl.when(ti == tiles_per_seq - 1)
    def _():
        st_ref[...] = e_ref[:, 1:HIST_PAD, :]


def _inproj_rows(x, norm_w, w_t, w_gate_t, prev, w_pool, scale, *, bb, tt, pos0, tiles_per_seq, convert=()):
    assert MAIN_W == POOL_W + N_POOL_GROUPS * INPROJ_COL_CHUNK
    tm = bb * tt
    m = x.shape[0]
    n_seq = m // (tt * tiles_per_seq)
    row = lambda w: pl.BlockSpec((tm, w), lambda i: (i, 0))
    whole = lambda a: pl.BlockSpec(a.shape, lambda i: (0,) * a.ndim, pipeline_mode=pl.Buffered(1))
    seq_blk = lambda i: (i // tiles_per_seq, 0, 0)
    prev_spec = whole(prev) if prev.shape[0] == 1 else pl.BlockSpec((bb, POOL_HIST, POOL_W), seq_blk)
    assert all(n == m // tm for _, n in convert), "the kernel converts one block of each weight at every step"
    conv_specs, conv_shapes = _convert_specs(convert, m // tm, lambda i: i)
    return pl.pallas_call(
        functools.partial(_inproj_rows_kernel, bb=bb, tt=tt, pos0=pos0, tiles_per_seq=tiles_per_seq,
                          conv_blocks=tuple(n for _, n in convert)),
        grid=(m // tm,),
        in_specs=[row(D_MODEL), whole(norm_w), whole(w_t), whole(w_gate_t), prev_spec, whole(w_pool),
                  whole(scale)] + conv_specs,
        out_specs=[row(POOL_W), row(QKVO_W), row(LANES),
                   pl.BlockSpec((bb, POOL_HIST, POOL_W), seq_blk)] + conv_specs,
        out_shape=[
            jax.ShapeDtypeStruct((m, POOL_W), BF16),
            jax.ShapeDtypeStruct((m, QKVO_W), BF16),
            jax.ShapeDtypeStruct((m, LANES), F32),
            jax.ShapeDtypeStruct((n_seq, POOL_HIST, POOL_W), F32),
        ] + conv_shapes,
        scratch_shapes=[pltpu.VMEM((bb, HIST_PAD + tt, POOL_W), F32)],
        compiler_params=pltpu.CompilerParams(
            dimension_semantics=("arbitrary",), vmem_limit_bytes=VMEM_LIMIT_RESIDENT_WEIGHT),
        name="inproj_rows",
    )(x, norm_w, w_t, w_gate_t, prev, w_pool, scale, *[w for w, _ in convert])


def _convert_specs(weights, n_steps, step_of):
    specs, shapes = [], []
    for w, n_blocks in weights:
        assert n_blocks <= n_steps
        rows = w.shape[0] // n_blocks
        specs.append(pl.BlockSpec(
            (rows, w.shape[1]), lambda *idx, n_blocks=n_blocks: (jnp.minimum(step_of(*idx), n_blocks - 1), 0)))
        shapes.append(jax.ShapeDtypeStruct(w.shape, BF16))
    return specs, shapes


def _mlstm_chunk(q, k, v, gi, gf, c, n, m, *, chunk):
    ti = lax.broadcasted_iota(jnp.int32, (chunk, chunk), 0)
    si = lax.broadcasted_iota(jnp.int32, (chunk, chunk), 1)
    tril = si <= ti
    eye = si == ti

    def to_row(col):
        return jnp.sum(jnp.where(eye, col, 0.0), axis=0, keepdims=True)

    lf = jax.nn.log_sigmoid(gf)
    b = jnp.sum(jnp.where(tril, to_row(lf), 0.0), axis=1, keepdims=True)
    a = gi - b
    a_row = to_row(a)
    cummax_a = jnp.max(jnp.where(tril, a_row, -jnp.inf), axis=1, keepdims=True)
    m_t = jnp.maximum(m + b, cummax_a + b)
    inter = jnp.exp(m + b - m_t)
    dmat = jnp.exp(jnp.where(tril, a_row + (b - m_t), -jnp.inf))

    k = k * (HEAD_DIM ** -0.5)
    s = lax.dot_general(q, k, (((1,), (1,)), ((), ())), preferred_element_type=F32) * dmat
    num = jnp.dot(s.astype(BF16), v, preferred_element_type=F32)
    num = num + inter * jnp.dot(q, c.astype(BF16), preferred_element_type=F32)
    qn = jnp.sum(s, axis=1, keepdims=True) + inter * jnp.sum(q.astype(F32) * n, axis=1, keepdims=True)
    h = num / jnp.maximum(jnp.abs(qn), jnp.exp(-m_t))

    b_last = b[chunk - 1:chunk, :]
    m_new = m_t[chunk - 1:chunk, :]
    decay = jnp.exp(m + b_last - m_new)
    w = jnp.exp(a + (b_last - m_new))
    wv = (w * v.astype(F32)).astype(BF16)
    c_new = decay * c + lax.dot_general(k, wv, (((0,), (0,)), ((), ())), preferred_element_type=F32)
    n_new = decay * n + jnp.sum(w * k.astype(F32), axis=0, keepdims=True)

    return h, c_new, n_new, m_new


def _gate_cols(g, bias_refs, head):
    bi_ref, bf_ref = bias_refs
    lane = lax.broadcasted_iota(jnp.int32, g.shape, 1)
    gi = jnp.sum(jnp.where(lane == head, g, 0.0), axis=1, keepdims=True) + bi_ref[head]
    gf = jnp.sum(jnp.where(lane == head + N_HEADS, g, 0.0), axis=1, keepdims=True) + bf_ref[head]
    return gi, gf


N_STEP_IN = 7


N_OUTPROJ_IN = 6


def _gated_head(h, o, gnorm):
    return (jax.nn.sigmoid(o.astype(F32)) * _rms(h, gnorm)).astype(BF16)


N_STEP_OUT = 4


def _mlstm_seq_kernel(bi_ref, bf_ref, q_ref, k_ref, v_ref, g_ref, *rest,
                      chunk, n_chunks, n_steps, zero_init, step, outproj):
    bias_refs = (bi_ref, bf_ref)
    n_init = 0 if zero_init else 3
    init_refs, rest = rest[:n_init], rest[n_init:]
    no_in = N_OUTPROJ_IN if outproj else 0
    ns_in = N_STEP_IN if step else 0
    op_in, step_in = rest[:no_in], rest[no_in:no_in + ns_in]
    outs = rest[no_in + ns_in:]
    n_rows_out = 2 if outproj else 1
    rows_out, (c_ref, n_ref, m_ref) = outs[:n_rows_out], outs[n_rows_out:n_rows_out + 3]
    step_out = outs[n_rows_out + 3:n_rows_out + 3 + N_STEP_OUT] if step else ()
    s = pl.program_id(0)
    active = s < n_steps
    cur = jnp.minimum(s, n_steps - 1)

    @pl.when((cur % n_chunks == 0) & active)
    def _():
        for dst, src in zip((c_ref, n_ref, m_ref), init_refs or (None,) * 3):
            dst[...] = jnp.zeros(dst.shape, F32) if src is None else src[...]

    if outproj:
        ym_ref = outs[-1]

        @pl.when(s == 0)
        def _():
            ym_ref[...] = jnp.zeros(ym_ref.shape, BF16)

    stages = iter(())
    if step:
        stages = _mlstm_step_stages(cur % N_HEADS, bias_refs, *step_in, *step_out, bb=step[0], seq=step[1])
    next(stages, None)

    if outproj:
        x_ref, yp_ref, o_ref, gn_ref, wo_ref, nw_ref = op_in
        x2_ref, xn2_ref = rows_out
        slot = s % 2
        ssq = jnp.zeros((chunk, 1), F32)
        out_cols = D_MODEL // N_HEADS

    g = g_ref[...]
    for head in range(N_HEADS):
        if outproj:
            oc = slice(head * out_cols, (head + 1) * out_cols)
            x2 = x_ref[:, oc] + jnp.dot(yp_ref[...], wo_ref[0:POOL_W, oc], preferred_element_type=F32)
            x2 = x2 + jnp.dot(ym_ref[1 - slot], wo_ref[POOL_W:, oc], preferred_element_type=F32)
            x2_ref[:, oc] = x2
            ssq = ssq + jnp.sum(x2 * x2, axis=1, keepdims=True)
        cols = slice(head * HEAD_DIM, (head + 1) * HEAD_DIM)
        gi, gf = _gate_cols(g, bias_refs, head)
        c_old, n_old, m_old = c_ref[0, head], n_ref[0, head], m_ref[0, head]
        h, c_new, n_new, m_new = _mlstm_chunk(
            q_ref[:, cols], k_ref[:, cols], v_ref[:, cols], gi, gf, c_old, n_old, m_old, chunk=chunk)
        if outproj:
            ym_ref[slot, :, cols] = _gated_head(h, o_ref[:, cols], gn_ref[:, cols])
            c_new, n_new, m_new = (jnp.where(active, new, old)
                                   for new, old in ((c_new, c_old), (n_new, n_old), (m_new, m_old)))
        else:
            rows_out[0][:, cols] = h
        c_ref[0, head] = c_new
        n_ref[0, head] = n_new
        m_ref[0, head] = m_new
        next(stages, None)
    for _ in stages:
        pass

    if outproj:
        scale = lax.rsqrt(ssq * (1.0 / D_MODEL) + EPS)
        xn2_ref[...] = (x2_ref[...] * scale * nw_ref[...]).astype(BF16)


def _mlstm_seq(qkvo, gates, b_i, b_f, init, *, batch, seq, chunk, step=None, outproj=None):
    nc = seq // chunk
    n_steps = batch * nc
    cur = lambda s: jnp.minimum(s, n_steps - 1)
    prev = lambda s: jnp.maximum(s - 1, 0)
    whole = lambda a: pl.BlockSpec(a.shape, lambda s: (0,) * a.ndim, pipeline_mode=pl.Buffered(1))
    if outproj:
        x, y_pool, gnorm, w_out, norm_w = outproj
        op_args = [x, y_pool, qkvo, gnorm, w_out, norm_w]
        late_rows = lambda w: pl.BlockSpec((chunk, w), lambda s: (prev(s), 0))
        op_in = [late_rows(D_MODEL), late_rows(POOL_W),
                 pl.BlockSpec((chunk, MLSTM_W), lambda s: (cur(s), 3)),
                 whole(gnorm), whole(w_out), whole(norm_w)]
        rows_out = [late_rows(D_MODEL), late_rows(D_MODEL)]
        rows_shapes = [jax.ShapeDtypeStruct((batch * seq, D_MODEL), F32),
                       jax.ShapeDtypeStruct((batch * seq, D_MODEL), BF16)]
        scratch = [pltpu.VMEM((2, chunk, MLSTM_W), BF16)]
    else:
        op_args, op_in, scratch = [], [], []
        rows_out = [pl.BlockSpec((chunk, MLSTM_W), lambda s: (cur(s), 0))]
        rows_shapes = [jax.ShapeDtypeStruct((batch * seq, MLSTM_W), F32)]
    step_args, step_in, step_out, step_shapes, step_cfg = [], [], [], [], None
    if step:
        (s_qkvo, s_gates, s_c0, s_n0, s_m0), s_batch, s_seq, s_bb = step
        step_in, step_out, step_shapes = _mlstm_step_specs(s_batch, s_seq, s_bb, n_steps, cur)
        step_args = [s_qkvo, s_qkvo, s_qkvo, s_gates, s_c0, s_n0, s_m0]
        step_cfg = (s_bb, s_seq)
    def col_spec(group):
        return pl.BlockSpec((chunk, MLSTM_W), lambda s: (cur(s), group))

    def st_spec(shape, in_b):
        nd = len(shape)
        return pl.BlockSpec((1, N_HEADS) + shape, lambda s: (in_b(s), 0) + (0,) * nd)

    state_shapes = ((HEAD_DIM, HEAD_DIM), (1, HEAD_DIM), (1, 1))
    init_in = [st_spec(shape, lambda s: 0) for shape in state_shapes] if init else []
    seq_of = lambda s: cur(s) // nc
    smem = pl.BlockSpec(memory_space=pltpu.SMEM)
    return pl.pallas_call(
        functools.partial(_mlstm_seq_kernel, chunk=chunk, n_chunks=nc, n_steps=n_steps, zero_init=init is None,
                          step=step_cfg, outproj=bool(outproj)),
        grid=(n_steps + (1 if outproj else 0),),
        in_specs=[
            smem, smem,
            col_spec(0), col_spec(1), col_spec(2),
            pl.BlockSpec((chunk, LANES), lambda s: (cur(s), 0)),
        ] + init_in + op_in + step_in,
        out_specs=rows_out + [st_spec(shape, seq_of) for shape in state_shapes] + step_out,
        out_shape=rows_shapes + [
            jax.ShapeDtypeStruct((batch, N_HEADS, HEAD_DIM, HEAD_DIM), F32),
            jax.ShapeDtypeStruct((batch, N_HEADS, 1, HEAD_DIM), F32),
            jax.ShapeDtypeStruct((batch, N_HEADS, 1, 1), F32),
        ] + step_shapes,
        scratch_shapes=scratch,
        compiler_params=pltpu.CompilerParams(
            dimension_semantics=("arbitrary",),
            vmem_limit_bytes=VMEM_LIMIT_RESIDENT_WEIGHT if outproj else VMEM_LIMIT),
        name="mlstm_seq",
    )(b_i, b_f, qkvo, qkvo, qkvo, gates, *(init or ()), *op_args, *step_args)


def _mlstm_step_stages(head, bias_refs, q_ref, k_ref, v_ref, g_ref, c0_ref, n0_ref, m0_ref,
                       y_ref, c_ref, n_ref, m_ref, *, bb, seq):
    rows = bb * seq
    ti = lax.broadcasted_iota(jnp.int32, (rows, rows), 0)
    si = lax.broadcasted_iota(jnp.int32, (rows, rows), 1)
    same = (ti // seq) == (si // seq)
    tril = same & (si <= ti)
    eye = si == ti

    def to_row(col):
        return jnp.sum(jnp.where(eye, col, 0.0), axis=0, keepdims=True)

    gi, gf = _gate_cols(g_ref[...], bias_refs, head)
    lane = lax.broadcasted_iota(jnp.int32, m0_ref.shape, 1)
    m0 = jnp.sum(jnp.where(lane == head, m0_ref[...], 0.0), axis=1, keepdims=True)

    lf_row = to_row(jax.nn.log_sigmoid(gf))
    b = jnp.sum(jnp.where(tril, lf_row, 0.0), axis=1, keepdims=True)
    b_last = jnp.sum(jnp.where(same, lf_row, 0.0), axis=1, keepdims=True)
    a = gi - b
    a_row = to_row(a)
    cummax_a = jnp.max(jnp.where(tril, a_row, -jnp.inf), axis=1, keepdims=True)
    seqmax_a = jnp.max(jnp.where(same, a_row, -jnp.inf), axis=1, keepdims=True)
    m_t = jnp.maximum(m0 + b, cummax_a + b)
    m_new = jnp.maximum(m0 + b_last, seqmax_a + b_last)
    inter = jnp.exp(m0 + b - m_t)
    dmat = jnp.exp(jnp.where(tril, a_row + (b - m_t), -jnp.inf))
    decay = jnp.exp(m0 + b_last - m_new)
    w = jnp.exp(a + (b_last - m_new))

    q = q_ref[...]
    k = k_ref[...] * (HEAD_DIM ** -0.5)
    v = v_ref[...]
    s = lax.dot_general(q, k, (((1,), (1,)), ((), ())), preferred_element_type=F32) * dmat
    sv = jnp.dot(s.astype(BF16), v, preferred_element_type=F32)

    qf = q.astype(F32)
    kf = k.astype(F32)
    wk = w * kf
    wv = w * v.astype(F32)
    qc, qdn = [], []
    yield
    for i in range(bb):
        r = slice(i * seq, (i + 1) * seq)
        c_i = c0_ref[i, 0]
        n_i = n0_ref[i:i + 1, :]
        d_i = decay[i * seq:i * seq + 1, :]
        qc.append(jnp.dot(qf[r].astype(BF16), c_i.astype(BF16), preferred_element_type=F32))
        qdn.append(jnp.sum(qf[r] * n_i, axis=1, keepdims=True))
        upd = lax.dot_general(kf[r].astype(BF16), wv[r].astype(BF16), (((0,), (0,)), ((), ())),
                              preferred_element_type=F32)
        c_ref[i, 0] = d_i * c_i + upd
        n_ref[i:i + 1, :] = d_i * n_i + jnp.sum(wk[r], axis=0, keepdims=True)
        if (i + 1) % (bb // N_HEADS) == 0:
            yield

    num = sv + inter * jnp.concatenate(qc, axis=0)
    qn = jnp.sum(s, axis=1, keepdims=True) + inter * jnp.concatenate(qdn, axis=0)
    h = num / jnp.maximum(jnp.abs(qn), jnp.exp(-m_t))
    y_ref[...] = h
    m_ref[0, 0] = to_row(m_new)


def _mlstm_step_specs(batch, seq, bb, n_steps, step_of):
    assert (batch // bb) * N_HEADS == n_steps
    rows = bb * seq
    blk = lambda *idx: step_of(*idx) // N_HEADS
    head = lambda *idx: step_of(*idx) % N_HEADS

    def col_spec(off):
        return pl.BlockSpec((rows, HEAD_DIM), lambda *idx: (blk(*idx), off + head(*idx)))

    c_spec = pl.BlockSpec((bb, 1, HEAD_DIM, HEAD_DIM), lambda *idx: (blk(*idx), head(*idx), 0, 0))
    n_spec = pl.BlockSpec((bb, HEAD_DIM), lambda *idx: (blk(*idx), head(*idx)))
    in_specs = [
        col_spec(0), col_spec(N_HEADS), col_spec(2 * N_HEADS),
        pl.BlockSpec((rows, LANES), lambda *idx: (blk(*idx), 0)),
        c_spec, n_spec,
        pl.BlockSpec((rows, N_HEADS), lambda *idx: (blk(*idx), 0)),
    ]
    out_specs = [
        pl.BlockSpec((rows, HEAD_DIM), lambda *idx: (blk(*idx), head(*idx))),
        c_spec, n_spec,
        pl.BlockSpec((1, 1, 1, rows), lambda *idx: (blk(*idx), head(*idx), 0, 0)),
    ]
    out_shapes = [
        jax.ShapeDtypeStruct((batch * seq, MLSTM_W), F32),
        jax.ShapeDtypeStruct((batch, N_HEADS, HEAD_DIM, HEAD_DIM), F32),
        jax.ShapeDtypeStruct((batch, MLSTM_W), F32),
        jax.ShapeDtypeStruct((batch // bb, N_HEADS, 1, rows), F32),
    ]
    return in_specs, out_specs, out_shapes


def _mixer_out_kernel(x_ref, u_ref, prev_ref, wp_ref, sc_ref, h_ref, o_ref, gn_ref, wo_ref, nw_ref,
                      x2_ref, xn2_ref, st_ref, e_ref, *, bb, tt, pos0):
    e_ref[:, 0:1, :] = jnp.zeros((bb, 1, POOL_W), F32)
    for t in range(POOL_HIST):
        e_ref[:, 1 + t, :] = prev_ref[t]
    e_ref[:, HIST_PAD:, :] = u_ref[...]
    st_ref[...] = e_ref[:, tt + 1:tt + HIST_PAD, :]
    pos = pos0 + lax.broadcasted_iota(jnp.int32, (tt, 1), 0)
    ym = [_gated_head(h_ref[:, c], o_ref[:, c], gn_ref[:, c])
          for c in (slice(hd * HEAD_DIM, (hd + 1) * HEAD_DIM) for hd in range(N_HEADS))]
    x2 = x_ref[...] + jnp.dot(jnp.concatenate(ym, axis=1), wo_ref[POOL_W:, :], preferred_element_type=F32)
    yp = [_pool_group(e_ref, wp_ref, sc_ref, g, w, pos, bb) for g, w in enumerate(POOL_WINDOWS)]
    x2 = x2 + jnp.dot(jnp.concatenate(yp, axis=1), wo_ref[0:POOL_W, :], preferred_element_type=F32)
    x2_ref[...] = x2
    xn2_ref[...] = _rms(x2, nw_ref[...]).astype(BF16)


def _mixer_out(x, u, prev, w_pool, scale, h_ml, qkvo, gnorm, w_out, norm_w, *, bb, pos0):
    b, tt, _ = u.shape
    tm = bb * tt
    row = lambda w: pl.BlockSpec((tm, w), lambda i: (i, 0))
    whole = lambda a: pl.BlockSpec(a.shape, lambda i: (0,) * a.ndim, pipeline_mode=pl.Buffered(1))
    seq_blk = lambda w: pl.BlockSpec((bb, w, POOL_W), lambda i: (i, 0, 0))
    return pl.pallas_call(
        functools.partial(_mixer_out_kernel, bb=bb, tt=tt, pos0=pos0),
        grid=(b // bb,),
        in_specs=[
            row(D_MODEL), seq_blk(tt), pl.BlockSpec((POOL_HIST, bb, POOL_W), lambda i: (0, i, 0)),
            whole(w_pool), whole(scale),
            row(MLSTM_W), pl.BlockSpec((tm, MLSTM_W), lambda i: (i, 3)), whole(gnorm), whole(w_out), whole(norm_w),
        ],
        out_specs=[row(D_MODEL), row(D_MODEL), seq_blk(POOL_HIST)],
        out_shape=[
            jax.ShapeDtypeStruct((b * tt, D_MODEL), F32),
            jax.ShapeDtypeStruct((b * tt, D_MODEL), BF16),
            jax.ShapeDtypeStruct((b, POOL_HIST, POOL_W), F32),
        ],
        scratch_shapes=[pltpu.VMEM((bb, HIST_PAD + tt, POOL_W), F32)],
        compiler_params=_params(("arbitrary",)),
        name="mixer_out",
    )(x, u, prev, w_pool, scale, h_ml, qkvo, gnorm, w_out, norm_w)


def _ffn_kernel(xn_ref, x2_ref, wg_ref, wu_ref, wd_ref, nw_ref, y_ref, acc_ref):
    f = pl.program_id(1)
    nf = pl.num_programs(1)

    @pl.when(f == 0)
    def _():
        acc_ref[...] = x2_ref[...]

    xn = xn_ref[...]
    g = jnp.dot(xn, wg_ref[...], preferred_element_type=F32)
    u = jnp.dot(xn, wu_ref[...], preferred_element_type=F32)
    h = (jax.nn.silu(g) * u).astype(BF16)
    acc_ref[...] += jnp.dot(h, wd_ref[...], preferred_element_type=F32)

    @pl.when(f == nf - 1)
    def _():
        y_ref[...] = _rms(acc_ref[...], nw_ref[...])


def _ffn(xn2, x2, w_gate, w_up, w_down, norm_w, *, tm, tf):
    m = xn2.shape[0]
    return pl.pallas_call(
        _ffn_kernel,
        grid=(m // tm, D_FF // tf),
        in_specs=[
            pl.BlockSpec((tm, D_MODEL), lambda i, f: (i, 0)),
            pl.BlockSpec((tm, D_MODEL), lambda i, f: (i, 0)),
            pl.BlockSpec((D_MODEL, tf), lambda i, f: (0, f)),
            pl.BlockSpec((D_MODEL, tf), lambda i, f: (0, f)),
            pl.BlockSpec((tf, D_MODEL), lambda i, f: (f, 0)),
            pl.BlockSpec((1, D_MODEL), lambda i, f: (0, 0)),
        ],
        out_specs=pl.BlockSpec((tm, D_MODEL), lambda i, f: (i, 0)),
        out_shape=jax.ShapeDtypeStruct((m, D_MODEL), F32),
        scratch_shapes=[pltpu.VMEM((tm, D_MODEL), F32)],
        compiler_params=_params(("arbitrary", "arbitrary")),
        name="ffn",
    )(xn2, x2, w_gate, w_up, w_down, norm_w)


def kernel(x_prompt, x_sample, state_pool, state_mlstm_C, state_mlstm_n, state_mlstm_m, meta_tokens, norm_mix_w, w_in, b_igate, b_fgate, w_pool, pool_scale, mlstm_norm_w, w_out, norm_ffn_w, w_gate, w_up, w_down, norm_final_w):
    bp, tp, _ = x_prompt.shape
    bs, ts, _ = x_sample.shape

    w_in_t = w_in[0].T
    w_g8 = jnp.pad(w_in_t[MAIN_W:], ((0, LANES - 2 * N_HEADS), (0, 0))).astype(BF16)
    wp = w_pool[0]
    nmix = norm_mix_w[0].reshape(1, D_MODEL)
    nffn = norm_ffn_w[0].reshape(1, D_MODEL)
    nfin = norm_final_w.reshape(1, D_MODEL)
    scale = pool_scale[0].reshape(1, POOL_W)
    gnorm = mlstm_norm_w[0].reshape(1, MLSTM_W)
    b_i, b_f = b_igate[0], b_fgate[0]

    xp = x_prompt.reshape(bp * tp, D_MODEL)
    xs = x_sample.reshape(bs * ts, D_MODEL)

    u_s, qkvo_s, g_s, u_m, qkvo_m, g_m, w_main = _inproj([xs, meta_tokens], nmix, w_in_t, w_g8, tn=META_TN)
    prev_p = u_m[1:N_META].reshape(1, POOL_HIST, POOL_W)
    n_prompt_tiles = bp * tp // PROMPT_TT
    convert = tuple((w[0], n_prompt_tiles) for w in (w_out, w_gate, w_up, w_down))
    yp_p, qkvo_p, g_p, pool_p, wo, wg, wu, wd = _inproj_rows(
        xp, nmix, w_main, w_g8, prev_p, wp, scale, bb=1, tt=PROMPT_TT, pos0=N_META, tiles_per_seq=tp // PROMPT_TT,
        convert=convert)

    _, *state_m = _mlstm_seq(qkvo_m, g_m, b_i, b_f, None, batch=1, seq=N_META, chunk=N_META)
    step_ops = (qkvo_s, g_s, state_mlstm_C[0], state_mlstm_n[0].reshape(bs, MLSTM_W),
                jnp.repeat(state_mlstm_m[0], ts, axis=0))
    x2_p, xn2_p, c_p, n_p, m_p, h_s, c_s, n_s, m_s = _mlstm_seq(
        qkvo_p, g_p, b_i, b_f, state_m, batch=bp, seq=tp, chunk=MLSTM_CHUNK,
        step=(step_ops, bs, ts, STEP_BB), outproj=(xp, yp_p, gnorm, wo, nffn))
    m_s = m_s.reshape(bs // STEP_BB, N_HEADS, STEP_BB, ts)[..., 0].transpose(0, 2, 1)

    prev_s = state_pool[0].transpose(1, 0, 2)
    x2_s, xn2_s, pool_s = _mixer_out(xs, u_s.reshape(bs, ts, POOL_W), prev_s, wp, scale, h_s, qkvo_s, gnorm,
                                     wo, nffn, bb=DECODE_BB, pos0=PAST_LEN)
    y_p = _ffn(xn2_p, x2_p, wg, wu, wd, nfin, tm=FFN_TM, tf=FFN_TF)
    y_s = _ffn(xn2_s, x2_s, wg, wu, wd, nfin, tm=FFN_TM, tf=FFN_TF)

    return (
        y_p.reshape(bp, tp, D_MODEL),
        y_s.reshape(bs, ts, D_MODEL),
        pool_p[None],
        c_p[None],
        n_p.reshape(1, bp, N_HEADS, HEAD_DIM),
        m_p.reshape(1, bp, N_HEADS),
        pool_s[None],
        c_s[None],
        n_s.reshape(1, bs, N_HEADS, HEAD_DIM),
        m_s.reshape(1, bs, N_HEADS),
    )
```
